```python
import math
import jax
import jax.numpy as jnp
from jax import lax
import numpy as np

D_MODEL = 1024
BATCH = 2
SEQ = 16384
DEPTH = 4

GRID_W = 64
CTX_LEN = 256
N_MIXERS = 3
N_RWKV = (DEPTH + 2) // N_MIXERS
N_HYENA = (DEPTH + 1) // N_MIXERS
N_MLA = DEPTH // N_MIXERS
N_MOD = 6
NORM_EPS = 1e-6
RWKV_HEAD = 64
RWKV_HEADS = D_MODEL // RWKV_HEAD
DECAY_LORA = 64
AAA_LORA = 64
MV_LORA = 32
GATE_LORA = 160
GN_EPS = 64e-5
FILTER_EMB = 33
FILTER_WIDTH = 64
FILTER_OUT_SCALE = 0.004
FAST_DECAY = math.log(1e-2) / 0.3
SLOW_DECAY = math.log(1e-2) / 1.5
MLA_HEADS = 8
QK_NOPE = 128
QK_ROPE = 64
QK_HEAD = QK_NOPE + QK_ROPE
V_HEAD = 128
Q_LORA = 384
KV_LORA = 256
ROPE_THETA = 10000.0
Q_BLOCK = 128
N_EXPERTS = 32
TOP_K = 4
D_EXPERT = 1024
SWIGLU_LIMIT = 7.0
SWIGLU_ALPHA = 1.702
MOE_BLOCK = 256

kernel_name = 'hybrid_rwkv7_hyena_mla_moe_dit'


def _rms_norm(x, g):
    xf = x.astype(jnp.float32)
    y = xf * lax.rsqrt(jnp.mean(xf * xf, axis=-1, keepdims=True) + NORM_EPS)
    return (y * g.astype(jnp.float32)).astype(x.dtype)


def _pad_seq(x):
    return jnp.pad(x, ((0, 0), (1, 1), (0, 0)))


def _centred_shift(x):
    xp = _pad_seq(x)
    return 0.5 * (xp[:, :-2] + xp[:, 2:]) - x


def _dwconv3(x, w, b):
    xp = _pad_seq(x)
    return xp[:, :-2] * w[0] + xp[:, 1:-1] * w[1] + xp[:, 2:] * w[2] + b


def _split_heads(t, n_heads):
    b, l, ch = t.shape
    return t.reshape(b, l, n_heads, ch // n_heads)


def _rwkv_project(h, p, v_first, vres):
    xx = _centred_shift(h)
    xr, xw, xk, xv, xa, xg = [h + xx * p['mu'][m] for m in range(6)]
    r = xr @ p['w_rkv'][0]
    k = xk @ p['w_rkv'][1]
    v = xv @ p['w_rkv'][2]
    if vres is not None:
        v0, v1, v2 = vres
        v = v + (v_first - v) * jax.nn.sigmoid(v0 + (xv @ v1) @ v2)
    g = jax.nn.sigmoid(xg @ p['g1']) @ p['g2']
    kk = _split_heads(k * p['k_k'], RWKV_HEADS).astype(jnp.float32)
    kk = kk / jnp.maximum(jnp.sqrt(jnp.sum(kk * kk, axis=-1, keepdims=True)), 1e-12)
    dirs = []
    for d in range(2):
        w_log = -jax.nn.softplus(-(p['w0'][d] + jnp.tanh(xw @ p['w1'][d]) @ p['w2'][d])) - 0.5
        decay = jnp.exp(-jnp.exp(w_log.astype(jnp.float32)))
        a = jax.nn.sigmoid(p['a0'][d] + (xa @ p['a1'][d]) @ p['a2'][d])
        k_d = k * (1.0 + (a - 1.0) * p['k_a'])
        dirs.append((_split_heads(decay, RWKV_HEADS), _split_heads(a, RWKV_HEADS), _split_heads(k_d, RWKV_HEADS)))
    return {'r': _split_heads(r, RWKV_HEADS), 'v': _split_heads(v, RWKV_HEADS), 'v_raw': v,
            'kk': kk, 'g': g, 'dirs': dirs}


def _wkv_scan(state, r, decay, kk, a, k, v, reverse):
    def step(s, inp):
        r_t, w_t, kk_t, a_t, k_t, v_t = inp
        sa = jnp.einsum('bhvk,bhk->bhv', s, -kk_t)
        s = (s * w_t[:, :, None, :]
             + sa[..., :, None] * (kk_t * a_t)[..., None, :]
             + v_t[..., :, None] * k_t[..., None, :])
        return s, jnp.einsum('bhvk,bhk->bhv', s, r_t)
    xs = tuple(jnp.swapaxes(t, 0, 1) for t in (r, decay, kk, a, k, v))
    state, ys = lax.scan(step, state, xs, reverse=reverse)
    return state, jnp.swapaxes(ys, 0, 1)


def _rwkv_readout(y, s, p):
    b, l, nh, hd = y.shape
    mean = jnp.mean(y, axis=-1, keepdims=True)
    var = jnp.mean(jnp.square(y - mean), axis=-1, keepdims=True)
    yn = ((y - mean) * lax.rsqrt(var + GN_EPS)).reshape(b, l, nh * hd).astype(s['g'].dtype)
    yn = yn * p['ln_g'] + p['ln_b']
    bonus = sum(jnp.sum(s['r'] * k_h * p['r_k'], axis=-1, keepdims=True) for (_, _, k_h) in s['dirs'])
    bonus = (bonus * s['v']).reshape(b, l, nh * hd)
    return ((yn + bonus) * s['g']) @ p['w_o']


def _rwkv_mixer(h_lat, h_ctx, p, v_first, vres, need_ctx):
    vf_c, vf_l = v_first if vres is not None else (None, None)
    pc = _rwkv_project(h_ctx, p, vf_c, vres)
    pl = _rwkv_project(h_lat, p, vf_l, vres)
    s0 = jnp.zeros((h_lat.shape[0], RWKV_HEADS, RWKV_HEAD, RWKV_HEAD), jnp.float32)
    y_lat = 0.0
    y_ctx = 0.0
    for d, rev in enumerate((False, True)):
        dec_c, a_c, k_c = pc['dirs'][d]
        s_c, yc = _wkv_scan(s0, pc['r'], dec_c, pc['kk'], a_c, k_c, pc['v'], rev)
        dec_l, a_l, k_l = pl['dirs'][d]
        _, yl = _wkv_scan(s_c, pl['r'], dec_l, pl['kk'], a_l, k_l, pl['v'], rev)
        y_ctx = y_ctx + yc
        y_lat = y_lat + yl
    out_lat = _rwkv_readout(y_lat, pl, p)
    out_ctx = _rwkv_readout(y_ctx, pc, p) if need_ctx else None
    return out_ctx, out_lat, (pc['v_raw'], pl['v_raw'])


def _hyena_filters(length, p):
    t = jnp.linspace(0.0, 1.0, length, dtype=jnp.float32)[:, None]
    bands = (FILTER_EMB - 1) // 2
    f = jnp.linspace(1e-4, bands - 1, bands, dtype=jnp.float32)[None, :]
    ang = 2.0 * math.pi * f * jnp.arange(length, dtype=jnp.float32)[:, None] / length
    z = jnp.concatenate([t, jnp.cos(ang), -jnp.sin(ang)], axis=-1)
    hdn = jnp.sin(p['f_freq'][0] * (z @ p['f_w1'] + p['f_b1']))
    hdn = jnp.sin(p['f_freq'][1] * (hdn @ p['f_w2'] + p['f_b2']))
    hdn = jnp.sin(p['f_freq'][2] * (hdn @ p['f_w3'] + p['f_b3']))
    filt = (hdn @ p['f_wout']).astype(jnp.float32)
    deltas = jnp.abs(jnp.linspace(FAST_DECAY, SLOW_DECAY, D_MODEL, dtype=jnp.float32))
    window = jnp.exp(-t * deltas)
    return filt[:, :D_MODEL] * window, filt[:, D_MODEL:] * window


def _two_sided_long_conv(z, h_fwd, h_bwd):
    length = z.shape[1]
    filt2 = jnp.concatenate([h_fwd, jnp.zeros_like(h_fwd[:1]), h_bwd[1:][::-1]], axis=0)
    f_hat = jnp.fft.rfft(filt2, n=2 * length, axis=0)
    z_hat = jnp.fft.rfft(z.astype(jnp.float32), n=2 * length, axis=1)
    return jnp.fft.irfft(z_hat * f_hat[None], n=2 * length, axis=1)[:, :length].astype(z.dtype)


def _hyena_stream(h, p):
    u = _dwconv3(h @ p['w_in'] + p['b_in'], p['short_w'], p['short_b'])
    x0, x1, v = jnp.split(u, 3, axis=-1)
    h_fwd, h_bwd = _hyena_filters(h.shape[1], p)
    z = v * x1
    z = _two_sided_long_conv(z, h_fwd, h_bwd) + z * p['bias']
    return (x0 * z) @ p['w_out'] + p['b_out']


def _rope_1d(x, pos):
    d = x.shape[-1]
    inv = ROPE_THETA ** (-jnp.arange(0, d, 2, dtype=jnp.float32) / d)
    ang = pos.astype(jnp.float32)[:, None] * inv[None, :]
    cos = jnp.cos(ang)[:, None, :]
    sin = jnp.sin(ang)[:, None, :]
    x1, x2 = jnp.split(x, 2, axis=-1)
    return jnp.concatenate([x1 * cos - x2 * sin, x1 * sin + x2 * cos], axis=-1).astype(x.dtype)


def _axial_rope(x, row, col):
    half = x.shape[-1] // 2
    return jnp.concatenate([_rope_1d(x[..., :half], row), _rope_1d(x[..., half:], col)], axis=-1)


def _mla_project(h, p, rope_pos):
    b, l, _ = h.shape
    q_c, kv_c, k_pe = jnp.split(h @ p['w_a'], [Q_LORA, Q_LORA + KV_LORA], axis=-1)
    q = _split_heads(_rms_norm(q_c, p['q_norm_g']) @ p['w_qb'], MLA_HEADS)
    kv = _split_heads(_rms_norm(kv_c, p['kv_norm_g']) @ p['w_kvb'], MLA_HEADS)
    k_nope, v = jnp.split(kv, [QK_NOPE], axis=-1)
    k = jnp.concatenate([k_nope, jnp.broadcast_to(k_pe[:, :, None, :], (b, l, MLA_HEADS, QK_ROPE))], axis=-1)
    q = _rms_norm(q, p['qn_g'])
    k = _rms_norm(k, p['kn_g'])
    if rope_pos is not None:
        row, col = rope_pos
        q = jnp.concatenate([q[..., :QK_NOPE], _axial_rope(q[..., QK_NOPE:], row, col)], axis=-1)
        k = jnp.concatenate([k[..., :QK_NOPE], _axial_rope(k[..., QK_NOPE:], row, col)], axis=-1)
    return q, k, v


def _attend(q, k, v):
    s = jnp.einsum('bqhd,bkhd->bhqk', q, k).astype(jnp.float32) * (QK_HEAD ** -0.5)
    pr = jax.nn.softmax(s, axis=-1).astype(v.dtype)
    return jnp.einsum('bhqk,bkhd->bqhd', pr, v)


def _mla_mixer(h_lat, h_ctx, p, rope_pos, need_ctx):
    q_c, k_c, v_c = _mla_project(h_ctx, p, None)
    q_l, k_l, v_l = _mla_project(h_lat, p, rope_pos)
    b, s, nh, dq = q_l.shape
    k_all = jnp.concatenate([k_c, k_l], axis=1)
    v_all = jnp.concatenate([v_c, v_l], axis=1)
    q_blocks = jnp.swapaxes(q_l.reshape(b, s // Q_BLOCK, Q_BLOCK, nh, dq), 0, 1)
    o_l = lax.map(lambda qb: _attend(qb, k_all, v_all), q_blocks)
    out_lat = jnp.swapaxes(o_l, 0, 1).reshape(b, s, nh * V_HEAD) @ p['w_o']
    out_ctx = _attend(q_c, k_c, v_c).reshape(b, h_ctx.shape[1], nh * V_HEAD) @ p['w_o'] if need_ctx else None
    return out_ctx, out_lat


def _moe(h, router_w, router_b, w_in, b_in, w_out, b_out):
    n_tok, d = h.shape
    n_assign = n_tok * TOP_K
    logits = (h @ router_w + router_b).astype(jnp.float32)
    top_val, top_idx = lax.top_k(logits, TOP_K)
    gates = jax.nn.softmax(top_val, axis=-1).astype(h.dtype).reshape(n_assign)
    flat_e = top_idx.reshape(n_assign)
    order = jnp.argsort(flat_e)
    e_sorted = flat_e[order]
    tok_sorted = order // TOP_K
    counts = jnp.zeros((N_EXPERTS,), jnp.int32).at[flat_e].add(1)
    padded = (counts + MOE_BLOCK - 1) // MOE_BLOCK * MOE_BLOCK
    start = jnp.cumsum(counts) - counts
    pad_end = jnp.cumsum(padded)
    pad_start = pad_end - padded
    dest = pad_start[e_sorted] + jnp.arange(n_assign, dtype=jnp.int32) - start[e_sorted]
    n_blocks = -(-n_assign // MOE_BLOCK) + N_EXPERTS
    row_tok = jnp.zeros((n_blocks * MOE_BLOCK,), jnp.int32).at[dest].set(tok_sorted)
    block_e = jnp.minimum(jnp.searchsorted(pad_end, jnp.arange(n_blocks, dtype=jnp.int32) * MOE_BLOCK, side='right'), N_EXPERTS - 1)
    xb = h[row_tok].reshape(n_blocks, MOE_BLOCK, d)

    def expert_block(args):
        xe, e = args
        gu = xe @ w_in[e] + b_in[e]
        glu = jnp.minimum(gu[:, :D_EXPERT], SWIGLU_LIMIT)
        lin = jnp.clip(gu[:, D_EXPERT:], -SWIGLU_LIMIT, SWIGLU_LIMIT)
        return (glu * jax.nn.sigmoid(SWIGLU_ALPHA * glu) * (lin + 1.0)) @ w_out[e] + b_out[e]

    yb = lax.map(expert_block, (xb, block_e)).reshape(n_blocks * MOE_BLOCK, d)
    return jax.ops.segment_sum(yb[dest] * gates[order][:, None], tok_sorted, num_segments=n_tok)


def setup_inputs(seed: int = 0) -> dict:
    key = jax.random.key(seed)
    keys = iter(jax.random.split(key, 64))
    f32 = jnp.float32
    D, H, N, E, F = D_MODEL, RWKV_HEADS, RWKV_HEAD, N_EXPERTS, D_EXPERT

    def nrm(shape, scale):
        return jax.random.normal(next(keys), shape, f32) * scale

    def gain(shape):
        return 1.0 + nrm(shape, 0.02)

    def unif(shape, lo, hi):
        return jax.random.uniform(next(keys), shape, f32, lo, hi)

    return {
        'x': nrm((BATCH, SEQ, D), 1.0),
        'c': nrm((BATCH, D), 1.0),
        'ctx': nrm((BATCH, CTX_LEN, D), 1.0),
        'c_ctx': nrm((D,), 1.0),
        'ada_w': nrm((DEPTH, D, N_MOD * D), 0.5 * D ** -0.5),
        'ada_b': nrm((DEPTH, N_MOD * D), 0.02),
        'norm_mix_g': gain((DEPTH, D)),
        'norm_ffn_g': gain((DEPTH, D)),
        'rwkv_mu': unif((N_RWKV, 6, D), 0.0, 1.0),
        'rwkv_w_rkv': nrm((N_RWKV, 3, D, D), D ** -0.5),
        'rwkv_w0': unif((N_RWKV, 2, D), -6.0, -1.0),
        'rwkv_w1': nrm((N_RWKV, 2, D, DECAY_LORA), D ** -0.5),
        'rwkv_w2': nrm((N_RWKV, 2, DECAY_LORA, D), 0.5 * DECAY_LORA ** -0.5),
        'rwkv_a0': nrm((N_RWKV, 2, D), 0.1),
        'rwkv_a1': nrm((N_RWKV, 2, D, AAA_LORA), D ** -0.5),
        'rwkv_a2': nrm((N_RWKV, 2, AAA_LORA, D), 0.5 * AAA_LORA ** -0.5),
        'rwkv_g1': nrm((N_RWKV, D, GATE_LORA), D ** -0.5),
        'rwkv_g2': nrm((N_RWKV, GATE_LORA, D), GATE_LORA ** -0.5),
        'rwkv_k_k': 0.85 + nrm((N_RWKV, D), 0.02),
        'rwkv_k_a': gain((N_RWKV, D)),
        'rwkv_r_k': nrm((N_RWKV, H, N), 0.1),
        'rwkv_ln_g': gain((N_RWKV, D)),
        'rwkv_ln_b': nrm((N_RWKV, D), 0.02),
        'rwkv_w_o': nrm((N_RWKV, D, D), D ** -0.5),
        'rwkv_v0': 1.0 + nrm((N_RWKV - 1, D), 0.1),
        'rwkv_v1': nrm((N_RWKV - 1, D, MV_LORA), D ** -0.5),
        'rwkv_v2': nrm((N_RWKV - 1, MV_LORA, D), 0.5 * MV_LORA ** -0.5),
        'hy_w_in': nrm((N_HYENA, D, 3 * D), D ** -0.5),
        'hy_b_in': nrm((N_HYENA, 3 * D), 0.02),
        'hy_short_w': nrm((N_HYENA, 3, 3 * D), 3 ** -0.5),
        'hy_short_b': nrm((N_HYENA, 3 * D), 0.02),
        'hy_f_w1': nrm((N_HYENA, FILTER_EMB, FILTER_WIDTH), FILTER_EMB ** -0.5),
        'hy_f_b1': nrm((N_HYENA, FILTER_WIDTH), 0.1),
        'hy_f_w2': nrm((N_HYENA, FILTER_WIDTH, FILTER_WIDTH), FILTER_WIDTH ** -0.5),
        'hy_f_b2': nrm((N_HYENA, FILTER_WIDTH), 0.1),
        'hy_f_w3': nrm((N_HYENA, FILTER_WIDTH, FILTER_WIDTH), FILTER_WIDTH ** -0.5),
        'hy_f_b3': nrm((N_HYENA, FILTER_WIDTH), 0.1),
        'hy_f_freq': 1.0 + nrm((N_HYENA, 3, FILTER_WIDTH), 0.1),
        'hy_f_wout': nrm((N_HYENA, FILTER_WIDTH, 2 * D), FILTER_OUT_SCALE),
        'hy_bias': nrm((N_HYENA, D), 1.0),
        'hy_w_out': nrm((N_HYENA, D, D), D ** -0.5),
        'hy_b_out': nrm((N_HYENA, D), 0.02),
        'mla_w_a': nrm((N_MLA, D, Q_LORA + KV_LORA + QK_ROPE), D ** -0.5),
        'mla_q_norm_g': gain((N_MLA, Q_LORA)),
        'mla_kv_norm_g': gain((N_MLA, KV_LORA)),
        'mla_w_qb': nrm((N_MLA, Q_LORA, MLA_HEADS * QK_HEAD), Q_LORA ** -0.5),
        'mla_w_kvb': nrm((N_MLA, KV_LORA, MLA_HEADS * (QK_NOPE + V_HEAD)), KV_LORA ** -0.5),
        'mla_qn_g': gain((N_MLA, QK_HEAD)),
        'mla_kn_g': gain((N_MLA, QK_HEAD)),
        'mla_w_o': nrm((N_MLA, MLA_HEADS * V_HEAD, D), (MLA_HEADS * V_HEAD) ** -0.5),
        'moe_router_w': nrm((DEPTH, D, E), D ** -0.5),
        'moe_router_b': nrm((DEPTH, E), 0.01),
        'moe_w_in': nrm((DEPTH, E, D, 2 * F), D ** -0.5),
        'moe_b_in': nrm((DEPTH, E, 2 * F), 0.01),
        'moe_w_out': nrm((DEPTH, E, F, D), F ** -0.5),
        'moe_b_out': nrm((DEPTH, E, D), 0.01),
    }


def reference(x, c, ctx, c_ctx, ada_w, ada_b, norm_mix_g, norm_ffn_g,
              rwkv_mu, rwkv_w_rkv, rwkv_w0, rwkv_w1, rwkv_w2, rwkv_a0, rwkv_a1, rwkv_a2,
              rwkv_g1, rwkv_g2, rwkv_k_k, rwkv_k_a, rwkv_r_k, rwkv_ln_g, rwkv_ln_b, rwkv_w_o,
              rwkv_v0, rwkv_v1, rwkv_v2,
              hy_w_in, hy_b_in, hy_short_w, hy_short_b, hy_f_w1, hy_f_b1, hy_f_w2, hy_f_b2,
              hy_f_w3, hy_f_b3, hy_f_freq, hy_f_wout, hy_bias, hy_w_out, hy_b_out,
              mla_w_a, mla_q_norm_g, mla_kv_norm_g, mla_w_qb, mla_w_kvb, mla_qn_g, mla_kn_g, mla_w_o,
              moe_router_w, moe_router_b, moe_w_in, moe_b_in, moe_w_out, moe_b_out):
    _, s_len, _ = x.shape
    ROWS = s_len // GRID_W
    row = jnp.repeat(jnp.arange(ROWS, dtype=jnp.int32), GRID_W)
    col = jnp.tile(jnp.arange(GRID_W, dtype=jnp.int32), ROWS)
    silu_c = jax.nn.silu(c)
    silu_cc = jax.nn.silu(c_ctx)
    xc = ctx
    v_first = None
    for i in range(DEPTH):
        need_ctx = i < DEPTH - 1
        j = i // N_MIXERS
        kind = i % N_MIXERS
        mod_l = (silu_c @ ada_w[i] + ada_b[i])[:, None, :]
        mod_c = (silu_cc @ ada_w[i] + ada_b[i])[None, None, :]
        sh1, sc1, g1, sh2, sc2, g2 = jnp.split(mod_l, N_MOD, axis=-1)
        csh1, csc1, cg1, csh2, csc2, cg2 = jnp.split(mod_c, N_MOD, axis=-1)
        h_l = _rms_norm(x, norm_mix_g[i]) * (1.0 + sc1) + sh1
        h_c = _rms_norm(xc, norm_mix_g[i]) * (1.0 + csc1) + csh1
        if kind == 0:
            p = {'mu': rwkv_mu[j], 'w_rkv': rwkv_w_rkv[j], 'w0': rwkv_w0[j], 'w1': rwkv_w1[j],
                 'w2': rwkv_w2[j], 'a0': rwkv_a0[j], 'a1': rwkv_a1[j], 'a2': rwkv_a2[j],
                 'g1': rwkv_g1[j], 'g2': rwkv_g2[j], 'k_k': rwkv_k_k[j], 'k_a': rwkv_k_a[j],
                 'r_k': rwkv_r_k[j], 'ln_g': rwkv_ln_g[j], 'ln_b': rwkv_ln_b[j], 'w_o': rwkv_w_o[j]}
            vres = None if j == 0 else (rwkv_v0[j - 1], rwkv_v1[j - 1], rwkv_v2[j - 1])
            y_c, y_l, v_cur = _rwkv_mixer(h_l, h_c, p, v_first, vres, need_ctx)
            if j == 0:
                v_first = v_cur
        elif kind == 1:
            p = {'w_in': hy_w_in[j], 'b_in': hy_b_in[j], 'short_w': hy_short_w[j], 'short_b': hy_short_b[j],
                 'f_w1': hy_f_w1[j], 'f_b1': hy_f_b1[j], 'f_w2': hy_f_w2[j], 'f_b2': hy_f_b2[j],
                 'f_w3': hy_f_w3[j], 'f_b3': hy_f_b3[j], 'f_freq': hy_f_freq[j], 'f_wout': hy_f_wout[j],
                 'bias': hy_bias[j], 'w_out': hy_w_out[j], 'b_out': hy_b_out[j]}
            y_l = _hyena_stream(h_l, p)
            y_c = _hyena_stream(h_c, p) if need_ctx else None
        else:
            p = {'w_a': mla_w_a[j], 'q_norm_g': mla_q_norm_g[j], 'kv_norm_g': mla_kv_norm_g[j],
                 'w_qb': mla_w_qb[j], 'w_kvb': mla_w_kvb[j], 'qn_g': mla_qn_g[j], 'kn_g': mla_kn_g[j],
                 'w_o': mla_w_o[j]}
            y_c, y_l = _mla_mixer(h_l, h_c, p, (row, col), need_ctx)
        x = x + g1 * y_l
        h2_l = _rms_norm(x, norm_ffn_g[i]) * (1.0 + sc2) + sh2
        moe_p = (moe_router_w[i], moe_router_b[i], moe_w_in[i], moe_b_in[i], moe_w_out[i], moe_b_out[i])
        if need_ctx:
            xc = xc + cg1 * y_c
            h2_c = _rms_norm(xc, norm_ffn_g[i]) * (1.0 + csc2) + csh2
            n_c = h2_c.shape[0] * h2_c.shape[1]
            f = _moe(jnp.concatenate([h2_c.reshape(-1, D_MODEL), h2_l.reshape(-1, D_MODEL)], axis=0), *moe_p)
            xc = xc + cg2 * f[:n_c].reshape(xc.shape)
            x = x + g2 * f[n_c:].reshape(x.shape)
        else:
            x = x + g2 * _moe(h2_l.reshape(-1, D_MODEL), *moe_p).reshape(x.shape)
    return x
```

```python
import functools
import math

import jax
import jax.numpy as jnp
import numpy as np
from jax import lax
from jax.experimental import pallas as pl
from jax.experimental.pallas import tpu as pltpu

F32 = jnp.float32
BF16 = jnp.bfloat16

D_MODEL = 1024
DEPTH = 4
GRID_W = 64
N_MIXERS = 3
N_MOD = 6
NORM_EPS = 1e-6
RWKV_HEAD = 64
RWKV_HEADS = D_MODEL // RWKV_HEAD
GN_EPS = 64e-5
WKV_CHUNK = 64
FILTER_EMB = 33
FILTER_EMB_PAD = 128
DFT_MIN_OUTER = 32
FAST_DECAY = math.log(1e-2) / 0.3
SLOW_DECAY = math.log(1e-2) / 1.5
MLA_HEADS = 8
QK_NOPE = 128
QK_ROPE = 64
QK_HEAD = QK_NOPE + QK_ROPE
V_HEAD = 128
Q_LORA = 384
KV_LORA = 256
ROPE_THETA = 10000.0
N_EXPERTS = 32
TOP_K = 4
D_EXPERT = 1024
SWIGLU_LIMIT = 7.0
SWIGLU_ALPHA = 1.702
MOE_ROWS = 256
ROUTER_LANES = 128

V7X_VMEM_BYTES = 64 * 1024 * 1024
VMEM_LIMIT = V7X_VMEM_BYTES * 3 // 4


def _cparams(*sem):
    return pltpu.CompilerParams(dimension_semantics=sem, vmem_limit_bytes=VMEM_LIMIT)


def _tile(n, prefs):
    for t in prefs:
        if n % t == 0:
            return t
    return n


def _linear_body(a_ref, w_ref, b_ref, o_ref, *, in_act):
    a = a_ref[...]
    if in_act == 'tanh':
        a = jnp.tanh(a.astype(F32))
    elif in_act == 'sigmoid':
        a = jax.nn.sigmoid(a.astype(F32))
    acc = jnp.dot(a.astype(BF16), w_ref[...], preferred_element_type=F32)
    o_ref[...] = (acc + b_ref[...]).astype(o_ref.dtype)


def _linear(a, w, b=None, *, in_act=None, out_dtype=F32, tn=None):
    m, k = a.shape
    n = w.shape[1]
    tm = _tile(m, (512, 256, 128, 64, 32, 16, 8))
    tn = n if tn is None else tn
    assert n % tn == 0
    if b is None:
        b = jnp.zeros((n,), F32)
    return pl.pallas_call(
        functools.partial(_linear_body, in_act=in_act),
        grid=(m // tm, n // tn),
        in_specs=[pl.BlockSpec((tm, k), lambda i, j: (i, 0)),
                  pl.BlockSpec((k, tn), lambda i, j: (0, j)),
                  pl.BlockSpec((1, tn), lambda i, j: (0, j))],
        out_specs=pl.BlockSpec((tm, tn), lambda i, j: (i, j)),
        out_shape=jax.ShapeDtypeStruct((m, n), out_dtype),
        compiler_params=_cparams("parallel", "parallel"),
        name="linear",
    )(a, w.astype(BF16), b.reshape(1, n).astype(F32))


def _norm_mod_body(x_ref, g_ref, sc_ref, sh_ref, o_ref):
    x = x_ref[0]
    y = x * lax.rsqrt(jnp.mean(x * x, axis=-1, keepdims=True) + NORM_EPS)
    o_ref[0] = ((y * g_ref[...]) * (1.0 + sc_ref[0]) + sh_ref[0]).astype(o_ref.dtype)


def _norm_mod(x, g, sc, sh):
    b, l, d = x.shape
    tl = _tile(l, (512, 256, 128, 64, 32, 16, 8))
    return pl.pallas_call(
        _norm_mod_body,
        grid=(b, l // tl),
        in_specs=[pl.BlockSpec((1, tl, d), lambda i, j: (i, j, 0)),
                  pl.BlockSpec((1, d), lambda i, j: (0, 0)),
                  pl.BlockSpec((1, 1, d), lambda i, j: (i, 0, 0)),
                  pl.BlockSpec((1, 1, d), lambda i, j: (i, 0, 0))],
        out_specs=pl.BlockSpec((1, tl, d), lambda i, j: (i, j, 0)),
        out_shape=jax.ShapeDtypeStruct((b, l, d), F32),
        compiler_params=_cparams("parallel", "parallel"),
        name="norm_mod",
    )(x, g.reshape(1, d), sc.reshape(b, 1, d), sh.reshape(b, 1, d))


def _bmm(a, b):
    return jnp.einsum('hmk,hkn->hmn', a.astype(BF16), b.astype(BF16), preferred_element_type=F32)


def _bmm_nt(a, b):
    return jnp.einsum('hmk,hnk->hmn', a.astype(BF16), b.astype(BF16), preferred_element_type=F32)


def _wkv_prep_body(r_ref, lw_ref, kk_ref, a_ref, kd_ref, v_ref, vt_ref,
                   w_out, upt_out, rt_out, arb_out, bh_out, yp_out, sp_out, gc_out):
    r = r_ref[0]
    lw = lw_ref[0]
    kk = kk_ref[0]
    ag = a_ref[0]
    kd = kd_ref[0]
    v = v_ref[0]
    vt = vt_ref[0, :, 0]
    h, c, n = r.shape
    row = lax.broadcasted_iota(jnp.int32, (c, c), 0)
    col = lax.broadcasted_iota(jnp.int32, (c, c), 1)
    incl = row >= col
    strict = row > col
    tri = jnp.broadcast_to(jnp.where(incl, 1.0, 0.0).astype(BF16), (h, c, c))
    hi = lw.astype(BF16)
    rem = lw - hi.astype(F32)
    mid = rem.astype(BF16)
    lo = (rem - mid.astype(F32)).astype(BF16)
    cum = _bmm(tri, hi) + _bmm(tri, mid) + _bmm(tri, lo)
    g_incl = jnp.exp(cum)
    g_inv = jnp.exp(-cum)
    g_prev = jnp.exp(cum - lw)
    cum_end = cum[:, c - 1:c, :]
    g_rest = jnp.exp(cum_end - cum)
    bvec = kk * ag
    rt = r * g_incl
    kt = kd * g_inv
    bt = bvec * g_inv
    at = -kk * g_prev
    ar = jnp.concatenate([at, rt], axis=1)
    ab = _bmm_nt(ar, bt)
    ak = _bmm_nt(ar, kt)
    a_ab = jnp.where(strict, ab[:, :c], 0.0)
    a_rb = jnp.where(incl, ab[:, c:], 0.0)
    a_ak = jnp.where(strict, ak[:, :c], 0.0)
    a_rk = jnp.where(incl, ak[:, c:], 0.0)
    same16 = jnp.right_shift(row, 4) == jnp.right_shift(col, 4)
    same32 = jnp.right_shift(row, 5) == jnp.right_shift(col, 5)
    eye = jnp.where(row == col, 1.0, 0.0)
    n0 = jnp.where(same16, a_ab, 0.0)
    p = eye + n0
    npow = n0
    for _ in range(3):
        npow = _bmm(npow, npow)
        p = p + _bmm(p, npow)
    n1 = jnp.where(jnp.logical_and(same32, jnp.logical_not(same16)), a_ab, 0.0)
    p = p + _bmm(p, _bmm(n1, p))
    n2 = jnp.where(same32, 0.0, a_ab)
    p = p + _bmm(p, _bmm(n2, p))
    w_out[0] = _bmm(p, at).astype(w_out.dtype)
    xt = _bmm_nt(vt, a_ak)
    upt_out[0] = _bmm_nt(xt, p)
    rt_out[0] = rt.astype(rt_out.dtype)
    arb_out[0] = a_rb.astype(arb_out.dtype)
    bh_out[0] = (bvec * g_rest).astype(bh_out.dtype)
    yp_out[0] = _bmm(a_rk, v)
    sp_out[0] = _bmm(vt, kd * g_rest)
    gc_out[0] = jnp.broadcast_to(jnp.exp(cum_end), (h, n, n))


def _wkv_scan_body(w_ref, upt_ref, rt_ref, arb_ref, bh_ref, yp_ref, sp_ref, gc_ref, y_ref, s_ref):
    @pl.when(pl.program_id(1) == 0)
    def _():
        s_ref[...] = jnp.zeros_like(s_ref)

    s0 = s_ref[...]
    sb = s0.astype(BF16)
    ut = _bmm_nt(sb, w_ref[0]) + upt_ref[0]
    y_ref[0] = _bmm_nt(rt_ref[0], sb) + _bmm_nt(arb_ref[0], ut) + yp_ref[0]
    s_ref[...] = s0 * gc_ref[0] + _bmm(ut, bh_ref[0]) + sp_ref[0]


def _wkv(r, lw, kk, ag, kd, v):
    g, h, l, n = r.shape
    c = WKV_CHUNK
    nc = l // c
    vt = jnp.swapaxes(v.reshape(g, h, nc, c, n), -1, -2)
    row_spec = pl.BlockSpec((1, h, c, n), lambda i, j: (i, 0, j, 0))
    vt_spec = pl.BlockSpec((1, h, 1, n, c), lambda i, j: (i, 0, j, 0, 0))
    sq_spec = pl.BlockSpec((1, h, c, c), lambda i, j: (i, 0, j, 0))
    outs = pl.pallas_call(
        _wkv_prep_body,
        grid=(g, nc),
        in_specs=[row_spec] * 6 + [vt_spec],
        out_specs=[row_spec, row_spec, row_spec, sq_spec, row_spec, row_spec, row_spec, row_spec],
        out_shape=[jax.ShapeDtypeStruct((g, h, l, n), BF16),
                   jax.ShapeDtypeStruct((g, h, nc * n, c), F32),
                   jax.ShapeDtypeStruct((g, h, l, n), BF16),
                   jax.ShapeDtypeStruct((g, h, l, c), BF16),
                   jax.ShapeDtypeStruct((g, h, l, n), BF16),
                   jax.ShapeDtypeStruct((g, h, l, n), F32),
                   jax.ShapeDtypeStruct((g, h, nc * n, n), F32),
                   jax.ShapeDtypeStruct((g, h, nc * n, n), F32)],
        compiler_params=_cparams("parallel", "parallel"),
        name="wkv_prep",
    )(r, lw, kk, ag, kd, v, vt)
    return pl.pallas_call(
        _wkv_scan_body,
        grid=(g, nc),
        in_specs=[row_spec, row_spec, row_spec, sq_spec, row_spec, row_spec, row_spec, row_spec],
        out_specs=row_spec,
        out_shape=jax.ShapeDtypeStruct((g, h, l, n), F32),
        scratch_shapes=[pltpu.VMEM((h, n, n), F32)],
        compiler_params=_cparams("parallel", "arbitrary"),
        name="wkv_scan",
    )(*outs)


def _centred_shift(x):
    xp = jnp.pad(x, ((0, 0), (1, 1), (0, 0)))
    return 0.5 * (xp[:, :-2] + xp[:, 2:]) - x


def _heads_dirs(t_fwd, t_rev, l_ctx):
    def flip(t):
        return jnp.concatenate([t[:, :l_ctx][:, ::-1], t[:, l_ctx:][:, ::-1]], axis=1)
    t = jnp.concatenate([t_fwd, flip(t_rev)], axis=0)
    b2, l, _ = t.shape
    return jnp.swapaxes(t.reshape(b2, l, RWKV_HEADS, RWKV_HEAD), 1, 2)


def _rwkv_mixer(h_lat, h_ctx, p, v_first, vres, need_ctx):
    b, l_ctx, d = h_ctx.shape
    l_lat = h_lat.shape[1]
    l = l_ctx + l_lat
    hh = jnp.concatenate([h_ctx, h_lat], axis=1)
    xx = jnp.concatenate([_centred_shift(h_ctx), _centred_shift(h_lat)], axis=1)
    mu = p['mu']
    mats = [(p['w_rkv'][0], 0), (p['w_rkv'][1], 2), (p['w_rkv'][2], 3),
            (p['w1'][0], 1), (p['w1'][1], 1), (p['a1'][0], 4), (p['a1'][1], 4), (p['g1'], 5)]
    if vres is not None:
        mats.append((vres[1], 3))
    widths = [w.shape[1] for w, _ in mats]
    total = sum(widths)
    pad = (-total) % 128
    top = jnp.concatenate([w for w, _ in mats] + [jnp.zeros((d, pad), F32)], axis=1)
    bot = jnp.concatenate([mu[m][:, None] * w for w, m in mats] + [jnp.zeros((d, pad), F32)], axis=1)
    wcat = jnp.concatenate([top, bot], axis=0)
    a_in = jnp.concatenate([hh, xx], axis=-1).reshape(b * l, 2 * d).astype(BF16)
    proj = _linear(a_in, wcat, tn=_tile(total + pad, (1792, 1024, 512, 256, 128)))
    offs = np.cumsum([0] + widths)
    parts = [proj[:, offs[i]:offs[i + 1]] for i in range(len(widths))]
    r, k, v, w1_0, w1_1, a1_0, a1_1, g1 = parts[:8]
    if vres is not None:
        vgate = jax.nn.sigmoid(vres[0] + _linear(parts[8], vres[2]))
        v = v + (v_first.reshape(b * l, d) - v) * vgate
    gate = _linear(g1, p['g2'], in_act='sigmoid')
    kk = (k * p['k_k']).reshape(b * l, RWKV_HEADS, RWKV_HEAD)
    kk = (kk / jnp.maximum(jnp.sqrt(jnp.sum(kk * kk, axis=-1, keepdims=True)), 1e-12)).reshape(b * l, d)
    lw_d, a_d, kd_d = [], [], []
    for dd, (w1o, a1o) in enumerate(((w1_0, a1_0), (w1_1, a1_1))):
        w_log = -jax.nn.softplus(-(p['w0'][dd] + _linear(w1o, p['w2'][dd], in_act='tanh'))) - 0.5
        lw_d.append(-jnp.exp(w_log))
        ag = jax.nn.sigmoid(p['a0'][dd] + _linear(a1o, p['a2'][dd]))
        a_d.append(ag)
        kd_d.append(k * (1.0 + (ag - 1.0) * p['k_a']))
    sh3 = lambda t: t.reshape(b, l, d)
    hd = lambda t0, t1: _heads_dirs(sh3(t0), sh3(t1), l_ctx)
    y = _wkv(hd(r, r), hd(lw_d[0], lw_d[1]), hd(kk, kk), hd(a_d[0], a_d[1]), hd(kd_d[0], kd_d[1]), hd(v, v))
    y = jnp.swapaxes(y, 1, 2).reshape(2 * b, l, d)
    y_rev = y[b:]
    y_rev = jnp.concatenate([y_rev[:, :l_ctx][:, ::-1], y_rev[:, l_ctx:][:, ::-1]], axis=1)
    y = (y[:b] + y_rev).reshape(b * l, RWKV_HEADS, RWKV_HEAD)
    mean = jnp.mean(y, axis=-1, keepdims=True)
    var = jnp.mean(jnp.square(y - mean), axis=-1, keepdims=True)
    yn = ((y - mean) * lax.rsqrt(var + GN_EPS)).reshape(b * l, d) * p['ln_g'] + p['ln_b']
    r_h = r.reshape(b * l, RWKV_HEADS, RWKV_HEAD)
    bonus = sum(jnp.sum(r_h * kd.reshape(r_h.shape) * p['r_k'], axis=-1, keepdims=True) for kd in kd_d)
    bonus = (bonus * v.reshape(r_h.shape)).reshape(b * l, d)
    out = sh3(_linear(((yn + bonus) * gate).astype(BF16), p['w_o']))
    out_ctx = out[:, :l_ctx] if need_ctx else None
    return out_ctx, out[:, l_ctx:], sh3(v)


def _filter_body(z_ref, t_ref, w1_ref, b1_ref, w2_ref, b2_ref, w3_ref, b3_ref, fr_ref, wo_ref, dl_ref, o_ref):
    hp = lax.Precision.HIGHEST
    fr = fr_ref[...]
    hdn = jnp.sin(fr[0:1] * (jnp.dot(z_ref[...], w1_ref[...], precision=hp, preferred_element_type=F32) + b1_ref[...]))
    hdn = jnp.sin(fr[1:2] * (jnp.dot(hdn, w2_ref[...], precision=hp, preferred_element_type=F32) + b2_ref[...]))
    hdn = jnp.sin(fr[2:3] * (jnp.dot(hdn, w3_ref[...], precision=hp, preferred_element_type=F32) + b3_ref[...]))
    filt = jnp.dot(hdn, wo_ref[...], precision=hp, preferred_element_type=F32)
    o_ref[...] = filt * jnp.exp(-t_ref[...] * dl_ref[...])


def _hyena_filters(length, p):
    t = jnp.linspace(0.0, 1.0, length, dtype=F32)[:, None]
    bands = (FILTER_EMB - 1) // 2
    f = jnp.linspace(1e-4, bands - 1, bands, dtype=F32)[None, :]
    ang = 2.0 * math.pi * f * jnp.arange(length, dtype=F32)[:, None] / length
    z = jnp.concatenate([t, jnp.cos(ang), -jnp.sin(ang), jnp.zeros((length, FILTER_EMB_PAD - FILTER_EMB), F32)], axis=-1)
    w1 = jnp.concatenate([p['f_w1'], jnp.zeros((FILTER_EMB_PAD - FILTER_EMB, p['f_w1'].shape[1]), F32)], axis=0)
    deltas = jnp.abs(jnp.linspace(FAST_DECAY, SLOW_DECAY, D_MODEL, dtype=F32))
    deltas2 = jnp.concatenate([deltas, deltas])[None, :]
    fw = p['f_w2'].shape[0]
    tl = _tile(length, (512, 256, 128, 64, 32, 16, 8))
    full = lambda a: pl.BlockSpec(a.shape, lambda i: (0,) * a.ndim)
    args = [w1, p['f_b1'].reshape(1, fw), p['f_w2'], p['f_b2'].reshape(1, fw), p['f_w3'], p['f_b3'].reshape(1, fw),
            p['f_freq'], p['f_wout'], deltas2]
    return pl.pallas_call(
        _filter_body,
        grid=(length // tl,),
        in_specs=[pl.BlockSpec((tl, FILTER_EMB_PAD), lambda i: (i, 0)), pl.BlockSpec((tl, 1), lambda i: (i, 0))]
                 + [full(a) for a in args],
        out_specs=pl.BlockSpec((tl, 2 * D_MODEL), lambda i: (i, 0)),
        out_shape=jax.ShapeDtypeStruct((length, 2 * D_MODEL), F32),
        compiler_params=_cparams("parallel"),
        name="hyena_filter",
    )(z, t, *args)


def _colmm_body(f_ref, x_ref, o_ref):
    o_ref[0] = jnp.dot(f_ref[...], x_ref[0].astype(BF16), preferred_element_type=F32)


def _colmm(f, x):
    bsz, q, n = x.shape
    pp = f.shape[0]
    tn = _tile(n, (4096, 2048, 1024, 512, 256, 128))
    return pl.pallas_call(
        _colmm_body,
        grid=(bsz, n // tn),
        in_specs=[pl.BlockSpec((pp, q), lambda i, j: (0, 0)),
                  pl.BlockSpec((1, q, tn), lambda i, j: (i, 0, j))],
        out_specs=pl.BlockSpec((1, pp, tn), lambda i, j: (i, 0, j)),
        out_shape=jax.ShapeDtypeStruct((bsz, pp, n), F32),
        compiler_params=_cparams("parallel", "parallel"),
        name="dft_outer",
    )(f.astype(BF16), x)


def _dft_inner_fwd_body(a_ref, m_ref, o_ref):
    n2 = a_ref.shape[3]
    a = jnp.concatenate([a_ref[0, 0, 0], a_ref[0, 1, 0]], axis=0).astype(BF16)
    x = jnp.dot(m_ref[0], a, preferred_element_type=F32)
    o_ref[0, 0, 0] = x[:n2]
    o_ref[0, 1, 0] = x[n2:]


def _dft_inner_conv_body(a_ref, g_ref, m_ref, mi_ref, o_ref):
    n2 = a_ref.shape[3]
    a = jnp.concatenate([a_ref[0, 0, 0], a_ref[0, 1, 0]], axis=0).astype(BF16)
    x = jnp.dot(m_ref[0], a, preferred_element_type=F32)
    xre, xim = x[:n2], x[n2:]
    gre, gim = g_ref[0, 0, 0], g_ref[0, 1, 0]
    y = jnp.concatenate([xre * gre - xim * gim, xre * gim + xim * gre], axis=0).astype(BF16)
    o = jnp.dot(mi_ref[0], y, preferred_element_type=F32)
    o_ref[0, 0, 0] = o[:n2]
    o_ref[0, 1, 0] = o[n2:]


def _dft_tables(n1, n2):
    n = n1 * n2
    i1 = jnp.arange(n1, dtype=jnp.int32)
    ph1 = (2.0 * math.pi / n1) * ((i1[:, None] * i1[None, :]) % n1).astype(F32)
    c1, s1 = jnp.cos(ph1), jnp.sin(ph1)
    k1 = i1[:, None, None]
    k2 = jnp.arange(n2, dtype=jnp.int32)[None, :, None]
    m2 = jnp.arange(n2, dtype=jnp.int32)[None, None, :]
    th = (2.0 * math.pi / n) * ((m2 * (k1 + n1 * k2)) % n).astype(F32)
    mre, mim = jnp.cos(th), -jnp.sin(th)
    m_fwd = jnp.concatenate([jnp.concatenate([mre, -mim], axis=2), jnp.concatenate([mim, mre], axis=2)], axis=1)
    qre, qim = jnp.swapaxes(mre, 1, 2), -jnp.swapaxes(mim, 1, 2)
    m_inv = jnp.concatenate([jnp.concatenate([qre, -qim], axis=2), jnp.concatenate([qim, qre], axis=2)], axis=1)
    f_fwd = jnp.concatenate([c1, -s1], axis=0)
    f_inv = jnp.concatenate([c1, -s1], axis=1)
    return f_fwd, f_inv, m_fwd.astype(BF16), m_inv.astype(BF16)


def _two_sided_long_conv(z, h_fwd, h_bwd):
    b, l, d = z.shape
    n = 2 * l
    n1 = max(DFT_MIN_OUTER, 2 ** ((int(math.log2(n)) - 1) // 2))
    n2 = n // n1
    f_fwd, f_inv, m_fwd, m_inv = _dft_tables(n1, n2)
    filt = jnp.concatenate([h_fwd, jnp.zeros_like(h_fwd[:1]), h_bwd[1:][::-1]], axis=0) * (1.0 / n)
    blk = pl.BlockSpec((1, 2, 1, n2, d), lambda i, j: (i, 0, j, 0, 0))
    gblk = pl.BlockSpec((1, 2, 1, n2, d), lambda i, j: (0, 0, j, 0, 0))
    mat = pl.BlockSpec((1, 2 * n2, 2 * n2), lambda i, j: (j, 0, 0))
    ga = _colmm(f_fwd, filt.reshape(1, n1, n2 * d)).reshape(1, 2, n1, n2, d)
    gspec = pl.pallas_call(
        _dft_inner_fwd_body,
        grid=(1, n1),
        in_specs=[blk, mat],
        out_specs=blk,
        out_shape=jax.ShapeDtypeStruct((1, 2, n1, n2, d), F32),
        compiler_params=_cparams("parallel", "parallel"),
        name="dft_inner_fwd",
    )(ga, m_fwd)
    za = _colmm(f_fwd[:, :n1 // 2], z.reshape(b, n1 // 2, n2 * d)).reshape(b, 2, n1, n2, d)
    bb = pl.pallas_call(
        _dft_inner_conv_body,
        grid=(b, n1),
        in_specs=[blk, gblk, mat, mat],
        out_specs=blk,
        out_shape=jax.ShapeDtypeStruct((b, 2, n1, n2, d), F32),
        compiler_params=_cparams("parallel", "parallel"),
        name="dft_inner_conv",
    )(za, gspec, m_fwd, m_inv)
    return _colmm(f_inv[:n1 // 2], bb.reshape(b, 2 * n1, n2 * d)).reshape(b, l, d)


def _hyena_stream(h, p):
    b, l, d = h.shape
    u = _linear(h.reshape(b * l, d).astype(BF16), p['w_in'], p['b_in'], tn=1024).reshape(b, l, 3 * d)
    up = jnp.pad(u, ((0, 0), (1, 1), (0, 0)))
    sw = p['short_w']
    u = up[:, :-2] * sw[0] + up[:, 1:-1] * sw[1] + up[:, 2:] * sw[2] + p['short_b']
    x0, x1, v = jnp.split(u, 3, axis=-1)
    filt = _hyena_filters(l, p)
    z = v * x1
    z = _two_sided_long_conv(z, filt[:, :d], filt[:, d:]) + z * p['bias']
    return _linear((x0 * z).reshape(b * l, d).astype(BF16), p['w_out'], p['b_out']).reshape(b, l, d)


def _attn_body(q_ref, k_ref, v_ref, o_ref, m_ref, l_ref, acc_ref):
    j = pl.program_id(3)

    @pl.when(j == 0)
    def _():
        m_ref[...] = jnp.full_like(m_ref, -jnp.inf)
        l_ref[...] = jnp.zeros_like(l_ref)
        acc_ref[...] = jnp.zeros_like(acc_ref)

    s = lax.dot_general(q_ref[0, 0], k_ref[0, 0], (((1,), (1,)), ((), ())), preferred_element_type=F32)
    m_prev = m_ref[...]
    m_new = jnp.maximum(m_prev, jnp.max(s, axis=-1, keepdims=True))
    alpha = jnp.exp(m_prev - m_new)
    pr = jnp.exp(s - m_new)
    l_ref[...] = alpha * l_ref[...] + jnp.sum(pr, axis=-1, keepdims=True)
    acc_ref[...] = alpha * acc_ref[...] + jnp.dot(pr.astype(BF16), v_ref[0], preferred_element_type=F32)
    m_ref[...] = m_new

    @pl.when(j == pl.num_programs(3) - 1)
    def _():
        o_ref[0] = acc_ref[...] / l_ref[...]


def _attend(q, k, v):
    b, nh, sq, dq = q.shape
    sk = k.shape[2]
    dv = v.shape[2] // nh
    tq = _tile(sq, (512, 256, 128, 64, 32, 16, 8))
    tk = _tile(sk, (1280, 1024, 640, 512, 256, 128))
    return pl.pallas_call(
        _attn_body,
        grid=(b, nh, sq // tq, sk // tk),
        in_specs=[pl.BlockSpec((1, 1, tq, dq), lambda bi, hi, i, j: (bi, hi, i, 0)),
                  pl.BlockSpec((1, 1, tk, dq), lambda bi, hi, i, j: (bi, hi, j, 0)),
                  pl.BlockSpec((1, tk, dv), lambda bi, hi, i, j: (bi, j, hi))],
        out_specs=pl.BlockSpec((1, tq, dv), lambda bi, hi, i, j: (bi, i, hi)),
        out_shape=jax.ShapeDtypeStruct((b, sq, nh * dv), F32),
        scratch_shapes=[pltpu.VMEM((tq, 1), F32), pltpu.VMEM((tq, 1), F32), pltpu.VMEM((tq, dv), F32)],
        compiler_params=_cparams("parallel", "parallel", "parallel", "arbitrary"),
        name="attention",
    )(q.astype(BF16), k.astype(BF16), v.astype(BF16))


def _rms(x, g):
    return x * lax.rsqrt(jnp.mean(x * x, axis=-1, keepdims=True) + NORM_EPS) * g


def _rope_1d(x, pos):
    d = x.shape[-1]
    inv = ROPE_THETA ** (-jnp.arange(0, d, 2, dtype=F32) / d)
    ang = pos.astype(F32)[:, None] * inv[None, :]
    cos = jnp.cos(ang)[:, None, :]
    sin = jnp.sin(ang)[:, None, :]
    x1, x2 = jnp.split(x, 2, axis=-1)
    return jnp.concatenate([x1 * cos - x2 * sin, x1 * sin + x2 * cos], axis=-1)


def _axial_rope(x, row, col):
    half = x.shape[-1] // 2
    return jnp.concatenate([_rope_1d(x[..., :half], row), _rope_1d(x[..., half:], col)], axis=-1)


def _mla_mixer(h_lat, h_ctx, p, rope_pos, need_ctx):
    b, l_ctx, d = h_ctx.shape
    l_lat = h_lat.shape[1]
    l = l_ctx + l_lat
    hh = jnp.concatenate([h_ctx, h_lat], axis=1).reshape(b * l, d).astype(BF16)
    qkv = _linear(hh, p['w_a'])
    q_c, kv_c, k_pe = jnp.split(qkv, [Q_LORA, Q_LORA + KV_LORA], axis=-1)
    q = _linear(_rms(q_c, p['q_norm_g']).astype(BF16), p['w_qb']).reshape(b, l, MLA_HEADS, QK_HEAD)
    kv = _linear(_rms(kv_c, p['kv_norm_g']).astype(BF16), p['w_kvb']).reshape(b, l, MLA_HEADS, QK_NOPE + V_HEAD)
    k_nope, v = kv[..., :QK_NOPE], kv[..., QK_NOPE:]
    k = jnp.concatenate([k_nope, jnp.broadcast_to(k_pe.reshape(b, l, 1, QK_ROPE), (b, l, MLA_HEADS, QK_ROPE))], axis=-1)
    q = _rms(q, p['qn_g'])
    k = _rms(k, p['kn_g'])
    row, col = rope_pos

    def rope_lat(t):
        t_l = t[:, l_ctx:]
        t_l = jnp.concatenate([t_l[..., :QK_NOPE], _axial_rope(t_l[..., QK_NOPE:], row, col)], axis=-1)
        return jnp.concatenate([t[:, :l_ctx], t_l], axis=1)

    q = jnp.swapaxes(rope_lat(q) * (QK_HEAD ** -0.5), 1, 2)
    k = jnp.swapaxes(rope_lat(k), 1, 2)
    v = v.reshape(b, l, MLA_HEADS * V_HEAD)
    o_lat = _attend(q[:, :, l_ctx:], k, v)
    if need_ctx:
        o_ctx = _attend(q[:, :, :l_ctx], k[:, :, :l_ctx], v[:, :l_ctx])
        o = jnp.concatenate([o_ctx, o_lat], axis=1).reshape(b * l, MLA_HEADS * V_HEAD)
        out = _linear(o.astype(BF16), p['w_o']).reshape(b, l, d)
        return out[:, :l_ctx], out[:, l_ctx:]
    out = _linear(o_lat.reshape(b * l_lat, MLA_HEADS * V_HEAD).astype(BF16), p['w_o']).reshape(b, l_lat, d)
    return None, out


def _router_body(h_ref, w_ref, b_ref, idx_ref, gate_ref):
    lg = jnp.dot(h_ref[...], w_ref[...], precision=lax.Precision.HIGHEST, preferred_element_type=F32) + b_ref[...]
    lane = lax.broadcasted_iota(jnp.int32, lg.shape, 1).astype(F32)
    idx_acc = jnp.zeros(lg.shape, F32)
    val_acc = jnp.zeros(lg.shape, F32)
    top = None
    den = None
    for j in range(TOP_K):
        mx = jnp.max(lg, axis=-1, keepdims=True)
        sel = jnp.min(jnp.where(lg == mx, lane, float(ROUTER_LANES)), axis=-1, keepdims=True)
        if j == 0:
            top = mx
        e = jnp.exp(mx - top)
        den = e if den is None else den + e
        idx_acc = jnp.where(lane == float(j), sel, idx_acc)
        val_acc = jnp.where(lane == float(j), e, val_acc)
        lg = jnp.where(lane == sel, -jnp.inf, lg)
    idx_ref[...] = idx_acc.astype(jnp.int32)
    gate_ref[...] = val_acc / den


def _router(h, router_w, router_b):
    t, d = h.shape
    tm = _tile(t, (512, 256, 128, 64, 32, 16, 8))
    w = jnp.concatenate([router_w, jnp.zeros((d, ROUTER_LANES - N_EXPERTS), F32)], axis=1)
    bias = jnp.concatenate([router_b, jnp.full((ROUTER_LANES - N_EXPERTS,), -1e30, F32)]).reshape(1, ROUTER_LANES)
    return pl.pallas_call(
        _router_body,
        grid=(t // tm,),
        in_specs=[pl.BlockSpec((tm, d), lambda i: (i, 0)),
                  pl.BlockSpec((d, ROUTER_LANES), lambda i: (0, 0)),
                  pl.BlockSpec((1, ROUTER_LANES), lambda i: (0, 0))],
        out_specs=[pl.BlockSpec((tm, ROUTER_LANES), lambda i: (i, 0))] * 2,
        out_shape=[jax.ShapeDtypeStruct((t, ROUTER_LANES), jnp.int32), jax.ShapeDtypeStruct((t, ROUTER_LANES), F32)],
        compiler_params=_cparams("parallel"),
        name="router",
    )(h, w, bias)


def _expert_body(be_ref, nb_ref, x_ref, win_ref, bin_ref, wout_ref, bout_ref, gate_ref, o_ref):
    i = pl.program_id(0)

    @pl.when(i < nb_ref[0])
    def _():
        gu = jnp.dot(x_ref[...], win_ref[0], preferred_element_type=F32) + bin_ref[0]
        glu = jnp.minimum(gu[:, :D_EXPERT], SWIGLU_LIMIT)
        lin = jnp.clip(gu[:, D_EXPERT:], -SWIGLU_LIMIT, SWIGLU_LIMIT)
        act = glu * jax.nn.sigmoid(SWIGLU_ALPHA * glu) * (lin + 1.0)
        y = jnp.dot(act.astype(BF16), wout_ref[0], preferred_element_type=F32) + bout_ref[0]
        o_ref[...] = y * gate_ref[...]

    @pl.when(i >= nb_ref[0])
    def _():
        o_ref[...] = jnp.zeros_like(o_ref)


def _moe(h, router_w, router_b, w_in, b_in, w_out, b_out):
    t, d = h.shape
    n_assign = t * TOP_K
    idx, gates = _router(h, router_w, router_b)
    flat_e = idx[:, :TOP_K].reshape(n_assign)
    flat_g = gates[:, :TOP_K].reshape(n_assign)
    onehot = (flat_e[:, None] == jnp.arange(N_EXPERTS, dtype=jnp.int32)[None, :]).astype(jnp.int32)
    csum = jnp.cumsum(onehot, axis=0)
    rank = jnp.take_along_axis(csum, flat_e[:, None], axis=1)[:, 0] - 1
    counts = csum[-1]
    padded = (counts + MOE_ROWS - 1) // MOE_ROWS * MOE_ROWS
    pad_end = jnp.cumsum(padded)
    pad_start = pad_end - padded
    dest = pad_start[flat_e] + rank
    n_blocks = -(-n_assign // MOE_ROWS) + N_EXPERTS
    n_rows = n_blocks * MOE_ROWS
    tok = jnp.arange(n_assign, dtype=jnp.int32) // TOP_K
    row_tok = jnp.zeros((n_rows,), jnp.int32).at[dest].set(tok)
    row_gate = jnp.zeros((n_rows,), F32).at[dest].set(flat_g)
    block_start = jnp.arange(n_blocks, dtype=jnp.int32) * MOE_ROWS
    block_e = jnp.minimum(jnp.searchsorted(pad_end, block_start, side='right'), N_EXPERTS - 1).astype(jnp.int32)
    n_used = (pad_end[-1] // MOE_ROWS).astype(jnp.int32).reshape(1)
    xb = h.astype(BF16)[row_tok]
    f2 = 2 * D_EXPERT
    grid_spec = pltpu.PrefetchScalarGridSpec(
        num_scalar_prefetch=2,
        grid=(n_blocks,),
        in_specs=[pl.BlockSpec((MOE_ROWS, d), lambda i, be, nb: (i, 0)),
                  pl.BlockSpec((1, d, f2), lambda i, be, nb: (be[i], 0, 0)),
                  pl.BlockSpec((1, 1, f2), lambda i, be, nb: (be[i], 0, 0)),
                  pl.BlockSpec((1, D_EXPERT, d), lambda i, be, nb: (be[i], 0, 0)),
                  pl.BlockSpec((1, 1, d), lambda i, be, nb: (be[i], 0, 0)),
                  pl.BlockSpec((MOE_ROWS, 1), lambda i, be, nb: (i, 0))],
        out_specs=pl.BlockSpec((MOE_ROWS, d), lambda i, be, nb: (i, 0)),
    )
    yb = pl.pallas_call(
        _expert_body,
        grid_spec=grid_spec,
        out_shape=jax.ShapeDtypeStruct((n_rows, d), F32),
        compiler_params=_cparams("arbitrary"),
        name="moe_experts",
    )(block_e, n_used, xb, w_in.astype(BF16), b_in.reshape(N_EXPERTS, 1, f2),
      w_out.astype(BF16), b_out.reshape(N_EXPERTS, 1, d), row_gate.reshape(n_rows, 1))
    return jnp.sum(yb[dest].reshape(t, TOP_K, d), axis=1)


def kernel(x, c, ctx, c_ctx, ada_w, ada_b, norm_mix_g, norm_ffn_g, rwkv_mu, rwkv_w_rkv, rwkv_w0, rwkv_w1, rwkv_w2, rwkv_a0, rwkv_a1, rwkv_a2, rwkv_g1, rwkv_g2, rwkv_k_k, rwkv_k_a, rwkv_r_k, rwkv_ln_g, rwkv_ln_b, rwkv_w_o, rwkv_v0, rwkv_v1, rwkv_v2, hy_w_in, hy_b_in, hy_short_w, hy_short_b, hy_f_w1, hy_f_b1, hy_f_w2, hy_f_b2, hy_f_w3, hy_f_b3, hy_f_freq, hy_f_wout, hy_bias, hy_w_out, hy_b_out, mla_w_a, mla_q_norm_g, mla_kv_norm_g, mla_w_qb, mla_w_kvb, mla_qn_g, mla_kn_g, mla_w_o, moe_router_w, moe_router_b, moe_w_in, moe_b_in, moe_w_out, moe_b_out):
    bsz, s_len, d = x.shape
    l_ctx = ctx.shape[1]
    rows = s_len // GRID_W
    row = jnp.repeat(jnp.arange(rows, dtype=jnp.int32), GRID_W)
    col = jnp.tile(jnp.arange(GRID_W, dtype=jnp.int32), rows)
    silu = jnp.concatenate([jax.nn.silu(c), jax.nn.silu(c_ctx)[None, :]], axis=0)
    silu = jnp.concatenate([silu, jnp.zeros((-(bsz + 1) % 8, d), F32)], axis=0)
    xc = ctx
    v_first = None
    for i in range(DEPTH):
        need_ctx = i < DEPTH - 1
        j = i // N_MIXERS
        kind = i % N_MIXERS
        mod = _linear(silu, ada_w[i], ada_b[i], tn=1024)
        mod_l = jnp.split(mod[:bsz], N_MOD, axis=-1)
        mod_c = [jnp.broadcast_to(m, (bsz, d)) for m in jnp.split(mod[bsz:bsz + 1], N_MOD, axis=-1)]
        sh1, sc1, g1, sh2, sc2, g2 = [m[:, None, :] for m in mod_l]
        csh1, csc1, cg1, csh2, csc2, cg2 = [m[:, None, :] for m in mod_c]
        h_l = _norm_mod(x, norm_mix_g[i], sc1, sh1)
        h_c = _norm_mod(xc, norm_mix_g[i], csc1, csh1)
        if kind == 0:
            p = {'mu': rwkv_mu[j], 'w_rkv': rwkv_w_rkv[j], 'w0': rwkv_w0[j], 'w1': rwkv_w1[j],
                 'w2': rwkv_w2[j], 'a0': rwkv_a0[j], 'a1': rwkv_a1[j], 'a2': rwkv_a2[j],
                 'g1': rwkv_g1[j], 'g2': rwkv_g2[j], 'k_k': rwkv_k_k[j], 'k_a': rwkv_k_a[j],
                 'r_k': rwkv_r_k[j], 'ln_g': rwkv_ln_g[j], 'ln_b': rwkv_ln_b[j], 'w_o': rwkv_w_o[j]}
            vres = None if j == 0 else (rwkv_v0[j - 1], rwkv_v1[j - 1], rwkv_v2[j - 1])
            y_c, y_l, v_cur = _rwkv_mixer(h_l, h_c, p, v_first, vres, need_ctx)
            if j == 0:
                v_first = v_cur
        elif kind == 1:
            p = {'w_in': hy_w_in[j], 'b_in': hy_b_in[j], 'short_w': hy_short_w[j], 'short_b': hy_short_b[j],
                 'f_w1': hy_f_w1[j], 'f_b1': hy_f_b1[j], 'f_w2': hy_f_w2[j], 'f_b2': hy_f_b2[j],
                 'f_w3': hy_f_w3[j], 'f_b3': hy_f_b3[j], 'f_freq': hy_f_freq[j], 'f_wout': hy_f_wout[j],
                 'bias': hy_bias[j], 'w_out': hy_w_out[j], 'b_out': hy_b_out[j]}
            y_l = _hyena_stream(h_l, p)
            y_c = _hyena_stream(h_c, p) if need_ctx else None
        else:
            p = {'w_a': mla_w_a[j], 'q_norm_g': mla_q_norm_g[j], 'kv_norm_g': mla_kv_norm_g[j],
                 'w_qb': mla_w_qb[j], 'w_kvb': mla_w_kvb[j], 'qn_g': mla_qn_g[j], 'kn_g': mla_kn_g[j],
                 'w_o': mla_w_o[j]}
            y_c, y_l = _mla_mixer(h_l, h_c, p, (row, col), need_ctx)
        x = x + g1 * y_l
        h2_l = _norm_mod(x, norm_ffn_g[i], sc2, sh2)
        moe_p = (moe_router_w[i], moe_router_b[i], moe_w_in[i], moe_b_in[i], moe_w_out[i], moe_b_out[i])
        if need_ctx:
            xc = xc + cg1 * y_c
            h2_c = _norm_mod(xc, norm_ffn_g[i], csc2, csh2)
            n_c = bsz * l_ctx
            f = _moe(jnp.concatenate([h2_c.reshape(-1, d), h2_l.reshape(-1, d)], axis=0), *moe_p)
            xc = xc + cg2 * f[:n_c].reshape(xc.shape)
            x = x + g2 * f[n_c:].reshape(x.shape)
        else:
            x = x + g2 * _moe(h2_l.reshape(-1, d), *moe_p).reshape(x.shape)
    return x
```

```python
import functools
import math

import jax
import jax.numpy as jnp
import numpy as np
from jax import lax
from jax.experimental import pallas as pl
from jax.experimental.pallas import tpu as pltpu

F32 = jnp.float32
BF16 = jnp.bfloat16

D_MODEL = 1024
DEPTH = 4
GRID_W = 64
N_MIXERS = 3
N_MOD = 6
NORM_EPS = 1e-6
RWKV_HEAD = 64
RWKV_HEADS = D_MODEL // RWKV_HEAD
GN_EPS = 64e-5
WKV_CHUNK = 64
FILTER_EMB = 33
FILTER_EMB_PAD = 128
DFT_MIN_OUTER = 32
FAST_DECAY = math.log(1e-2) / 0.3
SLOW_DECAY = math.log(1e-2) / 1.5
MLA_HEADS = 8
QK_NOPE = 128
QK_ROPE = 64
QK_HEAD = QK_NOPE + QK_ROPE
V_HEAD = 128
Q_LORA = 384
KV_LORA = 256
ROPE_THETA = 10000.0
N_EXPERTS = 32
TOP_K = 4
D_EXPERT = 1024
SWIGLU_LIMIT = 7.0
SWIGLU_ALPHA = 1.702
MOE_ROWS = 256
ROUTER_LANES = 128

V7X_VMEM_BYTES = 64 * 1024 * 1024
VMEM_LIMIT = V7X_VMEM_BYTES * 3 // 4


def _cparams(*sem):
    return pltpu.CompilerParams(dimension_semantics=sem, vmem_limit_bytes=VMEM_LIMIT)


def _tile(n, prefs):
    for t in prefs:
        if n % t == 0:
            return t
    return n


def _linear_body(a_ref, w_ref, b_ref, o_ref, *, in_act):
    a = a_ref[...]
    if in_act == 'tanh':
        a = jnp.tanh(a.astype(F32))
    elif in_act == 'sigmoid':
        a = jax.nn.sigmoid(a.astype(F32))
    acc = jnp.dot(a.astype(BF16), w_ref[...], preferred_element_type=F32)
    o_ref[...] = (acc + b_ref[...]).astype(o_ref.dtype)


def _linear(a, w, b=None, *, in_act=None, out_dtype=F32, tn=None):
    m, k = a.shape
    n = w.shape[1]
    tm = _tile(m, (512, 256, 128, 64, 32, 16, 8))
    tn = n if tn is None else tn
    assert n % tn == 0
    if b is None:
        b = jnp.zeros((n,), F32)
    return pl.pallas_call(
        functools.partial(_linear_body, in_act=in_act),
        grid=(n // tn, m // tm),
        in_specs=[pl.BlockSpec((tm, k), lambda j, i: (i, 0)),
                  pl.BlockSpec((k, tn), lambda j, i: (0, j)),
                  pl.BlockSpec((1, tn), lambda j, i: (0, j))],
        out_specs=pl.BlockSpec((tm, tn), lambda j, i: (i, j)),
        out_shape=jax.ShapeDtypeStruct((m, n), out_dtype),
        compiler_params=_cparams("parallel", "parallel"),
        name="linear",
    )(a, w.astype(BF16), b.reshape(1, n).astype(F32))


def _norm_mod_body(x_ref, g_ref, sc_ref, sh_ref, o_ref):
    x = x_ref[0]
    y = x * lax.rsqrt(jnp.mean(x * x, axis=-1, keepdims=True) + NORM_EPS)
    o_ref[0] = ((y * g_ref[...]) * (1.0 + sc_ref[0]) + sh_ref[0]).astype(o_ref.dtype)


def _norm_mod(x, g, sc, sh):
    b, l, d = x.shape
    tl = _tile(l, (512, 256, 128, 64, 32, 16, 8))
    return pl.pallas_call(
        _norm_mod_body,
        grid=(b, l // tl),
        in_specs=[pl.BlockSpec((1, tl, d), lambda i, j: (i, j, 0)),
                  pl.BlockSpec((1, d), lambda i, j: (0, 0)),
                  pl.BlockSpec((1, 1, d), lambda i, j: (i, 0, 0)),
                  pl.BlockSpec((1, 1, d), lambda i, j: (i, 0, 0))],
        out_specs=pl.BlockSpec((1, tl, d), lambda i, j: (i, j, 0)),
        out_shape=jax.ShapeDtypeStruct((b, l, d), F32),
        compiler_params=_cparams("parallel", "parallel"),
        name="norm_mod",
    )(x, g.reshape(1, d), sc.reshape(b, 1, d), sh.reshape(b, 1, d))


PROJ_R, PROJ_K, PROJ_V = 0, 1024, 2048
PROJ_W1 = (3072, 3200)
PROJ_A1 = (3328, 3456)
PROJ_G1 = 3584
PROJ_V1 = 3840
PROJ_WIDTH = 4096
LORA_PAD = 128
GATE_PAD = 256


def _bmm(a, b):
    return jnp.einsum('hmk,hkn->hmn', a.astype(BF16), b.astype(BF16), preferred_element_type=F32)


def _bmm_nt(a, b):
    return jnp.einsum('hmk,hnk->hmn', a.astype(BF16), b.astype(BF16), preferred_element_type=F32)


def _sigmoid(x):
    return jax.nn.sigmoid(x)


def _rwkv_post_body(*refs, has_vres):
    if has_vres:
        (proj_ref, vf_ref, w2_ref, a2_ref, g2_ref, v2_ref, vec_ref, gs_ref,
         kk_out, v_out, gate_out, lw_out, ag_out, kd_out) = refs
    else:
        (proj_ref, w2_ref, a2_ref, g2_ref, vec_ref, gs_ref,
         kk_out, v_out, gate_out, lw_out, ag_out, kd_out) = refs
    d = D_MODEL
    vec = vec_ref[...]
    k = proj_ref[:, PROJ_K:PROJ_K + d].astype(F32)
    v = proj_ref[:, PROJ_V:PROJ_V + d].astype(F32)
    if has_vres:
        v1 = proj_ref[:, PROJ_V1:PROJ_V1 + LORA_PAD]
        vgate = _sigmoid(vec[6:7] + jnp.dot(v1, v2_ref[...], preferred_element_type=F32))
        v = v + (vf_ref[...].astype(F32) - v) * vgate
    v_out[...] = v.astype(v_out.dtype)
    kraw = k * vec[4:5]
    ss = jnp.dot((kraw * kraw).astype(BF16), gs_ref[...], preferred_element_type=F32)
    kk_out[...] = (kraw / jnp.maximum(jnp.sqrt(ss), 1e-12)).astype(kk_out.dtype)
    g1 = proj_ref[:, PROJ_G1:PROJ_G1 + GATE_PAD].astype(F32)
    gate_out[...] = jnp.dot(_sigmoid(g1).astype(BF16), g2_ref[...], preferred_element_type=F32).astype(gate_out.dtype)
    for dd in range(2):
        w1 = proj_ref[:, PROJ_W1[dd]:PROJ_W1[dd] + LORA_PAD].astype(F32)
        z = vec[dd:dd + 1] + jnp.dot(jnp.tanh(w1).astype(BF16), w2_ref[dd], preferred_element_type=F32)
        softplus = jnp.maximum(-z, 0.0) + jnp.log(1.0 + jnp.exp(-jnp.abs(z)))
        lw_out[dd] = -jnp.exp(-softplus - 0.5)
        a1 = proj_ref[:, PROJ_A1[dd]:PROJ_A1[dd] + LORA_PAD]
        ag = _sigmoid(vec[2 + dd:3 + dd] + jnp.dot(a1, a2_ref[dd], preferred_element_type=F32))
        ag_out[dd] = ag.astype(ag_out.dtype)
        kd_out[dd] = (k * (1.0 + (ag - 1.0) * vec[5:6])).astype(kd_out.dtype)


def _rwkv_post(proj, v_first, w2p, a2p, g2p, v2p, vec, gsum):
    t, d = proj.shape[0], D_MODEL
    tm = _tile(t, (256, 128, 64, 32, 16))
    has_vres = v_first is not None
    full = lambda a: pl.BlockSpec(a.shape, lambda i: (0,) * a.ndim)
    row = pl.BlockSpec((tm, d), lambda i: (i, 0))
    row2 = pl.BlockSpec((2, tm, d), lambda i: (0, i, 0))
    ins = [proj] + ([v_first] if has_vres else []) + [w2p, a2p, g2p] + ([v2p] if has_vres else []) + [vec, gsum]
    in_specs = ([pl.BlockSpec((tm, PROJ_WIDTH), lambda i: (i, 0))] + ([row] if has_vres else [])
                + [full(a) for a in ins[(2 if has_vres else 1):]])
    return pl.pallas_call(
        functools.partial(_rwkv_post_body, has_vres=has_vres),
        grid=(t // tm,),
        in_specs=in_specs,
        out_specs=[row, row, row, row2, row2, row2],
        out_shape=[jax.ShapeDtypeStruct((t, d), BF16),
                   jax.ShapeDtypeStruct((t, d), BF16),
                   jax.ShapeDtypeStruct((t, d), BF16),
                   jax.ShapeDtypeStruct((2, t, d), F32),
                   jax.ShapeDtypeStruct((2, t, d), BF16),
                   jax.ShapeDtypeStruct((2, t, d), BF16)],
        compiler_params=_cparams("parallel"),
        name="rwkv_post",
    )(*ins)


def _wkv_body(r_ref, kk_ref, v_ref, lw_ref, ag_ref, kd_ref, y_ref, s_ref, *, n_batch):
    rev = pl.program_id(0) >= n_batch

    @pl.when(pl.program_id(1) == 0)
    def _():
        s_ref[...] = jnp.zeros_like(s_ref)

    nh, n = RWKV_HEADS, RWKV_HEAD

    def heads(x):
        x = x.astype(F32)
        return jnp.stack([x[:, i * n:(i + 1) * n] for i in range(nh)], axis=0)

    r = heads(r_ref[0])
    kk = heads(kk_ref[0])
    v = heads(v_ref[0])
    lw = heads(lw_ref[0, 0])
    ag = heads(ag_ref[0, 0])
    kd = heads(kd_ref[0, 0])
    c = r.shape[1]
    row0 = lax.broadcasted_iota(jnp.int32, (c, c), 0)
    col0 = lax.broadcasted_iota(jnp.int32, (c, c), 1)
    row = jnp.where(rev, col0, row0)
    col = jnp.where(rev, row0, col0)
    incl = row >= col
    strict = row > col
    tri = jnp.broadcast_to(jnp.where(incl, 1.0, 0.0).astype(BF16), (nh, c, c))
    hi = lw.astype(BF16)
    rem = lw - hi.astype(F32)
    mid = rem.astype(BF16)
    lo = (rem - mid.astype(F32)).astype(BF16)
    cum = _bmm(tri, hi) + _bmm(tri, mid) + _bmm(tri, lo)
    cum_end = jnp.where(rev, cum[:, 0:1, :], cum[:, c - 1:c, :])
    g_rest = jnp.exp(cum_end - cum)
    g_inv = jnp.exp(-cum)
    bvec = kk * ag
    rt = r * jnp.exp(cum)
    kt = kd * g_inv
    bt = bvec * g_inv
    at = -kk * jnp.exp(cum - lw)
    ar = jnp.concatenate([at, rt], axis=1)
    ab = _bmm_nt(ar, bt)
    ak = _bmm_nt(ar, kt)
    a_ab = jnp.where(strict, ab[:, :c], 0.0)
    a_rb = jnp.where(incl, ab[:, c:], 0.0)
    a_ak = jnp.where(strict, ak[:, :c], 0.0)
    a_rk = jnp.where(incl, ak[:, c:], 0.0)
    same16 = jnp.right_shift(row0, 4) == jnp.right_shift(col0, 4)
    same32 = jnp.right_shift(row0, 5) == jnp.right_shift(col0, 5)
    eye = jnp.where(row0 == col0, 1.0, 0.0)
    n0 = jnp.where(same16, a_ab, 0.0)
    p = eye + n0
    npow = n0
    for _ in range(3):
        npow = _bmm(npow, npow)
        p = p + _bmm(p, npow)
    n1 = jnp.where(jnp.logical_and(same32, jnp.logical_not(same16)), a_ab, 0.0)
    p = p + _bmm(p, _bmm(n1, p))
    n2 = jnp.where(same32, 0.0, a_ab)
    p = p + _bmm(p, _bmm(n2, p))
    wmat = _bmm(p, at)
    vt = _bmm_nt(jnp.broadcast_to(eye, (nh, n, n)), v)
    upt = _bmm_nt(_bmm_nt(vt, a_ak), p)
    yp = _bmm(a_rk, v)
    sp = _bmm(vt, kd * g_rest)
    s0 = s_ref[...]
    sb = s0.astype(BF16)
    ut = _bmm_nt(sb, wmat) + upt
    y = _bmm_nt(rt, sb) + _bmm_nt(a_rb, ut) + yp
    s_ref[...] = s0 * jnp.exp(cum_end) + _bmm(ut, bvec * g_rest) + sp
    y_ref[0, 0] = jnp.concatenate([y[i] for i in range(nh)], axis=-1)


def _wkv(proj, kk, v, lw, ag, kd, n_batch, l_ctx):
    b, l, d = kk.shape
    c = WKV_CHUNK
    nc, nctx = l // c, l_ctx // c
    assert c == RWKV_HEAD and l % c == 0 and l_ctx % c == 0

    def chunk(g, j):
        return jnp.where(g >= n_batch, jnp.where(j < nctx, nctx - 1 - j, nc + nctx - 1 - j), j)

    shared = pl.BlockSpec((1, c, d), lambda g, j: (g % n_batch, chunk(g, j), 0))
    per_dir = pl.BlockSpec((1, 1, c, d), lambda g, j: (g // n_batch, g % n_batch, chunk(g, j), 0))
    return pl.pallas_call(
        functools.partial(_wkv_body, n_batch=n_batch),
        grid=(2 * n_batch, nc),
        in_specs=[shared, shared, shared, per_dir, per_dir, per_dir],
        out_specs=per_dir,
        out_shape=jax.ShapeDtypeStruct((2, b, l, d), F32),
        scratch_shapes=[pltpu.VMEM((RWKV_HEADS, RWKV_HEAD, RWKV_HEAD), F32)],
        compiler_params=_cparams("parallel", "arbitrary"),
        name="wkv",
    )(proj, kk, v, lw, ag, kd)


def _rwkv_readout_body(y_ref, r_ref, kd_ref, v_ref, gate_ref, gm_ref, gs_ref, vec_ref, wo_ref, o_ref):
    vec = vec_ref[...]
    y = y_ref[0] + y_ref[1]
    gm = gm_ref[...]
    yh = y.astype(BF16)
    yl = (y - yh.astype(F32)).astype(BF16)
    mean = jnp.dot(yh, gm, preferred_element_type=F32) + jnp.dot(yl, gm, preferred_element_type=F32)
    yc = y - mean
    var = jnp.dot((yc * yc).astype(BF16), gm, preferred_element_type=F32)
    yn = yc * lax.rsqrt(var + GN_EPS) * vec[0:1] + vec[1:2]
    r = r_ref[...].astype(F32)
    kds = kd_ref[0].astype(F32) + kd_ref[1].astype(F32)
    bonus = jnp.dot((r * kds * vec[2:3]).astype(BF16), gs_ref[...], preferred_element_type=F32)
    out = (yn + bonus * v_ref[...].astype(F32)) * gate_ref[...].astype(F32)
    o_ref[...] = jnp.dot(out.astype(BF16), wo_ref[...], preferred_element_type=F32)


def _rwkv_readout(y, proj, kd, v, gate, gmean, gsum, vec, w_o):
    t, d = v.shape
    tm = _tile(t, (256, 128, 64, 32, 16))
    full = lambda a: pl.BlockSpec(a.shape, lambda i: (0,) * a.ndim)
    row = pl.BlockSpec((tm, d), lambda i: (i, 0))
    row2 = pl.BlockSpec((2, tm, d), lambda i: (0, i, 0))
    return pl.pallas_call(
        _rwkv_readout_body,
        grid=(t // tm,),
        in_specs=[row2, row, row2, row, row, full(gmean), full(gsum), full(vec), full(w_o)],
        out_specs=row,
        out_shape=jax.ShapeDtypeStruct((t, d), F32),
        compiler_params=_cparams("parallel"),
        name="rwkv_readout",
    )(y, proj, kd, v, gate, gmean, gsum, vec, w_o)


def _centred_shift(x):
    xp = jnp.pad(x, ((0, 0), (1, 1), (0, 0)))
    return 0.5 * (xp[:, :-2] + xp[:, 2:]) - x


def _pad_cols(w, width):
    return jnp.concatenate([w, jnp.zeros((w.shape[0], width - w.shape[1]), w.dtype)], axis=1)


def _pad_rows(w, height):
    return jnp.concatenate([w, jnp.zeros((height - w.shape[0], w.shape[1]), w.dtype)], axis=0)


def _rwkv_mixer(h_lat, h_ctx, p, v_first, vres, need_ctx):
    b, l_ctx, d = h_ctx.shape
    l = l_ctx + h_lat.shape[1]
    t = b * l
    hh = jnp.concatenate([h_ctx, h_lat], axis=1)
    xx = jnp.concatenate([_centred_shift(h_ctx), _centred_shift(h_lat)], axis=1)
    mu = p['mu']
    mats = [(p['w_rkv'][0], 0, d), (p['w_rkv'][1], 2, d), (p['w_rkv'][2], 3, d),
            (p['w1'][0], 1, LORA_PAD), (p['w1'][1], 1, LORA_PAD), (p['a1'][0], 4, LORA_PAD), (p['a1'][1], 4, LORA_PAD),
            (p['g1'], 5, GATE_PAD), (vres[1] if vres is not None else jnp.zeros((d, 1), F32), 3, 2 * LORA_PAD)]
    top = jnp.concatenate([_pad_cols(w, width) for w, _, width in mats], axis=1)
    bot = jnp.concatenate([_pad_cols(mu[m][:, None] * w, width) for w, m, width in mats], axis=1)
    wcat = jnp.concatenate([top, bot], axis=0)
    assert wcat.shape[1] == PROJ_WIDTH
    a_in = jnp.concatenate([hh, xx], axis=-1).reshape(t, 2 * d).astype(BF16)
    proj = _linear(a_in, wcat, out_dtype=BF16, tn=1024)
    w2p = jnp.stack([_pad_rows(p['w2'][i], LORA_PAD) for i in range(2)]).astype(BF16)
    a2p = jnp.stack([_pad_rows(p['a2'][i], LORA_PAD) for i in range(2)]).astype(BF16)
    g2p = _pad_rows(p['g2'], GATE_PAD).astype(BF16)
    head_of = jnp.arange(d, dtype=jnp.int32) // RWKV_HEAD
    same_head = head_of[:, None] == head_of[None, :]
    gsum = jnp.where(same_head, 1.0, 0.0).astype(BF16)
    gmean = jnp.where(same_head, 1.0 / RWKV_HEAD, 0.0).astype(BF16)
    zero = jnp.zeros((d,), F32)
    vec = jnp.stack([p['w0'][0], p['w0'][1], p['a0'][0], p['a0'][1], p['k_k'], p['k_a'],
                     vres[0] if vres is not None else zero, zero])
    if vres is not None:
        kk, v, gate, lw, ag, kd = _rwkv_post(proj, v_first.reshape(t, d), w2p, a2p, g2p,
                                             _pad_rows(vres[2], LORA_PAD).astype(BF16), vec, gsum)
    else:
        kk, v, gate, lw, ag, kd = _rwkv_post(proj, None, w2p, a2p, g2p, None, vec, gsum)
    sh3 = lambda a: a.reshape(a.shape[:-2] + (b, l, a.shape[-1]))
    y = _wkv(sh3(proj), sh3(kk), sh3(v), sh3(lw), sh3(ag), sh3(kd), b, l_ctx)
    vec_o = jnp.stack([p['ln_g'], p['ln_b'], p['r_k'].reshape(d)] + [zero] * 5)
    out = _rwkv_readout(y.reshape(2, t, d), proj, kd, v, gate, gmean, gsum, vec_o, p['w_o'].astype(BF16))
    out = out.reshape(b, l, d)
    out_ctx = out[:, :l_ctx] if need_ctx else None
    return out_ctx, out[:, l_ctx:], v.reshape(b, l, d)


def _filter_body(z_ref, t_ref, w1_ref, b1_ref, w2_ref, b2_ref, w3_ref, b3_ref, fr_ref, wo_ref, dl_ref, o_ref):
    hp = lax.Precision.HIGHEST
    fr = fr_ref[...]
    hdn = jnp.sin(fr[0:1] * (jnp.dot(z_ref[...], w1_ref[...], precision=hp, preferred_element_type=F32) + b1_ref[...]))
    hdn = jnp.sin(fr[1:2] * (jnp.dot(hdn, w2_ref[...], precision=hp, preferred_element_type=F32) + b2_ref[...]))
    hdn = jnp.sin(fr[2:3] * (jnp.dot(hdn, w3_ref[...], precision=hp, preferred_element_type=F32) + b3_ref[...]))
    filt = jnp.dot(hdn, wo_ref[...], precision=hp, preferred_element_type=F32)
    o_ref[...] = filt * jnp.exp(-t_ref[...] * dl_ref[...])


def _hyena_filters(length, p):
    t = jnp.linspace(0.0, 1.0, length, dtype=F32)[:, None]
    bands = (FILTER_EMB - 1) // 2
    f = jnp.linspace(1e-4, bands - 1, bands, dtype=F32)[None, :]
    ang = 2.0 * math.pi * f * jnp.arange(length, dtype=F32)[:, None] / length
    z = jnp.concatenate([t, jnp.cos(ang), -jnp.sin(ang), jnp.zeros((length, FILTER_EMB_PAD - FILTER_EMB), F32)], axis=-1)
    w1 = jnp.concatenate([p['f_w1'], jnp.zeros((FILTER_EMB_PAD - FILTER_EMB, p['f_w1'].shape[1]), F32)], axis=0)
    deltas = jnp.abs(jnp.linspace(FAST_DECAY, SLOW_DECAY, D_MODEL, dtype=F32))
    deltas2 = jnp.concatenate([deltas, deltas])[None, :]
    fw = p['f_w2'].shape[0]
    tl = _tile(length, (512, 256, 128, 64, 32, 16, 8))
    full = lambda a: pl.BlockSpec(a.shape, lambda i: (0,) * a.ndim)
    args = [w1, p['f_b1'].reshape(1, fw), p['f_w2'], p['f_b2'].reshape(1, fw), p['f_w3'], p['f_b3'].reshape(1, fw),
            p['f_freq'], p['f_wout'], deltas2]
    return pl.pallas_call(
        _filter_body,
        grid=(length // tl,),
        in_specs=[pl.BlockSpec((tl, FILTER_EMB_PAD), lambda i: (i, 0)), pl.BlockSpec((tl, 1), lambda i: (i, 0))]
                 + [full(a) for a in args],
        out_specs=pl.BlockSpec((tl, 2 * D_MODEL), lambda i: (i, 0)),
        out_shape=jax.ShapeDtypeStruct((length, 2 * D_MODEL), F32),
        compiler_params=_cparams("parallel"),
        name="hyena_filter",
    )(z, t, *args)


def _colmm_body(f_ref, x_ref, o_ref):
    o_ref[0] = jnp.dot(f_ref[...], x_ref[0].astype(BF16), preferred_element_type=F32)


def _colmm(f, x):
    bsz, q, n = x.shape
    pp = f.shape[0]
    tn = _tile(n, (4096, 2048, 1024, 512, 256, 128))
    return pl.pallas_call(
        _colmm_body,
        grid=(bsz, n // tn),
        in_specs=[pl.BlockSpec((pp, q), lambda i, j: (0, 0)),
                  pl.BlockSpec((1, q, tn), lambda i, j: (i, 0, j))],
        out_specs=pl.BlockSpec((1, pp, tn), lambda i, j: (i, 0, j)),
        out_shape=jax.ShapeDtypeStruct((bsz, pp, n), F32),
        compiler_params=_cparams("parallel", "parallel"),
        name="dft_outer",
    )(f.astype(BF16), x)


def _dft_inner_fwd_body(a_ref, m_ref, o_ref):
    n2 = a_ref.shape[3]
    a = jnp.concatenate([a_ref[0, 0, 0], a_ref[0, 1, 0]], axis=0).astype(BF16)
    x = jnp.dot(m_ref[0], a, preferred_element_type=F32)
    o_ref[0, 0, 0] = x[:n2]
    o_ref[0, 1, 0] = x[n2:]


def _dft_inner_conv_body(a_ref, g_ref, m_ref, mi_ref, o_ref):
    n2 = a_ref.shape[3]
    a = jnp.concatenate([a_ref[0, 0, 0], a_ref[0, 1, 0]], axis=0).astype(BF16)
    x = jnp.dot(m_ref[0], a, preferred_element_type=F32)
    xre, xim = x[:n2], x[n2:]
    gre, gim = g_ref[0, 0, 0], g_ref[0, 1, 0]
    y = jnp.concatenate([xre * gre - xim * gim, xre * gim + xim * gre], axis=0).astype(BF16)
    o = jnp.dot(mi_ref[0], y, preferred_element_type=F32)
    o_ref[0, 0, 0] = o[:n2]
    o_ref[0, 1, 0] = o[n2:]


def _dft_tables(n1, n2):
    n = n1 * n2
    i1 = jnp.arange(n1, dtype=jnp.int32)
    ph1 = (2.0 * math.pi / n1) * ((i1[:, None] * i1[None, :]) % n1).astype(F32)
    c1, s1 = jnp.cos(ph1), jnp.sin(ph1)
    k1 = i1[:, None, None]
    k2 = jnp.arange(n2, dtype=jnp.int32)[None, :, None]
    m2 = jnp.arange(n2, dtype=jnp.int32)[None, None, :]
    th = (2.0 * math.pi / n) * ((m2 * (k1 + n1 * k2)) % n).astype(F32)
    mre, mim = jnp.cos(th), -jnp.sin(th)
    m_fwd = jnp.concatenate([jnp.concatenate([mre, -mim], axis=2), jnp.concatenate([mim, mre], axis=2)], axis=1)
    qre, qim = jnp.swapaxes(mre, 1, 2), -jnp.swapaxes(mim, 1, 2)
    m_inv = jnp.concatenate([jnp.concatenate([qre, -qim], axis=2), jnp.concatenate([qim, qre], axis=2)], axis=1)
    f_fwd = jnp.concatenate([c1, -s1], axis=0)
    f_inv = jnp.concatenate([c1, -s1], axis=1)
    return f_fwd, f_inv, m_fwd.astype(BF16), m_inv.astype(BF16)


def _two_sided_long_conv(z, h_fwd, h_bwd):
    b, l, d = z.shape
    n = 2 * l
    n1 = max(DFT_MIN_OUTER, 2 ** ((int(math.log2(n)) - 1) // 2))
    n2 = n // n1
    f_fwd, f_inv, m_fwd, m_inv = _dft_tables(n1, n2)
    filt = jnp.concatenate([h_fwd, jnp.zeros_like(h_fwd[:1]), h_bwd[1:][::-1]], axis=0) * (1.0 / n)
    blk = pl.BlockSpec((1, 2, 1, n2, d), lambda i, j: (i, 0, j, 0, 0))
    gblk = pl.BlockSpec((1, 2, 1, n2, d), lambda i, j: (0, 0, j, 0, 0))
    mat = pl.BlockSpec((1, 2 * n2, 2 * n2), lambda i, j: (j, 0, 0))
    ga = _colmm(f_fwd, filt.reshape(1, n1, n2 * d)).reshape(1, 2, n1, n2, d)
    gspec = pl.pallas_call(
        _dft_inner_fwd_body,
        grid=(1, n1),
        in_specs=[blk, mat],
        out_specs=blk,
        out_shape=jax.ShapeDtypeStruct((1, 2, n1, n2, d), F32),
        compiler_params=_cparams("parallel", "parallel"),
        name="dft_inner_fwd",
    )(ga, m_fwd)
    za = _colmm(f_fwd[:, :n1 // 2], z.reshape(b, n1 // 2, n2 * d)).reshape(b, 2, n1, n2, d)
    bb = pl.pallas_call(
        _dft_inner_conv_body,
        grid=(b, n1),
        in_specs=[blk, gblk, mat, mat],
        out_specs=blk,
        out_shape=jax.ShapeDtypeStruct((b, 2, n1, n2, d), F32),
        compiler_params=_cparams("parallel", "parallel"),
        name="dft_inner_conv",
    )(za, gspec, m_fwd, m_inv)
    return _colmm(f_inv[:n1 // 2], bb.reshape(b, 2 * n1, n2 * d)).reshape(b, l, d)


def _hyena_stream(h, p):
    b, l, d = h.shape
    u = _linear(h.reshape(b * l, d).astype(BF16), p['w_in'], p['b_in'], tn=1024).reshape(b, l, 3 * d)
    up = jnp.pad(u, ((0, 0), (1, 1), (0, 0)))
    sw = p['short_w']
    u = up[:, :-2] * sw[0] + up[:, 1:-1] * sw[1] + up[:, 2:] * sw[2] + p['short_b']
    x0, x1, v = jnp.split(u, 3, axis=-1)
    filt = _hyena_filters(l, p)
    z = v * x1
    z = _two_sided_long_conv(z, filt[:, :d], filt[:, d:]) + z * p['bias']
    return _linear((x0 * z).reshape(b * l, d).astype(BF16), p['w_out'], p['b_out']).reshape(b, l, d)


def _attn_body(q_ref, k_ref, v_ref, o_ref, m_ref, l_ref, acc_ref):
    j = pl.program_id(3)

    @pl.when(j == 0)
    def _():
        m_ref[...] = jnp.full_like(m_ref, -jnp.inf)
        l_ref[...] = jnp.zeros_like(l_ref)
        acc_ref[...] = jnp.zeros_like(acc_ref)

    s = lax.dot_general(q_ref[0, 0], k_ref[0, 0], (((1,), (1,)), ((), ())), preferred_element_type=F32)
    m_prev = m_ref[...]
    m_new = jnp.maximum(m_prev, jnp.max(s, axis=-1, keepdims=True))
    alpha = jnp.exp(m_prev - m_new)
    pr = jnp.exp(s - m_new)
    l_ref[...] = alpha * l_ref[...] + jnp.sum(pr, axis=-1, keepdims=True)
    acc_ref[...] = alpha * acc_ref[...] + jnp.dot(pr.astype(BF16), v_ref[0], preferred_element_type=F32)
    m_ref[...] = m_new

    @pl.when(j == pl.num_programs(3) - 1)
    def _():
        o_ref[0] = acc_ref[...] / l_ref[...]


def _attend(q, k, v):
    b, nh, sq, dq = q.shape
    sk = k.shape[2]
    dv = v.shape[2] // nh
    tq = _tile(sq, (512, 256, 128, 64, 32, 16, 8))
    tk = _tile(sk, (1280, 1024, 640, 512, 256, 128))
    return pl.pallas_call(
        _attn_body,
        grid=(b, nh, sq // tq, sk // tk),
        in_specs=[pl.BlockSpec((1, 1, tq, dq), lambda bi, hi, i, j: (bi, hi, i, 0)),
                  pl.BlockSpec((1, 1, tk, dq), lambda bi, hi, i, j: (bi, hi, j, 0)),
                  pl.BlockSpec((1, tk, dv), lambda bi, hi, i, j: (bi, j, hi))],
        out_specs=pl.BlockSpec((1, tq, dv), lambda bi, hi, i, j: (bi, i, hi)),
        out_shape=jax.ShapeDtypeStruct((b, sq, nh * dv), F32),
        scratch_shapes=[pltpu.VMEM((tq, 1), F32), pltpu.VMEM((tq, 1), F32), pltpu.VMEM((tq, dv), F32)],
        compiler_params=_cparams("parallel", "parallel", "parallel", "arbitrary"),
        name="attention",
    )(q.astype(BF16), k.astype(BF16), v.astype(BF16))


def _rms(x, g):
    return x * lax.rsqrt(jnp.mean(x * x, axis=-1, keepdims=True) + NORM_EPS) * g


def _rope_1d(x, pos):
    d = x.shape[-1]
    inv = ROPE_THETA ** (-jnp.arange(0, d, 2, dtype=F32) / d)
    ang = pos.astype(F32)[:, None] * inv[None, :]
    cos = jnp.cos(ang)[:, None, :]
    sin = jnp.sin(ang)[:, None, :]
    x1, x2 = jnp.split(x, 2, axis=-1)
    return jnp.concatenate([x1 * cos - x2 * sin, x1 * sin + x2 * cos], axis=-1)


def _axial_rope(x, row, col):
    half = x.shape[-1] // 2
    return jnp.concatenate([_rope_1d(x[..., :half], row), _rope_1d(x[..., half:], col)], axis=-1)


def _mla_mixer(h_lat, h_ctx, p, rope_pos, need_ctx):
    b, l_ctx, d = h_ctx.shape
    l_lat = h_lat.shape[1]
    l = l_ctx + l_lat
    hh = jnp.concatenate([h_ctx, h_lat], axis=1).reshape(b * l, d).astype(BF16)
    qkv = _linear(hh, p['w_a'])
    q_c, kv_c, k_pe = jnp.split(qkv, [Q_LORA, Q_LORA + KV_LORA], axis=-1)
    q = _linear(_rms(q_c, p['q_norm_g']).astype(BF16), p['w_qb']).reshape(b, l, MLA_HEADS, QK_HEAD)
    kv = _linear(_rms(kv_c, p['kv_norm_g']).astype(BF16), p['w_kvb']).reshape(b, l, MLA_HEADS, QK_NOPE + V_HEAD)
    k_nope, v = kv[..., :QK_NOPE], kv[..., QK_NOPE:]
    k = jnp.concatenate([k_nope, jnp.broadcast_to(k_pe.reshape(b, l, 1, QK_ROPE), (b, l, MLA_HEADS, QK_ROPE))], axis=-1)
    q = _rms(q, p['qn_g'])
    k = _rms(k, p['kn_g'])
    row, col = rope_pos

    def rope_lat(t):
        t_l = t[:, l_ctx:]
        t_l = jnp.concatenate([t_l[..., :QK_NOPE], _axial_rope(t_l[..., QK_NOPE:], row, col)], axis=-1)
        return jnp.concatenate([t[:, :l_ctx], t_l], axis=1)

    q = jnp.swapaxes(rope_lat(q) * (QK_HEAD ** -0.5), 1, 2)
    k = jnp.swapaxes(rope_lat(k), 1, 2)
    v = v.reshape(b, l, MLA_HEADS * V_HEAD)
    o_lat = _attend(q[:, :, l_ctx:], k, v)
    if need_ctx:
        o_ctx = _attend(q[:, :, :l_ctx], k[:, :, :l_ctx], v[:, :l_ctx])
        o = jnp.concatenate([o_ctx, o_lat], axis=1).reshape(b * l, MLA_HEADS * V_HEAD)
        out = _linear(o.astype(BF16), p['w_o']).reshape(b, l, d)
        return out[:, :l_ctx], out[:, l_ctx:]
    out = _linear(o_lat.reshape(b * l_lat, MLA_HEADS * V_HEAD).astype(BF16), p['w_o']).reshape(b, l_lat, d)
    return None, out


def _router_body(h_ref, w_ref, b_ref, idx_ref, gate_ref):
    lg = jnp.dot(h_ref[...], w_ref[...], precision=lax.Precision.HIGHEST, preferred_element_type=F32) + b_ref[...]
    lane = lax.broadcasted_iota(jnp.int32, lg.shape, 1).astype(F32)
    idx_acc = jnp.zeros(lg.shape, F32)
    val_acc = jnp.zeros(lg.shape, F32)
    top = None
    den = None
    for j in range(TOP_K):
        mx = jnp.max(lg, axis=-1, keepdims=True)
        sel = jnp.min(jnp.where(lg == mx, lane, float(ROUTER_LANES)), axis=-1, keepdims=True)
        if j == 0:
            top = mx
        e = jnp.exp(mx - top)
        den = e if den is None else den + e
        idx_acc = jnp.where(lane == float(j), sel, idx_acc)
        val_acc = jnp.where(lane == float(j), e, val_acc)
        lg = jnp.where(lane == sel, -jnp.inf, lg)
    idx_ref[...] = idx_acc.astype(jnp.int32)
    gate_ref[...] = val_acc / den


def _router(h, router_w, router_b):
    t, d = h.shape
    tm = _tile(t, (512, 256, 128, 64, 32, 16, 8))
    w = jnp.concatenate([router_w, jnp.zeros((d, ROUTER_LANES - N_EXPERTS), F32)], axis=1)
    bias = jnp.concatenate([router_b, jnp.full((ROUTER_LANES - N_EXPERTS,), -1e30, F32)]).reshape(1, ROUTER_LANES)
    return pl.pallas_call(
        _router_body,
        grid=(t // tm,),
        in_specs=[pl.BlockSpec((tm, d), lambda i: (i, 0)),
                  pl.BlockSpec((d, ROUTER_LANES), lambda i: (0, 0)),
                  pl.BlockSpec((1, ROUTER_LANES), lambda i: (0, 0))],
        out_specs=[pl.BlockSpec((tm, ROUTER_LANES), lambda i: (i, 0))] * 2,
        out_shape=[jax.ShapeDtypeStruct((t, ROUTER_LANES), jnp.int32), jax.ShapeDtypeStruct((t, ROUTER_LANES), F32)],
        compiler_params=_cparams("parallel"),
        name="router",
    )(h, w, bias)


def _expert_body(be_ref, nb_ref, x_ref, win_ref, bin_ref, wout_ref, bout_ref, gate_ref, o_ref, win_bf, wout_bf):
    i = pl.program_id(0)
    fresh = jnp.logical_or(i == 0, be_ref[i] != be_ref[jnp.maximum(i - 1, 0)])

    @pl.when(fresh)
    def _():
        win_bf[...] = win_ref[0].astype(BF16)
        wout_bf[...] = wout_ref[0].astype(BF16)

    @pl.when(i < nb_ref[0])
    def _():
        gu = jnp.dot(x_ref[...], win_bf[...], preferred_element_type=F32) + bin_ref[0]
        glu = jnp.minimum(gu[:, :D_EXPERT], SWIGLU_LIMIT)
        lin = jnp.clip(gu[:, D_EXPERT:], -SWIGLU_LIMIT, SWIGLU_LIMIT)
        act = glu * jax.nn.sigmoid(SWIGLU_ALPHA * glu) * (lin + 1.0)
        y = jnp.dot(act.astype(BF16), wout_bf[...], preferred_element_type=F32) + bout_ref[0]
        o_ref[...] = (y * gate_ref[...]).astype(o_ref.dtype)

    @pl.when(i >= nb_ref[0])
    def _():
        o_ref[...] = jnp.zeros_like(o_ref)


def _moe(h, router_w, router_b, w_in, b_in, w_out, b_out):
    t, d = h.shape
    n_assign = t * TOP_K
    idx, gates = _router(h, router_w, router_b)
    flat_e = idx[:, :TOP_K].reshape(n_assign)
    flat_g = gates[:, :TOP_K].reshape(n_assign)
    onehot = (flat_e[:, None] == jnp.arange(N_EXPERTS, dtype=jnp.int32)[None, :]).astype(jnp.int32)
    csum = jnp.cumsum(onehot, axis=0)
    rank = jnp.take_along_axis(csum, flat_e[:, None], axis=1)[:, 0] - 1
    counts = csum[-1]
    padded = (counts + MOE_ROWS - 1) // MOE_ROWS * MOE_ROWS
    pad_end = jnp.cumsum(padded)
    pad_start = pad_end - padded
    dest = pad_start[flat_e] + rank
    n_blocks = -(-n_assign // MOE_ROWS) + N_EXPERTS
    n_rows = n_blocks * MOE_ROWS
    row_src = jnp.full((n_rows,), -1, jnp.int32).at[dest].set(jnp.arange(n_assign, dtype=jnp.int32))
    row_live = row_src >= 0
    row_src = jnp.maximum(row_src, 0)
    row_tok = row_src // TOP_K
    row_gate = jnp.where(row_live, flat_g[row_src], 0.0)
    block_start = jnp.arange(n_blocks, dtype=jnp.int32) * MOE_ROWS
    block_e = jnp.minimum(jnp.searchsorted(pad_end, block_start, side='right'), N_EXPERTS - 1).astype(jnp.int32)
    n_used = (pad_end[-1] // MOE_ROWS).astype(jnp.int32).reshape(1)
    xb = h.astype(BF16)[row_tok]
    f2 = 2 * D_EXPERT
    grid_spec = pltpu.PrefetchScalarGridSpec(
        num_scalar_prefetch=2,
        grid=(n_blocks,),
        in_specs=[pl.BlockSpec((MOE_ROWS, d), lambda i, be, nb: (i, 0)),
                  pl.BlockSpec((1, d, f2), lambda i, be, nb: (be[i], 0, 0)),
                  pl.BlockSpec((1, 1, f2), lambda i, be, nb: (be[i], 0, 0)),
                  pl.BlockSpec((1, D_EXPERT, d), lambda i, be, nb: (be[i], 0, 0)),
                  pl.BlockSpec((1, 1, d), lambda i, be, nb: (be[i], 0, 0)),
                  pl.BlockSpec((MOE_ROWS, 1), lambda i, be, nb: (i, 0))],
        out_specs=pl.BlockSpec((MOE_ROWS, d), lambda i, be, nb: (i, 0)),
        scratch_shapes=[pltpu.VMEM((d, f2), BF16), pltpu.VMEM((D_EXPERT, d), BF16)],
    )
    yb = pl.pallas_call(
        _expert_body,
        grid_spec=grid_spec,
        out_shape=jax.ShapeDtypeStruct((n_rows, d), BF16),
        compiler_params=_cparams("arbitrary"),
        name="moe_experts",
    )(block_e, n_used, xb, w_in, b_in.reshape(N_EXPERTS, 1, f2),
      w_out, b_out.reshape(N_EXPERTS, 1, d), row_gate.reshape(n_rows, 1))
    return jnp.sum(yb[dest].reshape(t, TOP_K, d).astype(F32), axis=1)


def kernel(x, c, ctx, c_ctx, ada_w, ada_b, norm_mix_g, norm_ffn_g, rwkv_mu, rwkv_w_rkv, rwkv_w0, rwkv_w1, rwkv_w2, rwkv_a0, rwkv_a1, rwkv_a2, rwkv_g1, rwkv_g2, rwkv_k_k, rwkv_k_a, rwkv_r_k, rwkv_ln_g, rwkv_ln_b, rwkv_w_o, rwkv_v0, rwkv_v1, rwkv_v2, hy_w_in, hy_b_in, hy_short_w, hy_short_b, hy_f_w1, hy_f_b1, hy_f_w2, hy_f_b2, hy_f_w3, hy_f_b3, hy_f_freq, hy_f_wout, hy_bias, hy_w_out, hy_b_out, mla_w_a, mla_q_norm_g, mla_kv_norm_g, mla_w_qb, mla_w_kvb, mla_qn_g, mla_kn_g, mla_w_o, moe_router_w, moe_router_b, moe_w_in, moe_b_in, moe_w_out, moe_b_out):
    bsz, s_len, d = x.shape
    l_ctx = ctx.shape[1]
    rows = s_len // GRID_W
    row = jnp.repeat(jnp.arange(rows, dtype=jnp.int32), GRID_W)
    col = jnp.tile(jnp.arange(GRID_W, dtype=jnp.int32), rows)
    silu = jnp.concatenate([jax.nn.silu(c), jax.nn.silu(c_ctx)[None, :]], axis=0)
    silu = jnp.concatenate([silu, jnp.zeros((-(bsz + 1) % 8, d), F32)], axis=0)
    xc = ctx
    v_first = None
    for i in range(DEPTH):
        need_ctx = i < DEPTH - 1
        j = i // N_MIXERS
        kind = i % N_MIXERS
        mod = _linear(silu, ada_w[i], ada_b[i], tn=1024)
        mod_l = jnp.split(mod[:bsz], N_MOD, axis=-1)
        mod_c = [jnp.broadcast_to(m, (bsz, d)) for m in jnp.split(mod[bsz:bsz + 1], N_MOD, axis=-1)]
        sh1, sc1, g1, sh2, sc2, g2 = [m[:, None, :] for m in mod_l]
        csh1, csc1, cg1, csh2, csc2, cg2 = [m[:, None, :] for m in mod_c]
        h_l = _norm_mod(x, norm_mix_g[i], sc1, sh1)
        h_c = _norm_mod(xc, norm_mix_g[i], csc1, csh1)
        if kind == 0:
            p = {'mu': rwkv_mu[j], 'w_rkv': rwkv_w_rkv[j], 'w0': rwkv_w0[j], 'w1': rwkv_w1[j],
                 'w2': rwkv_w2[j], 'a0': rwkv_a0[j], 'a1': rwkv_a1[j], 'a2': rwkv_a2[j],
                 'g1': rwkv_g1[j], 'g2': rwkv_g2[j], 'k_k': rwkv_k_k[j], 'k_a': rwkv_k_a[j],
                 'r_k': rwkv_r_k[j], 'ln_g': rwkv_ln_g[j], 'ln_b': rwkv_ln_b[j], 'w_o': rwkv_w_o[j]}
            vres = None if j == 0 else (rwkv_v0[j - 1], rwkv_v1[j - 1], rwkv_v2[j - 1])
            y_c, y_l, v_cur = _rwkv_mixer(h_l, h_c, p, v_first, vres, need_ctx)
            if j == 0:
                v_first = v_cur
        elif kind == 1:
            p = {'w_in': hy_w_in[j], 'b_in': hy_b_in[j], 'short_w': hy_short_w[j], 'short_b': hy_short_b[j],
                 'f_w1': hy_f_w1[j], 'f_b1': hy_f_b1[j], 'f_w2': hy_f_w2[j], 'f_b2': hy_f_b2[j],
                 'f_w3': hy_f_w3[j], 'f_b3': hy_f_b3[j], 'f_freq': hy_f_freq[j], 'f_wout': hy_f_wout[j],
                 'bias': hy_bias[j], 'w_out': hy_w_out[j], 'b_out': hy_b_out[j]}
            y_l = _hyena_stream(h_l, p)
            y_c = _hyena_stream(h_c, p) if need_ctx else None
        else:
            p = {'w_a': mla_w_a[j], 'q_norm_g': mla_q_norm_g[j], 'kv_norm_g': mla_kv_norm_g[j],
                 'w_qb': mla_w_qb[j], 'w_kvb': mla_w_kvb[j], 'qn_g': mla_qn_g[j], 'kn_g': mla_kn_g[j],
                 'w_o': mla_w_o[j]}
            y_c, y_l = _mla_mixer(h_l, h_c, p, (row, col), need_ctx)
        x = x + g1 * y_l
        h2_l = _norm_mod(x, norm_ffn_g[i], sc2, sh2)
        moe_p = (moe_router_w[i], moe_router_b[i], moe_w_in[i], moe_b_in[i], moe_w_out[i], moe_b_out[i])
        if need_ctx:
            xc = xc + cg1 * y_c
            h2_c = _norm_mod(xc, norm_ffn_g[i], csc2, csh2)
            n_c = bsz * l_ctx
            f = _moe(jnp.concatenate([h2_c.reshape(-1, d), h2_l.reshape(-1, d)], axis=0), *moe_p)
            xc = xc + cg2 * f[:n_c].reshape(xc.shape)
            x = x + g2 * f[n_c:].reshape(x.shape)
        else:
            x = x + g2 * _moe(h2_l.reshape(-1, d), *moe_p).reshape(x.shape)
    return x
```

```python
import functools
import math

import jax
import jax.numpy as jnp
import numpy as np
from jax import lax
from jax.experimental import pallas as pl
from jax.experimental.pallas import tpu as pltpu

F32 = jnp.float32
BF16 = jnp.bfloat16

D_MODEL = 1024
DEPTH = 4
GRID_W = 64
N_MIXERS = 3
N_MOD = 6
NORM_EPS = 1e-6
RWKV_HEAD = 64
RWKV_HEADS = D_MODEL // RWKV_HEAD
GN_EPS = 64e-5
WKV_CHUNK = 64
WKV_GROUP = 4
FILTER_EMB = 33
FILTER_EMB_PAD = 128
DFT_MIN_OUTER = 32
FAST_DECAY = math.log(1e-2) / 0.3
SLOW_DECAY = math.log(1e-2) / 1.5
MLA_HEADS = 8
QK_NOPE = 128
QK_ROPE = 64
QK_HEAD = QK_NOPE + QK_ROPE
V_HEAD = 128
Q_LORA = 384
KV_LORA = 256
ROPE_THETA = 10000.0
N_EXPERTS = 32
TOP_K = 4
D_EXPERT = 1024
SWIGLU_LIMIT = 7.0
SWIGLU_ALPHA = 1.702
MOE_ROWS = 256
ROUTER_LANES = 128

V7X_VMEM_BYTES = 64 * 1024 * 1024
VMEM_LIMIT = V7X_VMEM_BYTES * 3 // 4


def _cparams(*sem):
    return pltpu.CompilerParams(dimension_semantics=sem, vmem_limit_bytes=VMEM_LIMIT)


def _tile(n, prefs):
    for t in prefs:
        if n % t == 0:
            return t
    return n


def _linear_body(a_ref, w_ref, b_ref, o_ref, *, in_act):
    a = a_ref[...]
    if in_act == 'tanh':
        a = jnp.tanh(a.astype(F32))
    elif in_act == 'sigmoid':
        a = jax.nn.sigmoid(a.astype(F32))
    acc = jnp.dot(a.astype(BF16), w_ref[...], preferred_element_type=F32)
    o_ref[...] = (acc + b_ref[...]).astype(o_ref.dtype)


def _linear(a, w, b=None, *, in_act=None, out_dtype=F32, tn=None):
    m, k = a.shape
    n = w.shape[1]
    tm = _tile(m, (512, 256, 128, 64, 32, 16, 8))
    tn = n if tn is None else tn
    assert n % tn == 0
    if b is None:
        b = jnp.zeros((n,), F32)
    return pl.pallas_call(
        functools.partial(_linear_body, in_act=in_act),
        grid=(n // tn, m // tm),
        in_specs=[pl.BlockSpec((tm, k), lambda j, i: (i, 0)),
                  pl.BlockSpec((k, tn), lambda j, i: (0, j)),
                  pl.BlockSpec((1, tn), lambda j, i: (0, j))],
        out_specs=pl.BlockSpec((tm, tn), lambda j, i: (i, j)),
        out_shape=jax.ShapeDtypeStruct((m, n), out_dtype),
        compiler_params=_cparams("parallel", "parallel"),
        name="linear",
    )(a, w.astype(BF16), b.reshape(1, n).astype(F32))


def _norm_mod_body(x_ref, g_ref, sc_ref, sh_ref, o_ref):
    x = x_ref[0]
    y = x * lax.rsqrt(jnp.mean(x * x, axis=-1, keepdims=True) + NORM_EPS)
    o_ref[0] = ((y * g_ref[...]) * (1.0 + sc_ref[0]) + sh_ref[0]).astype(o_ref.dtype)


def _norm_mod(x, g, sc, sh):
    b, l, d = x.shape
    tl = _tile(l, (512, 256, 128, 64, 32, 16, 8))
    return pl.pallas_call(
        _norm_mod_body,
        grid=(b, l // tl),
        in_specs=[pl.BlockSpec((1, tl, d), lambda i, j: (i, j, 0)),
                  pl.BlockSpec((1, d), lambda i, j: (0, 0)),
                  pl.BlockSpec((1, 1, d), lambda i, j: (i, 0, 0)),
                  pl.BlockSpec((1, 1, d), lambda i, j: (i, 0, 0))],
        out_specs=pl.BlockSpec((1, tl, d), lambda i, j: (i, j, 0)),
        out_shape=jax.ShapeDtypeStruct((b, l, d), F32),
        compiler_params=_cparams("parallel", "parallel"),
        name="norm_mod",
    )(x, g.reshape(1, d), sc.reshape(b, 1, d), sh.reshape(b, 1, d))


PROJ_R, PROJ_K, PROJ_V = 0, 1024, 2048
PROJ_W1 = (3072, 3200)
PROJ_A1 = (3328, 3456)
PROJ_G1 = 3584
PROJ_V1 = 3840
PROJ_WIDTH = 4096
LORA_PAD = 128
GATE_PAD = 256


def _sigmoid(x):
    return jax.nn.sigmoid(x)


def _rwkv_post_body(*refs, has_vres):
    if has_vres:
        (proj_ref, vf_ref, w2_ref, a2_ref, g2_ref, v2_ref, vec_ref, gs_ref,
         kk_out, v_out, gate_out, lw_out, ag_out, kd_out) = refs
    else:
        (proj_ref, w2_ref, a2_ref, g2_ref, vec_ref, gs_ref,
         kk_out, v_out, gate_out, lw_out, ag_out, kd_out) = refs
    d = D_MODEL
    vec = vec_ref[...]
    k = proj_ref[:, PROJ_K:PROJ_K + d].astype(F32)
    v = proj_ref[:, PROJ_V:PROJ_V + d].astype(F32)
    if has_vres:
        v1 = proj_ref[:, PROJ_V1:PROJ_V1 + LORA_PAD]
        vgate = _sigmoid(vec[6:7] + jnp.dot(v1, v2_ref[...], preferred_element_type=F32))
        v = v + (vf_ref[...].astype(F32) - v) * vgate
    v_out[...] = v.astype(v_out.dtype)
    kraw = k * vec[4:5]
    ss = jnp.dot((kraw * kraw).astype(BF16), gs_ref[...], preferred_element_type=F32)
    kk_out[...] = (kraw / jnp.maximum(jnp.sqrt(ss), 1e-12)).astype(kk_out.dtype)
    g1 = proj_ref[:, PROJ_G1:PROJ_G1 + GATE_PAD].astype(F32)
    gate_out[...] = jnp.dot(_sigmoid(g1).astype(BF16), g2_ref[...], preferred_element_type=F32).astype(gate_out.dtype)
    for dd in range(2):
        w1 = proj_ref[:, PROJ_W1[dd]:PROJ_W1[dd] + LORA_PAD].astype(F32)
        z = vec[dd:dd + 1] + jnp.dot(jnp.tanh(w1).astype(BF16), w2_ref[dd], preferred_element_type=F32)
        softplus = jnp.maximum(-z, 0.0) + jnp.log(1.0 + jnp.exp(-jnp.abs(z)))
        lw_out[dd] = -jnp.exp(-softplus - 0.5)
        a1 = proj_ref[:, PROJ_A1[dd]:PROJ_A1[dd] + LORA_PAD]
        ag = _sigmoid(vec[2 + dd:3 + dd] + jnp.dot(a1, a2_ref[dd], preferred_element_type=F32))
        ag_out[dd] = ag.astype(ag_out.dtype)
        kd_out[dd] = (k * (1.0 + (ag - 1.0) * vec[5:6])).astype(kd_out.dtype)


def _rwkv_post(proj, v_first, w2p, a2p, g2p, v2p, vec, gsum):
    t, d = proj.shape[0], D_MODEL
    tm = _tile(t, (256, 128, 64, 32, 16))
    has_vres = v_first is not None
    full = lambda a: pl.BlockSpec(a.shape, lambda i: (0,) * a.ndim)
    row = pl.BlockSpec((tm, d), lambda i: (i, 0))
    row2 = pl.BlockSpec((2, tm, d), lambda i: (0, i, 0))
    ins = [proj] + ([v_first] if has_vres else []) + [w2p, a2p, g2p] + ([v2p] if has_vres else []) + [vec, gsum]
    in_specs = ([pl.BlockSpec((tm, PROJ_WIDTH), lambda i: (i, 0))] + ([row] if has_vres else [])
                + [full(a) for a in ins[(2 if has_vres else 1):]])
    return pl.pallas_call(
        functools.partial(_rwkv_post_body, has_vres=has_vres),
        grid=(t // tm,),
        in_specs=in_specs,
        out_specs=[row, row, row, row2, row2, row2],
        out_shape=[jax.ShapeDtypeStruct((t, d), BF16),
                   jax.ShapeDtypeStruct((t, d), BF16),
                   jax.ShapeDtypeStruct((t, d), BF16),
                   jax.ShapeDtypeStruct((2, t, d), F32),
                   jax.ShapeDtypeStruct((2, t, d), BF16),
                   jax.ShapeDtypeStruct((2, t, d), BF16)],
        compiler_params=_cparams("parallel"),
        name="rwkv_post",
    )(*ins)


def _head_groups(y):
    gw = WKV_GROUP * RWKV_HEAD
    lane_head = jnp.right_shift(lax.broadcasted_iota(jnp.int32, (y.shape[0], gw), 1), 6)
    out = []
    for g in range(D_MODEL // gw):
        yg = y[:, g * gw:(g + 1) * gw].astype(BF16)
        out.append(jnp.concatenate([jnp.where(lane_head == h, yg, jnp.zeros_like(yg)) for h in range(WKV_GROUP)], axis=0))
    return out


def _gmm(x, ybd):
    gw = WKV_GROUP * RWKV_HEAD
    return jnp.concatenate([jnp.dot(x[:, g * gw:(g + 1) * gw].astype(BF16), yb, preferred_element_type=F32)
                            for g, yb in enumerate(ybd)], axis=-1)


def _gmm_nt(x, ybd):
    gw = WKV_GROUP * RWKV_HEAD
    return jnp.concatenate([lax.dot_general(x[:, g * gw:(g + 1) * gw].astype(BF16), yb, (((1,), (1,)), ((), ())),
                                            preferred_element_type=F32)
                            for g, yb in enumerate(ybd)], axis=-1)


def _wkv_chunks(r, kk, v, lw, ag, kd, s0, rev):
    each = lambda f, *ls: [f(*a) for a in zip(*ls)]
    c, d = r[0].shape
    row0 = lax.broadcasted_iota(jnp.int32, (c, d), 0)
    col0 = jnp.bitwise_and(lax.broadcasted_iota(jnp.int32, (c, d), 1), RWKV_HEAD - 1)
    incl = [(row0 <= col0) if rv else (row0 >= col0) for rv in rev]
    strict = [(row0 < col0) if rv else (row0 > col0) for rv in rev]
    rs = lax.broadcasted_iota(jnp.int32, (c, c), 0)
    cs = lax.broadcasted_iota(jnp.int32, (c, c), 1)
    tri = [jnp.where((rs <= cs) if rv else (rs >= cs), 1.0, 0.0).astype(BF16) for rv in rev]
    hi = each(lambda x: x.astype(BF16), lw)
    rem = each(lambda x, h: x - h.astype(F32), lw, hi)
    mid = each(lambda x: x.astype(BF16), rem)
    lo = each(lambda x, m: (x - m.astype(F32)).astype(BF16), rem, mid)
    cum = each(lambda t, a, b_, c_: (jnp.dot(t, a, preferred_element_type=F32) + jnp.dot(t, b_, preferred_element_type=F32)
                                     + jnp.dot(t, c_, preferred_element_type=F32)), tri, hi, mid, lo)
    cum_end = [x[0:1, :] if rv else x[c - 1:c, :] for x, rv in zip(cum, rev)]
    g_rest = each(lambda e, x: jnp.exp(e - x), cum_end, cum)
    g_inv = each(lambda x: jnp.exp(-x), cum)
    bvec = each(lambda a, b_: a * b_, kk, ag)
    rt = each(lambda a, x: a * jnp.exp(x), r, cum)
    kt = each(lambda a, g: a * g, kd, g_inv)
    bt = each(lambda a, g: a * g, bvec, g_inv)
    at = each(lambda a, x, w: -a * jnp.exp(x - w), kk, cum, lw)
    ar = each(lambda a, b_: jnp.concatenate([a, b_], axis=0), at, rt)
    ab = each(lambda a, b_: _gmm_nt(a, _head_groups(b_)), ar, bt)
    ak = each(lambda a, b_: _gmm_nt(a, _head_groups(b_)), ar, kt)
    a_ab = each(lambda m, x: jnp.where(m, x[:c], 0.0), strict, ab)
    a_rb = each(lambda m, x: jnp.where(m, x[c:], 0.0), incl, ab)
    a_ak = each(lambda m, x: jnp.where(m, x[:c], 0.0), strict, ak)
    a_rk = each(lambda m, x: jnp.where(m, x[c:], 0.0), incl, ak)
    same16 = jnp.right_shift(row0, 4) == jnp.right_shift(col0, 4)
    same32 = jnp.right_shift(row0, 5) == jnp.right_shift(col0, 5)
    eye = jnp.where(row0 == col0, 1.0, 0.0)
    npow = each(lambda x: jnp.where(same16, x, 0.0), a_ab)
    p = each(lambda x: eye + x, npow)
    for _ in range(3):
        npow = each(lambda x: _gmm(x, _head_groups(x)), npow)
        p = each(lambda x, y: x + _gmm(x, _head_groups(y)), p, npow)
    off32 = jnp.logical_and(same32, jnp.logical_not(same16))
    t1 = each(lambda x, y: _gmm(jnp.where(off32, x, 0.0), _head_groups(y)), a_ab, p)
    p = each(lambda x, y: x + _gmm(x, _head_groups(y)), p, t1)
    t2 = each(lambda x, y: _gmm(jnp.where(same32, 0.0, x), _head_groups(y)), a_ab, p)
    p = each(lambda x, y: x + _gmm(x, _head_groups(y)), p, t2)
    pbd = each(_head_groups, p)
    vbd = each(_head_groups, v)
    wmat = each(lambda x, y: _gmm(x, _head_groups(y)), p, at)
    vt = each(lambda y: _gmm_nt(eye, y), vbd)
    xt = each(lambda x, y: _gmm_nt(x, _head_groups(y)), vt, a_ak)
    upt = each(_gmm_nt, xt, pbd)
    yp = each(_gmm, a_rk, vbd)
    sp = each(lambda x, a, g: _gmm(x, _head_groups(a * g)), vt, kd, g_rest)
    ut = each(lambda s_, w, u: _gmm_nt(s_, _head_groups(w)) + u, s0, wmat, upt)
    ys = each(lambda s_, q, x: _gmm_nt(q, _head_groups(s_)) + x, s0, rt, yp)
    y = each(lambda y0, a, u: y0 + _gmm_nt(a, _head_groups(u)), ys, a_rb, ut)
    s_end = each(lambda s_, e, u, b_, g, x: s_ * jnp.exp(e) + _gmm(u, _head_groups(b_ * g)) + x,
                 s0, cum_end, ut, bvec, g_rest, sp)
    return y, s_end


def _wkv_body(*refs, n_batch):
    ins, outs, s_ref = refs[:12], refs[12:14], refs[14]

    @pl.when(pl.program_id(0) == 0)
    def _():
        s_ref[...] = jnp.zeros_like(s_ref)

    scans = [(dd, bi) for dd in range(2) for bi in range(n_batch)]
    get = lambda k, lead: [ins[6 * dd + k][(0, bi) if lead else bi].astype(F32) for dd, bi in scans]
    y, s_end = _wkv_chunks(get(0, False), get(1, False), get(2, False), get(3, True), get(4, True), get(5, True),
                           [s_ref[i] for i in range(len(scans))], [dd == 1 for dd, _ in scans])
    for i, (dd, bi) in enumerate(scans):
        outs[dd][bi] = y[i]
        s_ref[i] = s_end[i]


def _wkv(proj, kk, v, lw, ag, kd, l_ctx):
    b, l, d = kk.shape
    c = WKV_CHUNK
    nc, nctx = l // c, l_ctx // c
    assert c == RWKV_HEAD and l % c == 0 and l_ctx % c == 0

    def fwd(j):
        return j

    def bwd(j):
        return jnp.where(j < nctx, nctx - 1 - j, nc + nctx - 1 - j)

    in_specs, out_specs = [], []
    for dd, ch in enumerate((fwd, bwd)):
        shared = pl.BlockSpec((b, c, d), lambda j, ch=ch: (0, ch(j), 0))
        per_dir = pl.BlockSpec((1, b, c, d), lambda j, dd=dd, ch=ch: (dd, 0, ch(j), 0))
        in_specs += [shared, shared, shared, per_dir, per_dir, per_dir]
        out_specs.append(shared)
    return pl.pallas_call(
        functools.partial(_wkv_body, n_batch=b),
        grid=(nc,),
        in_specs=in_specs,
        out_specs=out_specs,
        out_shape=[jax.ShapeDtypeStruct((b, l, d), F32)] * 2,
        scratch_shapes=[pltpu.VMEM((2 * b, RWKV_HEAD, d), F32)],
        compiler_params=_cparams("arbitrary"),
        name="wkv",
    )(*([proj, kk, v, lw, ag, kd] * 2))


def _rwkv_readout_body(yf_ref, yr_ref, r_ref, kd_ref, v_ref, gate_ref, gm_ref, gs_ref, vec_ref, wo_ref, o_ref):
    vec = vec_ref[...]
    y = yf_ref[...] + yr_ref[...]
    gm = gm_ref[...]
    yh = y.astype(BF16)
    yl = (y - yh.astype(F32)).astype(BF16)
    mean = jnp.dot(yh, gm, preferred_element_type=F32) + jnp.dot(yl, gm, preferred_element_type=F32)
    yc = y - mean
    var = jnp.dot((yc * yc).astype(BF16), gm, preferred_element_type=F32)
    yn = yc * lax.rsqrt(var + GN_EPS) * vec[0:1] + vec[1:2]
    r = r_ref[...].astype(F32)
    kds = kd_ref[0].astype(F32) + kd_ref[1].astype(F32)
    bonus = jnp.dot((r * kds * vec[2:3]).astype(BF16), gs_ref[...], preferred_element_type=F32)
    out = (yn + bonus * v_ref[...].astype(F32)) * gate_ref[...].astype(F32)
    o_ref[...] = jnp.dot(out.astype(BF16), wo_ref[...], preferred_element_type=F32)


def _rwkv_readout(y_fwd, y_rev, proj, kd, v, gate, gmean, gsum, vec, w_o):
    t, d = v.shape
    tm = _tile(t, (256, 128, 64, 32, 16))
    full = lambda a: pl.BlockSpec(a.shape, lambda i: (0,) * a.ndim)
    row = pl.BlockSpec((tm, d), lambda i: (i, 0))
    row2 = pl.BlockSpec((2, tm, d), lambda i: (0, i, 0))
    return pl.pallas_call(
        _rwkv_readout_body,
        grid=(t // tm,),
        in_specs=[row, row, row, row2, row, row, full(gmean), full(gsum), full(vec), full(w_o)],
        out_specs=row,
        out_shape=jax.ShapeDtypeStruct((t, d), F32),
        compiler_params=_cparams("parallel"),
        name="rwkv_readout",
    )(y_fwd, y_rev, proj, kd, v, gate, gmean, gsum, vec, w_o)


def _centred_shift(x):
    xp = jnp.pad(x, ((0, 0), (1, 1), (0, 0)))
    return 0.5 * (xp[:, :-2] + xp[:, 2:]) - x


def _pad_cols(w, width):
    return jnp.concatenate([w, jnp.zeros((w.shape[0], width - w.shape[1]), w.dtype)], axis=1)


def _pad_rows(w, height):
    return jnp.concatenate([w, jnp.zeros((height - w.shape[0], w.shape[1]), w.dtype)], axis=0)


def _rwkv_mixer(h_lat, h_ctx, p, v_first, vres, need_ctx):
    b, l_ctx, d = h_ctx.shape
    l = l_ctx + h_lat.shape[1]
    t = b * l
    hh = jnp.concatenate([h_ctx, h_lat], axis=1)
    xx = jnp.concatenate([_centred_shift(h_ctx), _centred_shift(h_lat)], axis=1)
    mu = p['mu']
    mats = [(p['w_rkv'][0], 0, d), (p['w_rkv'][1], 2, d), (p['w_rkv'][2], 3, d),
            (p['w1'][0], 1, LORA_PAD), (p['w1'][1], 1, LORA_PAD), (p['a1'][0], 4, LORA_PAD), (p['a1'][1], 4, LORA_PAD),
            (p['g1'], 5, GATE_PAD), (vres[1] if vres is not None else jnp.zeros((d, 1), F32), 3, 2 * LORA_PAD)]
    top = jnp.concatenate([_pad_cols(w, width) for w, _, width in mats], axis=1)
    bot = jnp.concatenate([_pad_cols(mu[m][:, None] * w, width) for w, m, width in mats], axis=1)
    wcat = jnp.concatenate([top, bot], axis=0)
    assert wcat.shape[1] == PROJ_WIDTH
    a_in = jnp.concatenate([hh, xx], axis=-1).reshape(t, 2 * d).astype(BF16)
    proj = _linear(a_in, wcat, out_dtype=BF16, tn=1024)
    w2p = jnp.stack([_pad_rows(p['w2'][i], LORA_PAD) for i in range(2)]).astype(BF16)
    a2p = jnp.stack([_pad_rows(p['a2'][i], LORA_PAD) for i in range(2)]).astype(BF16)
    g2p = _pad_rows(p['g2'], GATE_PAD).astype(BF16)
    head_of = jnp.arange(d, dtype=jnp.int32) // RWKV_HEAD
    same_head = head_of[:, None] == head_of[None, :]
    gsum = jnp.where(same_head, 1.0, 0.0).astype(BF16)
    gmean = jnp.where(same_head, 1.0 / RWKV_HEAD, 0.0).astype(BF16)
    zero = jnp.zeros((d,), F32)
    vec = jnp.stack([p['w0'][0], p['w0'][1], p['a0'][0], p['a0'][1], p['k_k'], p['k_a'],
                     vres[0] if vres is not None else zero, zero])
    if vres is not None:
        kk, v, gate, lw, ag, kd = _rwkv_post(proj, v_first.reshape(t, d), w2p, a2p, g2p,
                                             _pad_rows(vres[2], LORA_PAD).astype(BF16), vec, gsum)
    else:
        kk, v, gate, lw, ag, kd = _rwkv_post(proj, None, w2p, a2p, g2p, None, vec, gsum)
    sh3 = lambda a: a.reshape(a.shape[:-2] + (b, l, a.shape[-1]))
    y_fwd, y_rev = _wkv(sh3(proj), sh3(kk), sh3(v), sh3(lw), sh3(ag), sh3(kd), l_ctx)
    vec_o = jnp.stack([p['ln_g'], p['ln_b'], p['r_k'].reshape(d)] + [zero] * 5)
    out = _rwkv_readout(y_fwd.reshape(t, d), y_rev.reshape(t, d), proj, kd, v, gate, gmean, gsum, vec_o, p['w_o'].astype(BF16))
    out = out.reshape(b, l, d)
    out_ctx = out[:, :l_ctx] if need_ctx else None
    return out_ctx, out[:, l_ctx:], v.reshape(b, l, d)


def _filter_body(z_ref, t_ref, w1_ref, b1_ref, w2_ref, b2_ref, w3_ref, b3_ref, fr_ref, wo_ref, dl_ref, o_ref):
    hp = lax.Precision.HIGHEST
    fr = fr_ref[...]
    hdn = jnp.sin(fr[0:1] * (jnp.dot(z_ref[...], w1_ref[...], precision=hp, preferred_element_type=F32) + b1_ref[...]))
    hdn = jnp.sin(fr[1:2] * (jnp.dot(hdn, w2_ref[...], precision=hp, preferred_element_type=F32) + b2_ref[...]))
    hdn = jnp.sin(fr[2:3] * (jnp.dot(hdn, w3_ref[...], precision=hp, preferred_element_type=F32) + b3_ref[...]))
    filt = jnp.dot(hdn, wo_ref[...], precision=hp, preferred_element_type=F32)
    o_ref[...] = filt * jnp.exp(-t_ref[...] * dl_ref[...])


def _hyena_filters(length, p):
    t = jnp.linspace(0.0, 1.0, length, dtype=F32)[:, None]
    bands = (FILTER_EMB - 1) // 2
    f = jnp.linspace(1e-4, bands - 1, bands, dtype=F32)[None, :]
    ang = 2.0 * math.pi * f * jnp.arange(length, dtype=F32)[:, None] / length
    z = jnp.concatenate([t, jnp.cos(ang), -jnp.sin(ang), jnp.zeros((length, FILTER_EMB_PAD - FILTER_EMB), F32)], axis=-1)
    w1 = jnp.concatenate([p['f_w1'], jnp.zeros((FILTER_EMB_PAD - FILTER_EMB, p['f_w1'].shape[1]), F32)], axis=0)
    deltas = jnp.abs(jnp.linspace(FAST_DECAY, SLOW_DECAY, D_MODEL, dtype=F32))
    deltas2 = jnp.concatenate([deltas, deltas])[None, :]
    fw = p['f_w2'].shape[0]
    tl = _tile(length, (512, 256, 128, 64, 32, 16, 8))
    full = lambda a: pl.BlockSpec(a.shape, lambda i: (0,) * a.ndim)
    args = [w1, p['f_b1'].reshape(1, fw), p['f_w2'], p['f_b2'].reshape(1, fw), p['f_w3'], p['f_b3'].reshape(1, fw),
            p['f_freq'], p['f_wout'], deltas2]
    return pl.pallas_call(
        _filter_body,
        grid=(length // tl,),
        in_specs=[pl.BlockSpec((tl, FILTER_EMB_PAD), lambda i: (i, 0)), pl.BlockSpec((tl, 1), lambda i: (i, 0))]
                 + [full(a) for a in args],
        out_specs=pl.BlockSpec((tl, 2 * D_MODEL), lambda i: (i, 0)),
        out_shape=jax.ShapeDtypeStruct((length, 2 * D_MODEL), F32),
        compiler_params=_cparams("parallel"),
        name="hyena_filter",
    )(z, t, *args)


def _colmm_body(f_ref, x_ref, o_ref):
    o_ref[0] = jnp.dot(f_ref[...], x_ref[0].astype(BF16), preferred_element_type=F32)


def _colmm(f, x):
    bsz, q, n = x.shape
    pp = f.shape[0]
    tn = _tile(n, (4096, 2048, 1024, 512, 256, 128))
    return pl.pallas_call(
        _colmm_body,
        grid=(bsz, n // tn),
        in_specs=[pl.BlockSpec((pp, q), lambda i, j: (0, 0)),
                  pl.BlockSpec((1, q, tn), lambda i, j: (i, 0, j))],
        out_specs=pl.BlockSpec((1, pp, tn), lambda i, j: (i, 0, j)),
        out_shape=jax.ShapeDtypeStruct((bsz, pp, n), F32),
        compiler_params=_cparams("parallel", "parallel"),
        name="dft_outer",
    )(f.astype(BF16), x)


def _dft_inner_fwd_body(a_ref, m_ref, o_ref):
    n2 = a_ref.shape[3]
    a = jnp.concatenate([a_ref[0, 0, 0], a_ref[0, 1, 0]], axis=0).astype(BF16)
    x = jnp.dot(m_ref[0], a, preferred_element_type=F32)
    o_ref[0, 0, 0] = x[:n2]
    o_ref[0, 1, 0] = x[n2:]


def _dft_inner_conv_body(a_ref, g_ref, m_ref, mi_ref, o_ref):
    n2 = a_ref.shape[3]
    a = jnp.concatenate([a_ref[0, 0, 0], a_ref[0, 1, 0]], axis=0).astype(BF16)
    x = jnp.dot(m_ref[0], a, preferred_element_type=F32)
    xre, xim = x[:n2], x[n2:]
    gre, gim = g_ref[0, 0, 0], g_ref[0, 1, 0]
    y = jnp.concatenate([xre * gre - xim * gim, xre * gim + xim * gre], axis=0).astype(BF16)
    o = jnp.dot(mi_ref[0], y, preferred_element_type=F32)
    o_ref[0, 0, 0] = o[:n2]
    o_ref[0, 1, 0] = o[n2:]


def _dft_tables(n1, n2):
    n = n1 * n2
    i1 = jnp.arange(n1, dtype=jnp.int32)
    ph1 = (2.0 * math.pi / n1) * ((i1[:, None] * i1[None, :]) % n1).astype(F32)
    c1, s1 = jnp.cos(ph1), jnp.sin(ph1)
    k1 = i1[:, None, None]
    k2 = jnp.arange(n2, dtype=jnp.int32)[None, :, None]
    m2 = jnp.arange(n2, dtype=jnp.int32)[None, None, :]
    th = (2.0 * math.pi / n) * ((m2 * (k1 + n1 * k2)) % n).astype(F32)
    mre, mim = jnp.cos(th), -jnp.sin(th)
    m_fwd = jnp.concatenate([jnp.concatenate([mre, -mim], axis=2), jnp.concatenate([mim, mre], axis=2)], axis=1)
    qre, qim = jnp.swapaxes(mre, 1, 2), -jnp.swapaxes(mim, 1, 2)
    m_inv = jnp.concatenate([jnp.concatenate([qre, -qim], axis=2), jnp.concatenate([qim, qre], axis=2)], axis=1)
    f_fwd = jnp.concatenate([c1, -s1], axis=0)
    f_inv = jnp.concatenate([c1, -s1], axis=1)
    return f_fwd, f_inv, m_fwd.astype(BF16), m_inv.astype(BF16)


def _two_sided_long_conv(z, h_fwd, h_bwd):
    b, l, d = z.shape
    n = 2 * l
    n1 = max(DFT_MIN_OUTER, 2 ** ((int(math.log2(n)) - 1) // 2))
    n2 = n // n1
    f_fwd, f_inv, m_fwd, m_inv = _dft_tables(n1, n2)
    filt = jnp.concatenate([h_fwd, jnp.zeros_like(h_fwd[:1]), h_bwd[1:][::-1]], axis=0) * (1.0 / n)
    blk = pl.BlockSpec((1, 2, 1, n2, d), lambda i, j: (i, 0, j, 0, 0))
    gblk = pl.BlockSpec((1, 2, 1, n2, d), lambda i, j: (0, 0, j, 0, 0))
    mat = pl.BlockSpec((1, 2 * n2, 2 * n2), lambda i, j: (j, 0, 0))
    ga = _colmm(f_fwd, filt.reshape(1, n1, n2 * d)).reshape(1, 2, n1, n2, d)
    gspec = pl.pallas_call(
        _dft_inner_fwd_body,
        grid=(1, n1),
        in_specs=[blk, mat],
        out_specs=blk,
        out_shape=jax.ShapeDtypeStruct((1, 2, n1, n2, d), F32),
        compiler_params=_cparams("parallel", "parallel"),
        name="dft_inner_fwd",
    )(ga, m_fwd)
    za = _colmm(f_fwd[:, :n1 // 2], z.reshape(b, n1 // 2, n2 * d)).reshape(b, 2, n1, n2, d)
    bb = pl.pallas_call(
        _dft_inner_conv_body,
        grid=(b, n1),
        in_specs=[blk, gblk, mat, mat],
        out_specs=blk,
        out_shape=jax.ShapeDtypeStruct((b, 2, n1, n2, d), F32),
        compiler_params=_cparams("parallel", "parallel"),
        name="dft_inner_conv",
    )(za, gspec, m_fwd, m_inv)
    return _colmm(f_inv[:n1 // 2], bb.reshape(b, 2 * n1, n2 * d)).reshape(b, l, d)


def _hyena_stream(h, p):
    b, l, d = h.shape
    u = _linear(h.reshape(b * l, d).astype(BF16), p['w_in'], p['b_in'], tn=1024).reshape(b, l, 3 * d)
    up = jnp.pad(u, ((0, 0), (1, 1), (0, 0)))
    sw = p['short_w']
    u = up[:, :-2] * sw[0] + up[:, 1:-1] * sw[1] + up[:, 2:] * sw[2] + p['short_b']
    x0, x1, v = jnp.split(u, 3, axis=-1)
    filt = _hyena_filters(l, p)
    z = v * x1
    z = _two_sided_long_conv(z, filt[:, :d], filt[:, d:]) + z * p['bias']
    return _linear((x0 * z).reshape(b * l, d).astype(BF16), p['w_out'], p['b_out']).reshape(b, l, d)


def _attn_body(q_ref, k_ref, v_ref, o_ref, m_ref, l_ref, acc_ref):
    j = pl.program_id(3)

    @pl.when(j == 0)
    def _():
        m_ref[...] = jnp.full_like(m_ref, -jnp.inf)
        l_ref[...] = jnp.zeros_like(l_ref)
        acc_ref[...] = jnp.zeros_like(acc_ref)

    s = lax.dot_general(q_ref[0, 0], k_ref[0, 0], (((1,), (1,)), ((), ())), preferred_element_type=F32)
    m_prev = m_ref[...]
    m_new = jnp.maximum(m_prev, jnp.max(s, axis=-1, keepdims=True))
    alpha = jnp.exp(m_prev - m_new)
    pr = jnp.exp(s - m_new)
    l_ref[...] = alpha * l_ref[...] + jnp.sum(pr, axis=-1, keepdims=True)
    acc_ref[...] = alpha * acc_ref[...] + jnp.dot(pr.astype(BF16), v_ref[0], preferred_element_type=F32)
    m_ref[...] = m_new

    @pl.when(j == pl.num_programs(3) - 1)
    def _():
        o_ref[0] = acc_ref[...] / l_ref[...]


def _attend(q, k, v):
    b, nh, sq, dq = q.shape
    sk = k.shape[2]
    dv = v.shape[2] // nh
    tq = _tile(sq, (512, 256, 128, 64, 32, 16, 8))
    tk = _tile(sk, (1280, 1024, 640, 512, 256, 128))
    return pl.pallas_call(
        _attn_body,
        grid=(b, nh, sq // tq, sk // tk),
        in_specs=[pl.BlockSpec((1, 1, tq, dq), lambda bi, hi, i, j: (bi, hi, i, 0)),
                  pl.BlockSpec((1, 1, tk, dq), lambda bi, hi, i, j: (bi, hi, j, 0)),
                  pl.BlockSpec((1, tk, dv), lambda bi, hi, i, j: (bi, j, hi))],
        out_specs=pl.BlockSpec((1, tq, dv), lambda bi, hi, i, j: (bi, i, hi)),
        out_shape=jax.ShapeDtypeStruct((b, sq, nh * dv), F32),
        scratch_shapes=[pltpu.VMEM((tq, 1), F32), pltpu.VMEM((tq, 1), F32), pltpu.VMEM((tq, dv), F32)],
        compiler_params=_cparams("parallel", "parallel", "parallel", "arbitrary"),
        name="attention",
    )(q.astype(BF16), k.astype(BF16), v.astype(BF16))


def _rms(x, g):
    return x * lax.rsqrt(jnp.mean(x * x, axis=-1, keepdims=True) + NORM_EPS) * g


def _rope_1d(x, pos):
    d = x.shape[-1]
    inv = ROPE_THETA ** (-jnp.arange(0, d, 2, dtype=F32) / d)
    ang = pos.astype(F32)[:, None] * inv[None, :]
    cos = jnp.cos(ang)[:, None, :]
    sin = jnp.sin(ang)[:, None, :]
    x1, x2 = jnp.split(x, 2, axis=-1)
    return jnp.concatenate([x1 * cos - x2 * sin, x1 * sin + x2 * cos], axis=-1)


def _axial_rope(x, row, col):
    half = x.shape[-1] // 2
    return jnp.concatenate([_rope_1d(x[..., :half], row), _rope_1d(x[..., half:], col)], axis=-1)


def _mla_mixer(h_lat, h_ctx, p, rope_pos, need_ctx):
    b, l_ctx, d = h_ctx.shape
    l_lat = h_lat.shape[1]
    l = l_ctx + l_lat
    hh = jnp.concatenate([h_ctx, h_lat], axis=1).reshape(b * l, d).astype(BF16)
    qkv = _linear(hh, p['w_a'])
    q_c, kv_c, k_pe = jnp.split(qkv, [Q_LORA, Q_LORA + KV_LORA], axis=-1)
    q = _linear(_rms(q_c, p['q_norm_g']).astype(BF16), p['w_qb']).reshape(b, l, MLA_HEADS, QK_HEAD)
    kv = _linear(_rms(kv_c, p['kv_norm_g']).astype(BF16), p['w_kvb']).reshape(b, l, MLA_HEADS, QK_NOPE + V_HEAD)
    k_nope, v = kv[..., :QK_NOPE], kv[..., QK_NOPE:]
    k = jnp.concatenate([k_nope, jnp.broadcast_to(k_pe.reshape(b, l, 1, QK_ROPE), (b, l, MLA_HEADS, QK_ROPE))], axis=-1)
    q = _rms(q, p['qn_g'])
    k = _rms(k, p['kn_g'])
    row, col = rope_pos

    def rope_lat(t):
        t_l = t[:, l_ctx:]
        t_l = jnp.concatenate([t_l[..., :QK_NOPE], _axial_rope(t_l[..., QK_NOPE:], row, col)], axis=-1)
        return jnp.concatenate([t[:, :l_ctx], t_l], axis=1)

    q = jnp.swapaxes(rope_lat(q) * (QK_HEAD ** -0.5), 1, 2)
    k = jnp.swapaxes(rope_lat(k), 1, 2)
    v = v.reshape(b, l, MLA_HEADS * V_HEAD)
    o_lat = _attend(q[:, :, l_ctx:], k, v)
    if need_ctx:
        o_ctx = _attend(q[:, :, :l_ctx], k[:, :, :l_ctx], v[:, :l_ctx])
        o = jnp.concatenate([o_ctx, o_lat], axis=1).reshape(b * l, MLA_HEADS * V_HEAD)
        out = _linear(o.astype(BF16), p['w_o']).reshape(b, l, d)
        return out[:, :l_ctx], out[:, l_ctx:]
    out = _linear(o_lat.reshape(b * l_lat, MLA_HEADS * V_HEAD).astype(BF16), p['w_o']).reshape(b, l_lat, d)
    return None, out


def _router_body(h_ref, w_ref, b_ref, idx_ref, gate_ref, rank_ref, cnt_ref, run_ref):
    @pl.when(pl.program_id(0) == 0)
    def _():
        run_ref[...] = jnp.zeros_like(run_ref)

    lg = jnp.dot(h_ref[...], w_ref[...], precision=lax.Precision.HIGHEST, preferred_element_type=F32) + b_ref[...]
    tm = lg.shape[0]
    lane = lax.broadcasted_iota(jnp.int32, lg.shape, 1).astype(F32)
    idx_acc = jnp.zeros(lg.shape, F32)
    val_acc = jnp.zeros(lg.shape, F32)
    picked = jnp.zeros(lg.shape, F32)
    sels = []
    top = None
    den = None
    for j in range(TOP_K):
        mx = jnp.max(lg, axis=-1, keepdims=True)
        sel = jnp.min(jnp.where(lg == mx, lane, float(ROUTER_LANES)), axis=-1, keepdims=True)
        if j == 0:
            top = mx
        e = jnp.exp(mx - top)
        den = e if den is None else den + e
        hit = lane == sel
        sels.append(hit)
        idx_acc = jnp.where(lane == float(j), sel, idx_acc)
        val_acc = jnp.where(lane == float(j), e, val_acc)
        picked = jnp.where(hit, 1.0, picked)
        lg = jnp.where(hit, -jnp.inf, lg)
    idx_ref[...] = idx_acc.astype(jnp.int32)
    gate_ref[...] = val_acc / den
    rs = lax.broadcasted_iota(jnp.int32, (tm, tm), 0)
    cs = lax.broadcasted_iota(jnp.int32, (tm, tm), 1)
    before = jnp.where(rs > cs, 1.0, 0.0).astype(BF16)
    base = run_ref[...] + jnp.dot(before, picked.astype(BF16), preferred_element_type=F32)
    rank_acc = jnp.zeros(lg.shape, F32)
    for j in range(TOP_K):
        rk = jnp.sum(jnp.where(sels[j], base, 0.0), axis=-1, keepdims=True)
        rank_acc = jnp.where(lane == float(j), rk, rank_acc)
    rank_ref[...] = rank_acc.astype(jnp.int32)
    run_ref[...] = run_ref[...] + jnp.sum(picked, axis=0, keepdims=True)
    cnt_ref[...] = run_ref[...].astype(jnp.int32)


def _router(h, router_w, router_b):
    t, d = h.shape
    tm = _tile(t, (512, 256, 128, 64, 32, 16, 8))
    w = jnp.concatenate([router_w, jnp.zeros((d, ROUTER_LANES - N_EXPERTS), F32)], axis=1)
    bias = jnp.concatenate([router_b, jnp.full((ROUTER_LANES - N_EXPERTS,), -1e30, F32)]).reshape(1, ROUTER_LANES)
    tok = pl.BlockSpec((tm, ROUTER_LANES), lambda i: (i, 0))
    return pl.pallas_call(
        _router_body,
        grid=(t // tm,),
        in_specs=[pl.BlockSpec((tm, d), lambda i: (i, 0)),
                  pl.BlockSpec((d, ROUTER_LANES), lambda i: (0, 0)),
                  pl.BlockSpec((1, ROUTER_LANES), lambda i: (0, 0))],
        out_specs=[tok, tok, tok, pl.BlockSpec((1, ROUTER_LANES), lambda i: (0, 0))],
        out_shape=[jax.ShapeDtypeStruct((t, ROUTER_LANES), jnp.int32), jax.ShapeDtypeStruct((t, ROUTER_LANES), F32),
                   jax.ShapeDtypeStruct((t, ROUTER_LANES), jnp.int32), jax.ShapeDtypeStruct((1, ROUTER_LANES), jnp.int32)],
        scratch_shapes=[pltpu.VMEM((1, ROUTER_LANES), F32)],
        compiler_params=_cparams("arbitrary"),
        name="router",
    )(h, w, bias)


def _expert_body(be_ref, nb_ref, x_ref, win_ref, bin_ref, wout_ref, bout_ref, gate_ref, o_ref, win_bf, wout_bf):
    i = pl.program_id(0)
    fresh = jnp.logical_or(i == 0, be_ref[i] != be_ref[jnp.maximum(i - 1, 0)])

    @pl.when(fresh)
    def _():
        win_bf[...] = win_ref[0].astype(BF16)
        wout_bf[...] = wout_ref[0].astype(BF16)

    @pl.when(i < nb_ref[0])
    def _():
        gu = jnp.dot(x_ref[...], win_bf[...], preferred_element_type=F32) + bin_ref[0]
        glu = jnp.minimum(gu[:, :D_EXPERT], SWIGLU_LIMIT)
        lin = jnp.clip(gu[:, D_EXPERT:], -SWIGLU_LIMIT, SWIGLU_LIMIT)
        act = glu * jax.nn.sigmoid(SWIGLU_ALPHA * glu) * (lin + 1.0)
        y = jnp.dot(act.astype(BF16), wout_bf[...], preferred_element_type=F32) + bout_ref[0]
        o_ref[...] = (y * gate_ref[...]).astype(o_ref.dtype)

    @pl.when(i >= nb_ref[0])
    def _():
        o_ref[...] = jnp.zeros_like(o_ref)


def _moe(h, router_w, router_b, w_in, b_in, w_out, b_out):
    t, d = h.shape
    n_assign = t * TOP_K
    idx, gates, ranks, counts = _router(h, router_w, router_b)
    flat_e = idx[:, :TOP_K].reshape(n_assign)
    flat_g = gates[:, :TOP_K].reshape(n_assign)
    rank = ranks[:, :TOP_K].reshape(n_assign)
    counts = counts[0, :N_EXPERTS]
    padded = (counts + MOE_ROWS - 1) // MOE_ROWS * MOE_ROWS
    pad_end = jnp.cumsum(padded)
    pad_start = pad_end - padded
    dest = pad_start[flat_e] + rank
    n_blocks = -(-n_assign // MOE_ROWS) + N_EXPERTS
    n_rows = n_blocks * MOE_ROWS
    row_src = jnp.full((n_rows,), -1, jnp.int32).at[dest].set(jnp.arange(n_assign, dtype=jnp.int32))
    row_live = row_src >= 0
    row_src = jnp.maximum(row_src, 0)
    row_tok = row_src // TOP_K
    row_gate = jnp.where(row_live, flat_g[row_src], 0.0)
    block_start = jnp.arange(n_blocks, dtype=jnp.int32) * MOE_ROWS
    block_e = jnp.minimum(jnp.searchsorted(pad_end, block_start, side='right'), N_EXPERTS - 1).astype(jnp.int32)
    n_used = (pad_end[-1] // MOE_ROWS).astype(jnp.int32).reshape(1)
    xb = h.astype(BF16)[row_tok]
    f2 = 2 * D_EXPERT
    grid_spec = pltpu.PrefetchScalarGridSpec(
        num_scalar_prefetch=2,
        grid=(n_blocks,),
        in_specs=[pl.BlockSpec((MOE_ROWS, d), lambda i, be, nb: (i, 0)),
                  pl.BlockSpec((1, d, f2), lambda i, be, nb: (be[i], 0, 0)),
                  pl.BlockSpec((1, 1, f2), lambda i, be, nb: (be[i], 0, 0)),
                  pl.BlockSpec((1, D_EXPERT, d), lambda i, be, nb: (be[i], 0, 0)),
                  pl.BlockSpec((1, 1, d), lambda i, be, nb: (be[i], 0, 0)),
                  pl.BlockSpec((MOE_ROWS, 1), lambda i, be, nb: (i, 0))],
        out_specs=pl.BlockSpec((MOE_ROWS, d), lambda i, be, nb: (i, 0)),
        scratch_shapes=[pltpu.VMEM((d, f2), BF16), pltpu.VMEM((D_EXPERT, d), BF16)],
    )
    yb = pl.pallas_call(
        _expert_body,
        grid_spec=grid_spec,
        out_shape=jax.ShapeDtypeStruct((n_rows, d), BF16),
        compiler_params=_cparams("arbitrary"),
        name="moe_experts",
    )(block_e, n_used, xb, w_in, b_in.reshape(N_EXPERTS, 1, f2),
      w_out, b_out.reshape(N_EXPERTS, 1, d), row_gate.reshape(n_rows, 1))
    dest = dest.reshape(t, TOP_K)
    return sum(yb[dest[:, j]].astype(F32) for j in range(TOP_K))


def kernel(x, c, ctx, c_ctx, ada_w, ada_b, norm_mix_g, norm_ffn_g, rwkv_mu, rwkv_w_rkv, rwkv_w0, rwkv_w1, rwkv_w2, rwkv_a0, rwkv_a1, rwkv_a2, rwkv_g1, rwkv_g2, rwkv_k_k, rwkv_k_a, rwkv_r_k, rwkv_ln_g, rwkv_ln_b, rwkv_w_o, rwkv_v0, rwkv_v1, rwkv_v2, hy_w_in, hy_b_in, hy_short_w, hy_short_b, hy_f_w1, hy_f_b1, hy_f_w2, hy_f_b2, hy_f_w3, hy_f_b3, hy_f_freq, hy_f_wout, hy_bias, hy_w_out, hy_b_out, mla_w_a, mla_q_norm_g, mla_kv_norm_g, mla_w_qb, mla_w_kvb, mla_qn_g, mla_kn_g, mla_w_o, moe_router_w, moe_router_b, moe_w_in, moe_b_in, moe_w_out, moe_b_out):
    bsz, s_len, d = x.shape
    l_ctx = ctx.shape[1]
    rows = s_len // GRID_W
    row = jnp.repeat(jnp.arange(rows, dtype=jnp.int32), GRID_W)
    col = jnp.tile(jnp.arange(GRID_W, dtype=jnp.int32), rows)
    silu = jnp.concatenate([jax.nn.silu(c), jax.nn.silu(c_ctx)[None, :]], axis=0)
    silu = jnp.concatenate([silu, jnp.zeros((-(bsz + 1) % 8, d), F32)], axis=0)
    xc = ctx
    v_first = None
    for i in range(DEPTH):
        need_ctx = i < DEPTH - 1
        j = i // N_MIXERS
        kind = i % N_MIXERS
        mod = _linear(silu, ada_w[i], ada_b[i], tn=1024)
        mod_l = jnp.split(mod[:bsz], N_MOD, axis=-1)
        mod_c = [jnp.broadcast_to(m, (bsz, d)) for m in jnp.split(mod[bsz:bsz + 1], N_MOD, axis=-1)]
        sh1, sc1, g1, sh2, sc2, g2 = [m[:, None, :] for m in mod_l]
        csh1, csc1, cg1, csh2, csc2, cg2 = [m[:, None, :] for m in mod_c]
        h_l = _norm_mod(x, norm_mix_g[i], sc1, sh1)
        h_c = _norm_mod(xc, norm_mix_g[i], csc1, csh1)
        if kind == 0:
            p = {'mu': rwkv_mu[j], 'w_rkv': rwkv_w_rkv[j], 'w0': rwkv_w0[j], 'w1': rwkv_w1[j],
                 'w2': rwkv_w2[j], 'a0': rwkv_a0[j], 'a1': rwkv_a1[j], 'a2': rwkv_a2[j],
                 'g1': rwkv_g1[j], 'g2': rwkv_g2[j], 'k_k': rwkv_k_k[j], 'k_a': rwkv_k_a[j],
                 'r_k': rwkv_r_k[j], 'ln_g': rwkv_ln_g[j], 'ln_b': rwkv_ln_b[j], 'w_o': rwkv_w_o[j]}
            vres = None if j == 0 else (rwkv_v0[j - 1], rwkv_v1[j - 1], rwkv_v2[j - 1])
            y_c, y_l, v_cur = _rwkv_mixer(h_l, h_c, p, v_first, vres, need_ctx)
            if j == 0:
                v_first = v_cur
        elif kind == 1:
            p = {'w_in': hy_w_in[j], 'b_in': hy_b_in[j], 'short_w': hy_short_w[j], 'short_b': hy_short_b[j],
                 'f_w1': hy_f_w1[j], 'f_b1': hy_f_b1[j], 'f_w2': hy_f_w2[j], 'f_b2': hy_f_b2[j],
                 'f_w3': hy_f_w3[j], 'f_b3': hy_f_b3[j], 'f_freq': hy_f_freq[j], 'f_wout': hy_f_wout[j],
                 'bias': hy_bias[j], 'w_out': hy_w_out[j], 'b_out': hy_b_out[j]}
            y_l = _hyena_stream(h_l, p)
            y_c = _hyena_stream(h_c, p) if need_ctx else None
        else:
            p = {'w_a': mla_w_a[j], 'q_norm_g': mla_q_norm_g[j], 'kv_norm_g': mla_kv_norm_g[j],
                 'w_qb': mla_w_qb[j], 'w_kvb': mla_w_kvb[j], 'qn_g': mla_qn_g[j], 'kn_g': mla_kn_g[j],
                 'w_o': mla_w_o[j]}
            y_c, y_l = _mla_mixer(h_l, h_c, p, (row, col), need_ctx)
        x = x + g1 * y_l
        h2_l = _norm_mod(x, norm_ffn_g[i], sc2, sh2)
        moe_p = (moe_router_w[i], moe_router_b[i], moe_w_in[i], moe_b_in[i], moe_w_out[i], moe_b_out[i])
        if need_ctx:
            xc = xc + cg1 * y_c
            h2_c = _norm_mod(xc, norm_ffn_g[i], csc2, csh2)
            n_c = bsz * l_ctx
            f = _moe(jnp.concatenate([h2_c.reshape(-1, d), h2_l.reshape(-1, d)], axis=0), *moe_p)
            xc = xc + cg2 * f[:n_c].reshape(xc.shape)
            x = x + g2 * f[n_c:].reshape(x.shape)
        else:
            x = x + g2 * _moe(h2_l.reshape(-1, d), *moe_p).reshape(x.shape)
    return x
```

```python
import functools
import math

import jax
import jax.numpy as jnp
import numpy as np
from jax import lax
from jax.experimental import pallas as pl
from jax.experimental.pallas import tpu as pltpu

F32 = jnp.float32
BF16 = jnp.bfloat16

D_MODEL = 1024
DEPTH = 4
GRID_W = 64
N_MIXERS = 3
N_MOD = 6
NORM_EPS = 1e-6
RWKV_HEAD = 64
RWKV_HEADS = D_MODEL // RWKV_HEAD
GN_EPS = 64e-5
WKV_CHUNK = 64
WKV_GROUP = 4
FILTER_EMB = 33
FILTER_EMB_PAD = 128
DFT_MIN_OUTER = 32
FAST_DECAY = math.log(1e-2) / 0.3
SLOW_DECAY = math.log(1e-2) / 1.5
MLA_HEADS = 8
QK_NOPE = 128
QK_ROPE = 64
QK_HEAD = QK_NOPE + QK_ROPE
V_HEAD = 128
Q_LORA = 384
KV_LORA = 256
ROPE_THETA = 10000.0
N_EXPERTS = 32
TOP_K = 4
D_EXPERT = 1024
SWIGLU_LIMIT = 7.0
SWIGLU_ALPHA = 1.702
MOE_ROWS = 256
LANES = 128
ROUTER_LANES = LANES
ATTN_SLAB = 512
LOG2_E = 1.4426950408889634

V7X_VMEM_BYTES = 64 * 1024 * 1024
VMEM_LIMIT = V7X_VMEM_BYTES * 3 // 4


def _cparams(*sem):
    return pltpu.CompilerParams(dimension_semantics=sem, vmem_limit_bytes=VMEM_LIMIT)


def _tile(n, prefs):
    for t in prefs:
        if n % t == 0:
            return t
    return n


def _linear_body(a_ref, w_ref, b_ref, o_ref, *, in_act):
    a = a_ref[...]
    if in_act == 'tanh':
        a = jnp.tanh(a.astype(F32))
    elif in_act == 'sigmoid':
        a = jax.nn.sigmoid(a.astype(F32))
    acc = jnp.dot(a.astype(BF16), w_ref[...], preferred_element_type=F32)
    o_ref[...] = (acc + b_ref[...]).astype(o_ref.dtype)


def _linear(a, w, b=None, *, in_act=None, out_dtype=F32, tn=None):
    m, k = a.shape
    n = w.shape[1]
    tm = _tile(m, (512, 256, 128, 64, 32, 16, 8))
    tn = n if tn is None else tn
    assert n % tn == 0
    if b is None:
        b = jnp.zeros((n,), F32)
    return pl.pallas_call(
        functools.partial(_linear_body, in_act=in_act),
        grid=(n // tn, m // tm),
        in_specs=[pl.BlockSpec((tm, k), lambda j, i: (i, 0)),
                  pl.BlockSpec((k, tn), lambda j, i: (0, j)),
                  pl.BlockSpec((1, tn), lambda j, i: (0, j))],
        out_specs=pl.BlockSpec((tm, tn), lambda j, i: (i, j)),
        out_shape=jax.ShapeDtypeStruct((m, n), out_dtype),
        compiler_params=_cparams("parallel", "parallel"),
        name="linear",
    )(a, w.astype(BF16), b.reshape(1, n).astype(F32))


def _norm_mod_body(x_ref, g_ref, sc_ref, sh_ref, o_ref):
    x = x_ref[0]
    y = x * lax.rsqrt(jnp.mean(x * x, axis=-1, keepdims=True) + NORM_EPS)
    o_ref[0] = ((y * g_ref[...]) * (1.0 + sc_ref[0]) + sh_ref[0]).astype(o_ref.dtype)


def _norm_mod(x, g, sc, sh):
    b, l, d = x.shape
    tl = _tile(l, (512, 256, 128, 64, 32, 16, 8))
    return pl.pallas_call(
        _norm_mod_body,
        grid=(b, l // tl),
        in_specs=[pl.BlockSpec((1, tl, d), lambda i, j: (i, j, 0)),
                  pl.BlockSpec((1, d), lambda i, j: (0, 0)),
                  pl.BlockSpec((1, 1, d), lambda i, j: (i, 0, 0)),
                  pl.BlockSpec((1, 1, d), lambda i, j: (i, 0, 0))],
        out_specs=pl.BlockSpec((1, tl, d), lambda i, j: (i, j, 0)),
        out_shape=jax.ShapeDtypeStruct((b, l, d), F32),
        compiler_params=_cparams("parallel", "parallel"),
        name="norm_mod",
    )(x, g.reshape(1, d), sc.reshape(b, 1, d), sh.reshape(b, 1, d))


PROJ_R, PROJ_K, PROJ_V = 0, 1024, 2048
PROJ_W1 = (3072, 3200)
PROJ_A1 = (3328, 3456)
PROJ_G1 = 3584
PROJ_V1 = 3840
PROJ_WIDTH = 4096
LORA_PAD = 128
GATE_PAD = 256


def _sigmoid(x):
    return jax.nn.sigmoid(x)


def _rwkv_post_body(*refs, has_vres):
    if has_vres:
        (proj_ref, vf_ref, w2_ref, a2_ref, g2_ref, v2_ref, vec_ref, gs_ref,
         kk_out, v_out, gate_out, lw_out, ag_out, kd_out) = refs
    else:
        (proj_ref, w2_ref, a2_ref, g2_ref, vec_ref, gs_ref,
         kk_out, v_out, gate_out, lw_out, ag_out, kd_out) = refs
    d = D_MODEL
    vec = vec_ref[...]
    k = proj_ref[:, PROJ_K:PROJ_K + d].astype(F32)
    v = proj_ref[:, PROJ_V:PROJ_V + d].astype(F32)
    if has_vres:
        v1 = proj_ref[:, PROJ_V1:PROJ_V1 + LORA_PAD]
        vgate = _sigmoid(vec[6:7] + jnp.dot(v1, v2_ref[...], preferred_element_type=F32))
        v = v + (vf_ref[...].astype(F32) - v) * vgate
    v_out[...] = v.astype(v_out.dtype)
    kraw = k * vec[4:5]
    ss = jnp.dot((kraw * kraw).astype(BF16), gs_ref[...], preferred_element_type=F32)
    kk_out[...] = (kraw / jnp.maximum(jnp.sqrt(ss), 1e-12)).astype(kk_out.dtype)
    g1 = proj_ref[:, PROJ_G1:PROJ_G1 + GATE_PAD].astype(F32)
    gate_out[...] = jnp.dot(_sigmoid(g1).astype(BF16), g2_ref[...], preferred_element_type=F32).astype(gate_out.dtype)
    for dd in range(2):
        w1 = proj_ref[:, PROJ_W1[dd]:PROJ_W1[dd] + LORA_PAD].astype(F32)
        z = vec[dd:dd + 1] + jnp.dot(jnp.tanh(w1).astype(BF16), w2_ref[dd], preferred_element_type=F32)
        softplus = jnp.maximum(-z, 0.0) + jnp.log(1.0 + jnp.exp(-jnp.abs(z)))
        lw_out[dd] = -jnp.exp(-softplus - 0.5)
        a1 = proj_ref[:, PROJ_A1[dd]:PROJ_A1[dd] + LORA_PAD]
        ag = _sigmoid(vec[2 + dd:3 + dd] + jnp.dot(a1, a2_ref[dd], preferred_element_type=F32))
        ag_out[dd] = ag.astype(ag_out.dtype)
        kd_out[dd] = (k * (1.0 + (ag - 1.0) * vec[5:6])).astype(kd_out.dtype)


def _rwkv_post(proj, v_first, w2p, a2p, g2p, v2p, vec, gsum):
    t, d = proj.shape[0], D_MODEL
    tm = _tile(t, (256, 128, 64, 32, 16))
    has_vres = v_first is not None
    full = lambda a: pl.BlockSpec(a.shape, lambda i: (0,) * a.ndim)
    row = pl.BlockSpec((tm, d), lambda i: (i, 0))
    row2 = pl.BlockSpec((2, tm, d), lambda i: (0, i, 0))
    ins = [proj] + ([v_first] if has_vres else []) + [w2p, a2p, g2p] + ([v2p] if has_vres else []) + [vec, gsum]
    in_specs = ([pl.BlockSpec((tm, PROJ_WIDTH), lambda i: (i, 0))] + ([row] if has_vres else [])
                + [full(a) for a in ins[(2 if has_vres else 1):]])
    return pl.pallas_call(
        functools.partial(_rwkv_post_body, has_vres=has_vres),
        grid=(t // tm,),
        in_specs=in_specs,
        out_specs=[row, row, row, row2, row2, row2],
        out_shape=[jax.ShapeDtypeStruct((t, d), BF16),
                   jax.ShapeDtypeStruct((t, d), BF16),
                   jax.ShapeDtypeStruct((t, d), BF16),
                   jax.ShapeDtypeStruct((2, t, d), F32),
                   jax.ShapeDtypeStruct((2, t, d), BF16),
                   jax.ShapeDtypeStruct((2, t, d), BF16)],
        compiler_params=_cparams("parallel"),
        name="rwkv_post",
    )(*ins)


def _head_groups(y):
    gw = WKV_GROUP * RWKV_HEAD
    lane_head = jnp.right_shift(lax.broadcasted_iota(jnp.int32, (y.shape[0], gw), 1), 6)
    out = []
    for g in range(D_MODEL // gw):
        yg = y[:, g * gw:(g + 1) * gw].astype(BF16)
        out.append(jnp.concatenate([jnp.where(lane_head == h, yg, jnp.zeros_like(yg)) for h in range(WKV_GROUP)], axis=0))
    return out


def _gmm(x, ybd):
    gw = WKV_GROUP * RWKV_HEAD
    return jnp.concatenate([jnp.dot(x[:, g * gw:(g + 1) * gw].astype(BF16), yb, preferred_element_type=F32)
                            for g, yb in enumerate(ybd)], axis=-1)


def _gmm_nt(x, ybd):
    gw = WKV_GROUP * RWKV_HEAD
    return jnp.concatenate([lax.dot_general(x[:, g * gw:(g + 1) * gw].astype(BF16), yb, (((1,), (1,)), ((), ())),
                                            preferred_element_type=F32)
                            for g, yb in enumerate(ybd)], axis=-1)


def _wkv_chunks(r, kk, v, lw, ag, kd, s0, rev):
    each = lambda f, *ls: [f(*a) for a in zip(*ls)]
    c, d = r[0].shape
    row0 = lax.broadcasted_iota(jnp.int32, (c, d), 0)
    col0 = jnp.bitwise_and(lax.broadcasted_iota(jnp.int32, (c, d), 1), RWKV_HEAD - 1)
    incl = [(row0 <= col0) if rv else (row0 >= col0) for rv in rev]
    strict = [(row0 < col0) if rv else (row0 > col0) for rv in rev]
    rs = lax.broadcasted_iota(jnp.int32, (c, c), 0)
    cs = lax.broadcasted_iota(jnp.int32, (c, c), 1)
    tri = [jnp.where((rs <= cs) if rv else (rs >= cs), 1.0, 0.0).astype(BF16) for rv in rev]
    hi = each(lambda x: x.astype(BF16), lw)
    rem = each(lambda x, h: x - h.astype(F32), lw, hi)
    mid = each(lambda x: x.astype(BF16), rem)
    lo = each(lambda x, m: (x - m.astype(F32)).astype(BF16), rem, mid)
    cum = each(lambda t, a, b_, c_: (jnp.dot(t, a, preferred_element_type=F32) + jnp.dot(t, b_, preferred_element_type=F32)
                                     + jnp.dot(t, c_, preferred_element_type=F32)), tri, hi, mid, lo)
    cum_end = [x[0:1, :] if rv else x[c - 1:c, :] for x, rv in zip(cum, rev)]
    g_rest = each(lambda e, x: jnp.exp(e - x), cum_end, cum)
    g_inv = each(lambda x: jnp.exp(-x), cum)
    bvec = each(lambda a, b_: a * b_, kk, ag)
    rt = each(lambda a, x: a * jnp.exp(x), r, cum)
    kt = each(lambda a, g: a * g, kd, g_inv)
    bt = each(lambda a, g: a * g, bvec, g_inv)
    at = each(lambda a, x, w: -a * jnp.exp(x - w), kk, cum, lw)
    ar = each(lambda a, b_: jnp.concatenate([a, b_], axis=0), at, rt)
    ab = each(lambda a, b_: _gmm_nt(a, _head_groups(b_)), ar, bt)
    ak = each(lambda a, b_: _gmm_nt(a, _head_groups(b_)), ar, kt)
    a_ab = each(lambda m, x: jnp.where(m, x[:c], 0.0), strict, ab)
    a_rb = each(lambda m, x: jnp.where(m, x[c:], 0.0), incl, ab)
    a_ak = each(lambda m, x: jnp.where(m, x[:c], 0.0), strict, ak)
    a_rk = each(lambda m, x: jnp.where(m, x[c:], 0.0), incl, ak)
    same16 = jnp.right_shift(row0, 4) == jnp.right_shift(col0, 4)
    same32 = jnp.right_shift(row0, 5) == jnp.right_shift(col0, 5)
    eye = jnp.where(row0 == col0, 1.0, 0.0)
    npow = each(lambda x: jnp.where(same16, x, 0.0), a_ab)
    p = each(lambda x: eye + x, npow)
    for _ in range(3):
        npow = each(lambda x: _gmm(x, _head_groups(x)), npow)
        p = each(lambda x, y: x + _gmm(x, _head_groups(y)), p, npow)
    off32 = jnp.logical_and(same32, jnp.logical_not(same16))
    t1 = each(lambda x, y: _gmm(jnp.where(off32, x, 0.0), _head_groups(y)), a_ab, p)
    p = each(lambda x, y: x + _gmm(x, _head_groups(y)), p, t1)
    t2 = each(lambda x, y: _gmm(jnp.where(same32, 0.0, x), _head_groups(y)), a_ab, p)
    p = each(lambda x, y: x + _gmm(x, _head_groups(y)), p, t2)
    pbd = each(_head_groups, p)
    vbd = each(_head_groups, v)
    wmat = each(lambda x, y: _gmm(x, _head_groups(y)), p, at)
    vt = each(lambda y: _gmm_nt(eye, y), vbd)
    xt = each(lambda x, y: _gmm_nt(x, _head_groups(y)), vt, a_ak)
    upt = each(_gmm_nt, xt, pbd)
    yp = each(_gmm, a_rk, vbd)
    sp = each(lambda x, a, g: _gmm(x, _head_groups(a * g)), vt, kd, g_rest)
    ut = each(lambda s_, w, u: _gmm_nt(s_, _head_groups(w)) + u, s0, wmat, upt)
    ys = each(lambda s_, q, x: _gmm_nt(q, _head_groups(s_)) + x, s0, rt, yp)
    y = each(lambda y0, a, u: y0 + _gmm_nt(a, _head_groups(u)), ys, a_rb, ut)
    s_end = each(lambda s_, e, u, b_, g, x: s_ * jnp.exp(e) + _gmm(u, _head_groups(b_ * g)) + x,
                 s0, cum_end, ut, bvec, g_rest, sp)
    return y, s_end


def _wkv_body(*refs, n_batch):
    ins, outs, s_ref = refs[:12], refs[12:14], refs[14]

    @pl.when(pl.program_id(0) == 0)
    def _():
        s_ref[...] = jnp.zeros_like(s_ref)

    scans = [(dd, bi) for dd in range(2) for bi in range(n_batch)]
    get = lambda k, lead: [ins[6 * dd + k][(0, bi) if lead else bi].astype(F32) for dd, bi in scans]
    y, s_end = _wkv_chunks(get(0, False), get(1, False), get(2, False), get(3, True), get(4, True), get(5, True),
                           [s_ref[i] for i in range(len(scans))], [dd == 1 for dd, _ in scans])
    for i, (dd, bi) in enumerate(scans):
        outs[dd][bi] = y[i]
        s_ref[i] = s_end[i]


def _wkv(proj, kk, v, lw, ag, kd, l_ctx):
    b, l, d = kk.shape
    c = WKV_CHUNK
    nc, nctx = l // c, l_ctx // c
    assert c == RWKV_HEAD and l % c == 0 and l_ctx % c == 0

    def fwd(j):
        return j

    def bwd(j):
        return jnp.where(j < nctx, nctx - 1 - j, nc + nctx - 1 - j)

    in_specs, out_specs = [], []
    for dd, ch in enumerate((fwd, bwd)):
        shared = pl.BlockSpec((b, c, d), lambda j, ch=ch: (0, ch(j), 0))
        per_dir = pl.BlockSpec((1, b, c, d), lambda j, dd=dd, ch=ch: (dd, 0, ch(j), 0))
        in_specs += [shared, shared, shared, per_dir, per_dir, per_dir]
        out_specs.append(shared)
    return pl.pallas_call(
        functools.partial(_wkv_body, n_batch=b),
        grid=(nc,),
        in_specs=in_specs,
        out_specs=out_specs,
        out_shape=[jax.ShapeDtypeStruct((b, l, d), F32)] * 2,
        scratch_shapes=[pltpu.VMEM((2 * b, RWKV_HEAD, d), F32)],
        compiler_params=_cparams("arbitrary"),
        name="wkv",
    )(*([proj, kk, v, lw, ag, kd] * 2))


def _rwkv_readout_body(yf_ref, yr_ref, r_ref, kd_ref, v_ref, gate_ref, gm_ref, gs_ref, vec_ref, wo_ref, o_ref):
    vec = vec_ref[...]
    y = yf_ref[...] + yr_ref[...]
    gm = gm_ref[...]
    yh = y.astype(BF16)
    yl = (y - yh.astype(F32)).astype(BF16)
    mean = jnp.dot(yh, gm, preferred_element_type=F32) + jnp.dot(yl, gm, preferred_element_type=F32)
    yc = y - mean
    var = jnp.dot((yc * yc).astype(BF16), gm, preferred_element_type=F32)
    yn = yc * lax.rsqrt(var + GN_EPS) * vec[0:1] + vec[1:2]
    r = r_ref[...].astype(F32)
    kds = kd_ref[0].astype(F32) + kd_ref[1].astype(F32)
    bonus = jnp.dot((r * kds * vec[2:3]).astype(BF16), gs_ref[...], preferred_element_type=F32)
    out = (yn + bonus * v_ref[...].astype(F32)) * gate_ref[...].astype(F32)
    o_ref[...] = jnp.dot(out.astype(BF16), wo_ref[...], preferred_element_type=F32)


def _rwkv_readout(y_fwd, y_rev, proj, kd, v, gate, gmean, gsum, vec, w_o):
    t, d = v.shape
    tm = _tile(t, (256, 128, 64, 32, 16))
    full = lambda a: pl.BlockSpec(a.shape, lambda i: (0,) * a.ndim)
    row = pl.BlockSpec((tm, d), lambda i: (i, 0))
    row2 = pl.BlockSpec((2, tm, d), lambda i: (0, i, 0))
    return pl.pallas_call(
        _rwkv_readout_body,
        grid=(t // tm,),
        in_specs=[row, row, row, row2, row, row, full(gmean), full(gsum), full(vec), full(w_o)],
        out_specs=row,
        out_shape=jax.ShapeDtypeStruct((t, d), F32),
        compiler_params=_cparams("parallel"),
        name="rwkv_readout",
    )(y_fwd, y_rev, proj, kd, v, gate, gmean, gsum, vec, w_o)


def _centred_shift(x):
    xp = jnp.pad(x, ((0, 0), (1, 1), (0, 0)))
    return 0.5 * (xp[:, :-2] + xp[:, 2:]) - x


def _pad_cols(w, width):
    return jnp.concatenate([w, jnp.zeros((w.shape[0], width - w.shape[1]), w.dtype)], axis=1)


def _pad_rows(w, height):
    return jnp.concatenate([w, jnp.zeros((height - w.shape[0], w.shape[1]), w.dtype)], axis=0)


def _rwkv_mixer(h_lat, h_ctx, p, v_first, vres, need_ctx):
    b, l_ctx, d = h_ctx.shape
    l = l_ctx + h_lat.shape[1]
    t = b * l
    hh = jnp.concatenate([h_ctx, h_lat], axis=1)
    xx = jnp.concatenate([_centred_shift(h_ctx), _centred_shift(h_lat)], axis=1)
    mu = p['mu']
    mats = [(p['w_rkv'][0], 0, d), (p['w_rkv'][1], 2, d), (p['w_rkv'][2], 3, d),
            (p['w1'][0], 1, LORA_PAD), (p['w1'][1], 1, LORA_PAD), (p['a1'][0], 4, LORA_PAD), (p['a1'][1], 4, LORA_PAD),
            (p['g1'], 5, GATE_PAD), (vres[1] if vres is not None else jnp.zeros((d, 1), F32), 3, 2 * LORA_PAD)]
    top = jnp.concatenate([_pad_cols(w, width) for w, _, width in mats], axis=1)
    bot = jnp.concatenate([_pad_cols(mu[m][:, None] * w, width) for w, m, width in mats], axis=1)
    wcat = jnp.concatenate([top, bot], axis=0)
    assert wcat.shape[1] == PROJ_WIDTH
    a_in = jnp.concatenate([hh, xx], axis=-1).reshape(t, 2 * d).astype(BF16)
    proj = _linear(a_in, wcat, out_dtype=BF16, tn=1024)
    w2p = jnp.stack([_pad_rows(p['w2'][i], LORA_PAD) for i in range(2)]).astype(BF16)
    a2p = jnp.stack([_pad_rows(p['a2'][i], LORA_PAD) for i in range(2)]).astype(BF16)
    g2p = _pad_rows(p['g2'], GATE_PAD).astype(BF16)
    head_of = jnp.arange(d, dtype=jnp.int32) // RWKV_HEAD
    same_head = head_of[:, None] == head_of[None, :]
    gsum = jnp.where(same_head, 1.0, 0.0).astype(BF16)
    gmean = jnp.where(same_head, 1.0 / RWKV_HEAD, 0.0).astype(BF16)
    zero = jnp.zeros((d,), F32)
    vec = jnp.stack([p['w0'][0], p['w0'][1], p['a0'][0], p['a0'][1], p['k_k'], p['k_a'],
                     vres[0] if vres is not None else zero, zero])
    if vres is not None:
        kk, v, gate, lw, ag, kd = _rwkv_post(proj, v_first.reshape(t, d), w2p, a2p, g2p,
                                             _pad_rows(vres[2], LORA_PAD).astype(BF16), vec, gsum)
    else:
        kk, v, gate, lw, ag, kd = _rwkv_post(proj, None, w2p, a2p, g2p, None, vec, gsum)
    sh3 = lambda a: a.reshape(a.shape[:-2] + (b, l, a.shape[-1]))
    y_fwd, y_rev = _wkv(sh3(proj), sh3(kk), sh3(v), sh3(lw), sh3(ag), sh3(kd), l_ctx)
    vec_o = jnp.stack([p['ln_g'], p['ln_b'], p['r_k'].reshape(d)] + [zero] * 5)
    out = _rwkv_readout(y_fwd.reshape(t, d), y_rev.reshape(t, d), proj, kd, v, gate, gmean, gsum, vec_o, p['w_o'].astype(BF16))
    out = out.reshape(b, l, d)
    out_ctx = out[:, :l_ctx] if need_ctx else None
    return out_ctx, out[:, l_ctx:], v.reshape(b, l, d)


def _filter_body(z_ref, t_ref, w1_ref, b1_ref, w2_ref, b2_ref, w3_ref, b3_ref, fr_ref, wo_ref, dl_ref, o_ref, *, length):
    hp = lax.Precision.HIGHEST
    fr = fr_ref[...]
    hdn = jnp.sin(fr[0:1] * (jnp.dot(z_ref[...], w1_ref[...], precision=hp, preferred_element_type=F32) + b1_ref[...]))
    hdn = jnp.sin(fr[1:2] * (jnp.dot(hdn, w2_ref[...], precision=hp, preferred_element_type=F32) + b2_ref[...]))
    hdn = jnp.sin(fr[2:3] * (jnp.dot(hdn, w3_ref[...], precision=hp, preferred_element_type=F32) + b3_ref[...]))
    filt = jnp.dot(hdn, wo_ref[...], precision=hp, preferred_element_type=F32) * jnp.exp(-t_ref[...] * dl_ref[...])
    tl = filt.shape[0]
    n = pl.program_id(0) * tl + lax.broadcasted_iota(jnp.int32, filt.shape, 0)
    o_ref[...] = jnp.where(n == length, 0.0, filt * (0.5 / length))


def _hyena_filters(length, p):
    n = jnp.arange(2 * length, dtype=jnp.int32)
    pos = jnp.where(n < length, n, 2 * length - n).astype(F32)[:, None]
    t = pos / (length - 1)
    bands = (FILTER_EMB - 1) // 2
    f = jnp.linspace(1e-4, bands - 1, bands, dtype=F32)[None, :]
    ang = 2.0 * math.pi * f * pos / length
    z = jnp.concatenate([t, jnp.cos(ang), -jnp.sin(ang), jnp.zeros((2 * length, FILTER_EMB_PAD - FILTER_EMB), F32)], axis=-1)
    w1 = jnp.concatenate([p['f_w1'], jnp.zeros((FILTER_EMB_PAD - FILTER_EMB, p['f_w1'].shape[1]), F32)], axis=0)
    deltas = jnp.abs(jnp.linspace(FAST_DECAY, SLOW_DECAY, D_MODEL, dtype=F32))[None, :]
    fw = p['f_w2'].shape[0]
    tl = _tile(length, (512, 256, 128, 64, 32, 16, 8))
    half = length // tl
    full = lambda a: pl.BlockSpec(a.shape, lambda i: (0,) * a.ndim)
    args = [w1, p['f_b1'].reshape(1, fw), p['f_w2'], p['f_b2'].reshape(1, fw), p['f_w3'], p['f_b3'].reshape(1, fw),
            p['f_freq']]
    return pl.pallas_call(
        functools.partial(_filter_body, length=length),
        grid=(2 * half,),
        in_specs=[pl.BlockSpec((tl, FILTER_EMB_PAD), lambda i: (i, 0)), pl.BlockSpec((tl, 1), lambda i: (i, 0))]
                 + [full(a) for a in args]
                 + [pl.BlockSpec((fw, D_MODEL), lambda i: (0, jnp.where(i < half, 0, 1))),
                    full(deltas)],
        out_specs=pl.BlockSpec((tl, D_MODEL), lambda i: (i, 0)),
        out_shape=jax.ShapeDtypeStruct((2 * length, D_MODEL), F32),
        compiler_params=_cparams("parallel"),
        name="hyena_filter",
    )(z, t, *args, p['f_wout'], deltas)


def _colmm_body(f_ref, x_ref, o_ref):
    o_ref[0] = jnp.dot(f_ref[...], x_ref[0].astype(BF16), preferred_element_type=F32)


def _colmm(f, x):
    bsz, q, n = x.shape
    pp = f.shape[0]
    tn = _tile(n, (4096, 2048, 1024, 512, 256, 128))
    return pl.pallas_call(
        _colmm_body,
        grid=(bsz, n // tn),
        in_specs=[pl.BlockSpec((pp, q), lambda i, j: (0, 0)),
                  pl.BlockSpec((1, q, tn), lambda i, j: (i, 0, j))],
        out_specs=pl.BlockSpec((1, pp, tn), lambda i, j: (i, 0, j)),
        out_shape=jax.ShapeDtypeStruct((bsz, pp, n), F32),
        compiler_params=_cparams("parallel", "parallel"),
        name="dft_outer",
    )(f.astype(BF16), x)


def _dft_inner_fwd_body(a_ref, m_ref, o_ref):
    n2 = a_ref.shape[3]
    a = jnp.concatenate([a_ref[0, 0, 0], a_ref[0, 1, 0]], axis=0).astype(BF16)
    x = jnp.dot(m_ref[0], a, preferred_element_type=F32)
    o_ref[0, 0, 0] = x[:n2]
    o_ref[0, 1, 0] = x[n2:]


def _dft_inner_conv_body(a_ref, g_ref, m_ref, mi_ref, o_ref):
    n2 = a_ref.shape[3]
    a = jnp.concatenate([a_ref[0, 0, 0], a_ref[0, 1, 0]], axis=0).astype(BF16)
    x = jnp.dot(m_ref[0], a, preferred_element_type=F32)
    xre, xim = x[:n2], x[n2:]
    gre, gim = g_ref[0, 0, 0], g_ref[0, 1, 0]
    y = jnp.concatenate([xre * gre - xim * gim, xre * gim + xim * gre], axis=0).astype(BF16)
    o = jnp.dot(mi_ref[0], y, preferred_element_type=F32)
    o_ref[0, 0, 0] = o[:n2]
    o_ref[0, 1, 0] = o[n2:]


def _dft_tables(n1, n2):
    n = n1 * n2
    i1 = jnp.arange(n1, dtype=jnp.int32)
    ph1 = (2.0 * math.pi / n1) * ((i1[:, None] * i1[None, :]) % n1).astype(F32)
    c1, s1 = jnp.cos(ph1), jnp.sin(ph1)
    k1 = i1[:, None, None]
    k2 = jnp.arange(n2, dtype=jnp.int32)[None, :, None]
    m2 = jnp.arange(n2, dtype=jnp.int32)[None, None, :]
    th = (2.0 * math.pi / n) * ((m2 * (k1 + n1 * k2)) % n).astype(F32)
    mre, mim = jnp.cos(th), -jnp.sin(th)
    m_fwd = jnp.concatenate([jnp.concatenate([mre, -mim], axis=2), jnp.concatenate([mim, mre], axis=2)], axis=1)
    qre, qim = jnp.swapaxes(mre, 1, 2), -jnp.swapaxes(mim, 1, 2)
    m_inv = jnp.concatenate([jnp.concatenate([qre, -qim], axis=2), jnp.concatenate([qim, qre], axis=2)], axis=1)
    f_fwd = jnp.concatenate([c1, -s1], axis=0)
    f_inv = jnp.concatenate([c1, -s1], axis=1)
    return f_fwd, f_inv, m_fwd.astype(BF16), m_inv.astype(BF16)


def _two_sided_long_conv(z, filt):
    b, l, d = z.shape
    n = 2 * l
    n1 = max(DFT_MIN_OUTER, 2 ** ((int(math.log2(n)) - 1) // 2))
    n2 = n // n1
    f_fwd, f_inv, m_fwd, m_inv = _dft_tables(n1, n2)
    blk = pl.BlockSpec((1, 2, 1, n2, d), lambda i, j: (i, 0, j, 0, 0))
    gblk = pl.BlockSpec((1, 2, 1, n2, d), lambda i, j: (0, 0, j, 0, 0))
    mat = pl.BlockSpec((1, 2 * n2, 2 * n2), lambda i, j: (j, 0, 0))
    ga = _colmm(f_fwd, filt.reshape(1, n1, n2 * d)).reshape(1, 2, n1, n2, d)
    gspec = pl.pallas_call(
        _dft_inner_fwd_body,
        grid=(1, n1),
        in_specs=[blk, mat],
        out_specs=blk,
        out_shape=jax.ShapeDtypeStruct((1, 2, n1, n2, d), F32),
        compiler_params=_cparams("parallel", "parallel"),
        name="dft_inner_fwd",
    )(ga, m_fwd)
    za = _colmm(f_fwd[:, :n1 // 2], z.reshape(b, n1 // 2, n2 * d)).reshape(b, 2, n1, n2, d)
    bb = pl.pallas_call(
        _dft_inner_conv_body,
        grid=(b, n1),
        in_specs=[blk, gblk, mat, mat],
        out_specs=blk,
        out_shape=jax.ShapeDtypeStruct((b, 2, n1, n2, d), F32),
        compiler_params=_cparams("parallel", "parallel"),
        name="dft_inner_conv",
    )(za, gspec, m_fwd, m_inv)
    return _colmm(f_inv[:n1 // 2], bb.reshape(b, 2 * n1, n2 * d)).reshape(b, l, d)


def _hyena_stream(h, p):
    b, l, d = h.shape
    u = _linear(h.reshape(b * l, d).astype(BF16), p['w_in'], p['b_in'], tn=1024).reshape(b, l, 3 * d)
    up = jnp.pad(u, ((0, 0), (1, 1), (0, 0)))
    sw = p['short_w']
    u = up[:, :-2] * sw[0] + up[:, 1:-1] * sw[1] + up[:, 2:] * sw[2] + p['short_b']
    x0, x1, v = jnp.split(u, 3, axis=-1)
    z = v * x1
    z = _two_sided_long_conv(z, _hyena_filters(l, p)) + z * p['bias']
    return _linear((x0 * z).reshape(b * l, d).astype(BF16), p['w_out'], p['b_out']).reshape(b, l, d)


def _attn_body(q_ref, k_ref, v_ref, o_ref, m_ref, acc_ref, *, dv, parts):
    j = pl.program_id(3)
    tq, tk = q_ref.shape[2], k_ref.shape[2]
    sub = tq // parts

    @pl.when(j == 0)
    def _():
        m_ref[...] = jnp.full_like(m_ref, -jnp.inf)
        acc_ref[...] = jnp.zeros_like(acc_ref)

    k = k_ref[0, 0]
    v = v_ref[0]
    slabs = [pl.ds(i * sub, sub) for i in range(parts)]
    s = [lax.dot_general(q_ref[0, 0, x, :], k, (((1,), (1,)), ((), ())), preferred_element_type=F32) for x in slabs]
    m_prev = [m_ref[x, :] for x in slabs]
    m_new = [jnp.maximum(mp, jnp.max(si, axis=-1, keepdims=True)) for mp, si in zip(m_prev, s)]
    pr = [jnp.exp2(si - pltpu.repeat(mn, tk // LANES, axis=1)).astype(BF16) for si, mn in zip(s, m_new)]
    alpha = [jnp.exp2(mp - mn) for mp, mn in zip(m_prev, m_new)]
    pv = [jnp.dot(p_, v, preferred_element_type=F32) for p_ in pr]
    for x, a, o, mn in zip(slabs, alpha, pv, m_new):
        acc_ref[x, :] = pltpu.repeat(a, acc_ref.shape[1] // LANES, axis=1) * acc_ref[x, :] + o
        m_ref[x, :] = mn

    @pl.when(j == pl.num_programs(3) - 1)
    def _():
        acc = acc_ref[...]
        o_ref[0] = acc[:, :dv] / acc[:, dv:dv + 1]


def _attend(q, k, v):
    b, nh, sq, dq = q.shape
    sk = k.shape[2]
    dv = v.shape[2] // nh
    tq = _tile(sq, (2048, 1024, 512, 256, 128, 64, 32, 16, 8))
    tk = _tile(sk, (1280, 1024, 640, 512, 256, 128))
    parts = max(1, tq // ATTN_SLAB)
    ones = jnp.concatenate([jnp.ones((b, sk, nh, 1), BF16), jnp.zeros((b, sk, nh, dv - 1), BF16)], axis=-1)
    v_ext = jnp.concatenate([v.astype(BF16).reshape(b, sk, nh, dv), ones], axis=-1).reshape(b, sk, nh * 2 * dv)
    return pl.pallas_call(
        functools.partial(_attn_body, dv=dv, parts=parts),
        grid=(b, nh, sq // tq, sk // tk),
        in_specs=[pl.BlockSpec((1, 1, tq, dq), lambda bi, hi, i, j: (bi, hi, i, 0)),
                  pl.BlockSpec((1, 1, tk, dq), lambda bi, hi, i, j: (bi, hi, j, 0)),
                  pl.BlockSpec((1, tk, 2 * dv), lambda bi, hi, i, j: (bi, j, hi))],
        out_specs=pl.BlockSpec((1, tq, dv), lambda bi, hi, i, j: (bi, i, hi)),
        out_shape=jax.ShapeDtypeStruct((b, sq, nh * dv), F32),
        scratch_shapes=[pltpu.VMEM((tq, LANES), F32), pltpu.VMEM((tq, 2 * dv), F32)],
        compiler_params=_cparams("parallel", "parallel", "parallel", "arbitrary"),
        name="attention",
    )((q * LOG2_E).astype(BF16), k.astype(BF16), v_ext)


def _rms(x, g):
    return x * lax.rsqrt(jnp.mean(x * x, axis=-1, keepdims=True) + NORM_EPS) * g


def _rope_1d(x, pos):
    d = x.shape[-1]
    inv = ROPE_THETA ** (-jnp.arange(0, d, 2, dtype=F32) / d)
    ang = pos.astype(F32)[:, None] * inv[None, :]
    cos = jnp.cos(ang)[:, None, :]
    sin = jnp.sin(ang)[:, None, :]
    x1, x2 = jnp.split(x, 2, axis=-1)
    return jnp.concatenate([x1 * cos - x2 * sin, x1 * sin + x2 * cos], axis=-1)


def _axial_rope(x, row, col):
    half = x.shape[-1] // 2
    return jnp.concatenate([_rope_1d(x[..., :half], row), _rope_1d(x[..., half:], col)], axis=-1)


def _mla_mixer(h_lat, h_ctx, p, rope_pos, need_ctx):
    b, l_ctx, d = h_ctx.shape
    l_lat = h_lat.shape[1]
    l = l_ctx + l_lat
    hh = jnp.concatenate([h_ctx, h_lat], axis=1).reshape(b * l, d).astype(BF16)
    qkv = _linear(hh, p['w_a'])
    q_c, kv_c, k_pe = jnp.split(qkv, [Q_LORA, Q_LORA + KV_LORA], axis=-1)
    q = _linear(_rms(q_c, p['q_norm_g']).astype(BF16), p['w_qb']).reshape(b, l, MLA_HEADS, QK_HEAD)
    kv = _linear(_rms(kv_c, p['kv_norm_g']).astype(BF16), p['w_kvb']).reshape(b, l, MLA_HEADS, QK_NOPE + V_HEAD)
    k_nope, v = kv[..., :QK_NOPE], kv[..., QK_NOPE:]
    k = jnp.concatenate([k_nope, jnp.broadcast_to(k_pe.reshape(b, l, 1, QK_ROPE), (b, l, MLA_HEADS, QK_ROPE))], axis=-1)
    q = _rms(q, p['qn_g'])
    k = _rms(k, p['kn_g'])
    row, col = rope_pos

    def rope_lat(t):
        t_l = t[:, l_ctx:]
        t_l = jnp.concatenate([t_l[..., :QK_NOPE], _axial_rope(t_l[..., QK_NOPE:], row, col)], axis=-1)
        return jnp.concatenate([t[:, :l_ctx], t_l], axis=1)

    q = jnp.swapaxes(rope_lat(q) * (QK_HEAD ** -0.5), 1, 2)
    k = jnp.swapaxes(rope_lat(k), 1, 2)
    v = v.reshape(b, l, MLA_HEADS * V_HEAD)
    o_lat = _attend(q[:, :, l_ctx:], k, v)
    if need_ctx:
        o_ctx = _attend(q[:, :, :l_ctx], k[:, :, :l_ctx], v[:, :l_ctx])
        o = jnp.concatenate([o_ctx, o_lat], axis=1).reshape(b * l, MLA_HEADS * V_HEAD)
        out = _linear(o.astype(BF16), p['w_o']).reshape(b, l, d)
        return out[:, :l_ctx], out[:, l_ctx:]
    out = _linear(o_lat.reshape(b * l_lat, MLA_HEADS * V_HEAD).astype(BF16), p['w_o']).reshape(b, l_lat, d)
    return None, out


def _router_body(h_ref, w_ref, b_ref, idx_ref, gate_ref, rank_ref, cnt_ref, run_ref):
    @pl.when(pl.program_id(0) == 0)
    def _():
        run_ref[...] = jnp.zeros_like(run_ref)

    lg = jnp.dot(h_ref[...], w_ref[...], precision=lax.Precision.HIGHEST, preferred_element_type=F32) + b_ref[...]
    tm = lg.shape[0]
    lane = lax.broadcasted_iota(jnp.int32, lg.shape, 1).astype(F32)
    idx_acc = jnp.zeros(lg.shape, F32)
    val_acc = jnp.zeros(lg.shape, F32)
    picked = jnp.zeros(lg.shape, F32)
    sels = []
    top = None
    den = None
    for j in range(TOP_K):
        mx = jnp.max(lg, axis=-1, keepdims=True)
        sel = jnp.min(jnp.where(lg == mx, lane, float(ROUTER_LANES)), axis=-1, keepdims=True)
        if j == 0:
            top = mx
        e = jnp.exp(mx - top)
        den = e if den is None else den + e
        hit = lane == sel
        sels.append(hit)
        idx_acc = jnp.where(lane == float(j), sel, idx_acc)
        val_acc = jnp.where(lane == float(j), e, val_acc)
        picked = jnp.where(hit, 1.0, picked)
        lg = jnp.where(hit, -jnp.inf, lg)
    idx_ref[...] = idx_acc.astype(jnp.int32)
    gate_ref[...] = val_acc / den
    rs = lax.broadcasted_iota(jnp.int32, (tm, tm), 0)
    cs = lax.broadcasted_iota(jnp.int32, (tm, tm), 1)
    before = jnp.where(rs > cs, 1.0, 0.0).astype(BF16)
    base = run_ref[...] + jnp.dot(before, picked.astype(BF16), preferred_element_type=F32)
    rank_acc = jnp.zeros(lg.shape, F32)
    for j in range(TOP_K):
        rk = jnp.sum(jnp.where(sels[j], base, 0.0), axis=-1, keepdims=True)
        rank_acc = jnp.where(lane == float(j), rk, rank_acc)
    rank_ref[...] = rank_acc.astype(jnp.int32)
    run_ref[...] = run_ref[...] + jnp.sum(picked, axis=0, keepdims=True)
    cnt_ref[...] = run_ref[...].astype(jnp.int32)


def _router(h, router_w, router_b):
    t, d = h.shape
    tm = _tile(t, (512, 256, 128, 64, 32, 16, 8))
    w = jnp.concatenate([router_w, jnp.zeros((d, ROUTER_LANES - N_EXPERTS), F32)], axis=1)
    bias = jnp.concatenate([router_b, jnp.full((ROUTER_LANES - N_EXPERTS,), -1e30, F32)]).reshape(1, ROUTER_LANES)
    tok = pl.BlockSpec((tm, ROUTER_LANES), lambda i: (i, 0))
    return pl.pallas_call(
        _router_body,
        grid=(t // tm,),
        in_specs=[pl.BlockSpec((tm, d), lambda i: (i, 0)),
                  pl.BlockSpec((d, ROUTER_LANES), lambda i: (0, 0)),
                  pl.BlockSpec((1, ROUTER_LANES), lambda i: (0, 0))],
        out_specs=[tok, tok, tok, pl.BlockSpec((1, ROUTER_LANES), lambda i: (0, 0))],
        out_shape=[jax.ShapeDtypeStruct((t, ROUTER_LANES), jnp.int32), jax.ShapeDtypeStruct((t, ROUTER_LANES), F32),
                   jax.ShapeDtypeStruct((t, ROUTER_LANES), jnp.int32), jax.ShapeDtypeStruct((1, ROUTER_LANES), jnp.int32)],
        scratch_shapes=[pltpu.VMEM((1, ROUTER_LANES), F32)],
        compiler_params=_cparams("arbitrary"),
        name="router",
    )(h, w, bias)


def _expert_body(be_ref, nb_ref, x_ref, win_ref, bin_ref, wout_ref, bout_ref, gate_ref, o_ref, win_bf, wout_bf):
    i = pl.program_id(0)
    fresh = jnp.logical_or(i == 0, be_ref[i] != be_ref[jnp.maximum(i - 1, 0)])

    @pl.when(fresh)
    def _():
        win_bf[...] = win_ref[0].astype(BF16)
        wout_bf[...] = wout_ref[0].astype(BF16)

    @pl.when(i < nb_ref[0])
    def _():
        gu = jnp.dot(x_ref[...], win_bf[...], preferred_element_type=F32) + bin_ref[0]
        glu = jnp.minimum(gu[:, :D_EXPERT], SWIGLU_LIMIT)
        lin = jnp.clip(gu[:, D_EXPERT:], -SWIGLU_LIMIT, SWIGLU_LIMIT)
        act = glu * jax.nn.sigmoid(SWIGLU_ALPHA * glu) * (lin + 1.0)
        y = jnp.dot(act.astype(BF16), wout_bf[...], preferred_element_type=F32) + bout_ref[0]
        o_ref[...] = (y * gate_ref[...]).astype(o_ref.dtype)

    @pl.when(i >= nb_ref[0])
    def _():
        o_ref[...] = jnp.zeros_like(o_ref)


def _moe(h, router_w, router_b, w_in, b_in, w_out, b_out):
    t, d = h.shape
    n_assign = t * TOP_K
    idx, gates, ranks, counts = _router(h, router_w, router_b)
    flat_e = idx[:, :TOP_K].reshape(n_assign)
    flat_g = gates[:, :TOP_K].reshape(n_assign)
    rank = ranks[:, :TOP_K].reshape(n_assign)
    counts = counts[0, :N_EXPERTS]
    padded = (counts + MOE_ROWS - 1) // MOE_ROWS * MOE_ROWS
    pad_end = jnp.cumsum(padded)
    pad_start = pad_end - padded
    dest = pad_start[flat_e] + rank
    n_blocks = -(-n_assign // MOE_ROWS) + N_EXPERTS
    n_rows = n_blocks * MOE_ROWS
    row_src = jnp.full((n_rows,), -1, jnp.int32).at[dest].set(jnp.arange(n_assign, dtype=jnp.int32))
    row_live = row_src >= 0
    row_src = jnp.maximum(row_src, 0)
    row_tok = row_src // TOP_K
    row_gate = jnp.where(row_live, flat_g[row_src], 0.0)
    block_start = jnp.arange(n_blocks, dtype=jnp.int32) * MOE_ROWS
    block_e = jnp.minimum(jnp.searchsorted(pad_end, block_start, side='right'), N_EXPERTS - 1).astype(jnp.int32)
    n_used = (pad_end[-1] // MOE_ROWS).astype(jnp.int32).reshape(1)
    xb = h.astype(BF16)[row_tok]
    f2 = 2 * D_EXPERT
    grid_spec = pltpu.PrefetchScalarGridSpec(
        num_scalar_prefetch=2,
        grid=(n_blocks,),
        in_specs=[pl.BlockSpec((MOE_ROWS, d), lambda i, be, nb: (i, 0)),
                  pl.BlockSpec((1, d, f2), lambda i, be, nb: (be[i], 0, 0)),
                  pl.BlockSpec((1, 1, f2), lambda i, be, nb: (be[i], 0, 0)),
                  pl.BlockSpec((1, D_EXPERT, d), lambda i, be, nb: (be[i], 0, 0)),
                  pl.BlockSpec((1, 1, d), lambda i, be, nb: (be[i], 0, 0)),
                  pl.BlockSpec((MOE_ROWS, 1), lambda i, be, nb: (i, 0))],
        out_specs=pl.BlockSpec((MOE_ROWS, d), lambda i, be, nb: (i, 0)),
        scratch_shapes=[pltpu.VMEM((d, f2), BF16), pltpu.VMEM((D_EXPERT, d), BF16)],
    )
    yb = pl.pallas_call(
        _expert_body,
        grid_spec=grid_spec,
        out_shape=jax.ShapeDtypeStruct((n_rows, d), BF16),
        compiler_params=_cparams("arbitrary"),
        name="moe_experts",
    )(block_e, n_used, xb, w_in, b_in.reshape(N_EXPERTS, 1, f2),
      w_out, b_out.reshape(N_EXPERTS, 1, d), row_gate.reshape(n_rows, 1))
    dest = dest.reshape(t, TOP_K)
    return sum(yb[dest[:, j]].astype(F32) for j in range(TOP_K))


def kernel(x, c, ctx, c_ctx, ada_w, ada_b, norm_mix_g, norm_ffn_g, rwkv_mu, rwkv_w_rkv, rwkv_w0, rwkv_w1, rwkv_w2, rwkv_a0, rwkv_a1, rwkv_a2, rwkv_g1, rwkv_g2, rwkv_k_k, rwkv_k_a, rwkv_r_k, rwkv_ln_g, rwkv_ln_b, rwkv_w_o, rwkv_v0, rwkv_v1, rwkv_v2, hy_w_in, hy_b_in, hy_short_w, hy_short_b, hy_f_w1, hy_f_b1, hy_f_w2, hy_f_b2, hy_f_w3, hy_f_b3, hy_f_freq, hy_f_wout, hy_bias, hy_w_out, hy_b_out, mla_w_a, mla_q_norm_g, mla_kv_norm_g, mla_w_qb, mla_w_kvb, mla_qn_g, mla_kn_g, mla_w_o, moe_router_w, moe_router_b, moe_w_in, moe_b_in, moe_w_out, moe_b_out):
    bsz, s_len, d = x.shape
    l_ctx = ctx.shape[1]
    rows = s_len // GRID_W
    row = jnp.repeat(jnp.arange(rows, dtype=jnp.int32), GRID_W)
    col = jnp.tile(jnp.arange(GRID_W, dtype=jnp.int32), rows)
    silu = jnp.concatenate([jax.nn.silu(c), jax.nn.silu(c_ctx)[None, :]], axis=0)
    silu = jnp.concatenate([silu, jnp.zeros((-(bsz + 1) % 8, d), F32)], axis=0)
    xc = ctx
    v_first = None
    for i in range(DEPTH):
        need_ctx = i < DEPTH - 1
        j = i // N_MIXERS
        kind = i % N_MIXERS
        mod = _linear(silu, ada_w[i], ada_b[i], tn=1024)
        mod_l = jnp.split(mod[:bsz], N_MOD, axis=-1)
        mod_c = [jnp.broadcast_to(m, (bsz, d)) for m in jnp.split(mod[bsz:bsz + 1], N_MOD, axis=-1)]
        sh1, sc1, g1, sh2, sc2, g2 = [m[:, None, :] for m in mod_l]
        csh1, csc1, cg1, csh2, csc2, cg2 = [m[:, None, :] for m in mod_c]
        h_l = _norm_mod(x, norm_mix_g[i], sc1, sh1)
        h_c = _norm_mod(xc, norm_mix_g[i], csc1, csh1)
        if kind == 0:
            p = {'mu': rwkv_mu[j], 'w_rkv': rwkv_w_rkv[j], 'w0': rwkv_w0[j], 'w1': rwkv_w1[j],
                 'w2': rwkv_w2[j], 'a0': rwkv_a0[j], 'a1': rwkv_a1[j], 'a2': rwkv_a2[j],
                 'g1': rwkv_g1[j], 'g2': rwkv_g2[j], 'k_k': rwkv_k_k[j], 'k_a': rwkv_k_a[j],
                 'r_k': rwkv_r_k[j], 'ln_g': rwkv_ln_g[j], 'ln_b': rwkv_ln_b[j], 'w_o': rwkv_w_o[j]}
            vres = None if j == 0 else (rwkv_v0[j - 1], rwkv_v1[j - 1], rwkv_v2[j - 1])
            y_c, y_l, v_cur = _rwkv_mixer(h_l, h_c, p, v_first, vres, need_ctx)
            if j == 0:
                v_first = v_cur
        elif kind == 1:
            p = {'w_in': hy_w_in[j], 'b_in': hy_b_in[j], 'short_w': hy_short_w[j], 'short_b': hy_short_b[j],
                 'f_w1': hy_f_w1[j], 'f_b1': hy_f_b1[j], 'f_w2': hy_f_w2[j], 'f_b2': hy_f_b2[j],
                 'f_w3': hy_f_w3[j], 'f_b3': hy_f_b3[j], 'f_freq': hy_f_freq[j], 'f_wout': hy_f_wout[j],
                 'bias': hy_bias[j], 'w_out': hy_w_out[j], 'b_out': hy_b_out[j]}
            y_l = _hyena_stream(h_l, p)
            y_c = _hyena_stream(h_c, p) if need_ctx else None
        else:
            p = {'w_a': mla_w_a[j], 'q_norm_g': mla_q_norm_g[j], 'kv_norm_g': mla_kv_norm_g[j],
                 'w_qb': mla_w_qb[j], 'w_kvb': mla_w_kvb[j], 'qn_g': mla_qn_g[j], 'kn_g': mla_kn_g[j],
                 'w_o': mla_w_o[j]}
            y_c, y_l = _mla_mixer(h_l, h_c, p, (row, col), need_ctx)
        x = x + g1 * y_l
        h2_l = _norm_mod(x, norm_ffn_g[i], sc2, sh2)
        moe_p = (moe_router_w[i], moe_router_b[i], moe_w_in[i], moe_b_in[i], moe_w_out[i], moe_b_out[i])
        if need_ctx:
            xc = xc + cg1 * y_c
            h2_c = _norm_mod(xc, norm_ffn_g[i], csc2, csh2)
            n_c = bsz * l_ctx
            f = _moe(jnp.concatenate([h2_c.reshape(-1, d), h2_l.reshape(-1, d)], axis=0), *moe_p)
            xc = xc + cg2 * f[:n_c].reshape(xc.shape)
            x = x + g2 * f[n_c:].reshape(x.shape)
        else:
            x = x + g2 * _moe(h2_l.reshape(-1, d), *moe_p).reshape(x.shape)
    return x
```

```python
import functools
import math

import jax
import jax.numpy as jnp
import numpy as np
from jax import lax
from jax.experimental import pallas as pl
from jax.experimental.pallas import tpu as pltpu

F32 = jnp.float32
BF16 = jnp.bfloat16

D_MODEL = 1024
DEPTH = 4
GRID_W = 64
N_MIXERS = 3
N_MOD = 6
NORM_EPS = 1e-6
RWKV_HEAD = 64
RWKV_HEADS = D_MODEL // RWKV_HEAD
GN_EPS = 64e-5
WKV_CHUNK = 64
WKV_GROUP = 4
FILTER_EMB = 33
FILTER_EMB_PAD = 128
DFT_MIN_OUTER = 32
FAST_DECAY = math.log(1e-2) / 0.3
SLOW_DECAY = math.log(1e-2) / 1.5
MLA_HEADS = 8
QK_NOPE = 128
QK_ROPE = 64
QK_HEAD = QK_NOPE + QK_ROPE
V_HEAD = 128
Q_LORA = 384
KV_LORA = 256
ROPE_THETA = 10000.0
N_EXPERTS = 32
TOP_K = 4
D_EXPERT = 1024
SWIGLU_LIMIT = 7.0
SWIGLU_ALPHA = 1.702
MOE_ROWS = 256
LANES = 128
ROUTER_LANES = LANES
ATTN_SLAB = 512
LOG2_E = 1.4426950408889634
MLA_SLOT = 256

V7X_VMEM_BYTES = 64 * 1024 * 1024
VMEM_LIMIT = V7X_VMEM_BYTES * 3 // 4


def _cparams(*sem):
    return pltpu.CompilerParams(dimension_semantics=sem, vmem_limit_bytes=VMEM_LIMIT)


def _tile(n, prefs):
    for t in prefs:
        if n % t == 0:
            return t
    return n


def _linear_body(a_ref, w_ref, b_ref, o_ref, *, in_act):
    a = a_ref[...]
    if in_act == 'tanh':
        a = jnp.tanh(a.astype(F32))
    elif in_act == 'sigmoid':
        a = jax.nn.sigmoid(a.astype(F32))
    acc = jnp.dot(a.astype(BF16), w_ref[...], preferred_element_type=F32)
    o_ref[...] = (acc + b_ref[...]).astype(o_ref.dtype)


def _linear(a, w, b=None, *, in_act=None, out_dtype=F32, tn=None):
    m, k = a.shape
    n = w.shape[1]
    tm = _tile(m, (512, 256, 128, 64, 32, 16, 8))
    tn = n if tn is None else tn
    assert n % tn == 0
    if b is None:
        b = jnp.zeros((n,), F32)
    return pl.pallas_call(
        functools.partial(_linear_body, in_act=in_act),
        grid=(n // tn, m // tm),
        in_specs=[pl.BlockSpec((tm, k), lambda j, i: (i, 0)),
                  pl.BlockSpec((k, tn), lambda j, i: (0, j)),
                  pl.BlockSpec((1, tn), lambda j, i: (0, j))],
        out_specs=pl.BlockSpec((tm, tn), lambda j, i: (i, j)),
        out_shape=jax.ShapeDtypeStruct((m, n), out_dtype),
        compiler_params=_cparams("parallel", "parallel"),
        name="linear",
    )(a, w.astype(BF16), b.reshape(1, n).astype(F32))


def _norm_mod_body(x_ref, g_ref, sc_ref, sh_ref, o_ref):
    x = x_ref[0]
    y = x * lax.rsqrt(jnp.mean(x * x, axis=-1, keepdims=True) + NORM_EPS)
    o_ref[0] = ((y * g_ref[...]) * (1.0 + sc_ref[0]) + sh_ref[0]).astype(o_ref.dtype)


def _norm_mod(x, g, sc, sh):
    b, l, d = x.shape
    tl = _tile(l, (512, 256, 128, 64, 32, 16, 8))
    return pl.pallas_call(
        _norm_mod_body,
        grid=(b, l // tl),
        in_specs=[pl.BlockSpec((1, tl, d), lambda i, j: (i, j, 0)),
                  pl.BlockSpec((1, d), lambda i, j: (0, 0)),
                  pl.BlockSpec((1, 1, d), lambda i, j: (i, 0, 0)),
                  pl.BlockSpec((1, 1, d), lambda i, j: (i, 0, 0))],
        out_specs=pl.BlockSpec((1, tl, d), lambda i, j: (i, j, 0)),
        out_shape=jax.ShapeDtypeStruct((b, l, d), F32),
        compiler_params=_cparams("parallel", "parallel"),
        name="norm_mod",
    )(x, g.reshape(1, d), sc.reshape(b, 1, d), sh.reshape(b, 1, d))


PROJ_R, PROJ_K, PROJ_V = 0, 1024, 2048
PROJ_W1 = (3072, 3200)
PROJ_A1 = (3328, 3456)
PROJ_G1 = 3584
PROJ_V1 = 3840
PROJ_WIDTH = 4096
LORA_PAD = 128
GATE_PAD = 256


def _sigmoid(x):
    return jax.nn.sigmoid(x)


def _rwkv_post_body(*refs, has_vres):
    if has_vres:
        (proj_ref, vf_ref, w2_ref, a2_ref, g2_ref, v2_ref, vec_ref, gs_ref,
         kk_out, v_out, gate_out, lw_out, ag_out, kd_out) = refs
    else:
        (proj_ref, w2_ref, a2_ref, g2_ref, vec_ref, gs_ref,
         kk_out, v_out, gate_out, lw_out, ag_out, kd_out) = refs
    d = D_MODEL
    vec = vec_ref[...]
    k = proj_ref[:, PROJ_K:PROJ_K + d].astype(F32)
    v = proj_ref[:, PROJ_V:PROJ_V + d].astype(F32)
    if has_vres:
        v1 = proj_ref[:, PROJ_V1:PROJ_V1 + LORA_PAD]
        vgate = _sigmoid(vec[6:7] + jnp.dot(v1, v2_ref[...], preferred_element_type=F32))
        v = v + (vf_ref[...].astype(F32) - v) * vgate
    v_out[...] = v.astype(v_out.dtype)
    kraw = k * vec[4:5]
    ss = jnp.dot((kraw * kraw).astype(BF16), gs_ref[...], preferred_element_type=F32)
    kk_out[...] = (kraw / jnp.maximum(jnp.sqrt(ss), 1e-12)).astype(kk_out.dtype)
    g1 = proj_ref[:, PROJ_G1:PROJ_G1 + GATE_PAD].astype(F32)
    gate_out[...] = jnp.dot(_sigmoid(g1).astype(BF16), g2_ref[...], preferred_element_type=F32).astype(gate_out.dtype)
    for dd in range(2):
        w1 = proj_ref[:, PROJ_W1[dd]:PROJ_W1[dd] + LORA_PAD].astype(F32)
        z = vec[dd:dd + 1] + jnp.dot(jnp.tanh(w1).astype(BF16), w2_ref[dd], preferred_element_type=F32)
        softplus = jnp.maximum(-z, 0.0) + jnp.log(1.0 + jnp.exp(-jnp.abs(z)))
        lw_out[dd] = -jnp.exp(-softplus - 0.5)
        a1 = proj_ref[:, PROJ_A1[dd]:PROJ_A1[dd] + LORA_PAD]
        ag = _sigmoid(vec[2 + dd:3 + dd] + jnp.dot(a1, a2_ref[dd], preferred_element_type=F32))
        ag_out[dd] = ag.astype(ag_out.dtype)
        kd_out[dd] = (k * (1.0 + (ag - 1.0) * vec[5:6])).astype(kd_out.dtype)


def _rwkv_post(proj, v_first, w2p, a2p, g2p, v2p, vec, gsum):
    t, d = proj.shape[0], D_MODEL
    tm = _tile(t, (256, 128, 64, 32, 16))
    has_vres = v_first is not None
    full = lambda a: pl.BlockSpec(a.shape, lambda i: (0,) * a.ndim)
    row = pl.BlockSpec((tm, d), lambda i: (i, 0))
    row2 = pl.BlockSpec((2, tm, d), lambda i: (0, i, 0))
    ins = [proj] + ([v_first] if has_vres else []) + [w2p, a2p, g2p] + ([v2p] if has_vres else []) + [vec, gsum]
    in_specs = ([pl.BlockSpec((tm, PROJ_WIDTH), lambda i: (i, 0))] + ([row] if has_vres else [])
                + [full(a) for a in ins[(2 if has_vres else 1):]])
    return pl.pallas_call(
        functools.partial(_rwkv_post_body, has_vres=has_vres),
        grid=(t // tm,),
        in_specs=in_specs,
        out_specs=[row, row, row, row2, row2, row2],
        out_shape=[jax.ShapeDtypeStruct((t, d), BF16),
                   jax.ShapeDtypeStruct((t, d), BF16),
                   jax.ShapeDtypeStruct((t, d), BF16),
                   jax.ShapeDtypeStruct((2, t, d), F32),
                   jax.ShapeDtypeStruct((2, t, d), BF16),
                   jax.ShapeDtypeStruct((2, t, d), BF16)],
        compiler_params=_cparams("parallel"),
        name="rwkv_post",
    )(*ins)


def _head_groups(y):
    gw = WKV_GROUP * RWKV_HEAD
    lane_head = jnp.right_shift(lax.broadcasted_iota(jnp.int32, (y.shape[0], gw), 1), 6)
    out = []
    for g in range(D_MODEL // gw):
        yg = y[:, g * gw:(g + 1) * gw].astype(BF16)
        out.append(jnp.concatenate([jnp.where(lane_head == h, yg, jnp.zeros_like(yg)) for h in range(WKV_GROUP)], axis=0))
    return out


def _gmm(x, ybd):
    gw = WKV_GROUP * RWKV_HEAD
    return jnp.concatenate([jnp.dot(x[:, g * gw:(g + 1) * gw].astype(BF16), yb, preferred_element_type=F32)
                            for g, yb in enumerate(ybd)], axis=-1)


def _gmm_nt(x, ybd):
    gw = WKV_GROUP * RWKV_HEAD
    return jnp.concatenate([lax.dot_general(x[:, g * gw:(g + 1) * gw].astype(BF16), yb, (((1,), (1,)), ((), ())),
                                            preferred_element_type=F32)
                            for g, yb in enumerate(ybd)], axis=-1)


def _wkv_chunks(r, kk, v, lw, ag, kd, s0, rev):
    each = lambda f, *ls: [f(*a) for a in zip(*ls)]
    c, d = r[0].shape
    row0 = lax.broadcasted_iota(jnp.int32, (c, d), 0)
    col0 = jnp.bitwise_and(lax.broadcasted_iota(jnp.int32, (c, d), 1), RWKV_HEAD - 1)
    incl = [(row0 <= col0) if rv else (row0 >= col0) for rv in rev]
    strict = [(row0 < col0) if rv else (row0 > col0) for rv in rev]
    rs = lax.broadcasted_iota(jnp.int32, (c, c), 0)
    cs = lax.broadcasted_iota(jnp.int32, (c, c), 1)
    tri = [jnp.where((rs <= cs) if rv else (rs >= cs), 1.0, 0.0).astype(BF16) for rv in rev]
    hi = each(lambda x: x.astype(BF16), lw)
    rem = each(lambda x, h: x - h.astype(F32), lw, hi)
    mid = each(lambda x: x.astype(BF16), rem)
    lo = each(lambda x, m: (x - m.astype(F32)).astype(BF16), rem, mid)
    cum = each(lambda t, a, b_, c_: (jnp.dot(t, a, preferred_element_type=F32) + jnp.dot(t, b_, preferred_element_type=F32)
                                     + jnp.dot(t, c_, preferred_element_type=F32)), tri, hi, mid, lo)
    cum_end = [x[0:1, :] if rv else x[c - 1:c, :] for x, rv in zip(cum, rev)]
    g_rest = each(lambda e, x: jnp.exp(e - x), cum_end, cum)
    g_inv = each(lambda x: jnp.exp(-x), cum)
    bvec = each(lambda a, b_: a * b_, kk, ag)
    rt = each(lambda a, x: a * jnp.exp(x), r, cum)
    kt = each(lambda a, g: a * g, kd, g_inv)
    bt = each(lambda a, g: a * g, bvec, g_inv)
    at = each(lambda a, x, w: -a * jnp.exp(x - w), kk, cum, lw)
    ar = each(lambda a, b_: jnp.concatenate([a, b_], axis=0), at, rt)
    ab = each(lambda a, b_: _gmm_nt(a, _head_groups(b_)), ar, bt)
    ak = each(lambda a, b_: _gmm_nt(a, _head_groups(b_)), ar, kt)
    a_ab = each(lambda m, x: jnp.where(m, x[:c], 0.0), strict, ab)
    a_rb = each(lambda m, x: jnp.where(m, x[c:], 0.0), incl, ab)
    a_ak = each(lambda m, x: jnp.where(m, x[:c], 0.0), strict, ak)
    a_rk = each(lambda m, x: jnp.where(m, x[c:], 0.0), incl, ak)
    same16 = jnp.right_shift(row0, 4) == jnp.right_shift(col0, 4)
    same32 = jnp.right_shift(row0, 5) == jnp.right_shift(col0, 5)
    eye = jnp.where(row0 == col0, 1.0, 0.0)
    npow = each(lambda x: jnp.where(same16, x, 0.0), a_ab)
    p = each(lambda x: eye + x, npow)
    for _ in range(3):
        npow = each(lambda x: _gmm(x, _head_groups(x)), npow)
        p = each(lambda x, y: x + _gmm(x, _head_groups(y)), p, npow)
    off32 = jnp.logical_and(same32, jnp.logical_not(same16))
    t1 = each(lambda x, y: _gmm(jnp.where(off32, x, 0.0), _head_groups(y)), a_ab, p)
    p = each(lambda x, y: x + _gmm(x, _head_groups(y)), p, t1)
    t2 = each(lambda x, y: _gmm(jnp.where(same32, 0.0, x), _head_groups(y)), a_ab, p)
    p = each(lambda x, y: x + _gmm(x, _head_groups(y)), p, t2)
    pbd = each(_head_groups, p)
    vbd = each(_head_groups, v)
    wmat = each(lambda x, y: _gmm(x, _head_groups(y)), p, at)
    vt = each(lambda y: _gmm_nt(eye, y), vbd)
    xt = each(lambda x, y: _gmm_nt(x, _head_groups(y)), vt, a_ak)
    upt = each(_gmm_nt, xt, pbd)
    yp = each(_gmm, a_rk, vbd)
    sp = each(lambda x, a, g: _gmm(x, _head_groups(a * g)), vt, kd, g_rest)
    ut = each(lambda s_, w, u: _gmm_nt(s_, _head_groups(w)) + u, s0, wmat, upt)
    ys = each(lambda s_, q, x: _gmm_nt(q, _head_groups(s_)) + x, s0, rt, yp)
    y = each(lambda y0, a, u: y0 + _gmm_nt(a, _head_groups(u)), ys, a_rb, ut)
    s_end = each(lambda s_, e, u, b_, g, x: s_ * jnp.exp(e) + _gmm(u, _head_groups(b_ * g)) + x,
                 s0, cum_end, ut, bvec, g_rest, sp)
    return y, s_end


def _wkv_body(*refs, n_batch):
    ins, outs, s_ref = refs[:12], refs[12:14], refs[14]

    @pl.when(pl.program_id(0) == 0)
    def _():
        s_ref[...] = jnp.zeros_like(s_ref)

    scans = [(dd, bi) for dd in range(2) for bi in range(n_batch)]
    get = lambda k, lead: [ins[6 * dd + k][(0, bi) if lead else bi].astype(F32) for dd, bi in scans]
    y, s_end = _wkv_chunks(get(0, False), get(1, False), get(2, False), get(3, True), get(4, True), get(5, True),
                           [s_ref[i] for i in range(len(scans))], [dd == 1 for dd, _ in scans])
    for i, (dd, bi) in enumerate(scans):
        outs[dd][bi] = y[i]
        s_ref[i] = s_end[i]


def _wkv(proj, kk, v, lw, ag, kd, l_ctx):
    b, l, d = kk.shape
    c = WKV_CHUNK
    nc, nctx = l // c, l_ctx // c
    assert c == RWKV_HEAD and l % c == 0 and l_ctx % c == 0

    def fwd(j):
        return j

    def bwd(j):
        return jnp.where(j < nctx, nctx - 1 - j, nc + nctx - 1 - j)

    in_specs, out_specs = [], []
    for dd, ch in enumerate((fwd, bwd)):
        shared = pl.BlockSpec((b, c, d), lambda j, ch=ch: (0, ch(j), 0))
        per_dir = pl.BlockSpec((1, b, c, d), lambda j, dd=dd, ch=ch: (dd, 0, ch(j), 0))
        in_specs += [shared, shared, shared, per_dir, per_dir, per_dir]
        out_specs.append(shared)
    return pl.pallas_call(
        functools.partial(_wkv_body, n_batch=b),
        grid=(nc,),
        in_specs=in_specs,
        out_specs=out_specs,
        out_shape=[jax.ShapeDtypeStruct((b, l, d), F32)] * 2,
        scratch_shapes=[pltpu.VMEM((2 * b, RWKV_HEAD, d), F32)],
        compiler_params=_cparams("arbitrary"),
        name="wkv",
    )(*([proj, kk, v, lw, ag, kd] * 2))


def _rwkv_readout_body(yf_ref, yr_ref, r_ref, kd_ref, v_ref, gate_ref, gm_ref, gs_ref, vec_ref, wo_ref, o_ref):
    vec = vec_ref[...]
    y = yf_ref[...] + yr_ref[...]
    gm = gm_ref[...]
    yh = y.astype(BF16)
    yl = (y - yh.astype(F32)).astype(BF16)
    mean = jnp.dot(yh, gm, preferred_element_type=F32) + jnp.dot(yl, gm, preferred_element_type=F32)
    yc = y - mean
    var = jnp.dot((yc * yc).astype(BF16), gm, preferred_element_type=F32)
    yn = yc * lax.rsqrt(var + GN_EPS) * vec[0:1] + vec[1:2]
    r = r_ref[...].astype(F32)
    kds = kd_ref[0].astype(F32) + kd_ref[1].astype(F32)
    bonus = jnp.dot((r * kds * vec[2:3]).astype(BF16), gs_ref[...], preferred_element_type=F32)
    out = (yn + bonus * v_ref[...].astype(F32)) * gate_ref[...].astype(F32)
    o_ref[...] = jnp.dot(out.astype(BF16), wo_ref[...], preferred_element_type=F32)


def _rwkv_readout(y_fwd, y_rev, proj, kd, v, gate, gmean, gsum, vec, w_o):
    t, d = v.shape
    tm = _tile(t, (256, 128, 64, 32, 16))
    full = lambda a: pl.BlockSpec(a.shape, lambda i: (0,) * a.ndim)
    row = pl.BlockSpec((tm, d), lambda i: (i, 0))
    row2 = pl.BlockSpec((2, tm, d), lambda i: (0, i, 0))
    return pl.pallas_call(
        _rwkv_readout_body,
        grid=(t // tm,),
        in_specs=[row, row, row, row2, row, row, full(gmean), full(gsum), full(vec), full(w_o)],
        out_specs=row,
        out_shape=jax.ShapeDtypeStruct((t, d), F32),
        compiler_params=_cparams("parallel"),
        name="rwkv_readout",
    )(y_fwd, y_rev, proj, kd, v, gate, gmean, gsum, vec, w_o)


def _centred_shift(x):
    xp = jnp.pad(x, ((0, 0), (1, 1), (0, 0)))
    return 0.5 * (xp[:, :-2] + xp[:, 2:]) - x


def _pad_cols(w, width):
    return jnp.concatenate([w, jnp.zeros((w.shape[0], width - w.shape[1]), w.dtype)], axis=1)


def _pad_rows(w, height):
    return jnp.concatenate([w, jnp.zeros((height - w.shape[0], w.shape[1]), w.dtype)], axis=0)


def _rwkv_mixer(h_lat, h_ctx, p, v_first, vres, need_ctx):
    b, l_ctx, d = h_ctx.shape
    l = l_ctx + h_lat.shape[1]
    t = b * l
    hh = jnp.concatenate([h_ctx, h_lat], axis=1)
    xx = jnp.concatenate([_centred_shift(h_ctx), _centred_shift(h_lat)], axis=1)
    mu = p['mu']
    mats = [(p['w_rkv'][0], 0, d), (p['w_rkv'][1], 2, d), (p['w_rkv'][2], 3, d),
            (p['w1'][0], 1, LORA_PAD), (p['w1'][1], 1, LORA_PAD), (p['a1'][0], 4, LORA_PAD), (p['a1'][1], 4, LORA_PAD),
            (p['g1'], 5, GATE_PAD), (vres[1] if vres is not None else jnp.zeros((d, 1), F32), 3, 2 * LORA_PAD)]
    top = jnp.concatenate([_pad_cols(w, width) for w, _, width in mats], axis=1)
    bot = jnp.concatenate([_pad_cols(mu[m][:, None] * w, width) for w, m, width in mats], axis=1)
    wcat = jnp.concatenate([top, bot], axis=0)
    assert wcat.shape[1] == PROJ_WIDTH
    a_in = jnp.concatenate([hh, xx], axis=-1).reshape(t, 2 * d).astype(BF16)
    proj = _linear(a_in, wcat, out_dtype=BF16, tn=1024)
    w2p = jnp.stack([_pad_rows(p['w2'][i], LORA_PAD) for i in range(2)]).astype(BF16)
    a2p = jnp.stack([_pad_rows(p['a2'][i], LORA_PAD) for i in range(2)]).astype(BF16)
    g2p = _pad_rows(p['g2'], GATE_PAD).astype(BF16)
    head_of = jnp.arange(d, dtype=jnp.int32) // RWKV_HEAD
    same_head = head_of[:, None] == head_of[None, :]
    gsum = jnp.where(same_head, 1.0, 0.0).astype(BF16)
    gmean = jnp.where(same_head, 1.0 / RWKV_HEAD, 0.0).astype(BF16)
    zero = jnp.zeros((d,), F32)
    vec = jnp.stack([p['w0'][0], p['w0'][1], p['a0'][0], p['a0'][1], p['k_k'], p['k_a'],
                     vres[0] if vres is not None else zero, zero])
    if vres is not None:
        kk, v, gate, lw, ag, kd = _rwkv_post(proj, v_first.reshape(t, d), w2p, a2p, g2p,
                                             _pad_rows(vres[2], LORA_PAD).astype(BF16), vec, gsum)
    else:
        kk, v, gate, lw, ag, kd = _rwkv_post(proj, None, w2p, a2p, g2p, None, vec, gsum)
    sh3 = lambda a: a.reshape(a.shape[:-2] + (b, l, a.shape[-1]))
    y_fwd, y_rev = _wkv(sh3(proj), sh3(kk), sh3(v), sh3(lw), sh3(ag), sh3(kd), l_ctx)
    vec_o = jnp.stack([p['ln_g'], p['ln_b'], p['r_k'].reshape(d)] + [zero] * 5)
    out = _rwkv_readout(y_fwd.reshape(t, d), y_rev.reshape(t, d), proj, kd, v, gate, gmean, gsum, vec_o, p['w_o'].astype(BF16))
    out = out.reshape(b, l, d)
    out_ctx = out[:, :l_ctx] if need_ctx else None
    return out_ctx, out[:, l_ctx:], v.reshape(b, l, d)


def _filter_body(z_ref, t_ref, w1_ref, b1_ref, w2_ref, b2_ref, w3_ref, b3_ref, fr_ref, wo_ref, dl_ref, o_ref, *, length):
    hp = lax.Precision.HIGHEST
    fr = fr_ref[...]
    hdn = jnp.sin(fr[0:1] * (jnp.dot(z_ref[...], w1_ref[...], precision=hp, preferred_element_type=F32) + b1_ref[...]))
    hdn = jnp.sin(fr[1:2] * (jnp.dot(hdn, w2_ref[...], precision=hp, preferred_element_type=F32) + b2_ref[...]))
    hdn = jnp.sin(fr[2:3] * (jnp.dot(hdn, w3_ref[...], precision=hp, preferred_element_type=F32) + b3_ref[...]))
    filt = jnp.dot(hdn, wo_ref[...], precision=hp, preferred_element_type=F32) * jnp.exp(-t_ref[...] * dl_ref[...])
    tl = filt.shape[0]
    n = pl.program_id(0) * tl + lax.broadcasted_iota(jnp.int32, filt.shape, 0)
    o_ref[...] = jnp.where(n == length, 0.0, filt * (0.5 / length))


def _hyena_filters(length, p):
    n = jnp.arange(2 * length, dtype=jnp.int32)
    pos = jnp.where(n < length, n, 2 * length - n).astype(F32)[:, None]
    t = pos / (length - 1)
    bands = (FILTER_EMB - 1) // 2
    f = jnp.linspace(1e-4, bands - 1, bands, dtype=F32)[None, :]
    ang = 2.0 * math.pi * f * pos / length
    z = jnp.concatenate([t, jnp.cos(ang), -jnp.sin(ang), jnp.zeros((2 * length, FILTER_EMB_PAD - FILTER_EMB), F32)], axis=-1)
    w1 = jnp.concatenate([p['f_w1'], jnp.zeros((FILTER_EMB_PAD - FILTER_EMB, p['f_w1'].shape[1]), F32)], axis=0)
    deltas = jnp.abs(jnp.linspace(FAST_DECAY, SLOW_DECAY, D_MODEL, dtype=F32))[None, :]
    fw = p['f_w2'].shape[0]
    tl = _tile(length, (512, 256, 128, 64, 32, 16, 8))
    half = length // tl
    full = lambda a: pl.BlockSpec(a.shape, lambda i: (0,) * a.ndim)
    args = [w1, p['f_b1'].reshape(1, fw), p['f_w2'], p['f_b2'].reshape(1, fw), p['f_w3'], p['f_b3'].reshape(1, fw),
            p['f_freq']]
    return pl.pallas_call(
        functools.partial(_filter_body, length=length),
        grid=(2 * half,),
        in_specs=[pl.BlockSpec((tl, FILTER_EMB_PAD), lambda i: (i, 0)), pl.BlockSpec((tl, 1), lambda i: (i, 0))]
                 + [full(a) for a in args]
                 + [pl.BlockSpec((fw, D_MODEL), lambda i: (0, jnp.where(i < half, 0, 1))),
                    full(deltas)],
        out_specs=pl.BlockSpec((tl, D_MODEL), lambda i: (i, 0)),
        out_shape=jax.ShapeDtypeStruct((2 * length, D_MODEL), F32),
        compiler_params=_cparams("parallel"),
        name="hyena_filter",
    )(z, t, *args, p['f_wout'], deltas)


def _colmm_body(f_ref, x_ref, o_ref):
    o_ref[0] = jnp.dot(f_ref[...], x_ref[0].astype(BF16), preferred_element_type=F32).astype(o_ref.dtype)


def _colmm(f, x, out_dtype):
    bsz, q, n = x.shape
    pp = f.shape[0]
    tn = _tile(n, (4096, 2048, 1024, 512, 256, 128))
    return pl.pallas_call(
        _colmm_body,
        grid=(bsz, n // tn),
        in_specs=[pl.BlockSpec((pp, q), lambda i, j: (0, 0)),
                  pl.BlockSpec((1, q, tn), lambda i, j: (i, 0, j))],
        out_specs=pl.BlockSpec((1, pp, tn), lambda i, j: (i, 0, j)),
        out_shape=jax.ShapeDtypeStruct((bsz, pp, n), out_dtype),
        compiler_params=_cparams("parallel", "parallel"),
        name="dft_outer",
    )(f.astype(BF16), x)


def _dft_inner_fwd_body(a_ref, m_ref, o_ref):
    n2 = a_ref.shape[3]
    a = jnp.concatenate([a_ref[0, 0, 0], a_ref[0, 1, 0]], axis=0).astype(BF16)
    x = jnp.dot(m_ref[0], a, preferred_element_type=F32)
    o_ref[0, 0, 0] = x[:n2]
    o_ref[0, 1, 0] = x[n2:]


def _dft_inner_conv_body(a_ref, g_ref, m_ref, mi_ref, o_ref):
    n2 = a_ref.shape[3]
    a = jnp.concatenate([a_ref[0, 0, 0], a_ref[0, 1, 0]], axis=0).astype(BF16)
    x = jnp.dot(m_ref[0], a, preferred_element_type=F32)
    xre, xim = x[:n2], x[n2:]
    gre, gim = g_ref[0, 0, 0], g_ref[0, 1, 0]
    y = jnp.concatenate([xre * gre - xim * gim, xre * gim + xim * gre], axis=0).astype(BF16)
    o = jnp.dot(mi_ref[0], y, preferred_element_type=F32)
    o_ref[0, 0, 0] = o[:n2].astype(o_ref.dtype)
    o_ref[0, 1, 0] = o[n2:].astype(o_ref.dtype)


def _dft_tables(n1, n2):
    n = n1 * n2
    i1 = jnp.arange(n1, dtype=jnp.int32)
    ph1 = (2.0 * math.pi / n1) * ((i1[:, None] * i1[None, :]) % n1).astype(F32)
    c1, s1 = jnp.cos(ph1), jnp.sin(ph1)
    k1 = i1[:, None, None]
    k2 = jnp.arange(n2, dtype=jnp.int32)[None, :, None]
    m2 = jnp.arange(n2, dtype=jnp.int32)[None, None, :]
    th = (2.0 * math.pi / n) * ((m2 * (k1 + n1 * k2)) % n).astype(F32)
    mre, mim = jnp.cos(th), -jnp.sin(th)
    m_fwd = jnp.concatenate([jnp.concatenate([mre, -mim], axis=2), jnp.concatenate([mim, mre], axis=2)], axis=1)
    qre, qim = jnp.swapaxes(mre, 1, 2), -jnp.swapaxes(mim, 1, 2)
    m_inv = jnp.concatenate([jnp.concatenate([qre, -qim], axis=2), jnp.concatenate([qim, qre], axis=2)], axis=1)
    f_fwd = jnp.concatenate([c1, -s1], axis=0)
    f_inv = jnp.concatenate([c1, -s1], axis=1)
    return f_fwd, f_inv, m_fwd.astype(BF16), m_inv.astype(BF16)


def _two_sided_long_conv(z, filt):
    b, l, d = z.shape
    n = 2 * l
    n1 = max(DFT_MIN_OUTER, 2 ** ((int(math.log2(n)) - 1) // 2))
    n2 = n // n1
    f_fwd, f_inv, m_fwd, m_inv = _dft_tables(n1, n2)
    blk = pl.BlockSpec((1, 2, 1, n2, d), lambda i, j: (i, 0, j, 0, 0))
    gblk = pl.BlockSpec((1, 2, 1, n2, d), lambda i, j: (0, 0, j, 0, 0))
    mat = pl.BlockSpec((1, 2 * n2, 2 * n2), lambda i, j: (j, 0, 0))
    ga = _colmm(f_fwd, filt.reshape(1, n1, n2 * d), BF16).reshape(1, 2, n1, n2, d)
    gspec = pl.pallas_call(
        _dft_inner_fwd_body,
        grid=(1, n1),
        in_specs=[blk, mat],
        out_specs=blk,
        out_shape=jax.ShapeDtypeStruct((1, 2, n1, n2, d), F32),
        compiler_params=_cparams("parallel", "parallel"),
        name="dft_inner_fwd",
    )(ga, m_fwd)
    za = _colmm(f_fwd[:, :n1 // 2], z.reshape(b, n1 // 2, n2 * d), BF16).reshape(b, 2, n1, n2, d)
    bb = pl.pallas_call(
        _dft_inner_conv_body,
        grid=(b, n1),
        in_specs=[blk, gblk, mat, mat],
        out_specs=blk,
        out_shape=jax.ShapeDtypeStruct((b, 2, n1, n2, d), BF16),
        compiler_params=_cparams("parallel", "parallel"),
        name="dft_inner_conv",
    )(za, gspec, m_fwd, m_inv)
    return _colmm(f_inv[:n1 // 2], bb.reshape(b, 2 * n1, n2 * d), F32).reshape(b, l, d)


def _hyena_stream(h, p):
    b, l, d = h.shape
    u = _linear(h.reshape(b * l, d).astype(BF16), p['w_in'], p['b_in'], tn=1024).reshape(b, l, 3 * d)
    up = jnp.pad(u, ((0, 0), (1, 1), (0, 0)))
    sw = p['short_w']
    u = up[:, :-2] * sw[0] + up[:, 1:-1] * sw[1] + up[:, 2:] * sw[2] + p['short_b']
    x0, x1, v = jnp.split(u, 3, axis=-1)
    z = v * x1
    z = _two_sided_long_conv(z, _hyena_filters(l, p)) + z * p['bias']
    return _linear((x0 * z).reshape(b * l, d).astype(BF16), p['w_out'], p['b_out']).reshape(b, l, d)


def _attn_body(q_ref, k_ref, v_ref, o_ref, m_ref, acc_ref, *, dv, parts):
    j = pl.program_id(3)
    tq, tk = q_ref.shape[2], k_ref.shape[2]
    sub = tq // parts

    @pl.when(j == 0)
    def _():
        m_ref[...] = jnp.full_like(m_ref, -jnp.inf)
        acc_ref[...] = jnp.zeros_like(acc_ref)

    k = k_ref[0, 0]
    v = v_ref[0]
    slabs = [pl.ds(i * sub, sub) for i in range(parts)]
    s = [lax.dot_general(q_ref[0, 0, x, :], k, (((1,), (1,)), ((), ())), preferred_element_type=F32) for x in slabs]
    m_prev = [m_ref[x, :] for x in slabs]
    m_new = [jnp.maximum(mp, jnp.max(si, axis=-1, keepdims=True)) for mp, si in zip(m_prev, s)]
    pr = [jnp.exp2(si - jnp.concatenate([mn] * (tk // LANES), axis=1)).astype(BF16) for si, mn in zip(s, m_new)]
    alpha = [jnp.exp2(mp - mn) for mp, mn in zip(m_prev, m_new)]
    pv = [jnp.dot(p_, v, preferred_element_type=F32) for p_ in pr]
    for x, a, o, mn in zip(slabs, alpha, pv, m_new):
        acc_ref[x, :] = jnp.concatenate([a] * (acc_ref.shape[1] // LANES), axis=1) * acc_ref[x, :] + o
        m_ref[x, :] = mn

    @pl.when(j == pl.num_programs(3) - 1)
    def _():
        acc = acc_ref[...]
        o_ref[0] = acc[:, :dv] / acc[:, dv:dv + 1]


def _attend(q, k, v_ext):
    b, nh, sq, dq = q.shape
    sk = k.shape[2]
    dv = v_ext.shape[2] // (2 * nh)
    tq = _tile(sq, (2048, 1024, 512, 256, 128, 64, 32, 16, 8))
    tk = _tile(sk, (1280, 1024, 640, 512, 256, 128))
    parts = max(1, tq // ATTN_SLAB)
    return pl.pallas_call(
        functools.partial(_attn_body, dv=dv, parts=parts),
        grid=(b, nh, sq // tq, sk // tk),
        in_specs=[pl.BlockSpec((1, 1, tq, dq), lambda bi, hi, i, j: (bi, hi, i, 0)),
                  pl.BlockSpec((1, 1, tk, dq), lambda bi, hi, i, j: (bi, hi, j, 0)),
                  pl.BlockSpec((1, tk, 2 * dv), lambda bi, hi, i, j: (bi, j, hi))],
        out_specs=pl.BlockSpec((1, tq, dv), lambda bi, hi, i, j: (bi, i, hi)),
        out_shape=jax.ShapeDtypeStruct((b, sq, nh * dv), F32),
        scratch_shapes=[pltpu.VMEM((tq, LANES), F32), pltpu.VMEM((tq, 2 * dv), F32)],
        compiler_params=_cparams("parallel", "parallel", "parallel", "arbitrary"),
        name="attention",
    )(q, k, v_ext)


def _mla_prep_body(qkv_ref, wq_ref, wkv_ref, gq_ref, gkv_ref, gqn_ref, gkn_ref, cos_ref, sin_ref, q_out, k_out, v_out):
    x = qkv_ref[0]
    qc, kvc, kpe = x[:, :Q_LORA], x[:, Q_LORA:Q_LORA + KV_LORA], x[:, Q_LORA + KV_LORA:]
    qn = qc * lax.rsqrt(jnp.mean(qc * qc, axis=-1, keepdims=True) + NORM_EPS) * gq_ref[...]
    kvn = kvc * lax.rsqrt(jnp.mean(kvc * kvc, axis=-1, keepdims=True) + NORM_EPS) * gkv_ref[...]
    q = jnp.dot(qn.astype(BF16), wq_ref[...], preferred_element_type=F32)
    kv = jnp.dot(kvn.astype(BF16), wkv_ref[...], preferred_element_type=F32)
    cos, sin = cos_ref[0], sin_ref[0]
    quarter = QK_ROPE // 4

    def rope(t):
        swapped = jnp.concatenate([t[:, quarter:2 * quarter], t[:, :quarter],
                                   t[:, 3 * quarter:], t[:, 2 * quarter:3 * quarter]], axis=-1)
        return t * cos + swapped * sin

    gqn, gkn = gqn_ref[...], gkn_ref[...]
    kpe_rot = rope(kpe * gkn[:, QK_NOPE:])
    kpe_ss = jnp.sum(kpe * kpe, axis=-1, keepdims=True)
    ones_col = jnp.where(lax.broadcasted_iota(jnp.int32, (x.shape[0], V_HEAD), 1) == 0, 1.0, 0.0).astype(BF16)
    for h in range(MLA_HEADS):
        qh = q[:, h * MLA_SLOT:h * MLA_SLOT + QK_HEAD]
        qh = qh * lax.rsqrt(jnp.sum(qh * qh, axis=-1, keepdims=True) * (1.0 / QK_HEAD) + NORM_EPS) * gqn
        qh = jnp.concatenate([qh[:, :QK_NOPE], rope(qh[:, QK_NOPE:])], axis=-1)
        q_out[0, h] = (qh * (QK_HEAD ** -0.5 * LOG2_E)).astype(q_out.dtype)
        kn = kv[:, h * MLA_SLOT:h * MLA_SLOT + QK_NOPE]
        rstd = lax.rsqrt((jnp.sum(kn * kn, axis=-1, keepdims=True) + kpe_ss) * (1.0 / QK_HEAD) + NORM_EPS)
        k_out[0, h] = jnp.concatenate([kn * rstd * gkn[:, :QK_NOPE], kpe_rot * rstd], axis=-1).astype(k_out.dtype)
        v_out[0, :, h * MLA_SLOT:h * MLA_SLOT + V_HEAD] = kv[:, h * MLA_SLOT + QK_NOPE:(h + 1) * MLA_SLOT].astype(v_out.dtype)
        v_out[0, :, h * MLA_SLOT + V_HEAD:(h + 1) * MLA_SLOT] = ones_col


def _mla_prep(qkv, p, cos, sin):
    b, l, w = qkv.shape
    tm = _tile(l, (256, 128, 64, 32, 16))
    nh = MLA_HEADS
    pad_heads = lambda m, width: jnp.pad(m.reshape(m.shape[0], nh, width), ((0, 0), (0, 0), (0, MLA_SLOT - width))
                                         ).reshape(m.shape[0], nh * MLA_SLOT)
    wq = pad_heads(p['w_qb'], QK_HEAD).astype(BF16)
    wkv = p['w_kvb'].astype(BF16)
    full = lambda a: pl.BlockSpec(a.shape, lambda i, j: (0,) * a.ndim)
    args = [wq, wkv, p['q_norm_g'].reshape(1, Q_LORA), p['kv_norm_g'].reshape(1, KV_LORA),
            p['qn_g'].reshape(1, QK_HEAD), p['kn_g'].reshape(1, QK_HEAD)]
    tab = pl.BlockSpec((1, tm, QK_ROPE), lambda i, j: (0, j, 0))
    head = pl.BlockSpec((1, nh, tm, QK_HEAD), lambda i, j: (i, 0, j, 0))
    return pl.pallas_call(
        _mla_prep_body,
        grid=(b, l // tm),
        in_specs=[pl.BlockSpec((1, tm, w), lambda i, j: (i, j, 0))] + [full(a) for a in args] + [tab, tab],
        out_specs=[head, head, pl.BlockSpec((1, tm, nh * MLA_SLOT), lambda i, j: (i, j, 0))],
        out_shape=[jax.ShapeDtypeStruct((b, nh, l, QK_HEAD), BF16), jax.ShapeDtypeStruct((b, nh, l, QK_HEAD), BF16),
                   jax.ShapeDtypeStruct((b, l, nh * MLA_SLOT), BF16)],
        compiler_params=_cparams("parallel", "parallel"),
        name="mla_prep",
    )(qkv, *args, cos, sin)


def _rope_tables(row, col, l_ctx):
    half = QK_ROPE // 2
    inv = ROPE_THETA ** (-jnp.arange(0, half, 2, dtype=F32) / half)
    ang_r = row.astype(F32)[:, None] * inv[None, :]
    ang_c = col.astype(F32)[:, None] * inv[None, :]
    cos = jnp.concatenate([jnp.cos(ang_r)] * 2 + [jnp.cos(ang_c)] * 2, axis=-1)
    sin = jnp.concatenate([-jnp.sin(ang_r), jnp.sin(ang_r), -jnp.sin(ang_c), jnp.sin(ang_c)], axis=-1)
    cos = jnp.concatenate([jnp.ones((l_ctx, QK_ROPE), F32), cos], axis=0)
    sin = jnp.concatenate([jnp.zeros((l_ctx, QK_ROPE), F32), sin], axis=0)
    return cos[None], sin[None]


def _mla_mixer(h_lat, h_ctx, p, rope_pos, need_ctx):
    b, l_ctx, d = h_ctx.shape
    l_lat = h_lat.shape[1]
    l = l_ctx + l_lat
    hh = jnp.concatenate([h_ctx, h_lat], axis=1).reshape(b * l, d).astype(BF16)
    qkv = _linear(hh, p['w_a']).reshape(b, l, -1)
    cos, sin = _rope_tables(rope_pos[0], rope_pos[1], l_ctx)
    q, k, v_ext = _mla_prep(qkv, p, cos, sin)
    o_lat = _attend(q[:, :, l_ctx:], k, v_ext)
    if need_ctx:
        o_ctx = _attend(q[:, :, :l_ctx], k[:, :, :l_ctx], v_ext[:, :l_ctx])
        o = jnp.concatenate([o_ctx, o_lat], axis=1).reshape(b * l, MLA_HEADS * V_HEAD)
        out = _linear(o.astype(BF16), p['w_o']).reshape(b, l, d)
        return out[:, :l_ctx], out[:, l_ctx:]
    out = _linear(o_lat.reshape(b * l_lat, MLA_HEADS * V_HEAD).astype(BF16), p['w_o']).reshape(b, l_lat, d)
    return None, out


def _router_body(h_ref, w_ref, b_ref, idx_ref, gate_ref, rank_ref, cnt_ref, run_ref):
    @pl.when(pl.program_id(0) == 0)
    def _():
        run_ref[...] = jnp.zeros_like(run_ref)

    lg = jnp.dot(h_ref[...], w_ref[...], precision=lax.Precision.HIGHEST, preferred_element_type=F32) + b_ref[...]
    tm = lg.shape[0]
    lane = lax.broadcasted_iota(jnp.int32, lg.shape, 1).astype(F32)
    idx_acc = jnp.zeros(lg.shape, F32)
    val_acc = jnp.zeros(lg.shape, F32)
    picked = jnp.zeros(lg.shape, F32)
    sels = []
    top = None
    den = None
    for j in range(TOP_K):
        mx = jnp.max(lg, axis=-1, keepdims=True)
        sel = jnp.min(jnp.where(lg == mx, lane, float(ROUTER_LANES)), axis=-1, keepdims=True)
        if j == 0:
            top = mx
        e = jnp.exp(mx - top)
        den = e if den is None else den + e
        hit = lane == sel
        sels.append(hit)
        idx_acc = jnp.where(lane == float(j), sel, idx_acc)
        val_acc = jnp.where(lane == float(j), e, val_acc)
        picked = jnp.where(hit, 1.0, picked)
        lg = jnp.where(hit, -jnp.inf, lg)
    idx_ref[...] = idx_acc.astype(jnp.int32)
    gate_ref[...] = val_acc / den
    rs = lax.broadcasted_iota(jnp.int32, (tm, tm), 0)
    cs = lax.broadcasted_iota(jnp.int32, (tm, tm), 1)
    before = jnp.where(rs > cs, 1.0, 0.0).astype(BF16)
    base = run_ref[...] + jnp.dot(before, picked.astype(BF16), preferred_element_type=F32)
    rank_acc = jnp.zeros(lg.shape, F32)
    for j in range(TOP_K):
        rk = jnp.sum(jnp.where(sels[j], base, 0.0), axis=-1, keepdims=True)
        rank_acc = jnp.where(lane == float(j), rk, rank_acc)
    rank_ref[...] = rank_acc.astype(jnp.int32)
    run_ref[...] = run_ref[...] + jnp.sum(picked, axis=0, keepdims=True)
    cnt_ref[...] = run_ref[...].astype(jnp.int32)


def _router(h, router_w, router_b):
    t, d = h.shape
    tm = _tile(t, (512, 256, 128, 64, 32, 16, 8))
    w = jnp.concatenate([router_w, jnp.zeros((d, ROUTER_LANES - N_EXPERTS), F32)], axis=1)
    bias = jnp.concatenate([router_b, jnp.full((ROUTER_LANES - N_EXPERTS,), -1e30, F32)]).reshape(1, ROUTER_LANES)
    tok = pl.BlockSpec((tm, ROUTER_LANES), lambda i: (i, 0))
    return pl.pallas_call(
        _router_body,
        grid=(t // tm,),
        in_specs=[pl.BlockSpec((tm, d), lambda i: (i, 0)),
                  pl.BlockSpec((d, ROUTER_LANES), lambda i: (0, 0)),
                  pl.BlockSpec((1, ROUTER_LANES), lambda i: (0, 0))],
        out_specs=[tok, tok, tok, pl.BlockSpec((1, ROUTER_LANES), lambda i: (0, 0))],
        out_shape=[jax.ShapeDtypeStruct((t, ROUTER_LANES), jnp.int32), jax.ShapeDtypeStruct((t, ROUTER_LANES), F32),
                   jax.ShapeDtypeStruct((t, ROUTER_LANES), jnp.int32), jax.ShapeDtypeStruct((1, ROUTER_LANES), jnp.int32)],
        scratch_shapes=[pltpu.VMEM((1, ROUTER_LANES), F32)],
        compiler_params=_cparams("arbitrary"),
        name="router",
    )(h, w, bias)


def _expert_body(be_ref, nb_ref, x_ref, win_ref, bin_ref, wout_ref, bout_ref, gate_ref, o_ref, win_bf, wout_bf):
    i = pl.program_id(0)
    fresh = jnp.logical_or(i == 0, be_ref[i] != be_ref[jnp.maximum(i - 1, 0)])

    @pl.when(fresh)
    def _():
        win_bf[...] = win_ref[0].astype(BF16)
        wout_bf[...] = wout_ref[0].astype(BF16)

    @pl.when(i < nb_ref[0])
    def _():
        gu = jnp.dot(x_ref[...], win_bf[...], preferred_element_type=F32) + bin_ref[0]
        glu = jnp.minimum(gu[:, :D_EXPERT], SWIGLU_LIMIT)
        lin = jnp.clip(gu[:, D_EXPERT:], -SWIGLU_LIMIT, SWIGLU_LIMIT)
        act = glu * jax.nn.sigmoid(SWIGLU_ALPHA * glu) * (lin + 1.0)
        y = jnp.dot(act.astype(BF16), wout_bf[...], preferred_element_type=F32) + bout_ref[0]
        o_ref[...] = (y * gate_ref[...]).astype(o_ref.dtype)

    @pl.when(i >= nb_ref[0])
    def _():
        o_ref[...] = jnp.zeros_like(o_ref)


def _moe(h, router_w, router_b, w_in, b_in, w_out, b_out):
    t, d = h.shape
    n_assign = t * TOP_K
    idx, gates, ranks, counts = _router(h, router_w, router_b)
    flat_e = idx[:, :TOP_K].reshape(n_assign)
    flat_g = gates[:, :TOP_K].reshape(n_assign)
    rank = ranks[:, :TOP_K].reshape(n_assign)
    counts = counts[0, :N_EXPERTS]
    padded = (counts + MOE_ROWS - 1) // MOE_ROWS * MOE_ROWS
    pad_end = jnp.cumsum(padded)
    pad_start = pad_end - padded
    dest = pad_start[flat_e] + rank
    n_blocks = -(-n_assign // MOE_ROWS) + N_EXPERTS
    n_rows = n_blocks * MOE_ROWS
    row_src = jnp.full((n_rows,), -1, jnp.int32).at[dest].set(jnp.arange(n_assign, dtype=jnp.int32))
    row_live = row_src >= 0
    row_src = jnp.maximum(row_src, 0)
    row_tok = row_src // TOP_K
    row_gate = jnp.where(row_live, flat_g[row_src], 0.0)
    block_start = jnp.arange(n_blocks, dtype=jnp.int32) * MOE_ROWS
    block_e = jnp.minimum(jnp.sum((pad_end[None, :] <= block_start[:, None]).astype(jnp.int32), axis=1), N_EXPERTS - 1)
    n_used = (pad_end[-1] // MOE_ROWS).astype(jnp.int32).reshape(1)
    xb = h.astype(BF16)[row_tok]
    f2 = 2 * D_EXPERT
    grid_spec = pltpu.PrefetchScalarGridSpec(
        num_scalar_prefetch=2,
        grid=(n_blocks,),
        in_specs=[pl.BlockSpec((MOE_ROWS, d), lambda i, be, nb: (i, 0)),
                  pl.BlockSpec((1, d, f2), lambda i, be, nb: (be[i], 0, 0)),
                  pl.BlockSpec((1, 1, f2), lambda i, be, nb: (be[i], 0, 0)),
                  pl.BlockSpec((1, D_EXPERT, d), lambda i, be, nb: (be[i], 0, 0)),
                  pl.BlockSpec((1, 1, d), lambda i, be, nb: (be[i], 0, 0)),
                  pl.BlockSpec((MOE_ROWS, 1), lambda i, be, nb: (i, 0))],
        out_specs=pl.BlockSpec((MOE_ROWS, d), lambda i, be, nb: (i, 0)),
        scratch_shapes=[pltpu.VMEM((d, f2), BF16), pltpu.VMEM((D_EXPERT, d), BF16)],
    )
    yb = pl.pallas_call(
        _expert_body,
        grid_spec=grid_spec,
        out_shape=jax.ShapeDtypeStruct((n_rows, d), BF16),
        compiler_params=_cparams("arbitrary"),
        name="moe_experts",
    )(block_e, n_used, xb, w_in, b_in.reshape(N_EXPERTS, 1, f2),
      w_out, b_out.reshape(N_EXPERTS, 1, d), row_gate.reshape(n_rows, 1))
    dest = dest.reshape(t, TOP_K)
    return sum(yb[dest[:, j]].astype(F32) for j in range(TOP_K))


def kernel(x, c, ctx, c_ctx, ada_w, ada_b, norm_mix_g, norm_ffn_g, rwkv_mu, rwkv_w_rkv, rwkv_w0, rwkv_w1, rwkv_w2, rwkv_a0, rwkv_a1, rwkv_a2, rwkv_g1, rwkv_g2, rwkv_k_k, rwkv_k_a, rwkv_r_k, rwkv_ln_g, rwkv_ln_b, rwkv_w_o, rwkv_v0, rwkv_v1, rwkv_v2, hy_w_in, hy_b_in, hy_short_w, hy_short_b, hy_f_w1, hy_f_b1, hy_f_w2, hy_f_b2, hy_f_w3, hy_f_b3, hy_f_freq, hy_f_wout, hy_bias, hy_w_out, hy_b_out, mla_w_a, mla_q_norm_g, mla_kv_norm_g, mla_w_qb, mla_w_kvb, mla_qn_g, mla_kn_g, mla_w_o, moe_router_w, moe_router_b, moe_w_in, moe_b_in, moe_w_out, moe_b_out):
    bsz, s_len, d = x.shape
    l_ctx = ctx.shape[1]
    rows = s_len // GRID_W
    row = jnp.repeat(jnp.arange(rows, dtype=jnp.int32), GRID_W)
    col = jnp.tile(jnp.arange(GRID_W, dtype=jnp.int32), rows)
    silu = jnp.concatenate([jax.nn.silu(c), jax.nn.silu(c_ctx)[None, :]], axis=0)
    silu = jnp.concatenate([silu, jnp.zeros((-(bsz + 1) % 8, d), F32)], axis=0)
    xc = ctx
    v_first = None
    for i in range(DEPTH):
        need_ctx = i < DEPTH - 1
        j = i // N_MIXERS
        kind = i % N_MIXERS
        mod = _linear(silu, ada_w[i], ada_b[i], tn=1024)
        mod_l = jnp.split(mod[:bsz], N_MOD, axis=-1)
        mod_c = [jnp.broadcast_to(m, (bsz, d)) for m in jnp.split(mod[bsz:bsz + 1], N_MOD, axis=-1)]
        sh1, sc1, g1, sh2, sc2, g2 = [m[:, None, :] for m in mod_l]
        csh1, csc1, cg1, csh2, csc2, cg2 = [m[:, None, :] for m in mod_c]
        h_l = _norm_mod(x, norm_mix_g[i], sc1, sh1)
        h_c = _norm_mod(xc, norm_mix_g[i], csc1, csh1)
        if kind == 0:
            p = {'mu': rwkv_mu[j], 'w_rkv': rwkv_w_rkv[j], 'w0': rwkv_w0[j], 'w1': rwkv_w1[j],
                 'w2': rwkv_w2[j], 'a0': rwkv_a0[j], 'a1': rwkv_a1[j], 'a2': rwkv_a2[j],
                 'g1': rwkv_g1[j], 'g2': rwkv_g2[j], 'k_k': rwkv_k_k[j], 'k_a': rwkv_k_a[j],
                 'r_k': rwkv_r_k[j], 'ln_g': rwkv_ln_g[j], 'ln_b': rwkv_ln_b[j], 'w_o': rwkv_w_o[j]}
            vres = None if j == 0 else (rwkv_v0[j - 1], rwkv_v1[j - 1], rwkv_v2[j - 1])
            y_c, y_l, v_cur = _rwkv_mixer(h_l, h_c, p, v_first, vres, need_ctx)
            if j == 0:
                v_first = v_cur
        elif kind == 1:
            p = {'w_in': hy_w_in[j], 'b_in': hy_b_in[j], 'short_w': hy_short_w[j], 'short_b': hy_short_b[j],
                 'f_w1': hy_f_w1[j], 'f_b1': hy_f_b1[j], 'f_w2': hy_f_w2[j], 'f_b2': hy_f_b2[j],
                 'f_w3': hy_f_w3[j], 'f_b3': hy_f_b3[j], 'f_freq': hy_f_freq[j], 'f_wout': hy_f_wout[j],
                 'bias': hy_bias[j], 'w_out': hy_w_out[j], 'b_out': hy_b_out[j]}
            y_l = _hyena_stream(h_l, p)
            y_c = _hyena_stream(h_c, p) if need_ctx else None
        else:
            p = {'w_a': mla_w_a[j], 'q_norm_g': mla_q_norm_g[j], 'kv_norm_g': mla_kv_norm_g[j],
                 'w_qb': mla_w_qb[j], 'w_kvb': mla_w_kvb[j], 'qn_g': mla_qn_g[j], 'kn_g': mla_kn_g[j],
                 'w_o': mla_w_o[j]}
            y_c, y_l = _mla_mixer(h_l, h_c, p, (row, col), need_ctx)
        x = x + g1 * y_l
        h2_l = _norm_mod(x, norm_ffn_g[i], sc2, sh2)
        moe_p = (moe_router_w[i], moe_router_b[i], moe_w_in[i], moe_b_in[i], moe_w_out[i], moe_b_out[i])
        if need_ctx:
            xc = xc + cg1 * y_c
            h2_c = _norm_mod(xc, norm_ffn_g[i], csc2, csh2)
            n_c = bsz * l_ctx
            f = _moe(jnp.concatenate([h2_c.reshape(-1, d), h2_l.reshape(-1, d)], axis=0), *moe_p)
            xc = xc + cg2 * f[:n_c].reshape(xc.shape)
            x = x + g2 * f[n_c:].reshape(x.shape)
        else:
            x = x + g2 * _moe(h2_l.reshape(-1, d), *moe_p).reshape(x.shape)
    return x
```

```python
import functools
import math

import jax
import jax.numpy as jnp
import numpy as np
from jax import lax
from jax.experimental import pallas as pl
from jax.experimental.pallas import tpu as pltpu

F32 = jnp.float32
BF16 = jnp.bfloat16

D_MODEL = 1024
DEPTH = 4
GRID_W = 64
N_MIXERS = 3
N_MOD = 6
NORM_EPS = 1e-6
RWKV_HEAD = 64
RWKV_HEADS = D_MODEL // RWKV_HEAD
GN_EPS = 64e-5
WKV_CHUNK = 64
WKV_GROUP = 4
FILTER_EMB = 33
FILTER_EMB_PAD = 128
DFT_MIN_OUTER = 32
FAST_DECAY = math.log(1e-2) / 0.3
SLOW_DECAY = math.log(1e-2) / 1.5
MLA_HEADS = 8
QK_NOPE = 128
QK_ROPE = 64
QK_HEAD = QK_NOPE + QK_ROPE
V_HEAD = 128
Q_LORA = 384
KV_LORA = 256
ROPE_THETA = 10000.0
N_EXPERTS = 32
TOP_K = 4
D_EXPERT = 1024
SWIGLU_LIMIT = 7.0
SWIGLU_ALPHA = 1.702
MOE_ROWS = 256
MOE_SLABS = 2
LANES = 128
ROUTER_LANES = LANES
ATTN_SLAB = 512
LOG2_E = 1.4426950408889634
MLA_SLOT = 256

V7X_VMEM_BYTES = 64 * 1024 * 1024
VMEM_LIMIT = V7X_VMEM_BYTES * 3 // 4


def _cparams(*sem):
    return pltpu.CompilerParams(dimension_semantics=sem, vmem_limit_bytes=VMEM_LIMIT)


def _tile(n, prefs):
    for t in prefs:
        if n % t == 0:
            return t
    return n


def _linear_body(a_ref, w_ref, b_ref, o_ref, *, in_act):
    a = a_ref[...]
    if in_act == 'tanh':
        a = jnp.tanh(a.astype(F32))
    elif in_act == 'sigmoid':
        a = jax.nn.sigmoid(a.astype(F32))
    acc = jnp.dot(a.astype(BF16), w_ref[...], preferred_element_type=F32)
    o_ref[...] = (acc + b_ref[...]).astype(o_ref.dtype)


def _linear(a, w, b=None, *, in_act=None, out_dtype=F32, tn=None):
    m, k = a.shape
    n = w.shape[1]
    tm = _tile(m, (512, 256, 128, 64, 32, 16, 8))
    tn = n if tn is None else tn
    assert n % tn == 0
    if b is None:
        b = jnp.zeros((n,), F32)
    return pl.pallas_call(
        functools.partial(_linear_body, in_act=in_act),
        grid=(n // tn, m // tm),
        in_specs=[pl.BlockSpec((tm, k), lambda j, i: (i, 0)),
                  pl.BlockSpec((k, tn), lambda j, i: (0, j)),
                  pl.BlockSpec((1, tn), lambda j, i: (0, j))],
        out_specs=pl.BlockSpec((tm, tn), lambda j, i: (i, j)),
        out_shape=jax.ShapeDtypeStruct((m, n), out_dtype),
        compiler_params=_cparams("parallel", "parallel"),
        name="linear",
    )(a, w.astype(BF16), b.reshape(1, n).astype(F32))


def _norm_mod_body(x_ref, g_ref, sc_ref, sh_ref, o_ref):
    x = x_ref[0]
    y = x * lax.rsqrt(jnp.mean(x * x, axis=-1, keepdims=True) + NORM_EPS)
    o_ref[0] = ((y * g_ref[...]) * (1.0 + sc_ref[0, 0]) + sh_ref[0, 0]).astype(o_ref.dtype)


def _norm_mod(x, g, sc, sh, l_ctx):
    b, l, d = x.shape
    tl = _tile(math.gcd(l_ctx, l - l_ctx), (512, 256, 128, 64, 32, 16, 8))
    stream = lambda i, j: (i, jnp.where(j * tl < l_ctx, 0, 1), 0, 0)
    return pl.pallas_call(
        _norm_mod_body,
        grid=(b, l // tl),
        in_specs=[pl.BlockSpec((1, tl, d), lambda i, j: (i, j, 0)),
                  pl.BlockSpec((1, d), lambda i, j: (0, 0)),
                  pl.BlockSpec((1, 1, 1, d), stream),
                  pl.BlockSpec((1, 1, 1, d), stream)],
        out_specs=pl.BlockSpec((1, tl, d), lambda i, j: (i, j, 0)),
        out_shape=jax.ShapeDtypeStruct((b, l, d), F32),
        compiler_params=_cparams("parallel", "parallel"),
        name="norm_mod",
    )(x, g.reshape(1, d), sc.reshape(b, 2, 1, d), sh.reshape(b, 2, 1, d))


def _stream_neighbours(t, l_ctx):
    zero = jnp.zeros_like(t[:, :1])
    pos = jnp.arange(t.shape[1], dtype=jnp.int32)[None, :, None]
    prev = jnp.where(pos == l_ctx, 0.0, jnp.concatenate([zero, t[:, :-1]], axis=1))
    nxt = jnp.where(pos == l_ctx - 1, 0.0, jnp.concatenate([t[:, 1:], zero], axis=1))
    return prev, nxt


PROJ_R, PROJ_K, PROJ_V = 0, 1024, 2048
PROJ_W1 = (3072, 3200)
PROJ_A1 = (3328, 3456)
PROJ_G1 = 3584
PROJ_V1 = 3840
PROJ_WIDTH = 4096
LORA_PAD = 128
GATE_PAD = 256


def _sigmoid(x):
    return jax.nn.sigmoid(x)


def _rwkv_post_body(*refs, has_vres):
    if has_vres:
        (proj_ref, vf_ref, w2_ref, a2_ref, g2_ref, v2_ref, vec_ref, gs_ref,
         kk_out, v_out, gate_out, lw_out, ag_out, kd_out) = refs
    else:
        (proj_ref, w2_ref, a2_ref, g2_ref, vec_ref, gs_ref,
         kk_out, v_out, gate_out, lw_out, ag_out, kd_out) = refs
    d = D_MODEL
    vec = vec_ref[...]
    k = proj_ref[:, PROJ_K:PROJ_K + d].astype(F32)
    v = proj_ref[:, PROJ_V:PROJ_V + d].astype(F32)
    if has_vres:
        v1 = proj_ref[:, PROJ_V1:PROJ_V1 + LORA_PAD]
        vgate = _sigmoid(vec[6:7] + jnp.dot(v1, v2_ref[...], preferred_element_type=F32))
        v = v + (vf_ref[...].astype(F32) - v) * vgate
    v_out[...] = v.astype(v_out.dtype)
    kraw = k * vec[4:5]
    ss = jnp.dot((kraw * kraw).astype(BF16), gs_ref[...], preferred_element_type=F32)
    kk_out[...] = (kraw / jnp.maximum(jnp.sqrt(ss), 1e-12)).astype(kk_out.dtype)
    g1 = proj_ref[:, PROJ_G1:PROJ_G1 + GATE_PAD].astype(F32)
    gate_out[...] = jnp.dot(_sigmoid(g1).astype(BF16), g2_ref[...], preferred_element_type=F32).astype(gate_out.dtype)
    for dd in range(2):
        w1 = proj_ref[:, PROJ_W1[dd]:PROJ_W1[dd] + LORA_PAD].astype(F32)
        z = vec[dd:dd + 1] + jnp.dot(jnp.tanh(w1).astype(BF16), w2_ref[dd], preferred_element_type=F32)
        softplus = jnp.maximum(-z, 0.0) + jnp.log(1.0 + jnp.exp(-jnp.abs(z)))
        lw_out[dd] = -jnp.exp(-softplus - 0.5)
        a1 = proj_ref[:, PROJ_A1[dd]:PROJ_A1[dd] + LORA_PAD]
        ag = _sigmoid(vec[2 + dd:3 + dd] + jnp.dot(a1, a2_ref[dd], preferred_element_type=F32))
        ag_out[dd] = ag.astype(ag_out.dtype)
        kd_out[dd] = (k * (1.0 + (ag - 1.0) * vec[5:6])).astype(kd_out.dtype)


def _rwkv_post(proj, v_first, w2p, a2p, g2p, v2p, vec, gsum):
    t, d = proj.shape[0], D_MODEL
    tm = _tile(t, (256, 128, 64, 32, 16))
    has_vres = v_first is not None
    full = lambda a: pl.BlockSpec(a.shape, lambda i: (0,) * a.ndim)
    row = pl.BlockSpec((tm, d), lambda i: (i, 0))
    row2 = pl.BlockSpec((2, tm, d), lambda i: (0, i, 0))
    ins = [proj] + ([v_first] if has_vres else []) + [w2p, a2p, g2p] + ([v2p] if has_vres else []) + [vec, gsum]
    in_specs = ([pl.BlockSpec((tm, PROJ_WIDTH), lambda i: (i, 0))] + ([row] if has_vres else [])
                + [full(a) for a in ins[(2 if has_vres else 1):]])
    return pl.pallas_call(
        functools.partial(_rwkv_post_body, has_vres=has_vres),
        grid=(t // tm,),
        in_specs=in_specs,
        out_specs=[row, row, row, row2, row2, row2],
        out_shape=[jax.ShapeDtypeStruct((t, d), BF16),
                   jax.ShapeDtypeStruct((t, d), BF16),
                   jax.ShapeDtypeStruct((t, d), BF16),
                   jax.ShapeDtypeStruct((2, t, d), F32),
                   jax.ShapeDtypeStruct((2, t, d), BF16),
                   jax.ShapeDtypeStruct((2, t, d), BF16)],
        compiler_params=_cparams("parallel"),
        name="rwkv_post",
    )(*ins)


def _head_groups(y):
    gw = WKV_GROUP * RWKV_HEAD
    lane_head = jnp.right_shift(lax.broadcasted_iota(jnp.int32, (y.shape[0], gw), 1), 6)
    out = []
    for g in range(D_MODEL // gw):
        yg = y[:, g * gw:(g + 1) * gw].astype(BF16)
        out.append(jnp.concatenate([jnp.where(lane_head == h, yg, jnp.zeros_like(yg)) for h in range(WKV_GROUP)], axis=0))
    return out


def _gmm(x, ybd):
    gw = WKV_GROUP * RWKV_HEAD
    return jnp.concatenate([jnp.dot(x[:, g * gw:(g + 1) * gw].astype(BF16), yb, preferred_element_type=F32)
                            for g, yb in enumerate(ybd)], axis=-1)


def _gmm_nt(x, ybd):
    gw = WKV_GROUP * RWKV_HEAD
    return jnp.concatenate([lax.dot_general(x[:, g * gw:(g + 1) * gw].astype(BF16), yb, (((1,), (1,)), ((), ())),
                                            preferred_element_type=F32)
                            for g, yb in enumerate(ybd)], axis=-1)


def _wkv_chunks(r, kk, v, lw, ag, kd, s0, rev):
    each = lambda f, *ls: [f(*a) for a in zip(*ls)]
    c, d = r[0].shape
    row0 = lax.broadcasted_iota(jnp.int32, (c, d), 0)
    col0 = jnp.bitwise_and(lax.broadcasted_iota(jnp.int32, (c, d), 1), RWKV_HEAD - 1)
    incl = [(row0 <= col0) if rv else (row0 >= col0) for rv in rev]
    strict = [(row0 < col0) if rv else (row0 > col0) for rv in rev]
    rs = lax.broadcasted_iota(jnp.int32, (c, c), 0)
    cs = lax.broadcasted_iota(jnp.int32, (c, c), 1)
    tri = [jnp.where((rs <= cs) if rv else (rs >= cs), 1.0, 0.0).astype(BF16) for rv in rev]
    hi = each(lambda x: x.astype(BF16), lw)
    rem = each(lambda x, h: x - h.astype(F32), lw, hi)
    mid = each(lambda x: x.astype(BF16), rem)
    lo = each(lambda x, m: (x - m.astype(F32)).astype(BF16), rem, mid)
    cum = each(lambda t, a, b_, c_: (jnp.dot(t, a, preferred_element_type=F32) + jnp.dot(t, b_, preferred_element_type=F32)
                                     + jnp.dot(t, c_, preferred_element_type=F32)), tri, hi, mid, lo)
    cum_end = [x[0:1, :] if rv else x[c - 1:c, :] for x, rv in zip(cum, rev)]
    g_rest = each(lambda e, x: jnp.exp(e - x), cum_end, cum)
    g_inv = each(lambda x: jnp.exp(-x), cum)
    bvec = each(lambda a, b_: a * b_, kk, ag)
    rt = each(lambda a, x: a * jnp.exp(x), r, cum)
    kt = each(lambda a, g: a * g, kd, g_inv)
    bt = each(lambda a, g: a * g, bvec, g_inv)
    at = each(lambda a, x, w: -a * jnp.exp(x - w), kk, cum, lw)
    ar = each(lambda a, b_: jnp.concatenate([a, b_], axis=0), at, rt)
    ab = each(lambda a, b_: _gmm_nt(a, _head_groups(b_)), ar, bt)
    ak = each(lambda a, b_: _gmm_nt(a, _head_groups(b_)), ar, kt)
    a_ab = each(lambda m, x: jnp.where(m, x[:c], 0.0), strict, ab)
    a_rb = each(lambda m, x: jnp.where(m, x[c:], 0.0), incl, ab)
    a_ak = each(lambda m, x: jnp.where(m, x[:c], 0.0), strict, ak)
    a_rk = each(lambda m, x: jnp.where(m, x[c:], 0.0), incl, ak)
    same16 = jnp.right_shift(row0, 4) == jnp.right_shift(col0, 4)
    same32 = jnp.right_shift(row0, 5) == jnp.right_shift(col0, 5)
    eye = jnp.where(row0 == col0, 1.0, 0.0)
    npow = each(lambda x: jnp.where(same16, x, 0.0), a_ab)
    p = each(lambda x: eye + x, npow)
    for _ in range(3):
        npow = each(lambda x: _gmm(x, _head_groups(x)), npow)
        p = each(lambda x, y: x + _gmm(x, _head_groups(y)), p, npow)
    off32 = jnp.logical_and(same32, jnp.logical_not(same16))
    t1 = each(lambda x, y: _gmm(jnp.where(off32, x, 0.0), _head_groups(y)), a_ab, p)
    p = each(lambda x, y: x + _gmm(x, _head_groups(y)), p, t1)
    t2 = each(lambda x, y: _gmm(jnp.where(same32, 0.0, x), _head_groups(y)), a_ab, p)
    p = each(lambda x, y: x + _gmm(x, _head_groups(y)), p, t2)
    pbd = each(_head_groups, p)
    vbd = each(_head_groups, v)
    wmat = each(lambda x, y: _gmm(x, _head_groups(y)), p, at)
    vt = each(lambda y: _gmm_nt(eye, y), vbd)
    xt = each(lambda x, y: _gmm_nt(x, _head_groups(y)), vt, a_ak)
    upt = each(_gmm_nt, xt, pbd)
    yp = each(_gmm, a_rk, vbd)
    sp = each(lambda x, a, g: _gmm(x, _head_groups(a * g)), vt, kd, g_rest)
    ut = each(lambda s_, w, u: _gmm_nt(s_, _head_groups(w)) + u, s0, wmat, upt)
    ys = each(lambda s_, q, x: _gmm_nt(q, _head_groups(s_)) + x, s0, rt, yp)
    y = each(lambda y0, a, u: y0 + _gmm_nt(a, _head_groups(u)), ys, a_rb, ut)
    s_end = each(lambda s_, e, u, b_, g, x: s_ * jnp.exp(e) + _gmm(u, _head_groups(b_ * g)) + x,
                 s0, cum_end, ut, bvec, g_rest, sp)
    return y, s_end


def _wkv_body(*refs, n_batch):
    ins, outs, s_ref = refs[:12], refs[12:14], refs[14]

    @pl.when(pl.program_id(0) == 0)
    def _():
        s_ref[...] = jnp.zeros_like(s_ref)

    scans = [(dd, bi) for dd in range(2) for bi in range(n_batch)]
    get = lambda k, lead: [ins[6 * dd + k][(0, bi) if lead else bi].astype(F32) for dd, bi in scans]
    y, s_end = _wkv_chunks(get(0, False), get(1, False), get(2, False), get(3, True), get(4, True), get(5, True),
                           [s_ref[i] for i in range(len(scans))], [dd == 1 for dd, _ in scans])
    for i, (dd, bi) in enumerate(scans):
        outs[dd][bi] = y[i]
        s_ref[i] = s_end[i]


def _wkv(proj, kk, v, lw, ag, kd, l_ctx):
    b, l, d = kk.shape
    c = WKV_CHUNK
    nc, nctx = l // c, l_ctx // c
    assert c == RWKV_HEAD and l % c == 0 and l_ctx % c == 0

    def fwd(j):
        return j

    def bwd(j):
        return jnp.where(j < nctx, nctx - 1 - j, nc + nctx - 1 - j)

    in_specs, out_specs = [], []
    for dd, ch in enumerate((fwd, bwd)):
        shared = pl.BlockSpec((b, c, d), lambda j, ch=ch: (0, ch(j), 0))
        per_dir = pl.BlockSpec((1, b, c, d), lambda j, dd=dd, ch=ch: (dd, 0, ch(j), 0))
        in_specs += [shared, shared, shared, per_dir, per_dir, per_dir]
        out_specs.append(shared)
    return pl.pallas_call(
        functools.partial(_wkv_body, n_batch=b),
        grid=(nc,),
        in_specs=in_specs,
        out_specs=out_specs,
        out_shape=[jax.ShapeDtypeStruct((b, l, d), F32)] * 2,
        scratch_shapes=[pltpu.VMEM((2 * b, RWKV_HEAD, d), F32)],
        compiler_params=_cparams("arbitrary"),
        name="wkv",
    )(*([proj, kk, v, lw, ag, kd] * 2))


def _rwkv_readout_body(yf_ref, yr_ref, r_ref, kd_ref, v_ref, gate_ref, gm_ref, gs_ref, vec_ref, wo_ref, o_ref):
    vec = vec_ref[...]
    y = yf_ref[...] + yr_ref[...]
    gm = gm_ref[...]
    yh = y.astype(BF16)
    yl = (y - yh.astype(F32)).astype(BF16)
    mean = jnp.dot(yh, gm, preferred_element_type=F32) + jnp.dot(yl, gm, preferred_element_type=F32)
    yc = y - mean
    var = jnp.dot((yc * yc).astype(BF16), gm, preferred_element_type=F32)
    yn = yc * lax.rsqrt(var + GN_EPS) * vec[0:1] + vec[1:2]
    r = r_ref[...].astype(F32)
    kds = kd_ref[0].astype(F32) + kd_ref[1].astype(F32)
    bonus = jnp.dot((r * kds * vec[2:3]).astype(BF16), gs_ref[...], preferred_element_type=F32)
    out = (yn + bonus * v_ref[...].astype(F32)) * gate_ref[...].astype(F32)
    o_ref[...] = jnp.dot(out.astype(BF16), wo_ref[...], preferred_element_type=F32)


def _rwkv_readout(y_fwd, y_rev, proj, kd, v, gate, gmean, gsum, vec, w_o):
    t, d = v.shape
    tm = _tile(t, (256, 128, 64, 32, 16))
    full = lambda a: pl.BlockSpec(a.shape, lambda i: (0,) * a.ndim)
    row = pl.BlockSpec((tm, d), lambda i: (i, 0))
    row2 = pl.BlockSpec((2, tm, d), lambda i: (0, i, 0))
    return pl.pallas_call(
        _rwkv_readout_body,
        grid=(t // tm,),
        in_specs=[row, row, row, row2, row, row, full(gmean), full(gsum), full(vec), full(w_o)],
        out_specs=row,
        out_shape=jax.ShapeDtypeStruct((t, d), F32),
        compiler_params=_cparams("parallel"),
        name="rwkv_readout",
    )(y_fwd, y_rev, proj, kd, v, gate, gmean, gsum, vec, w_o)


def _pad_cols(w, width):
    return jnp.concatenate([w, jnp.zeros((w.shape[0], width - w.shape[1]), w.dtype)], axis=1)


def _pad_rows(w, height):
    return jnp.concatenate([w, jnp.zeros((height - w.shape[0], w.shape[1]), w.dtype)], axis=0)


def _rwkv_mixer(h, l_ctx, p, v_first, vres):
    b, l, d = h.shape
    t = b * l
    prev, nxt = _stream_neighbours(h, l_ctx)
    xx = 0.5 * (prev + nxt) - h
    mu = p['mu']
    mats = [(p['w_rkv'][0], 0, d), (p['w_rkv'][1], 2, d), (p['w_rkv'][2], 3, d),
            (p['w1'][0], 1, LORA_PAD), (p['w1'][1], 1, LORA_PAD), (p['a1'][0], 4, LORA_PAD), (p['a1'][1], 4, LORA_PAD),
            (p['g1'], 5, GATE_PAD), (vres[1] if vres is not None else jnp.zeros((d, 1), F32), 3, 2 * LORA_PAD)]
    top = jnp.concatenate([_pad_cols(w, width) for w, _, width in mats], axis=1)
    bot = jnp.concatenate([_pad_cols(mu[m][:, None] * w, width) for w, m, width in mats], axis=1)
    wcat = jnp.concatenate([top, bot], axis=0)
    assert wcat.shape[1] == PROJ_WIDTH
    a_in = jnp.concatenate([h, xx], axis=-1).reshape(t, 2 * d).astype(BF16)
    proj = _linear(a_in, wcat, out_dtype=BF16, tn=1024)
    w2p = jnp.stack([_pad_rows(p['w2'][i], LORA_PAD) for i in range(2)]).astype(BF16)
    a2p = jnp.stack([_pad_rows(p['a2'][i], LORA_PAD) for i in range(2)]).astype(BF16)
    g2p = _pad_rows(p['g2'], GATE_PAD).astype(BF16)
    head_of = jnp.arange(d, dtype=jnp.int32) // RWKV_HEAD
    same_head = head_of[:, None] == head_of[None, :]
    gsum = jnp.where(same_head, 1.0, 0.0).astype(BF16)
    gmean = jnp.where(same_head, 1.0 / RWKV_HEAD, 0.0).astype(BF16)
    zero = jnp.zeros((d,), F32)
    vec = jnp.stack([p['w0'][0], p['w0'][1], p['a0'][0], p['a0'][1], p['k_k'], p['k_a'],
                     vres[0] if vres is not None else zero, zero])
    if vres is not None:
        kk, v, gate, lw, ag, kd = _rwkv_post(proj, v_first.reshape(t, d), w2p, a2p, g2p,
                                             _pad_rows(vres[2], LORA_PAD).astype(BF16), vec, gsum)
    else:
        kk, v, gate, lw, ag, kd = _rwkv_post(proj, None, w2p, a2p, g2p, None, vec, gsum)
    sh3 = lambda a: a.reshape(a.shape[:-2] + (b, l, a.shape[-1]))
    y_fwd, y_rev = _wkv(sh3(proj), sh3(kk), sh3(v), sh3(lw), sh3(ag), sh3(kd), l_ctx)
    vec_o = jnp.stack([p['ln_g'], p['ln_b'], p['r_k'].reshape(d)] + [zero] * 5)
    out = _rwkv_readout(y_fwd.reshape(t, d), y_rev.reshape(t, d), proj, kd, v, gate, gmean, gsum, vec_o, p['w_o'].astype(BF16))
    return out.reshape(b, l, d), v.reshape(b, l, d)


def _filter_body(z_ref, t_ref, w1_ref, b1_ref, w2_ref, b2_ref, w3_ref, b3_ref, fr_ref, wo_ref, dl_ref, o_ref, *, length):
    hp = lax.Precision.HIGHEST
    fr = fr_ref[...]
    hdn = jnp.sin(fr[0:1] * (jnp.dot(z_ref[...], w1_ref[...], precision=hp, preferred_element_type=F32) + b1_ref[...]))
    hdn = jnp.sin(fr[1:2] * (jnp.dot(hdn, w2_ref[...], precision=hp, preferred_element_type=F32) + b2_ref[...]))
    hdn = jnp.sin(fr[2:3] * (jnp.dot(hdn, w3_ref[...], precision=hp, preferred_element_type=F32) + b3_ref[...]))
    filt = jnp.dot(hdn, wo_ref[...], precision=hp, preferred_element_type=F32) * jnp.exp(-t_ref[...] * dl_ref[...])
    tl = filt.shape[0]
    n = pl.program_id(0) * tl + lax.broadcasted_iota(jnp.int32, filt.shape, 0)
    o_ref[...] = jnp.where(n == length, 0.0, filt * (0.5 / length))


def _hyena_filters(length, p):
    n = jnp.arange(2 * length, dtype=jnp.int32)
    pos = jnp.where(n < length, n, 2 * length - n).astype(F32)[:, None]
    t = pos / (length - 1)
    bands = (FILTER_EMB - 1) // 2
    f = jnp.linspace(1e-4, bands - 1, bands, dtype=F32)[None, :]
    ang = 2.0 * math.pi * f * pos / length
    z = jnp.concatenate([t, jnp.cos(ang), -jnp.sin(ang), jnp.zeros((2 * length, FILTER_EMB_PAD - FILTER_EMB), F32)], axis=-1)
    w1 = jnp.concatenate([p['f_w1'], jnp.zeros((FILTER_EMB_PAD - FILTER_EMB, p['f_w1'].shape[1]), F32)], axis=0)
    deltas = jnp.abs(jnp.linspace(FAST_DECAY, SLOW_DECAY, D_MODEL, dtype=F32))[None, :]
    fw = p['f_w2'].shape[0]
    tl = _tile(length, (512, 256, 128, 64, 32, 16, 8))
    half = length // tl
    full = lambda a: pl.BlockSpec(a.shape, lambda i: (0,) * a.ndim)
    args = [w1, p['f_b1'].reshape(1, fw), p['f_w2'], p['f_b2'].reshape(1, fw), p['f_w3'], p['f_b3'].reshape(1, fw),
            p['f_freq']]
    return pl.pallas_call(
        functools.partial(_filter_body, length=length),
        grid=(2 * half,),
        in_specs=[pl.BlockSpec((tl, FILTER_EMB_PAD), lambda i: (i, 0)), pl.BlockSpec((tl, 1), lambda i: (i, 0))]
                 + [full(a) for a in args]
                 + [pl.BlockSpec((fw, D_MODEL), lambda i: (0, jnp.where(i < half, 0, 1))),
                    full(deltas)],
        out_specs=pl.BlockSpec((tl, D_MODEL), lambda i: (i, 0)),
        out_shape=jax.ShapeDtypeStruct((2 * length, D_MODEL), F32),
        compiler_params=_cparams("parallel"),
        name="hyena_filter",
    )(z, t, *args, p['f_wout'], deltas)


def _colmm_body(f_ref, x_ref, o_ref):
    o_ref[0] = jnp.dot(f_ref[...], x_ref[0].astype(BF16), preferred_element_type=F32).astype(o_ref.dtype)


def _colmm(f, x, out_dtype):
    bsz, q, n = x.shape
    pp = f.shape[0]
    tn = _tile(n, (4096, 2048, 1024, 512, 256, 128))
    return pl.pallas_call(
        _colmm_body,
        grid=(bsz, n // tn),
        in_specs=[pl.BlockSpec((pp, q), lambda i, j: (0, 0)),
                  pl.BlockSpec((1, q, tn), lambda i, j: (i, 0, j))],
        out_specs=pl.BlockSpec((1, pp, tn), lambda i, j: (i, 0, j)),
        out_shape=jax.ShapeDtypeStruct((bsz, pp, n), out_dtype),
        compiler_params=_cparams("parallel", "parallel"),
        name="dft_outer",
    )(f.astype(BF16), x)


def _dft_inner_fwd_body(a_ref, m_ref, o_ref):
    n2 = a_ref.shape[3]
    a = jnp.concatenate([a_ref[0, 0, 0], a_ref[0, 1, 0]], axis=0).astype(BF16)
    x = jnp.dot(m_ref[0], a, preferred_element_type=F32)
    o_ref[0, 0, 0] = x[:n2]
    o_ref[0, 1, 0] = x[n2:]


def _dft_inner_conv_body(a_ref, g_ref, m_ref, mi_ref, o_ref):
    n2 = a_ref.shape[3]
    a = jnp.concatenate([a_ref[0, 0, 0], a_ref[0, 1, 0]], axis=0).astype(BF16)
    x = jnp.dot(m_ref[0], a, preferred_element_type=F32)
    xre, xim = x[:n2], x[n2:]
    gre, gim = g_ref[0, 0, 0], g_ref[0, 1, 0]
    y = jnp.concatenate([xre * gre - xim * gim, xre * gim + xim * gre], axis=0).astype(BF16)
    o = jnp.dot(mi_ref[0], y, preferred_element_type=F32)
    o_ref[0, 0, 0] = o[:n2].astype(o_ref.dtype)
    o_ref[0, 1, 0] = o[n2:].astype(o_ref.dtype)


def _dft_tables(n1, n2):
    n = n1 * n2
    i1 = jnp.arange(n1, dtype=jnp.int32)
    ph1 = (2.0 * math.pi / n1) * ((i1[:, None] * i1[None, :]) % n1).astype(F32)
    c1, s1 = jnp.cos(ph1), jnp.sin(ph1)
    k1 = i1[:, None, None]
    k2 = jnp.arange(n2, dtype=jnp.int32)[None, :, None]
    m2 = jnp.arange(n2, dtype=jnp.int32)[None, None, :]
    th = (2.0 * math.pi / n) * ((m2 * (k1 + n1 * k2)) % n).astype(F32)
    mre, mim = jnp.cos(th), -jnp.sin(th)
    m_fwd = jnp.concatenate([jnp.concatenate([mre, -mim], axis=2), jnp.concatenate([mim, mre], axis=2)], axis=1)
    qre, qim = jnp.swapaxes(mre, 1, 2), -jnp.swapaxes(mim, 1, 2)
    m_inv = jnp.concatenate([jnp.concatenate([qre, -qim], axis=2), jnp.concatenate([qim, qre], axis=2)], axis=1)
    f_fwd = jnp.concatenate([c1, -s1], axis=0)
    f_inv = jnp.concatenate([c1, -s1], axis=1)
    return f_fwd, f_inv, m_fwd.astype(BF16), m_inv.astype(BF16)


def _two_sided_long_conv(z, filt):
    b, l, d = z.shape
    n = 2 * l
    n1 = max(DFT_MIN_OUTER, 2 ** ((int(math.log2(n)) - 1) // 2))
    n2 = n // n1
    f_fwd, f_inv, m_fwd, m_inv = _dft_tables(n1, n2)
    blk = pl.BlockSpec((1, 2, 1, n2, d), lambda i, j: (i, 0, j, 0, 0))
    gblk = pl.BlockSpec((1, 2, 1, n2, d), lambda i, j: (0, 0, j, 0, 0))
    mat = pl.BlockSpec((1, 2 * n2, 2 * n2), lambda i, j: (j, 0, 0))
    ga = _colmm(f_fwd, filt.reshape(1, n1, n2 * d), BF16).reshape(1, 2, n1, n2, d)
    gspec = pl.pallas_call(
        _dft_inner_fwd_body,
        grid=(1, n1),
        in_specs=[blk, mat],
        out_specs=blk,
        out_shape=jax.ShapeDtypeStruct((1, 2, n1, n2, d), F32),
        compiler_params=_cparams("parallel", "parallel"),
        name="dft_inner_fwd",
    )(ga, m_fwd)
    za = _colmm(f_fwd[:, :n1 // 2], z.reshape(b, n1 // 2, n2 * d), BF16).reshape(b, 2, n1, n2, d)
    bb = pl.pallas_call(
        _dft_inner_conv_body,
        grid=(b, n1),
        in_specs=[blk, gblk, mat, mat],
        out_specs=blk,
        out_shape=jax.ShapeDtypeStruct((b, 2, n1, n2, d), BF16),
        compiler_params=_cparams("parallel", "parallel"),
        name="dft_inner_conv",
    )(za, gspec, m_fwd, m_inv)
    return _colmm(f_inv[:n1 // 2], bb.reshape(b, 2 * n1, n2 * d), F32).reshape(b, l, d)


def _hyena_mixer(h, l_ctx, p):
    b, l, d = h.shape
    u = _linear(h.reshape(b * l, d).astype(BF16), p['w_in'], p['b_in'], tn=1024).reshape(b, l, 3 * d)
    prev, nxt = _stream_neighbours(u, l_ctx)
    sw = p['short_w']
    u = prev * sw[0] + u * sw[1] + nxt * sw[2] + p['short_b']
    x0, x1, v = jnp.split(u, 3, axis=-1)
    z = v * x1
    conv = jnp.concatenate([_two_sided_long_conv(z[:, :l_ctx], _hyena_filters(l_ctx, p)),
                            _two_sided_long_conv(z[:, l_ctx:], _hyena_filters(l - l_ctx, p))], axis=1)
    z = conv + z * p['bias']
    return _linear((x0 * z).reshape(b * l, d).astype(BF16), p['w_out'], p['b_out']).reshape(b, l, d)


def _attn_body(q_ref, k_ref, v_ref, o_ref, m_ref, acc_ref, *, dv, parts):
    j = pl.program_id(3)
    tq, tk = q_ref.shape[2], k_ref.shape[2]
    sub = tq // parts

    @pl.when(j == 0)
    def _():
        m_ref[...] = jnp.full_like(m_ref, -jnp.inf)
        acc_ref[...] = jnp.zeros_like(acc_ref)

    k = k_ref[0, 0]
    v = v_ref[0]
    slabs = [pl.ds(i * sub, sub) for i in range(parts)]
    s = [lax.dot_general(q_ref[0, 0, x, :], k, (((1,), (1,)), ((), ())), preferred_element_type=F32) for x in slabs]
    m_prev = [m_ref[x, :] for x in slabs]
    m_new = [jnp.maximum(mp, jnp.max(si, axis=-1, keepdims=True)) for mp, si in zip(m_prev, s)]
    pr = [jnp.exp2(si - jnp.concatenate([mn] * (tk // LANES), axis=1)).astype(BF16) for si, mn in zip(s, m_new)]
    alpha = [jnp.exp2(mp - mn) for mp, mn in zip(m_prev, m_new)]
    pv = [jnp.dot(p_, v, preferred_element_type=F32) for p_ in pr]
    for x, a, o, mn in zip(slabs, alpha, pv, m_new):
        acc_ref[x, :] = jnp.concatenate([a] * (acc_ref.shape[1] // LANES), axis=1) * acc_ref[x, :] + o
        m_ref[x, :] = mn

    @pl.when(j == pl.num_programs(3) - 1)
    def _():
        acc = acc_ref[...]
        o_ref[0] = (acc[:, :dv] / acc[:, dv:dv + 1]).astype(o_ref.dtype)


def _attend(q, k, v_ext):
    b, nh, sq, dq = q.shape
    sk = k.shape[2]
    dv = v_ext.shape[2] // (2 * nh)
    tq = _tile(sq, (2048, 1024, 512, 256, 128, 64, 32, 16, 8))
    tk = _tile(sk, (1280, 1024, 640, 512, 256, 128))
    parts = max(1, tq // ATTN_SLAB)
    return pl.pallas_call(
        functools.partial(_attn_body, dv=dv, parts=parts),
        grid=(b, nh, sq // tq, sk // tk),
        in_specs=[pl.BlockSpec((1, 1, tq, dq), lambda bi, hi, i, j: (bi, hi, i, 0)),
                  pl.BlockSpec((1, 1, tk, dq), lambda bi, hi, i, j: (bi, hi, j, 0)),
                  pl.BlockSpec((1, tk, 2 * dv), lambda bi, hi, i, j: (bi, j, hi))],
        out_specs=pl.BlockSpec((1, tq, dv), lambda bi, hi, i, j: (bi, i, hi)),
        out_shape=jax.ShapeDtypeStruct((b, sq, nh * dv), BF16),
        scratch_shapes=[pltpu.VMEM((tq, LANES), F32), pltpu.VMEM((tq, 2 * dv), F32)],
        compiler_params=_cparams("parallel", "parallel", "parallel", "arbitrary"),
        name="attention",
    )(q, k, v_ext)


def _mla_prep_body(qkv_ref, wq_ref, wkv_ref, gq_ref, gkv_ref, gqn_ref, gkn_ref, cos_ref, sin_ref, q_out, k_out, v_out):
    x = qkv_ref[0]
    qc, kvc, kpe = x[:, :Q_LORA], x[:, Q_LORA:Q_LORA + KV_LORA], x[:, Q_LORA + KV_LORA:]
    qn = qc * lax.rsqrt(jnp.mean(qc * qc, axis=-1, keepdims=True) + NORM_EPS) * gq_ref[...]
    kvn = kvc * lax.rsqrt(jnp.mean(kvc * kvc, axis=-1, keepdims=True) + NORM_EPS) * gkv_ref[...]
    q = jnp.dot(qn.astype(BF16), wq_ref[...], preferred_element_type=F32)
    kv = jnp.dot(kvn.astype(BF16), wkv_ref[...], preferred_element_type=F32)
    cos, sin = cos_ref[0], sin_ref[0]
    quarter = QK_ROPE // 4

    def rope(t):
        swapped = jnp.concatenate([t[:, quarter:2 * quarter], t[:, :quarter],
                                   t[:, 3 * quarter:], t[:, 2 * quarter:3 * quarter]], axis=-1)
        return t * cos + swapped * sin

    gqn, gkn = gqn_ref[...], gkn_ref[...]
    kpe_rot = rope(kpe * gkn[:, QK_NOPE:])
    kpe_ss = jnp.sum(kpe * kpe, axis=-1, keepdims=True)
    ones_col = jnp.where(lax.broadcasted_iota(jnp.int32, (x.shape[0], V_HEAD), 1) == 0, 1.0, 0.0).astype(BF16)
    for h in range(MLA_HEADS):
        qh = q[:, h * MLA_SLOT:h * MLA_SLOT + QK_HEAD]
        qh = qh * lax.rsqrt(jnp.sum(qh * qh, axis=-1, keepdims=True) * (1.0 / QK_HEAD) + NORM_EPS) * gqn
        qh = jnp.concatenate([qh[:, :QK_NOPE], rope(qh[:, QK_NOPE:])], axis=-1)
        q_out[0, h] = (qh * (QK_HEAD ** -0.5 * LOG2_E)).astype(q_out.dtype)
        kn = kv[:, h * MLA_SLOT:h * MLA_SLOT + QK_NOPE]
        rstd = lax.rsqrt((jnp.sum(kn * kn, axis=-1, keepdims=True) + kpe_ss) * (1.0 / QK_HEAD) + NORM_EPS)
        k_out[0, h] = jnp.concatenate([kn * rstd * gkn[:, :QK_NOPE], kpe_rot * rstd], axis=-1).astype(k_out.dtype)
        v_out[0, :, h * MLA_SLOT:h * MLA_SLOT + V_HEAD] = kv[:, h * MLA_SLOT + QK_NOPE:(h + 1) * MLA_SLOT].astype(v_out.dtype)
        v_out[0, :, h * MLA_SLOT + V_HEAD:(h + 1) * MLA_SLOT] = ones_col


def _mla_prep(qkv, p, cos, sin):
    b, l, w = qkv.shape
    tm = _tile(l, (256, 128, 64, 32, 16))
    nh = MLA_HEADS
    pad_heads = lambda m, width: jnp.pad(m.reshape(m.shape[0], nh, width), ((0, 0), (0, 0), (0, MLA_SLOT - width))
                                         ).reshape(m.shape[0], nh * MLA_SLOT)
    wq = pad_heads(p['w_qb'], QK_HEAD).astype(BF16)
    wkv = p['w_kvb'].astype(BF16)
    full = lambda a: pl.BlockSpec(a.shape, lambda i, j: (0,) * a.ndim)
    args = [wq, wkv, p['q_norm_g'].reshape(1, Q_LORA), p['kv_norm_g'].reshape(1, KV_LORA),
            p['qn_g'].reshape(1, QK_HEAD), p['kn_g'].reshape(1, QK_HEAD)]
    tab = pl.BlockSpec((1, tm, QK_ROPE), lambda i, j: (0, j, 0))
    head = pl.BlockSpec((1, nh, tm, QK_HEAD), lambda i, j: (i, 0, j, 0))
    return pl.pallas_call(
        _mla_prep_body,
        grid=(b, l // tm),
        in_specs=[pl.BlockSpec((1, tm, w), lambda i, j: (i, j, 0))] + [full(a) for a in args] + [tab, tab],
        out_specs=[head, head, pl.BlockSpec((1, tm, nh * MLA_SLOT), lambda i, j: (i, j, 0))],
        out_shape=[jax.ShapeDtypeStruct((b, nh, l, QK_HEAD), BF16), jax.ShapeDtypeStruct((b, nh, l, QK_HEAD), BF16),
                   jax.ShapeDtypeStruct((b, l, nh * MLA_SLOT), BF16)],
        compiler_params=_cparams("parallel", "parallel"),
        name="mla_prep",
    )(qkv, *args, cos, sin)


def _rope_tables(row, col, l_ctx):
    half = QK_ROPE // 2
    inv = ROPE_THETA ** (-jnp.arange(0, half, 2, dtype=F32) / half)
    ang_r = row.astype(F32)[:, None] * inv[None, :]
    ang_c = col.astype(F32)[:, None] * inv[None, :]
    cos = jnp.concatenate([jnp.cos(ang_r)] * 2 + [jnp.cos(ang_c)] * 2, axis=-1)
    sin = jnp.concatenate([-jnp.sin(ang_r), jnp.sin(ang_r), -jnp.sin(ang_c), jnp.sin(ang_c)], axis=-1)
    cos = jnp.concatenate([jnp.ones((l_ctx, QK_ROPE), F32), cos], axis=0)
    sin = jnp.concatenate([jnp.zeros((l_ctx, QK_ROPE), F32), sin], axis=0)
    return cos[None], sin[None]


def _mla_mixer(h, l_ctx, p, rope_pos):
    b, l, d = h.shape
    qkv = _linear(h.reshape(b * l, d).astype(BF16), p['w_a']).reshape(b, l, -1)
    cos, sin = _rope_tables(rope_pos[0], rope_pos[1], l_ctx)
    q, k, v_ext = _mla_prep(qkv, p, cos, sin)
    o = jnp.concatenate([_attend(q[:, :, :l_ctx], k[:, :, :l_ctx], v_ext[:, :l_ctx]),
                         _attend(q[:, :, l_ctx:], k, v_ext)], axis=1)
    return _linear(o.reshape(b * l, MLA_HEADS * V_HEAD), p['w_o']).reshape(b, l, d)


def _router_body(h_ref, w_ref, b_ref, idx_ref, gate_ref, rank_ref, cnt_ref, run_ref):
    @pl.when(pl.program_id(0) == 0)
    def _():
        run_ref[...] = jnp.zeros_like(run_ref)

    lg = jnp.dot(h_ref[...], w_ref[...], precision=lax.Precision.HIGHEST, preferred_element_type=F32) + b_ref[...]
    tm = lg.shape[0]
    lane = lax.broadcasted_iota(jnp.int32, lg.shape, 1).astype(F32)
    idx_acc = jnp.zeros(lg.shape, F32)
    val_acc = jnp.zeros(lg.shape, F32)
    picked = jnp.zeros(lg.shape, F32)
    sels = []
    top = None
    den = None
    for j in range(TOP_K):
        mx = jnp.max(lg, axis=-1, keepdims=True)
        sel = jnp.min(jnp.where(lg == mx, lane, float(ROUTER_LANES)), axis=-1, keepdims=True)
        if j == 0:
            top = mx
        e = jnp.exp(mx - top)
        den = e if den is None else den + e
        hit = lane == sel
        sels.append(hit)
        idx_acc = jnp.where(lane == float(j), sel, idx_acc)
        val_acc = jnp.where(lane == float(j), e, val_acc)
        picked = jnp.where(hit, 1.0, picked)
        lg = jnp.where(hit, -jnp.inf, lg)
    idx_ref[...] = idx_acc.astype(jnp.int32)
    gate_ref[...] = val_acc / den
    rs = lax.broadcasted_iota(jnp.int32, (tm, tm), 0)
    cs = lax.broadcasted_iota(jnp.int32, (tm, tm), 1)
    before = jnp.where(rs > cs, 1.0, 0.0).astype(BF16)
    base = run_ref[...] + jnp.dot(before, picked.astype(BF16), preferred_element_type=F32)
    rank_acc = jnp.zeros(lg.shape, F32)
    for j in range(TOP_K):
        rk = jnp.sum(jnp.where(sels[j], base, 0.0), axis=-1, keepdims=True)
        rank_acc = jnp.where(lane == float(j), rk, rank_acc)
    rank_ref[...] = rank_acc.astype(jnp.int32)
    run_ref[...] = run_ref[...] + jnp.sum(picked, axis=0, keepdims=True)
    cnt_ref[...] = run_ref[...].astype(jnp.int32)


def _router(h, router_w, router_b):
    t, d = h.shape
    tm = _tile(t, (512, 256, 128, 64, 32, 16, 8))
    w = jnp.concatenate([router_w, jnp.zeros((d, ROUTER_LANES - N_EXPERTS), F32)], axis=1)
    bias = jnp.concatenate([router_b, jnp.full((ROUTER_LANES - N_EXPERTS,), -1e30, F32)]).reshape(1, ROUTER_LANES)
    tok = pl.BlockSpec((tm, ROUTER_LANES), lambda i: (i, 0))
    return pl.pallas_call(
        _router_body,
        grid=(t // tm,),
        in_specs=[pl.BlockSpec((tm, d), lambda i: (i, 0)),
                  pl.BlockSpec((d, ROUTER_LANES), lambda i: (0, 0)),
                  pl.BlockSpec((1, ROUTER_LANES), lambda i: (0, 0))],
        out_specs=[tok, tok, tok, pl.BlockSpec((1, ROUTER_LANES), lambda i: (0, 0))],
        out_shape=[jax.ShapeDtypeStruct((t, ROUTER_LANES), jnp.int32), jax.ShapeDtypeStruct((t, ROUTER_LANES), F32),
                   jax.ShapeDtypeStruct((t, ROUTER_LANES), jnp.int32), jax.ShapeDtypeStruct((1, ROUTER_LANES), jnp.int32)],
        scratch_shapes=[pltpu.VMEM((1, ROUTER_LANES), F32)],
        compiler_params=_cparams("arbitrary"),
        name="router",
    )(h, w, bias)


def _expert_body(be_ref, nb_ref, x_ref, win_ref, bin_ref, wout_ref, bout_ref, gate_ref, o_ref, win_bf, wout_bf):
    i = pl.program_id(0)
    fresh = jnp.logical_or(i == 0, be_ref[i] != be_ref[jnp.maximum(i - 1, 0)])

    @pl.when(fresh)
    def _():
        win_bf[...] = win_ref[0].astype(BF16)
        wout_bf[...] = wout_ref[0].astype(BF16)

    @pl.when(i < nb_ref[0])
    def _():
        sub = MOE_ROWS // MOE_SLABS
        slabs = [pl.ds(j * sub, sub) for j in range(MOE_SLABS)]
        gu = [jnp.dot(x_ref[x, :], win_bf[...], preferred_element_type=F32) + bin_ref[0] for x in slabs]
        glu = [jnp.minimum(g[:, :D_EXPERT], SWIGLU_LIMIT) for g in gu]
        lin = [jnp.clip(g[:, D_EXPERT:], -SWIGLU_LIMIT, SWIGLU_LIMIT) for g in gu]
        act = [(a * jax.nn.sigmoid(SWIGLU_ALPHA * a) * (b + 1.0)).astype(BF16) for a, b in zip(glu, lin)]
        y = [jnp.dot(a, wout_bf[...], preferred_element_type=F32) + bout_ref[0] for a in act]
        for x, yy in zip(slabs, y):
            o_ref[x, :] = (yy * gate_ref[x, :]).astype(o_ref.dtype)

    @pl.when(i >= nb_ref[0])
    def _():
        o_ref[...] = jnp.zeros_like(o_ref)


def _moe(h, router_w, router_b, w_in, b_in, w_out, b_out):
    t, d = h.shape
    n_assign = t * TOP_K
    idx, gates, ranks, counts = _router(h, router_w, router_b)
    flat_e = idx[:, :TOP_K].reshape(n_assign)
    flat_g = gates[:, :TOP_K].reshape(n_assign)
    rank = ranks[:, :TOP_K].reshape(n_assign)
    counts = counts[0, :N_EXPERTS]
    padded = (counts + MOE_ROWS - 1) // MOE_ROWS * MOE_ROWS
    pad_end = jnp.cumsum(padded)
    pad_start = pad_end - padded
    experts = jnp.arange(N_EXPERTS, dtype=jnp.int32)
    lookup = lambda table, e: jnp.sum(jnp.where(e[:, None] == experts[None, :], table[None, :], 0), axis=1)
    dest = lookup(pad_start, flat_e) + rank
    n_blocks = -(-n_assign // MOE_ROWS) + N_EXPERTS
    n_rows = n_blocks * MOE_ROWS
    block_start = jnp.arange(n_blocks, dtype=jnp.int32) * MOE_ROWS
    block_e = jnp.minimum(jnp.sum((pad_end[None, :] <= block_start[:, None]).astype(jnp.int32), axis=1), N_EXPERTS - 1)
    order = jnp.argsort(flat_e, stable=True).astype(jnp.int32)
    start = jnp.cumsum(counts) - counts
    in_block = jnp.arange(MOE_ROWS, dtype=jnp.int32)[None, :]
    row_rank = (block_start - lookup(pad_start, block_e))[:, None] + in_block
    row_live = (row_rank < lookup(counts, block_e)[:, None]).reshape(n_rows)
    row_src = order[jnp.minimum(lookup(start, block_e)[:, None] + row_rank, n_assign - 1).reshape(n_rows)]
    row_tok = row_src // TOP_K
    row_gate = jnp.where(row_live, flat_g[row_src], 0.0)
    n_used = (pad_end[-1] // MOE_ROWS).astype(jnp.int32).reshape(1)
    xb = h.astype(BF16)[row_tok]
    f2 = 2 * D_EXPERT
    grid_spec = pltpu.PrefetchScalarGridSpec(
        num_scalar_prefetch=2,
        grid=(n_blocks,),
        in_specs=[pl.BlockSpec((MOE_ROWS, d), lambda i, be, nb: (i, 0)),
                  pl.BlockSpec((1, d, f2), lambda i, be, nb: (be[i], 0, 0)),
                  pl.BlockSpec((1, 1, f2), lambda i, be, nb: (be[i], 0, 0)),
                  pl.BlockSpec((1, D_EXPERT, d), lambda i, be, nb: (be[i], 0, 0)),
                  pl.BlockSpec((1, 1, d), lambda i, be, nb: (be[i], 0, 0)),
                  pl.BlockSpec((MOE_ROWS, 1), lambda i, be, nb: (i, 0))],
        out_specs=pl.BlockSpec((MOE_ROWS, d), lambda i, be, nb: (i, 0)),
        scratch_shapes=[pltpu.VMEM((d, f2), BF16), pltpu.VMEM((D_EXPERT, d), BF16)],
    )
    yb = pl.pallas_call(
        _expert_body,
        grid_spec=grid_spec,
        out_shape=jax.ShapeDtypeStruct((n_rows, d), BF16),
        compiler_params=_cparams("arbitrary"),
        name="moe_experts",
    )(block_e, n_used, xb, w_in, b_in.reshape(N_EXPERTS, 1, f2),
      w_out, b_out.reshape(N_EXPERTS, 1, d), row_gate.reshape(n_rows, 1))
    dest = dest.reshape(t, TOP_K)
    return sum(yb[dest[:, j]].astype(F32) for j in range(TOP_K))


def kernel(x, c, ctx, c_ctx, ada_w, ada_b, norm_mix_g, norm_ffn_g, rwkv_mu, rwkv_w_rkv, rwkv_w0, rwkv_w1, rwkv_w2, rwkv_a0, rwkv_a1, rwkv_a2, rwkv_g1, rwkv_g2, rwkv_k_k, rwkv_k_a, rwkv_r_k, rwkv_ln_g, rwkv_ln_b, rwkv_w_o, rwkv_v0, rwkv_v1, rwkv_v2, hy_w_in, hy_b_in, hy_short_w, hy_short_b, hy_f_w1, hy_f_b1, hy_f_w2, hy_f_b2, hy_f_w3, hy_f_b3, hy_f_freq, hy_f_wout, hy_bias, hy_w_out, hy_b_out, mla_w_a, mla_q_norm_g, mla_kv_norm_g, mla_w_qb, mla_w_kvb, mla_qn_g, mla_kn_g, mla_w_o, moe_router_w, moe_router_b, moe_w_in, moe_b_in, moe_w_out, moe_b_out):
    bsz, s_len, d = x.shape
    l_ctx = ctx.shape[1]
    rows = s_len // GRID_W
    row = jnp.repeat(jnp.arange(rows, dtype=jnp.int32), GRID_W)
    col = jnp.tile(jnp.arange(GRID_W, dtype=jnp.int32), rows)
    silu = jnp.concatenate([jax.nn.silu(c), jax.nn.silu(c_ctx)[None, :]], axis=0)
    silu = jnp.concatenate([silu, jnp.zeros((-(bsz + 1) % 8, d), F32)], axis=0)
    is_ctx = (jnp.arange(l_ctx + s_len, dtype=jnp.int32) < l_ctx)[None, :, None]
    per_row = lambda m: jnp.where(is_ctx, m[:, 0:1], m[:, 1:2])
    xs = jnp.concatenate([ctx, x], axis=1)
    v_first = None
    for i in range(DEPTH):
        j = i // N_MIXERS
        kind = i % N_MIXERS
        mod = _linear(silu, ada_w[i], ada_b[i], tn=1024)
        mod = jnp.stack([jnp.broadcast_to(mod[bsz:bsz + 1], (bsz, N_MOD * d)), mod[:bsz]], axis=1)
        sh1, sc1, g1, sh2, sc2, g2 = jnp.split(mod, N_MOD, axis=-1)
        h = _norm_mod(xs, norm_mix_g[i], sc1, sh1, l_ctx)
        if kind == 0:
            p = {'mu': rwkv_mu[j], 'w_rkv': rwkv_w_rkv[j], 'w0': rwkv_w0[j], 'w1': rwkv_w1[j],
                 'w2': rwkv_w2[j], 'a0': rwkv_a0[j], 'a1': rwkv_a1[j], 'a2': rwkv_a2[j],
                 'g1': rwkv_g1[j], 'g2': rwkv_g2[j], 'k_k': rwkv_k_k[j], 'k_a': rwkv_k_a[j],
                 'r_k': rwkv_r_k[j], 'ln_g': rwkv_ln_g[j], 'ln_b': rwkv_ln_b[j], 'w_o': rwkv_w_o[j]}
            vres = None if j == 0 else (rwkv_v0[j - 1], rwkv_v1[j - 1], rwkv_v2[j - 1])
            y, v_cur = _rwkv_mixer(h, l_ctx, p, v_first, vres)
            if j == 0:
                v_first = v_cur
        elif kind == 1:
            p = {'w_in': hy_w_in[j], 'b_in': hy_b_in[j], 'short_w': hy_short_w[j], 'short_b': hy_short_b[j],
                 'f_w1': hy_f_w1[j], 'f_b1': hy_f_b1[j], 'f_w2': hy_f_w2[j], 'f_b2': hy_f_b2[j],
                 'f_w3': hy_f_w3[j], 'f_b3': hy_f_b3[j], 'f_freq': hy_f_freq[j], 'f_wout': hy_f_wout[j],
                 'bias': hy_bias[j], 'w_out': hy_w_out[j], 'b_out': hy_b_out[j]}
            y = _hyena_mixer(h, l_ctx, p)
        else:
            p = {'w_a': mla_w_a[j], 'q_norm_g': mla_q_norm_g[j], 'kv_norm_g': mla_kv_norm_g[j],
                 'w_qb': mla_w_qb[j], 'w_kvb': mla_w_kvb[j], 'qn_g': mla_qn_g[j], 'kn_g': mla_kn_g[j],
                 'w_o': mla_w_o[j]}
            y = _mla_mixer(h, l_ctx, p, (row, col))
        xs = xs + per_row(g1) * y
        h2 = _norm_mod(xs, norm_ffn_g[i], sc2, sh2, l_ctx)
        moe_p = (moe_router_w[i], moe_router_b[i], moe_w_in[i], moe_b_in[i], moe_w_out[i], moe_b_out[i])
        xs = xs + per_row(g2) * _moe(h2.reshape(-1, d), *moe_p).reshape(xs.shape)
    return xs[:, l_ctx:]
```

```python
import functools
import math

import jax
import jax.numpy as jnp
import numpy as np
from jax import lax
from jax.experimental import pallas as pl
from jax.experimental.pallas import tpu as pltpu

F32 = jnp.float32
BF16 = jnp.bfloat16

D_MODEL = 1024
DEPTH = 4
GRID_W = 64
N_MIXERS = 3
N_MOD = 6
NORM_EPS = 1e-6
RWKV_HEAD = 64
RWKV_HEADS = D_MODEL // RWKV_HEAD
GN_EPS = 64e-5
WKV_CHUNK = 64
WKV_GROUP = 4
FILTER_EMB = 33
FILTER_EMB_PAD = 128
DFT_MIN_OUTER = 32
FAST_DECAY = math.log(1e-2) / 0.3
SLOW_DECAY = math.log(1e-2) / 1.5
MLA_HEADS = 8
QK_NOPE = 128
QK_ROPE = 64
QK_HEAD = QK_NOPE + QK_ROPE
V_HEAD = 128
Q_LORA = 384
KV_LORA = 256
ROPE_THETA = 10000.0
N_EXPERTS = 32
TOP_K = 4
D_EXPERT = 1024
SWIGLU_LIMIT = 7.0
SWIGLU_ALPHA = 1.702
MOE_ROWS = 256
MOE_SLABS = 2
LANES = 128
ROUTER_LANES = LANES
ATTN_SLAB = 512
LOG2_E = 1.4426950408889634
MLA_SLOT = 256

V7X_VMEM_BYTES = 64 * 1024 * 1024
VMEM_LIMIT = V7X_VMEM_BYTES * 3 // 4


def _cparams(*sem):
    return pltpu.CompilerParams(dimension_semantics=sem, vmem_limit_bytes=VMEM_LIMIT)


def _tile(n, prefs):
    for t in prefs:
        if n % t == 0:
            return t
    return n


def _linear_body(a_ref, w_ref, b_ref, o_ref, *, in_act):
    a = a_ref[...]
    if in_act == 'tanh':
        a = jnp.tanh(a.astype(F32))
    elif in_act == 'sigmoid':
        a = jax.nn.sigmoid(a.astype(F32))
    acc = jnp.dot(a.astype(BF16), w_ref[...], preferred_element_type=F32)
    o_ref[...] = (acc + b_ref[...]).astype(o_ref.dtype)


def _linear(a, w, b=None, *, in_act=None, out_dtype=F32, tn=None):
    m, k = a.shape
    n = w.shape[1]
    tm = _tile(m, (512, 256, 128, 64, 32, 16, 8))
    tn = n if tn is None else tn
    assert n % tn == 0
    if b is None:
        b = jnp.zeros((n,), F32)
    return pl.pallas_call(
        functools.partial(_linear_body, in_act=in_act),
        grid=(n // tn, m // tm),
        in_specs=[pl.BlockSpec((tm, k), lambda j, i: (i, 0)),
                  pl.BlockSpec((k, tn), lambda j, i: (0, j)),
                  pl.BlockSpec((1, tn), lambda j, i: (0, j))],
        out_specs=pl.BlockSpec((tm, tn), lambda j, i: (i, j)),
        out_shape=jax.ShapeDtypeStruct((m, n), out_dtype),
        compiler_params=_cparams("parallel", "parallel"),
        name="linear",
    )(a, w.astype(BF16), b.reshape(1, n).astype(F32))


def _norm_mod_body(x_ref, g_ref, sc_ref, sh_ref, o_ref):
    x = x_ref[0]
    y = x * lax.rsqrt(jnp.mean(x * x, axis=-1, keepdims=True) + NORM_EPS)
    o_ref[0] = ((y * g_ref[...]) * (1.0 + sc_ref[0, 0]) + sh_ref[0, 0]).astype(o_ref.dtype)


def _norm_mod(x, g, sc, sh, l_ctx):
    b, l, d = x.shape
    tl = _tile(math.gcd(l_ctx, l - l_ctx), (512, 256, 128, 64, 32, 16, 8))
    stream = lambda i, j: (i, jnp.where(j * tl < l_ctx, 0, 1), 0, 0)
    return pl.pallas_call(
        _norm_mod_body,
        grid=(b, l // tl),
        in_specs=[pl.BlockSpec((1, tl, d), lambda i, j: (i, j, 0)),
                  pl.BlockSpec((1, d), lambda i, j: (0, 0)),
                  pl.BlockSpec((1, 1, 1, d), stream),
                  pl.BlockSpec((1, 1, 1, d), stream)],
        out_specs=pl.BlockSpec((1, tl, d), lambda i, j: (i, j, 0)),
        out_shape=jax.ShapeDtypeStruct((b, l, d), F32),
        compiler_params=_cparams("parallel", "parallel"),
        name="norm_mod",
    )(x, g.reshape(1, d), sc.reshape(b, 2, 1, d), sh.reshape(b, 2, 1, d))


def _residual_norm_body(*refs, n_add):
    x_ref, adds, (gate_ref, g_ref, sc_ref, sh_ref, xo_ref, h_ref) = refs[0], refs[1:1 + n_add], refs[1 + n_add:]
    y = adds[0][0].astype(F32)
    for a in adds[1:]:
        y = y + a[0].astype(F32)
    x = x_ref[0] + gate_ref[0, 0] * y
    xo_ref[0] = x
    nrm = x * lax.rsqrt(jnp.mean(x * x, axis=-1, keepdims=True) + NORM_EPS)
    h_ref[0] = ((nrm * g_ref[...]) * (1.0 + sc_ref[0, 0]) + sh_ref[0, 0]).astype(h_ref.dtype)


def _residual_norm(x, adds, gate, g, sc, sh, l_ctx):
    b, l, d = x.shape
    tl = _tile(math.gcd(l_ctx, l - l_ctx), (512, 256, 128, 64, 32, 16, 8))
    stream = pl.BlockSpec((1, 1, 1, d), lambda i, j: (i, jnp.where(j * tl < l_ctx, 0, 1), 0, 0))
    rows = pl.BlockSpec((1, tl, d), lambda i, j: (i, j, 0))
    return pl.pallas_call(
        functools.partial(_residual_norm_body, n_add=len(adds)),
        grid=(b, l // tl),
        in_specs=[rows] * (1 + len(adds)) + [stream, pl.BlockSpec((1, d), lambda i, j: (0, 0)), stream, stream],
        out_specs=[rows, rows],
        out_shape=[jax.ShapeDtypeStruct((b, l, d), F32)] * 2,
        compiler_params=_cparams("parallel", "parallel"),
        name="residual_norm",
    )(x, *adds, gate.reshape(b, 2, 1, d), g.reshape(1, d), sc.reshape(b, 2, 1, d), sh.reshape(b, 2, 1, d))


def _stream_neighbours(t, l_ctx):
    zero = jnp.zeros_like(t[:, :1])
    pos = jnp.arange(t.shape[1], dtype=jnp.int32)[None, :, None]
    prev = jnp.where(pos == l_ctx, 0.0, jnp.concatenate([zero, t[:, :-1]], axis=1))
    nxt = jnp.where(pos == l_ctx - 1, 0.0, jnp.concatenate([t[:, 1:], zero], axis=1))
    return prev, nxt


PROJ_R, PROJ_K, PROJ_V = 0, 1024, 2048
PROJ_W1 = (3072, 3200)
PROJ_A1 = (3328, 3456)
PROJ_G1 = 3584
PROJ_V1 = 3840
PROJ_WIDTH = 4096
LORA_PAD = 128
GATE_PAD = 256


def _sigmoid(x):
    return jax.nn.sigmoid(x)


def _rwkv_post_body(*refs, has_vres):
    if has_vres:
        (proj_ref, vf_ref, w2_ref, a2_ref, g2_ref, v2_ref, vec_ref, gs_ref,
         kk_out, v_out, gate_out, lw_out, ag_out, kd_out) = refs
    else:
        (proj_ref, w2_ref, a2_ref, g2_ref, vec_ref, gs_ref,
         kk_out, v_out, gate_out, lw_out, ag_out, kd_out) = refs
    d = D_MODEL
    vec = vec_ref[...]
    k = proj_ref[:, PROJ_K:PROJ_K + d].astype(F32)
    v = proj_ref[:, PROJ_V:PROJ_V + d].astype(F32)
    if has_vres:
        v1 = proj_ref[:, PROJ_V1:PROJ_V1 + LORA_PAD]
        vgate = _sigmoid(vec[6:7] + jnp.dot(v1, v2_ref[...], preferred_element_type=F32))
        v = v + (vf_ref[...].astype(F32) - v) * vgate
    v_out[...] = v.astype(v_out.dtype)
    kraw = k * vec[4:5]
    ss = jnp.dot((kraw * kraw).astype(BF16), gs_ref[...], preferred_element_type=F32)
    kk_out[...] = (kraw / jnp.maximum(jnp.sqrt(ss), 1e-12)).astype(kk_out.dtype)
    g1 = proj_ref[:, PROJ_G1:PROJ_G1 + GATE_PAD].astype(F32)
    gate_out[...] = jnp.dot(_sigmoid(g1).astype(BF16), g2_ref[...], preferred_element_type=F32).astype(gate_out.dtype)
    for dd in range(2):
        w1 = proj_ref[:, PROJ_W1[dd]:PROJ_W1[dd] + LORA_PAD].astype(F32)
        z = vec[dd:dd + 1] + jnp.dot(jnp.tanh(w1).astype(BF16), w2_ref[dd], preferred_element_type=F32)
        softplus = jnp.maximum(-z, 0.0) + jnp.log(1.0 + jnp.exp(-jnp.abs(z)))
        lw_out[dd] = -jnp.exp(-softplus - 0.5)
        a1 = proj_ref[:, PROJ_A1[dd]:PROJ_A1[dd] + LORA_PAD]
        ag = _sigmoid(vec[2 + dd:3 + dd] + jnp.dot(a1, a2_ref[dd], preferred_element_type=F32))
        ag_out[dd] = ag.astype(ag_out.dtype)
        kd_out[dd] = (k * (1.0 + (ag - 1.0) * vec[5:6])).astype(kd_out.dtype)


def _rwkv_post(proj, v_first, w2p, a2p, g2p, v2p, vec, gsum):
    t, d = proj.shape[0], D_MODEL
    tm = _tile(t, (256, 128, 64, 32, 16))
    has_vres = v_first is not None
    full = lambda a: pl.BlockSpec(a.shape, lambda i: (0,) * a.ndim)
    row = pl.BlockSpec((tm, d), lambda i: (i, 0))
    row2 = pl.BlockSpec((2, tm, d), lambda i: (0, i, 0))
    ins = [proj] + ([v_first] if has_vres else []) + [w2p, a2p, g2p] + ([v2p] if has_vres else []) + [vec, gsum]
    in_specs = ([pl.BlockSpec((tm, PROJ_WIDTH), lambda i: (i, 0))] + ([row] if has_vres else [])
                + [full(a) for a in ins[(2 if has_vres else 1):]])
    return pl.pallas_call(
        functools.partial(_rwkv_post_body, has_vres=has_vres),
        grid=(t // tm,),
        in_specs=in_specs,
        out_specs=[row, row, row, row2, row2, row2],
        out_shape=[jax.ShapeDtypeStruct((t, d), BF16),
                   jax.ShapeDtypeStruct((t, d), BF16),
                   jax.ShapeDtypeStruct((t, d), BF16),
                   jax.ShapeDtypeStruct((2, t, d), F32),
                   jax.ShapeDtypeStruct((2, t, d), BF16),
                   jax.ShapeDtypeStruct((2, t, d), BF16)],
        compiler_params=_cparams("parallel"),
        name="rwkv_post",
    )(*ins)


def _head_groups(y):
    gw = WKV_GROUP * RWKV_HEAD
    lane_head = jnp.right_shift(lax.broadcasted_iota(jnp.int32, (y.shape[0], gw), 1), 6)
    out = []
    for g in range(D_MODEL // gw):
        yg = y[:, g * gw:(g + 1) * gw].astype(BF16)
        out.append(jnp.concatenate([jnp.where(lane_head == h, yg, jnp.zeros_like(yg)) for h in range(WKV_GROUP)], axis=0))
    return out


def _gmm(x, ybd):
    gw = WKV_GROUP * RWKV_HEAD
    return jnp.concatenate([jnp.dot(x[:, g * gw:(g + 1) * gw].astype(BF16), yb, preferred_element_type=F32)
                            for g, yb in enumerate(ybd)], axis=-1)


def _gmm_nt(x, ybd):
    gw = WKV_GROUP * RWKV_HEAD
    return jnp.concatenate([lax.dot_general(x[:, g * gw:(g + 1) * gw].astype(BF16), yb, (((1,), (1,)), ((), ())),
                                            preferred_element_type=F32)
                            for g, yb in enumerate(ybd)], axis=-1)


def _wkv_chunks(r, kk, v, lw, ag, kd, s0, rev):
    each = lambda f, *ls: [f(*a) for a in zip(*ls)]
    c, d = r[0].shape
    row0 = lax.broadcasted_iota(jnp.int32, (c, d), 0)
    col0 = jnp.bitwise_and(lax.broadcasted_iota(jnp.int32, (c, d), 1), RWKV_HEAD - 1)
    incl = [(row0 <= col0) if rv else (row0 >= col0) for rv in rev]
    strict = [(row0 < col0) if rv else (row0 > col0) for rv in rev]
    rs = lax.broadcasted_iota(jnp.int32, (c, c), 0)
    cs = lax.broadcasted_iota(jnp.int32, (c, c), 1)
    tri = [jnp.where((rs <= cs) if rv else (rs >= cs), 1.0, 0.0).astype(BF16) for rv in rev]
    hi = each(lambda x: x.astype(BF16), lw)
    rem = each(lambda x, h: x - h.astype(F32), lw, hi)
    mid = each(lambda x: x.astype(BF16), rem)
    lo = each(lambda x, m: (x - m.astype(F32)).astype(BF16), rem, mid)
    cum = each(lambda t, a, b_, c_: (jnp.dot(t, a, preferred_element_type=F32) + jnp.dot(t, b_, preferred_element_type=F32)
                                     + jnp.dot(t, c_, preferred_element_type=F32)), tri, hi, mid, lo)
    cum_end = [x[0:1, :] if rv else x[c - 1:c, :] for x, rv in zip(cum, rev)]
    g_rest = each(lambda e, x: jnp.exp(e - x), cum_end, cum)
    g_inv = each(lambda x: jnp.exp(-x), cum)
    bvec = each(lambda a, b_: a * b_, kk, ag)
    rt = each(lambda a, x: a * jnp.exp(x), r, cum)
    kt = each(lambda a, g: a * g, kd, g_inv)
    bt = each(lambda a, g: a * g, bvec, g_inv)
    at = each(lambda a, x, w: -a * jnp.exp(x - w), kk, cum, lw)
    ar = each(lambda a, b_: jnp.concatenate([a, b_], axis=0), at, rt)
    ab = each(lambda a, b_: _gmm_nt(a, _head_groups(b_)), ar, bt)
    ak = each(lambda a, b_: _gmm_nt(a, _head_groups(b_)), ar, kt)
    a_ab = each(lambda m, x: jnp.where(m, x[:c], 0.0), strict, ab)
    a_rb = each(lambda m, x: jnp.where(m, x[c:], 0.0), incl, ab)
    a_ak = each(lambda m, x: jnp.where(m, x[:c], 0.0), strict, ak)
    a_rk = each(lambda m, x: jnp.where(m, x[c:], 0.0), incl, ak)
    same16 = jnp.right_shift(row0, 4) == jnp.right_shift(col0, 4)
    same32 = jnp.right_shift(row0, 5) == jnp.right_shift(col0, 5)
    eye = jnp.where(row0 == col0, 1.0, 0.0)
    npow = each(lambda x: jnp.where(same16, x, 0.0), a_ab)
    p = each(lambda x: eye + x, npow)
    for _ in range(3):
        npow = each(lambda x: _gmm(x, _head_groups(x)), npow)
        p = each(lambda x, y: x + _gmm(x, _head_groups(y)), p, npow)
    off32 = jnp.logical_and(same32, jnp.logical_not(same16))
    t1 = each(lambda x, y: _gmm(jnp.where(off32, x, 0.0), _head_groups(y)), a_ab, p)
    p = each(lambda x, y: x + _gmm(x, _head_groups(y)), p, t1)
    t2 = each(lambda x, y: _gmm(jnp.where(same32, 0.0, x), _head_groups(y)), a_ab, p)
    p = each(lambda x, y: x + _gmm(x, _head_groups(y)), p, t2)
    pbd = each(_head_groups, p)
    vbd = each(_head_groups, v)
    wmat = each(lambda x, y: _gmm(x, _head_groups(y)), p, at)
    vt = each(lambda y: _gmm_nt(eye, y), vbd)
    xt = each(lambda x, y: _gmm_nt(x, _head_groups(y)), vt, a_ak)
    upt = each(_gmm_nt, xt, pbd)
    yp = each(_gmm, a_rk, vbd)
    sp = each(lambda x, a, g: _gmm(x, _head_groups(a * g)), vt, kd, g_rest)
    ut = each(lambda s_, w, u: _gmm_nt(s_, _head_groups(w)) + u, s0, wmat, upt)
    ys = each(lambda s_, q, x: _gmm_nt(q, _head_groups(s_)) + x, s0, rt, yp)
    y = each(lambda y0, a, u: y0 + _gmm_nt(a, _head_groups(u)), ys, a_rb, ut)
    s_end = each(lambda s_, e, u, b_, g, x: s_ * jnp.exp(e) + _gmm(u, _head_groups(b_ * g)) + x,
                 s0, cum_end, ut, bvec, g_rest, sp)
    return y, s_end


def _wkv_body(*refs, n_batch):
    ins, outs, s_ref = refs[:12], refs[12:14], refs[14]

    @pl.when(pl.program_id(0) == 0)
    def _():
        s_ref[...] = jnp.zeros_like(s_ref)

    scans = [(dd, bi) for dd in range(2) for bi in range(n_batch)]
    get = lambda k, lead: [ins[6 * dd + k][(0, bi) if lead else bi].astype(F32) for dd, bi in scans]
    y, s_end = _wkv_chunks(get(0, False), get(1, False), get(2, False), get(3, True), get(4, True), get(5, True),
                           [s_ref[i] for i in range(len(scans))], [dd == 1 for dd, _ in scans])
    for i, (dd, bi) in enumerate(scans):
        outs[dd][bi] = y[i]
        s_ref[i] = s_end[i]


def _wkv(proj, kk, v, lw, ag, kd, l_ctx):
    b, l, d = kk.shape
    c = WKV_CHUNK
    nc, nctx = l // c, l_ctx // c
    assert c == RWKV_HEAD and l % c == 0 and l_ctx % c == 0

    def fwd(j):
        return j

    def bwd(j):
        return jnp.where(j < nctx, nctx - 1 - j, nc + nctx - 1 - j)

    in_specs, out_specs = [], []
    for dd, ch in enumerate((fwd, bwd)):
        shared = pl.BlockSpec((b, c, d), lambda j, ch=ch: (0, ch(j), 0))
        per_dir = pl.BlockSpec((1, b, c, d), lambda j, dd=dd, ch=ch: (dd, 0, ch(j), 0))
        in_specs += [shared, shared, shared, per_dir, per_dir, per_dir]
        out_specs.append(shared)
    return pl.pallas_call(
        functools.partial(_wkv_body, n_batch=b),
        grid=(nc,),
        in_specs=in_specs,
        out_specs=out_specs,
        out_shape=[jax.ShapeDtypeStruct((b, l, d), F32)] * 2,
        scratch_shapes=[pltpu.VMEM((2 * b, RWKV_HEAD, d), F32)],
        compiler_params=_cparams("arbitrary"),
        name="wkv",
    )(*([proj, kk, v, lw, ag, kd] * 2))


def _rwkv_readout_body(yf_ref, yr_ref, r_ref, kd_ref, v_ref, gate_ref, gm_ref, gs_ref, vec_ref, wo_ref, o_ref):
    vec = vec_ref[...]
    y = yf_ref[...] + yr_ref[...]
    gm = gm_ref[...]
    yh = y.astype(BF16)
    yl = (y - yh.astype(F32)).astype(BF16)
    mean = jnp.dot(yh, gm, preferred_element_type=F32) + jnp.dot(yl, gm, preferred_element_type=F32)
    yc = y - mean
    var = jnp.dot((yc * yc).astype(BF16), gm, preferred_element_type=F32)
    yn = yc * lax.rsqrt(var + GN_EPS) * vec[0:1] + vec[1:2]
    r = r_ref[...].astype(F32)
    kds = kd_ref[0].astype(F32) + kd_ref[1].astype(F32)
    bonus = jnp.dot((r * kds * vec[2:3]).astype(BF16), gs_ref[...], preferred_element_type=F32)
    out = (yn + bonus * v_ref[...].astype(F32)) * gate_ref[...].astype(F32)
    o_ref[...] = jnp.dot(out.astype(BF16), wo_ref[...], preferred_element_type=F32)


def _rwkv_readout(y_fwd, y_rev, proj, kd, v, gate, gmean, gsum, vec, w_o):
    t, d = v.shape
    tm = _tile(t, (256, 128, 64, 32, 16))
    full = lambda a: pl.BlockSpec(a.shape, lambda i: (0,) * a.ndim)
    row = pl.BlockSpec((tm, d), lambda i: (i, 0))
    row2 = pl.BlockSpec((2, tm, d), lambda i: (0, i, 0))
    return pl.pallas_call(
        _rwkv_readout_body,
        grid=(t // tm,),
        in_specs=[row, row, row, row2, row, row, full(gmean), full(gsum), full(vec), full(w_o)],
        out_specs=row,
        out_shape=jax.ShapeDtypeStruct((t, d), F32),
        compiler_params=_cparams("parallel"),
        name="rwkv_readout",
    )(y_fwd, y_rev, proj, kd, v, gate, gmean, gsum, vec, w_o)


def _pad_cols(w, width):
    return jnp.concatenate([w, jnp.zeros((w.shape[0], width - w.shape[1]), w.dtype)], axis=1)


def _pad_rows(w, height):
    return jnp.concatenate([w, jnp.zeros((height - w.shape[0], w.shape[1]), w.dtype)], axis=0)


def _rwkv_mixer(h, l_ctx, p, v_first, vres):
    b, l, d = h.shape
    t = b * l
    prev, nxt = _stream_neighbours(h, l_ctx)
    xx = 0.5 * (prev + nxt) - h
    mu = p['mu']
    mats = [(p['w_rkv'][0], 0, d), (p['w_rkv'][1], 2, d), (p['w_rkv'][2], 3, d),
            (p['w1'][0], 1, LORA_PAD), (p['w1'][1], 1, LORA_PAD), (p['a1'][0], 4, LORA_PAD), (p['a1'][1], 4, LORA_PAD),
            (p['g1'], 5, GATE_PAD), (vres[1] if vres is not None else jnp.zeros((d, 1), F32), 3, 2 * LORA_PAD)]
    top = jnp.concatenate([_pad_cols(w, width) for w, _, width in mats], axis=1)
    bot = jnp.concatenate([_pad_cols(mu[m][:, None] * w, width) for w, m, width in mats], axis=1)
    wcat = jnp.concatenate([top, bot], axis=0)
    assert wcat.shape[1] == PROJ_WIDTH
    a_in = jnp.concatenate([h, xx], axis=-1).reshape(t, 2 * d).astype(BF16)
    proj = _linear(a_in, wcat, out_dtype=BF16, tn=1024)
    w2p = jnp.stack([_pad_rows(p['w2'][i], LORA_PAD) for i in range(2)]).astype(BF16)
    a2p = jnp.stack([_pad_rows(p['a2'][i], LORA_PAD) for i in range(2)]).astype(BF16)
    g2p = _pad_rows(p['g2'], GATE_PAD).astype(BF16)
    head_of = jnp.arange(d, dtype=jnp.int32) // RWKV_HEAD
    same_head = head_of[:, None] == head_of[None, :]
    gsum = jnp.where(same_head, 1.0, 0.0).astype(BF16)
    gmean = jnp.where(same_head, 1.0 / RWKV_HEAD, 0.0).astype(BF16)
    zero = jnp.zeros((d,), F32)
    vec = jnp.stack([p['w0'][0], p['w0'][1], p['a0'][0], p['a0'][1], p['k_k'], p['k_a'],
                     vres[0] if vres is not None else zero, zero])
    if vres is not None:
        kk, v, gate, lw, ag, kd = _rwkv_post(proj, v_first.reshape(t, d), w2p, a2p, g2p,
                                             _pad_rows(vres[2], LORA_PAD).astype(BF16), vec, gsum)
    else:
        kk, v, gate, lw, ag, kd = _rwkv_post(proj, None, w2p, a2p, g2p, None, vec, gsum)
    sh3 = lambda a: a.reshape(a.shape[:-2] + (b, l, a.shape[-1]))
    y_fwd, y_rev = _wkv(sh3(proj), sh3(kk), sh3(v), sh3(lw), sh3(ag), sh3(kd), l_ctx)
    vec_o = jnp.stack([p['ln_g'], p['ln_b'], p['r_k'].reshape(d)] + [zero] * 5)
    out = _rwkv_readout(y_fwd.reshape(t, d), y_rev.reshape(t, d), proj, kd, v, gate, gmean, gsum, vec_o, p['w_o'].astype(BF16))
    return out.reshape(b, l, d), v.reshape(b, l, d)


def _filter_body(z_ref, t_ref, w1_ref, b1_ref, w2_ref, b2_ref, w3_ref, b3_ref, fr_ref, wo_ref, dl_ref, o_ref, *, length):
    hp = lax.Precision.HIGHEST
    fr = fr_ref[...]
    hdn = jnp.sin(fr[0:1] * (jnp.dot(z_ref[...], w1_ref[...], precision=hp, preferred_element_type=F32) + b1_ref[...]))
    hdn = jnp.sin(fr[1:2] * (jnp.dot(hdn, w2_ref[...], precision=hp, preferred_element_type=F32) + b2_ref[...]))
    hdn = jnp.sin(fr[2:3] * (jnp.dot(hdn, w3_ref[...], precision=hp, preferred_element_type=F32) + b3_ref[...]))
    filt = jnp.dot(hdn, wo_ref[...], precision=hp, preferred_element_type=F32) * jnp.exp(-t_ref[...] * dl_ref[...])
    tl = filt.shape[0]
    n = pl.program_id(0) * tl + lax.broadcasted_iota(jnp.int32, filt.shape, 0)
    o_ref[...] = jnp.where(n == length, 0.0, filt * (0.5 / length))


def _hyena_filters(length, p):
    n = jnp.arange(2 * length, dtype=jnp.int32)
    pos = jnp.where(n < length, n, 2 * length - n).astype(F32)[:, None]
    t = pos / (length - 1)
    bands = (FILTER_EMB - 1) // 2
    f = jnp.linspace(1e-4, bands - 1, bands, dtype=F32)[None, :]
    ang = 2.0 * math.pi * f * pos / length
    z = jnp.concatenate([t, jnp.cos(ang), -jnp.sin(ang), jnp.zeros((2 * length, FILTER_EMB_PAD - FILTER_EMB), F32)], axis=-1)
    w1 = jnp.concatenate([p['f_w1'], jnp.zeros((FILTER_EMB_PAD - FILTER_EMB, p['f_w1'].shape[1]), F32)], axis=0)
    deltas = jnp.abs(jnp.linspace(FAST_DECAY, SLOW_DECAY, D_MODEL, dtype=F32))[None, :]
    fw = p['f_w2'].shape[0]
    tl = _tile(length, (512, 256, 128, 64, 32, 16, 8))
    half = length // tl
    full = lambda a: pl.BlockSpec(a.shape, lambda i: (0,) * a.ndim)
    args = [w1, p['f_b1'].reshape(1, fw), p['f_w2'], p['f_b2'].reshape(1, fw), p['f_w3'], p['f_b3'].reshape(1, fw),
            p['f_freq']]
    return pl.pallas_call(
        functools.partial(_filter_body, length=length),
        grid=(2 * half,),
        in_specs=[pl.BlockSpec((tl, FILTER_EMB_PAD), lambda i: (i, 0)), pl.BlockSpec((tl, 1), lambda i: (i, 0))]
                 + [full(a) for a in args]
                 + [pl.BlockSpec((fw, D_MODEL), lambda i: (0, jnp.where(i < half, 0, 1))),
                    full(deltas)],
        out_specs=pl.BlockSpec((tl, D_MODEL), lambda i: (i, 0)),
        out_shape=jax.ShapeDtypeStruct((2 * length, D_MODEL), F32),
        compiler_params=_cparams("parallel"),
        name="hyena_filter",
    )(z, t, *args, p['f_wout'], deltas)


def _colmm_body(f_ref, x_ref, o_ref):
    o_ref[0] = jnp.dot(f_ref[...], x_ref[0].astype(BF16), preferred_element_type=F32).astype(o_ref.dtype)


def _colmm(f, x, out_dtype):
    bsz, q, n = x.shape
    pp = f.shape[0]
    tn = _tile(n, (4096, 2048, 1024, 512, 256, 128))
    return pl.pallas_call(
        _colmm_body,
        grid=(bsz, n // tn),
        in_specs=[pl.BlockSpec((pp, q), lambda i, j: (0, 0)),
                  pl.BlockSpec((1, q, tn), lambda i, j: (i, 0, j))],
        out_specs=pl.BlockSpec((1, pp, tn), lambda i, j: (i, 0, j)),
        out_shape=jax.ShapeDtypeStruct((bsz, pp, n), out_dtype),
        compiler_params=_cparams("parallel", "parallel"),
        name="dft_outer",
    )(f.astype(BF16), x)


def _dft_inner_fwd_body(a_ref, m_ref, o_ref):
    n2 = a_ref.shape[3]
    a = jnp.concatenate([a_ref[0, 0, 0], a_ref[0, 1, 0]], axis=0).astype(BF16)
    x = jnp.dot(m_ref[0], a, preferred_element_type=F32)
    o_ref[0, 0, 0] = x[:n2]
    o_ref[0, 1, 0] = x[n2:]


def _dft_inner_conv_body(a_ref, g_ref, m_ref, mi_ref, o_ref):
    n2 = a_ref.shape[3]
    a = jnp.concatenate([a_ref[0, 0, 0], a_ref[0, 1, 0]], axis=0).astype(BF16)
    x = jnp.dot(m_ref[0], a, preferred_element_type=F32)
    xre, xim = x[:n2], x[n2:]
    gre, gim = g_ref[0, 0, 0], g_ref[0, 1, 0]
    y = jnp.concatenate([xre * gre - xim * gim, xre * gim + xim * gre], axis=0).astype(BF16)
    o = jnp.dot(mi_ref[0], y, preferred_element_type=F32)
    o_ref[0, 0, 0] = o[:n2].astype(o_ref.dtype)
    o_ref[0, 1, 0] = o[n2:].astype(o_ref.dtype)


def _dft_tables(n1, n2):
    n = n1 * n2
    i1 = jnp.arange(n1, dtype=jnp.int32)
    ph1 = (2.0 * math.pi / n1) * ((i1[:, None] * i1[None, :]) % n1).astype(F32)
    c1, s1 = jnp.cos(ph1), jnp.sin(ph1)
    k1 = i1[:, None, None]
    k2 = jnp.arange(n2, dtype=jnp.int32)[None, :, None]
    m2 = jnp.arange(n2, dtype=jnp.int32)[None, None, :]
    th = (2.0 * math.pi / n) * ((m2 * (k1 + n1 * k2)) % n).astype(F32)
    mre, mim = jnp.cos(th), -jnp.sin(th)
    m_fwd = jnp.concatenate([jnp.concatenate([mre, -mim], axis=2), jnp.concatenate([mim, mre], axis=2)], axis=1)
    qre, qim = jnp.swapaxes(mre, 1, 2), -jnp.swapaxes(mim, 1, 2)
    m_inv = jnp.concatenate([jnp.concatenate([qre, -qim], axis=2), jnp.concatenate([qim, qre], axis=2)], axis=1)
    f_fwd = jnp.concatenate([c1, -s1], axis=0)
    f_inv = jnp.concatenate([c1, -s1], axis=1)
    return f_fwd, f_inv, m_fwd.astype(BF16), m_inv.astype(BF16)


def _two_sided_long_conv(z, filt):
    b, l, d = z.shape
    n = 2 * l
    n1 = max(DFT_MIN_OUTER, 2 ** ((int(math.log2(n)) - 1) // 2))
    n2 = n // n1
    f_fwd, f_inv, m_fwd, m_inv = _dft_tables(n1, n2)
    blk = pl.BlockSpec((1, 2, 1, n2, d), lambda i, j: (i, 0, j, 0, 0))
    gblk = pl.BlockSpec((1, 2, 1, n2, d), lambda i, j: (0, 0, j, 0, 0))
    mat = pl.BlockSpec((1, 2 * n2, 2 * n2), lambda i, j: (j, 0, 0))
    ga = _colmm(f_fwd, filt.reshape(1, n1, n2 * d), BF16).reshape(1, 2, n1, n2, d)
    gspec = pl.pallas_call(
        _dft_inner_fwd_body,
        grid=(1, n1),
        in_specs=[blk, mat],
        out_specs=blk,
        out_shape=jax.ShapeDtypeStruct((1, 2, n1, n2, d), F32),
        compiler_params=_cparams("parallel", "parallel"),
        name="dft_inner_fwd",
    )(ga, m_fwd)
    za = _colmm(f_fwd[:, :n1 // 2], z.reshape(b, n1 // 2, n2 * d), BF16).reshape(b, 2, n1, n2, d)
    bb = pl.pallas_call(
        _dft_inner_conv_body,
        grid=(b, n1),
        in_specs=[blk, gblk, mat, mat],
        out_specs=blk,
        out_shape=jax.ShapeDtypeStruct((b, 2, n1, n2, d), BF16),
        compiler_params=_cparams("parallel", "parallel"),
        name="dft_inner_conv",
    )(za, gspec, m_fwd, m_inv)
    return _colmm(f_inv[:n1 // 2], bb.reshape(b, 2 * n1, n2 * d), F32).reshape(b, l, d)


def _hyena_mixer(h, l_ctx, p):
    b, l, d = h.shape
    u = _linear(h.reshape(b * l, d).astype(BF16), p['w_in'], p['b_in'], tn=1024).reshape(b, l, 3 * d)
    prev, nxt = _stream_neighbours(u, l_ctx)
    sw = p['short_w']
    u = prev * sw[0] + u * sw[1] + nxt * sw[2] + p['short_b']
    x0, x1, v = jnp.split(u, 3, axis=-1)
    z = v * x1
    conv = jnp.concatenate([_two_sided_long_conv(z[:, :l_ctx], _hyena_filters(l_ctx, p)),
                            _two_sided_long_conv(z[:, l_ctx:], _hyena_filters(l - l_ctx, p))], axis=1)
    z = conv + z * p['bias']
    return _linear((x0 * z).reshape(b * l, d).astype(BF16), p['w_out'], p['b_out']).reshape(b, l, d)


def _attn_body(q_ref, k_ref, v_ref, o_ref, m_ref, acc_ref, *, dv, parts):
    j = pl.program_id(3)
    tq, tk = q_ref.shape[2], k_ref.shape[2]
    sub = tq // parts

    @pl.when(j == 0)
    def _():
        m_ref[...] = jnp.full_like(m_ref, -jnp.inf)
        acc_ref[...] = jnp.zeros_like(acc_ref)

    k = k_ref[0, 0]
    v = v_ref[0]
    slabs = [pl.ds(i * sub, sub) for i in range(parts)]
    s = [lax.dot_general(q_ref[0, 0, x, :], k, (((1,), (1,)), ((), ())), preferred_element_type=F32) for x in slabs]
    m_prev = [m_ref[x, :] for x in slabs]
    m_new = [jnp.maximum(mp, jnp.max(si, axis=-1, keepdims=True)) for mp, si in zip(m_prev, s)]
    pr = [jnp.exp2(si - jnp.concatenate([mn] * (tk // LANES), axis=1)).astype(BF16) for si, mn in zip(s, m_new)]
    alpha = [jnp.exp2(mp - mn) for mp, mn in zip(m_prev, m_new)]
    pv = [jnp.dot(p_, v, preferred_element_type=F32) for p_ in pr]
    for x, a, o, mn in zip(slabs, alpha, pv, m_new):
        acc_ref[x, :] = jnp.concatenate([a] * (acc_ref.shape[1] // LANES), axis=1) * acc_ref[x, :] + o
        m_ref[x, :] = mn

    @pl.when(j == pl.num_programs(3) - 1)
    def _():
        acc = acc_ref[...]
        o_ref[0] = (acc[:, :dv] / acc[:, dv:dv + 1]).astype(o_ref.dtype)


def _attend(q, k, v_ext):
    b, nh, sq, dq = q.shape
    sk = k.shape[2]
    dv = v_ext.shape[2] // (2 * nh)
    tq = _tile(sq, (2048, 1024, 512, 256, 128, 64, 32, 16, 8))
    tk = _tile(sk, (1280, 1024, 640, 512, 256, 128))
    parts = max(1, tq // ATTN_SLAB)
    return pl.pallas_call(
        functools.partial(_attn_body, dv=dv, parts=parts),
        grid=(b, nh, sq // tq, sk // tk),
        in_specs=[pl.BlockSpec((1, 1, tq, dq), lambda bi, hi, i, j: (bi, hi, i, 0)),
                  pl.BlockSpec((1, 1, tk, dq), lambda bi, hi, i, j: (bi, hi, j, 0)),
                  pl.BlockSpec((1, tk, 2 * dv), lambda bi, hi, i, j: (bi, j, hi))],
        out_specs=pl.BlockSpec((1, tq, dv), lambda bi, hi, i, j: (bi, i, hi)),
        out_shape=jax.ShapeDtypeStruct((b, sq, nh * dv), BF16),
        scratch_shapes=[pltpu.VMEM((tq, LANES), F32), pltpu.VMEM((tq, 2 * dv), F32)],
        compiler_params=_cparams("parallel", "parallel", "parallel", "arbitrary"),
        name="attention",
    )(q, k, v_ext)


def _mla_prep_body(qkv_ref, wq_ref, wkv_ref, gq_ref, gkv_ref, gqn_ref, gkn_ref, cos_ref, sin_ref, q_out, k_out, v_out):
    x = qkv_ref[0]
    qc, kvc, kpe = x[:, :Q_LORA], x[:, Q_LORA:Q_LORA + KV_LORA], x[:, Q_LORA + KV_LORA:]
    qn = qc * lax.rsqrt(jnp.mean(qc * qc, axis=-1, keepdims=True) + NORM_EPS) * gq_ref[...]
    kvn = kvc * lax.rsqrt(jnp.mean(kvc * kvc, axis=-1, keepdims=True) + NORM_EPS) * gkv_ref[...]
    q = jnp.dot(qn.astype(BF16), wq_ref[...], preferred_element_type=F32)
    kv = jnp.dot(kvn.astype(BF16), wkv_ref[...], preferred_element_type=F32)
    cos, sin = cos_ref[0], sin_ref[0]
    quarter = QK_ROPE // 4

    def rope(t):
        swapped = jnp.concatenate([t[:, quarter:2 * quarter], t[:, :quarter],
                                   t[:, 3 * quarter:], t[:, 2 * quarter:3 * quarter]], axis=-1)
        return t * cos + swapped * sin

    gqn, gkn = gqn_ref[...], gkn_ref[...]
    kpe_rot = rope(kpe * gkn[:, QK_NOPE:])
    kpe_ss = jnp.sum(kpe * kpe, axis=-1, keepdims=True)
    ones_col = jnp.where(lax.broadcasted_iota(jnp.int32, (x.shape[0], V_HEAD), 1) == 0, 1.0, 0.0).astype(BF16)
    for h in range(MLA_HEADS):
        qh = q[:, h * MLA_SLOT:h * MLA_SLOT + QK_HEAD]
        qh = qh * lax.rsqrt(jnp.sum(qh * qh, axis=-1, keepdims=True) * (1.0 / QK_HEAD) + NORM_EPS) * gqn
        qh = jnp.concatenate([qh[:, :QK_NOPE], rope(qh[:, QK_NOPE:])], axis=-1)
        q_out[0, h] = (qh * (QK_HEAD ** -0.5 * LOG2_E)).astype(q_out.dtype)
        kn = kv[:, h * MLA_SLOT:h * MLA_SLOT + QK_NOPE]
        rstd = lax.rsqrt((jnp.sum(kn * kn, axis=-1, keepdims=True) + kpe_ss) * (1.0 / QK_HEAD) + NORM_EPS)
        k_out[0, h] = jnp.concatenate([kn * rstd * gkn[:, :QK_NOPE], kpe_rot * rstd], axis=-1).astype(k_out.dtype)
        v_out[0, :, h * MLA_SLOT:h * MLA_SLOT + V_HEAD] = kv[:, h * MLA_SLOT + QK_NOPE:(h + 1) * MLA_SLOT].astype(v_out.dtype)
        v_out[0, :, h * MLA_SLOT + V_HEAD:(h + 1) * MLA_SLOT] = ones_col


def _mla_prep(qkv, p, cos, sin):
    b, l, w = qkv.shape
    tm = _tile(l, (256, 128, 64, 32, 16))
    nh = MLA_HEADS
    pad_heads = lambda m, width: jnp.pad(m.reshape(m.shape[0], nh, width), ((0, 0), (0, 0), (0, MLA_SLOT - width))
                                         ).reshape(m.shape[0], nh * MLA_SLOT)
    wq = pad_heads(p['w_qb'], QK_HEAD).astype(BF16)
    wkv = p['w_kvb'].astype(BF16)
    full = lambda a: pl.BlockSpec(a.shape, lambda i, j: (0,) * a.ndim)
    args = [wq, wkv, p['q_norm_g'].reshape(1, Q_LORA), p['kv_norm_g'].reshape(1, KV_LORA),
            p['qn_g'].reshape(1, QK_HEAD), p['kn_g'].reshape(1, QK_HEAD)]
    tab = pl.BlockSpec((1, tm, QK_ROPE), lambda i, j: (0, j, 0))
    head = pl.BlockSpec((1, nh, tm, QK_HEAD), lambda i, j: (i, 0, j, 0))
    return pl.pallas_call(
        _mla_prep_body,
        grid=(b, l // tm),
        in_specs=[pl.BlockSpec((1, tm, w), lambda i, j: (i, j, 0))] + [full(a) for a in args] + [tab, tab],
        out_specs=[head, head, pl.BlockSpec((1, tm, nh * MLA_SLOT), lambda i, j: (i, j, 0))],
        out_shape=[jax.ShapeDtypeStruct((b, nh, l, QK_HEAD), BF16), jax.ShapeDtypeStruct((b, nh, l, QK_HEAD), BF16),
                   jax.ShapeDtypeStruct((b, l, nh * MLA_SLOT), BF16)],
        compiler_params=_cparams("parallel", "parallel"),
        name="mla_prep",
    )(qkv, *args, cos, sin)


def _rope_tables(row, col, l_ctx):
    half = QK_ROPE // 2
    inv = ROPE_THETA ** (-jnp.arange(0, half, 2, dtype=F32) / half)
    ang_r = row.astype(F32)[:, None] * inv[None, :]
    ang_c = col.astype(F32)[:, None] * inv[None, :]
    cos = jnp.concatenate([jnp.cos(ang_r)] * 2 + [jnp.cos(ang_c)] * 2, axis=-1)
    sin = jnp.concatenate([-jnp.sin(ang_r), jnp.sin(ang_r), -jnp.sin(ang_c), jnp.sin(ang_c)], axis=-1)
    cos = jnp.concatenate([jnp.ones((l_ctx, QK_ROPE), F32), cos], axis=0)
    sin = jnp.concatenate([jnp.zeros((l_ctx, QK_ROPE), F32), sin], axis=0)
    return cos[None], sin[None]


def _mla_mixer(h, l_ctx, p, rope_pos):
    b, l, d = h.shape
    qkv = _linear(h.reshape(b * l, d).astype(BF16), p['w_a']).reshape(b, l, -1)
    cos, sin = _rope_tables(rope_pos[0], rope_pos[1], l_ctx)
    q, k, v_ext = _mla_prep(qkv, p, cos, sin)
    o = jnp.concatenate([_attend(q[:, :, :l_ctx], k[:, :, :l_ctx], v_ext[:, :l_ctx]),
                         _attend(q[:, :, l_ctx:], k, v_ext)], axis=1)
    return _linear(o.reshape(b * l, MLA_HEADS * V_HEAD), p['w_o']).reshape(b, l, d)


def _router_body(h_ref, w_ref, b_ref, idx_ref, gate_ref, rank_ref, cnt_ref, run_ref):
    @pl.when(pl.program_id(0) == 0)
    def _():
        run_ref[...] = jnp.zeros_like(run_ref)

    lg = jnp.dot(h_ref[...], w_ref[...], precision=lax.Precision.HIGHEST, preferred_element_type=F32) + b_ref[...]
    tm = lg.shape[0]
    lane = lax.broadcasted_iota(jnp.int32, lg.shape, 1).astype(F32)
    idx_acc = jnp.zeros(lg.shape, F32)
    val_acc = jnp.zeros(lg.shape, F32)
    picked = jnp.zeros(lg.shape, F32)
    sels = []
    top = None
    den = None
    for j in range(TOP_K):
        mx = jnp.max(lg, axis=-1, keepdims=True)
        sel = jnp.min(jnp.where(lg == mx, lane, float(ROUTER_LANES)), axis=-1, keepdims=True)
        if j == 0:
            top = mx
        e = jnp.exp(mx - top)
        den = e if den is None else den + e
        hit = lane == sel
        sels.append(hit)
        idx_acc = jnp.where(lane == float(j), sel, idx_acc)
        val_acc = jnp.where(lane == float(j), e, val_acc)
        picked = jnp.where(hit, 1.0, picked)
        lg = jnp.where(hit, -jnp.inf, lg)
    idx_ref[...] = idx_acc.astype(jnp.int32)
    gate_ref[...] = val_acc / den
    rs = lax.broadcasted_iota(jnp.int32, (tm, tm), 0)
    cs = lax.broadcasted_iota(jnp.int32, (tm, tm), 1)
    before = jnp.where(rs > cs, 1.0, 0.0).astype(BF16)
    base = run_ref[...] + jnp.dot(before, picked.astype(BF16), preferred_element_type=F32)
    rank_acc = jnp.zeros(lg.shape, F32)
    for j in range(TOP_K):
        rk = jnp.sum(jnp.where(sels[j], base, 0.0), axis=-1, keepdims=True)
        rank_acc = jnp.where(lane == float(j), rk, rank_acc)
    rank_ref[...] = rank_acc.astype(jnp.int32)
    run_ref[...] = run_ref[...] + jnp.sum(picked, axis=0, keepdims=True)
    cnt_ref[...] = run_ref[...].astype(jnp.int32)


def _router(h, router_w, router_b):
    t, d = h.shape
    tm = _tile(t, (512, 256, 128, 64, 32, 16, 8))
    w = jnp.concatenate([router_w, jnp.zeros((d, ROUTER_LANES - N_EXPERTS), F32)], axis=1)
    bias = jnp.concatenate([router_b, jnp.full((ROUTER_LANES - N_EXPERTS,), -1e30, F32)]).reshape(1, ROUTER_LANES)
    tok = pl.BlockSpec((tm, ROUTER_LANES), lambda i: (i, 0))
    return pl.pallas_call(
        _router_body,
        grid=(t // tm,),
        in_specs=[pl.BlockSpec((tm, d), lambda i: (i, 0)),
                  pl.BlockSpec((d, ROUTER_LANES), lambda i: (0, 0)),
                  pl.BlockSpec((1, ROUTER_LANES), lambda i: (0, 0))],
        out_specs=[tok, tok, tok, pl.BlockSpec((1, ROUTER_LANES), lambda i: (0, 0))],
        out_shape=[jax.ShapeDtypeStruct((t, ROUTER_LANES), jnp.int32), jax.ShapeDtypeStruct((t, ROUTER_LANES), F32),
                   jax.ShapeDtypeStruct((t, ROUTER_LANES), jnp.int32), jax.ShapeDtypeStruct((1, ROUTER_LANES), jnp.int32)],
        scratch_shapes=[pltpu.VMEM((1, ROUTER_LANES), F32)],
        compiler_params=_cparams("arbitrary"),
        name="router",
    )(h, w, bias)


def _expert_body(be_ref, nb_ref, x_ref, win_ref, bin_ref, wout_ref, bout_ref, gate_ref, o_ref, win_bf, wout_bf):
    i = pl.program_id(0)
    fresh = jnp.logical_or(i == 0, be_ref[i] != be_ref[jnp.maximum(i - 1, 0)])

    @pl.when(fresh)
    def _():
        win_bf[...] = win_ref[0].astype(BF16)
        wout_bf[...] = wout_ref[0].astype(BF16)

    @pl.when(i < nb_ref[0])
    def _():
        sub = MOE_ROWS // MOE_SLABS
        slabs = [pl.ds(j * sub, sub) for j in range(MOE_SLABS)]
        gu = [jnp.dot(x_ref[x, :], win_bf[...], preferred_element_type=F32) + bin_ref[0] for x in slabs]
        glu = [jnp.minimum(g[:, :D_EXPERT], SWIGLU_LIMIT) for g in gu]
        lin = [jnp.clip(g[:, D_EXPERT:], -SWIGLU_LIMIT, SWIGLU_LIMIT) for g in gu]
        act = [(a * jax.nn.sigmoid(SWIGLU_ALPHA * a) * (b + 1.0)).astype(BF16) for a, b in zip(glu, lin)]
        y = [jnp.dot(a, wout_bf[...], preferred_element_type=F32) + bout_ref[0] for a in act]
        for x, yy in zip(slabs, y):
            o_ref[x, :] = (yy * gate_ref[x, :]).astype(o_ref.dtype)

    @pl.when(i >= nb_ref[0])
    def _():
        o_ref[...] = jnp.zeros_like(o_ref)


def _moe(h, router_w, router_b, w_in, b_in, w_out, b_out):
    t, d = h.shape
    n_assign = t * TOP_K
    idx, gates, ranks, counts = _router(h, router_w, router_b)
    flat_e = idx[:, :TOP_K].reshape(n_assign)
    flat_g = gates[:, :TOP_K].reshape(n_assign)
    rank = ranks[:, :TOP_K].reshape(n_assign)
    counts = counts[0, :N_EXPERTS]
    padded = (counts + MOE_ROWS - 1) // MOE_ROWS * MOE_ROWS
    pad_end = jnp.cumsum(padded)
    pad_start = pad_end - padded
    experts = jnp.arange(N_EXPERTS, dtype=jnp.int32)
    lookup = lambda table, e: jnp.sum(jnp.where(e[:, None] == experts[None, :], table[None, :], 0), axis=1)
    dest = lookup(pad_start, flat_e) + rank
    n_blocks = -(-n_assign // MOE_ROWS) + N_EXPERTS
    n_rows = n_blocks * MOE_ROWS
    block_start = jnp.arange(n_blocks, dtype=jnp.int32) * MOE_ROWS
    block_e = jnp.minimum(jnp.sum((pad_end[None, :] <= block_start[:, None]).astype(jnp.int32), axis=1), N_EXPERTS - 1)
    order = jnp.argsort(flat_e, stable=True).astype(jnp.int32)
    start = jnp.cumsum(counts) - counts
    in_block = jnp.arange(MOE_ROWS, dtype=jnp.int32)[None, :]
    row_rank = (block_start - lookup(pad_start, block_e))[:, None] + in_block
    row_live = (row_rank < lookup(counts, block_e)[:, None]).reshape(n_rows)
    row_src = order[jnp.minimum(lookup(start, block_e)[:, None] + row_rank, n_assign - 1).reshape(n_rows)]
    row_tok = row_src // TOP_K
    row_gate = jnp.where(row_live, flat_g[row_src], 0.0)
    n_used = (pad_end[-1] // MOE_ROWS).astype(jnp.int32).reshape(1)
    xb = h.astype(BF16)[row_tok]
    f2 = 2 * D_EXPERT
    grid_spec = pltpu.PrefetchScalarGridSpec(
        num_scalar_prefetch=2,
        grid=(n_blocks,),
        in_specs=[pl.BlockSpec((MOE_ROWS, d), lambda i, be, nb: (i, 0)),
                  pl.BlockSpec((1, d, f2), lambda i, be, nb: (be[i], 0, 0)),
                  pl.BlockSpec((1, 1, f2), lambda i, be, nb: (be[i], 0, 0)),
                  pl.BlockSpec((1, D_EXPERT, d), lambda i, be, nb: (be[i], 0, 0)),
                  pl.BlockSpec((1, 1, d), lambda i, be, nb: (be[i], 0, 0)),
                  pl.BlockSpec((MOE_ROWS, 1), lambda i, be, nb: (i, 0))],
        out_specs=pl.BlockSpec((MOE_ROWS, d), lambda i, be, nb: (i, 0)),
        scratch_shapes=[pltpu.VMEM((d, f2), BF16), pltpu.VMEM((D_EXPERT, d), BF16)],
    )
    yb = pl.pallas_call(
        _expert_body,
        grid_spec=grid_spec,
        out_shape=jax.ShapeDtypeStruct((n_rows, d), BF16),
        compiler_params=_cparams("arbitrary"),
        name="moe_experts",
    )(block_e, n_used, xb, w_in, b_in.reshape(N_EXPERTS, 1, f2),
      w_out, b_out.reshape(N_EXPERTS, 1, d), row_gate.reshape(n_rows, 1))
    dest = dest.reshape(t, TOP_K)
    return [yb[dest[:, j]] for j in range(TOP_K)]


def kernel(x, c, ctx, c_ctx, ada_w, ada_b, norm_mix_g, norm_ffn_g, rwkv_mu, rwkv_w_rkv, rwkv_w0, rwkv_w1, rwkv_w2, rwkv_a0, rwkv_a1, rwkv_a2, rwkv_g1, rwkv_g2, rwkv_k_k, rwkv_k_a, rwkv_r_k, rwkv_ln_g, rwkv_ln_b, rwkv_w_o, rwkv_v0, rwkv_v1, rwkv_v2, hy_w_in, hy_b_in, hy_short_w, hy_short_b, hy_f_w1, hy_f_b1, hy_f_w2, hy_f_b2, hy_f_w3, hy_f_b3, hy_f_freq, hy_f_wout, hy_bias, hy_w_out, hy_b_out, mla_w_a, mla_q_norm_g, mla_kv_norm_g, mla_w_qb, mla_w_kvb, mla_qn_g, mla_kn_g, mla_w_o, moe_router_w, moe_router_b, moe_w_in, moe_b_in, moe_w_out, moe_b_out):
    bsz, s_len, d = x.shape
    l_ctx = ctx.shape[1]
    rows = s_len // GRID_W
    row = jnp.repeat(jnp.arange(rows, dtype=jnp.int32), GRID_W)
    col = jnp.tile(jnp.arange(GRID_W, dtype=jnp.int32), rows)
    silu = jnp.concatenate([jax.nn.silu(c), jax.nn.silu(c_ctx)[None, :]], axis=0)
    silu = jnp.concatenate([silu, jnp.zeros((-(bsz + 1) % 8, d), F32)], axis=0)
    is_ctx = (jnp.arange(l_ctx + s_len, dtype=jnp.int32) < l_ctx)[None, :, None]
    per_row = lambda m: jnp.where(is_ctx, m[:, 0:1], m[:, 1:2])
    xs = jnp.concatenate([ctx, x], axis=1)
    mods = []
    for i in range(DEPTH):
        mod = _linear(silu, ada_w[i], ada_b[i], tn=1024)
        mod = jnp.stack([jnp.broadcast_to(mod[bsz:bsz + 1], (bsz, N_MOD * d)), mod[:bsz]], axis=1)
        mods.append(jnp.split(mod, N_MOD, axis=-1))
    v_first = None
    h = _norm_mod(xs, norm_mix_g[0], mods[0][1], mods[0][0], l_ctx)
    for i in range(DEPTH):
        j = i // N_MIXERS
        kind = i % N_MIXERS
        sh1, sc1, g1, sh2, sc2, g2 = mods[i]
        if kind == 0:
            p = {'mu': rwkv_mu[j], 'w_rkv': rwkv_w_rkv[j], 'w0': rwkv_w0[j], 'w1': rwkv_w1[j],
                 'w2': rwkv_w2[j], 'a0': rwkv_a0[j], 'a1': rwkv_a1[j], 'a2': rwkv_a2[j],
                 'g1': rwkv_g1[j], 'g2': rwkv_g2[j], 'k_k': rwkv_k_k[j], 'k_a': rwkv_k_a[j],
                 'r_k': rwkv_r_k[j], 'ln_g': rwkv_ln_g[j], 'ln_b': rwkv_ln_b[j], 'w_o': rwkv_w_o[j]}
            vres = None if j == 0 else (rwkv_v0[j - 1], rwkv_v1[j - 1], rwkv_v2[j - 1])
            y, v_cur = _rwkv_mixer(h, l_ctx, p, v_first, vres)
            if j == 0:
                v_first = v_cur
        elif kind == 1:
            p = {'w_in': hy_w_in[j], 'b_in': hy_b_in[j], 'short_w': hy_short_w[j], 'short_b': hy_short_b[j],
                 'f_w1': hy_f_w1[j], 'f_b1': hy_f_b1[j], 'f_w2': hy_f_w2[j], 'f_b2': hy_f_b2[j],
                 'f_w3': hy_f_w3[j], 'f_b3': hy_f_b3[j], 'f_freq': hy_f_freq[j], 'f_wout': hy_f_wout[j],
                 'bias': hy_bias[j], 'w_out': hy_w_out[j], 'b_out': hy_b_out[j]}
            y = _hyena_mixer(h, l_ctx, p)
        else:
            p = {'w_a': mla_w_a[j], 'q_norm_g': mla_q_norm_g[j], 'kv_norm_g': mla_kv_norm_g[j],
                 'w_qb': mla_w_qb[j], 'w_kvb': mla_w_kvb[j], 'qn_g': mla_qn_g[j], 'kn_g': mla_kn_g[j],
                 'w_o': mla_w_o[j]}
            y = _mla_mixer(h, l_ctx, p, (row, col))
        xs, h2 = _residual_norm(xs, [y], g1, norm_ffn_g[i], sc2, sh2, l_ctx)
        moe_p = (moe_router_w[i], moe_router_b[i], moe_w_in[i], moe_b_in[i], moe_w_out[i], moe_b_out[i])
        parts = [f.reshape(xs.shape) for f in _moe(h2.reshape(-1, d), *moe_p)]
        if i + 1 < DEPTH:
            xs, h = _residual_norm(xs, parts, g2, norm_mix_g[i + 1], mods[i + 1][1], mods[i + 1][0], l_ctx)
        else:
            xs = xs + per_row(g2) * sum(f.astype(F32) for f in parts)
    return xs[:, l_ctx:]
```

```python
import functools
import math

import jax
import jax.numpy as jnp
import numpy as np
from jax import lax
from jax.experimental import pallas as pl
from jax.experimental.pallas import tpu as pltpu

F32 = jnp.float32
BF16 = jnp.bfloat16

D_MODEL = 1024
DEPTH = 4
GRID_W = 64
N_MIXERS = 3
N_MOD = 6
NORM_EPS = 1e-6
RWKV_HEAD = 64
RWKV_HEADS = D_MODEL // RWKV_HEAD
GN_EPS = 64e-5
WKV_CHUNK = 64
WKV_GROUP = 4
FILTER_EMB = 33
FILTER_EMB_PAD = 128
DFT_MIN_OUTER = 32
FAST_DECAY = math.log(1e-2) / 0.3
SLOW_DECAY = math.log(1e-2) / 1.5
MLA_HEADS = 8
QK_NOPE = 128
QK_ROPE = 64
QK_HEAD = QK_NOPE + QK_ROPE
V_HEAD = 128
Q_LORA = 384
KV_LORA = 256
ROPE_THETA = 10000.0
N_EXPERTS = 32
TOP_K = 4
D_EXPERT = 1024
SWIGLU_LIMIT = 7.0
SWIGLU_ALPHA = 1.702
MOE_ROWS = 256
MOE_SLABS = 2
LANES = 128
ROUTER_LANES = LANES
ATTN_SLAB = 512
LOG2_E = 1.4426950408889634
MLA_SLOT = 256

V7X_VMEM_BYTES = 64 * 1024 * 1024
VMEM_LIMIT = V7X_VMEM_BYTES * 3 // 4


def _cparams(*sem):
    return pltpu.CompilerParams(dimension_semantics=sem, vmem_limit_bytes=VMEM_LIMIT)


def _tile(n, prefs):
    for t in prefs:
        if n % t == 0:
            return t
    return n


def _linear_body(a_ref, w_ref, b_ref, o_ref, *, in_act):
    a = a_ref[...]
    if in_act == 'tanh':
        a = jnp.tanh(a.astype(F32))
    elif in_act == 'sigmoid':
        a = jax.nn.sigmoid(a.astype(F32))
    acc = jnp.dot(a.astype(BF16), w_ref[...], preferred_element_type=F32)
    o_ref[...] = (acc + b_ref[...]).astype(o_ref.dtype)


def _linear(a, w, b=None, *, in_act=None, out_dtype=F32, tn=None):
    m, k = a.shape
    n = w.shape[1]
    tm = _tile(m, (512, 256, 128, 64, 32, 16, 8))
    tn = n if tn is None else tn
    assert n % tn == 0
    if b is None:
        b = jnp.zeros((n,), F32)
    return pl.pallas_call(
        functools.partial(_linear_body, in_act=in_act),
        grid=(n // tn, m // tm),
        in_specs=[pl.BlockSpec((tm, k), lambda j, i: (i, 0)),
                  pl.BlockSpec((k, tn), lambda j, i: (0, j)),
                  pl.BlockSpec((1, tn), lambda j, i: (0, j))],
        out_specs=pl.BlockSpec((tm, tn), lambda j, i: (i, j)),
        out_shape=jax.ShapeDtypeStruct((m, n), out_dtype),
        compiler_params=_cparams("parallel", "parallel"),
        name="linear",
    )(a, w.astype(BF16), b.reshape(1, n).astype(F32))


def _norm_mod_body(x_ref, g_ref, sc_ref, sh_ref, o_ref):
    x = x_ref[0]
    y = x * lax.rsqrt(jnp.mean(x * x, axis=-1, keepdims=True) + NORM_EPS)
    o_ref[0] = ((y * g_ref[...]) * (1.0 + sc_ref[0, 0]) + sh_ref[0, 0]).astype(o_ref.dtype)


def _norm_mod(x, g, sc, sh, l_ctx, out_dtype):
    b, l, d = x.shape
    tl = _tile(math.gcd(l_ctx, l - l_ctx), (512, 256, 128, 64, 32, 16, 8))
    stream = lambda i, j: (i, jnp.where(j * tl < l_ctx, 0, 1), 0, 0)
    return pl.pallas_call(
        _norm_mod_body,
        grid=(b, l // tl),
        in_specs=[pl.BlockSpec((1, tl, d), lambda i, j: (i, j, 0)),
                  pl.BlockSpec((1, d), lambda i, j: (0, 0)),
                  pl.BlockSpec((1, 1, 1, d), stream),
                  pl.BlockSpec((1, 1, 1, d), stream)],
        out_specs=pl.BlockSpec((1, tl, d), lambda i, j: (i, j, 0)),
        out_shape=jax.ShapeDtypeStruct((b, l, d), out_dtype),
        compiler_params=_cparams("parallel", "parallel"),
        name="norm_mod",
    )(x, g.reshape(1, d), sc.reshape(b, 2, 1, d), sh.reshape(b, 2, 1, d))


def _residual_norm_body(*refs, n_add):
    x_ref, adds, (gate_ref, g_ref, sc_ref, sh_ref, xo_ref), h_refs = refs[0], refs[1:1 + n_add], refs[1 + n_add:6 + n_add], refs[6 + n_add:]
    y = adds[0][0].astype(F32)
    for a in adds[1:]:
        y = y + a[0].astype(F32)
    x = x_ref[0] + gate_ref[0, 0] * y
    xo_ref[0] = x
    nrm = x * lax.rsqrt(jnp.mean(x * x, axis=-1, keepdims=True) + NORM_EPS)
    h = (nrm * g_ref[...]) * (1.0 + sc_ref[0, 0]) + sh_ref[0, 0]
    for h_ref in h_refs:
        h_ref[0] = h.astype(h_ref.dtype)


def _residual_norm(x, adds, gate, g, sc, sh, l_ctx, h_dtypes):
    b, l, d = x.shape
    tl = _tile(math.gcd(l_ctx, l - l_ctx), (512, 256, 128, 64, 32, 16, 8))
    stream = pl.BlockSpec((1, 1, 1, d), lambda i, j: (i, jnp.where(j * tl < l_ctx, 0, 1), 0, 0))
    rows = pl.BlockSpec((1, tl, d), lambda i, j: (i, j, 0))
    return pl.pallas_call(
        functools.partial(_residual_norm_body, n_add=len(adds)),
        grid=(b, l // tl),
        in_specs=[rows] * (1 + len(adds)) + [stream, pl.BlockSpec((1, d), lambda i, j: (0, 0)), stream, stream],
        out_specs=[rows] * (1 + len(h_dtypes)),
        out_shape=[jax.ShapeDtypeStruct((b, l, d), F32)] + [jax.ShapeDtypeStruct((b, l, d), t) for t in h_dtypes],
        compiler_params=_cparams("parallel", "parallel"),
        name="residual_norm",
    )(x, *adds, gate.reshape(b, 2, 1, d), g.reshape(1, d), sc.reshape(b, 2, 1, d), sh.reshape(b, 2, 1, d))


def _stream_neighbours(t, l_ctx):
    zero = jnp.zeros_like(t[:, :1])
    pos = jnp.arange(t.shape[1], dtype=jnp.int32)[None, :, None]
    prev = jnp.where(pos == l_ctx, 0.0, jnp.concatenate([zero, t[:, :-1]], axis=1))
    nxt = jnp.where(pos == l_ctx - 1, 0.0, jnp.concatenate([t[:, 1:], zero], axis=1))
    return prev, nxt


PROJ_R, PROJ_K, PROJ_V = 0, 1024, 2048
PROJ_W1 = (3072, 3200)
PROJ_A1 = (3328, 3456)
PROJ_G1 = 3584
PROJ_V1 = 3840
PROJ_WIDTH = 4096
LORA_PAD = 128
GATE_PAD = 256


def _sigmoid(x):
    return jax.nn.sigmoid(x)


def _rwkv_post_body(*refs, has_vres):
    if has_vres:
        (proj_ref, vf_ref, w2_ref, a2_ref, g2_ref, v2_ref, vec_ref, gs_ref,
         kk_out, v_out, gate_out, lw_out, ag_out, kd_out) = refs
    else:
        (proj_ref, w2_ref, a2_ref, g2_ref, vec_ref, gs_ref,
         kk_out, v_out, gate_out, lw_out, ag_out, kd_out) = refs
    d = D_MODEL
    vec = vec_ref[...]
    k = proj_ref[:, PROJ_K:PROJ_K + d].astype(F32)
    v = proj_ref[:, PROJ_V:PROJ_V + d].astype(F32)
    if has_vres:
        v1 = proj_ref[:, PROJ_V1:PROJ_V1 + LORA_PAD]
        vgate = _sigmoid(vec[6:7] + jnp.dot(v1, v2_ref[...], preferred_element_type=F32))
        v = v + (vf_ref[...].astype(F32) - v) * vgate
    v_out[...] = v.astype(v_out.dtype)
    kraw = k * vec[4:5]
    ss = jnp.dot((kraw * kraw).astype(BF16), gs_ref[...], preferred_element_type=F32)
    kk_out[...] = (kraw / jnp.maximum(jnp.sqrt(ss), 1e-12)).astype(kk_out.dtype)
    g1 = proj_ref[:, PROJ_G1:PROJ_G1 + GATE_PAD].astype(F32)
    gate_out[...] = jnp.dot(_sigmoid(g1).astype(BF16), g2_ref[...], preferred_element_type=F32).astype(gate_out.dtype)
    for dd in range(2):
        w1 = proj_ref[:, PROJ_W1[dd]:PROJ_W1[dd] + LORA_PAD].astype(F32)
        z = vec[dd:dd + 1] + jnp.dot(jnp.tanh(w1).astype(BF16), w2_ref[dd], preferred_element_type=F32)
        softplus = jnp.maximum(-z, 0.0) + jnp.log(1.0 + jnp.exp(-jnp.abs(z)))
        lw_out[dd] = -jnp.exp(-softplus - 0.5)
        a1 = proj_ref[:, PROJ_A1[dd]:PROJ_A1[dd] + LORA_PAD]
        ag = _sigmoid(vec[2 + dd:3 + dd] + jnp.dot(a1, a2_ref[dd], preferred_element_type=F32))
        ag_out[dd] = ag.astype(ag_out.dtype)
        kd_out[dd] = (k * (1.0 + (ag - 1.0) * vec[5:6])).astype(kd_out.dtype)


def _rwkv_post(proj, v_first, w2p, a2p, g2p, v2p, vec, gsum):
    t, d = proj.shape[0], D_MODEL
    tm = _tile(t, (256, 128, 64, 32, 16))
    has_vres = v_first is not None
    full = lambda a: pl.BlockSpec(a.shape, lambda i: (0,) * a.ndim)
    row = pl.BlockSpec((tm, d), lambda i: (i, 0))
    row2 = pl.BlockSpec((2, tm, d), lambda i: (0, i, 0))
    ins = [proj] + ([v_first] if has_vres else []) + [w2p, a2p, g2p] + ([v2p] if has_vres else []) + [vec, gsum]
    in_specs = ([pl.BlockSpec((tm, PROJ_WIDTH), lambda i: (i, 0))] + ([row] if has_vres else [])
                + [full(a) for a in ins[(2 if has_vres else 1):]])
    return pl.pallas_call(
        functools.partial(_rwkv_post_body, has_vres=has_vres),
        grid=(t // tm,),
        in_specs=in_specs,
        out_specs=[row, row, row, row2, row2, row2],
        out_shape=[jax.ShapeDtypeStruct((t, d), BF16),
                   jax.ShapeDtypeStruct((t, d), BF16),
                   jax.ShapeDtypeStruct((t, d), BF16),
                   jax.ShapeDtypeStruct((2, t, d), F32),
                   jax.ShapeDtypeStruct((2, t, d), BF16),
                   jax.ShapeDtypeStruct((2, t, d), BF16)],
        compiler_params=_cparams("parallel"),
        name="rwkv_post",
    )(*ins)


def _head_groups(y):
    gw = WKV_GROUP * RWKV_HEAD
    lane_head = jnp.right_shift(lax.broadcasted_iota(jnp.int32, (y.shape[0], gw), 1), 6)
    out = []
    for g in range(D_MODEL // gw):
        yg = y[:, g * gw:(g + 1) * gw].astype(BF16)
        out.append(jnp.concatenate([jnp.where(lane_head == h, yg, jnp.zeros_like(yg)) for h in range(WKV_GROUP)], axis=0))
    return out


def _gmm(x, ybd):
    gw = WKV_GROUP * RWKV_HEAD
    return jnp.concatenate([jnp.dot(x[:, g * gw:(g + 1) * gw].astype(BF16), yb, preferred_element_type=F32)
                            for g, yb in enumerate(ybd)], axis=-1)


def _gmm_nt(x, ybd):
    gw = WKV_GROUP * RWKV_HEAD
    return jnp.concatenate([lax.dot_general(x[:, g * gw:(g + 1) * gw].astype(BF16), yb, (((1,), (1,)), ((), ())),
                                            preferred_element_type=F32)
                            for g, yb in enumerate(ybd)], axis=-1)


def _wkv_chunks(r, kk, v, lw, ag, kd, s0, rev):
    each = lambda f, *ls: [f(*a) for a in zip(*ls)]
    c, d = r[0].shape
    row0 = lax.broadcasted_iota(jnp.int32, (c, d), 0)
    col0 = jnp.bitwise_and(lax.broadcasted_iota(jnp.int32, (c, d), 1), RWKV_HEAD - 1)
    incl = [(row0 <= col0) if rv else (row0 >= col0) for rv in rev]
    strict = [(row0 < col0) if rv else (row0 > col0) for rv in rev]
    rs = lax.broadcasted_iota(jnp.int32, (c, c), 0)
    cs = lax.broadcasted_iota(jnp.int32, (c, c), 1)
    tri = [jnp.where((rs <= cs) if rv else (rs >= cs), 1.0, 0.0).astype(BF16) for rv in rev]
    hi = each(lambda x: x.astype(BF16), lw)
    rem = each(lambda x, h: x - h.astype(F32), lw, hi)
    mid = each(lambda x: x.astype(BF16), rem)
    lo = each(lambda x, m: (x - m.astype(F32)).astype(BF16), rem, mid)
    cum = each(lambda t, a, b_, c_: (jnp.dot(t, a, preferred_element_type=F32) + jnp.dot(t, b_, preferred_element_type=F32)
                                     + jnp.dot(t, c_, preferred_element_type=F32)), tri, hi, mid, lo)
    cum_end = [x[0:1, :] if rv else x[c - 1:c, :] for x, rv in zip(cum, rev)]
    g_rest = each(lambda e, x: jnp.exp(e - x), cum_end, cum)
    g_inv = each(lambda x: jnp.exp(-x), cum)
    bvec = each(lambda a, b_: a * b_, kk, ag)
    rt = each(lambda a, x: a * jnp.exp(x), r, cum)
    kt = each(lambda a, g: a * g, kd, g_inv)
    bt = each(lambda a, g: a * g, bvec, g_inv)
    at = each(lambda a, x, w: -a * jnp.exp(x - w), kk, cum, lw)
    ar = each(lambda a, b_: jnp.concatenate([a, b_], axis=0), at, rt)
    ab = each(lambda a, b_: _gmm_nt(a, _head_groups(b_)), ar, bt)
    ak = each(lambda a, b_: _gmm_nt(a, _head_groups(b_)), ar, kt)
    a_ab = each(lambda m, x: jnp.where(m, x[:c], 0.0), strict, ab)
    a_rb = each(lambda m, x: jnp.where(m, x[c:], 0.0), incl, ab)
    a_ak = each(lambda m, x: jnp.where(m, x[:c], 0.0), strict, ak)
    a_rk = each(lambda m, x: jnp.where(m, x[c:], 0.0), incl, ak)
    same16 = jnp.right_shift(row0, 4) == jnp.right_shift(col0, 4)
    same32 = jnp.right_shift(row0, 5) == jnp.right_shift(col0, 5)
    eye = jnp.where(row0 == col0, 1.0, 0.0)
    npow = each(lambda x: jnp.where(same16, x, 0.0), a_ab)
    p = each(lambda x: eye + x, npow)
    for _ in range(3):
        npow = each(lambda x: _gmm(x, _head_groups(x)), npow)
        p = each(lambda x, y: x + _gmm(x, _head_groups(y)), p, npow)
    off32 = jnp.logical_and(same32, jnp.logical_not(same16))
    t1 = each(lambda x, y: _gmm(jnp.where(off32, x, 0.0), _head_groups(y)), a_ab, p)
    p = each(lambda x, y: x + _gmm(x, _head_groups(y)), p, t1)
    t2 = each(lambda x, y: _gmm(jnp.where(same32, 0.0, x), _head_groups(y)), a_ab, p)
    p = each(lambda x, y: x + _gmm(x, _head_groups(y)), p, t2)
    pbd = each(_head_groups, p)
    vbd = each(_head_groups, v)
    wmat = each(lambda x, y: _gmm(x, _head_groups(y)), p, at)
    vt = each(lambda y: _gmm_nt(eye, y), vbd)
    xt = each(lambda x, y: _gmm_nt(x, _head_groups(y)), vt, a_ak)
    upt = each(_gmm_nt, xt, pbd)
    yp = each(_gmm, a_rk, vbd)
    sp = each(lambda x, a, g: _gmm(x, _head_groups(a * g)), vt, kd, g_rest)
    ut = each(lambda s_, w, u: _gmm_nt(s_, _head_groups(w)) + u, s0, wmat, upt)
    ys = each(lambda s_, q, x: _gmm_nt(q, _head_groups(s_)) + x, s0, rt, yp)
    y = each(lambda y0, a, u: y0 + _gmm_nt(a, _head_groups(u)), ys, a_rb, ut)
    s_end = each(lambda s_, e, u, b_, g, x: s_ * jnp.exp(e) + _gmm(u, _head_groups(b_ * g)) + x,
                 s0, cum_end, ut, bvec, g_rest, sp)
    return y, s_end


def _wkv_body(*refs, n_batch):
    ins, outs, s_ref = refs[:12], refs[12:14], refs[14]

    @pl.when(pl.program_id(0) == 0)
    def _():
        s_ref[...] = jnp.zeros_like(s_ref)

    scans = [(dd, bi) for dd in range(2) for bi in range(n_batch)]
    get = lambda k, lead: [ins[6 * dd + k][(0, bi) if lead else bi].astype(F32) for dd, bi in scans]
    y, s_end = _wkv_chunks(get(0, False), get(1, False), get(2, False), get(3, True), get(4, True), get(5, True),
                           [s_ref[i] for i in range(len(scans))], [dd == 1 for dd, _ in scans])
    for i, (dd, bi) in enumerate(scans):
        outs[dd][bi] = y[i]
        s_ref[i] = s_end[i]


def _wkv(proj, kk, v, lw, ag, kd, l_ctx):
    b, l, d = kk.shape
    c = WKV_CHUNK
    nc, nctx = l // c, l_ctx // c
    assert c == RWKV_HEAD and l % c == 0 and l_ctx % c == 0

    def fwd(j):
        return j

    def bwd(j):
        return jnp.where(j < nctx, nctx - 1 - j, nc + nctx - 1 - j)

    in_specs, out_specs = [], []
    for dd, ch in enumerate((fwd, bwd)):
        shared = pl.BlockSpec((b, c, d), lambda j, ch=ch: (0, ch(j), 0))
        per_dir = pl.BlockSpec((1, b, c, d), lambda j, dd=dd, ch=ch: (dd, 0, ch(j), 0))
        in_specs += [shared, shared, shared, per_dir, per_dir, per_dir]
        out_specs.append(shared)
    return pl.pallas_call(
        functools.partial(_wkv_body, n_batch=b),
        grid=(nc,),
        in_specs=in_specs,
        out_specs=out_specs,
        out_shape=[jax.ShapeDtypeStruct((b, l, d), F32)] * 2,
        scratch_shapes=[pltpu.VMEM((2 * b, RWKV_HEAD, d), F32)],
        compiler_params=_cparams("arbitrary"),
        name="wkv",
    )(*([proj, kk, v, lw, ag, kd] * 2))


def _rwkv_readout_body(yf_ref, yr_ref, r_ref, kd_ref, v_ref, gate_ref, gm_ref, gs_ref, vec_ref, wo_ref, o_ref):
    vec = vec_ref[...]
    y = yf_ref[...] + yr_ref[...]
    gm = gm_ref[...]
    yh = y.astype(BF16)
    yl = (y - yh.astype(F32)).astype(BF16)
    mean = jnp.dot(yh, gm, preferred_element_type=F32) + jnp.dot(yl, gm, preferred_element_type=F32)
    yc = y - mean
    var = jnp.dot((yc * yc).astype(BF16), gm, preferred_element_type=F32)
    yn = yc * lax.rsqrt(var + GN_EPS) * vec[0:1] + vec[1:2]
    r = r_ref[...].astype(F32)
    kds = kd_ref[0].astype(F32) + kd_ref[1].astype(F32)
    bonus = jnp.dot((r * kds * vec[2:3]).astype(BF16), gs_ref[...], preferred_element_type=F32)
    out = (yn + bonus * v_ref[...].astype(F32)) * gate_ref[...].astype(F32)
    o_ref[...] = jnp.dot(out.astype(BF16), wo_ref[...], preferred_element_type=F32)


def _rwkv_readout(y_fwd, y_rev, proj, kd, v, gate, gmean, gsum, vec, w_o):
    t, d = v.shape
    tm = _tile(t, (256, 128, 64, 32, 16))
    full = lambda a: pl.BlockSpec(a.shape, lambda i: (0,) * a.ndim)
    row = pl.BlockSpec((tm, d), lambda i: (i, 0))
    row2 = pl.BlockSpec((2, tm, d), lambda i: (0, i, 0))
    return pl.pallas_call(
        _rwkv_readout_body,
        grid=(t // tm,),
        in_specs=[row, row, row, row2, row, row, full(gmean), full(gsum), full(vec), full(w_o)],
        out_specs=row,
        out_shape=jax.ShapeDtypeStruct((t, d), F32),
        compiler_params=_cparams("parallel"),
        name="rwkv_readout",
    )(y_fwd, y_rev, proj, kd, v, gate, gmean, gsum, vec, w_o)


def _pad_cols(w, width):
    return jnp.concatenate([w, jnp.zeros((w.shape[0], width - w.shape[1]), w.dtype)], axis=1)


def _pad_rows(w, height):
    return jnp.concatenate([w, jnp.zeros((height - w.shape[0], w.shape[1]), w.dtype)], axis=0)


def _rwkv_mixer(h, l_ctx, p, v_first, vres):
    b, l, d = h.shape
    t = b * l
    h32 = h.astype(F32)
    prev, nxt = _stream_neighbours(h32, l_ctx)
    xx = 0.5 * (prev + nxt) - h32
    mu = p['mu']
    mats = [(p['w_rkv'][0], 0, d), (p['w_rkv'][1], 2, d), (p['w_rkv'][2], 3, d),
            (p['w1'][0], 1, LORA_PAD), (p['w1'][1], 1, LORA_PAD), (p['a1'][0], 4, LORA_PAD), (p['a1'][1], 4, LORA_PAD),
            (p['g1'], 5, GATE_PAD), (vres[1] if vres is not None else jnp.zeros((d, 1), F32), 3, 2 * LORA_PAD)]
    top = jnp.concatenate([_pad_cols(w, width) for w, _, width in mats], axis=1)
    bot = jnp.concatenate([_pad_cols(mu[m][:, None] * w, width) for w, m, width in mats], axis=1)
    wcat = jnp.concatenate([top, bot], axis=0)
    assert wcat.shape[1] == PROJ_WIDTH
    a_in = jnp.concatenate([h.astype(BF16), xx.astype(BF16)], axis=-1).reshape(t, 2 * d)
    proj = _linear(a_in, wcat, out_dtype=BF16, tn=1024)
    w2p = jnp.stack([_pad_rows(p['w2'][i], LORA_PAD) for i in range(2)]).astype(BF16)
    a2p = jnp.stack([_pad_rows(p['a2'][i], LORA_PAD) for i in range(2)]).astype(BF16)
    g2p = _pad_rows(p['g2'], GATE_PAD).astype(BF16)
    head_of = jnp.arange(d, dtype=jnp.int32) // RWKV_HEAD
    same_head = head_of[:, None] == head_of[None, :]
    gsum = jnp.where(same_head, 1.0, 0.0).astype(BF16)
    gmean = jnp.where(same_head, 1.0 / RWKV_HEAD, 0.0).astype(BF16)
    zero = jnp.zeros((d,), F32)
    vec = jnp.stack([p['w0'][0], p['w0'][1], p['a0'][0], p['a0'][1], p['k_k'], p['k_a'],
                     vres[0] if vres is not None else zero, zero])
    if vres is not None:
        kk, v, gate, lw, ag, kd = _rwkv_post(proj, v_first.reshape(t, d), w2p, a2p, g2p,
                                             _pad_rows(vres[2], LORA_PAD).astype(BF16), vec, gsum)
    else:
        kk, v, gate, lw, ag, kd = _rwkv_post(proj, None, w2p, a2p, g2p, None, vec, gsum)
    sh3 = lambda a: a.reshape(a.shape[:-2] + (b, l, a.shape[-1]))
    y_fwd, y_rev = _wkv(sh3(proj), sh3(kk), sh3(v), sh3(lw), sh3(ag), sh3(kd), l_ctx)
    vec_o = jnp.stack([p['ln_g'], p['ln_b'], p['r_k'].reshape(d)] + [zero] * 5)
    out = _rwkv_readout(y_fwd.reshape(t, d), y_rev.reshape(t, d), proj, kd, v, gate, gmean, gsum, vec_o, p['w_o'].astype(BF16))
    return out.reshape(b, l, d), v.reshape(b, l, d)


def _filter_body(z_ref, t_ref, w1_ref, b1_ref, w2_ref, b2_ref, w3_ref, b3_ref, fr_ref, wo_ref, dl_ref, o_ref, *, length):
    hp = lax.Precision.HIGHEST
    fr = fr_ref[...]
    hdn = jnp.sin(fr[0:1] * (jnp.dot(z_ref[...], w1_ref[...], precision=hp, preferred_element_type=F32) + b1_ref[...]))
    hdn = jnp.sin(fr[1:2] * (jnp.dot(hdn, w2_ref[...], precision=hp, preferred_element_type=F32) + b2_ref[...]))
    hdn = jnp.sin(fr[2:3] * (jnp.dot(hdn, w3_ref[...], precision=hp, preferred_element_type=F32) + b3_ref[...]))
    filt = jnp.dot(hdn, wo_ref[...], precision=hp, preferred_element_type=F32) * jnp.exp(-t_ref[...] * dl_ref[...])
    tl = filt.shape[0]
    n = pl.program_id(0) * tl + lax.broadcasted_iota(jnp.int32, filt.shape, 0)
    o_ref[...] = jnp.where(n == length, 0.0, filt * (0.5 / length))


def _hyena_filters(length, p):
    n = jnp.arange(2 * length, dtype=jnp.int32)
    pos = jnp.where(n < length, n, 2 * length - n).astype(F32)[:, None]
    t = pos / (length - 1)
    bands = (FILTER_EMB - 1) // 2
    f = jnp.linspace(1e-4, bands - 1, bands, dtype=F32)[None, :]
    ang = 2.0 * math.pi * f * pos / length
    z = jnp.concatenate([t, jnp.cos(ang), -jnp.sin(ang), jnp.zeros((2 * length, FILTER_EMB_PAD - FILTER_EMB), F32)], axis=-1)
    w1 = jnp.concatenate([p['f_w1'], jnp.zeros((FILTER_EMB_PAD - FILTER_EMB, p['f_w1'].shape[1]), F32)], axis=0)
    deltas = jnp.abs(jnp.linspace(FAST_DECAY, SLOW_DECAY, D_MODEL, dtype=F32))[None, :]
    fw = p['f_w2'].shape[0]
    tl = _tile(length, (512, 256, 128, 64, 32, 16, 8))
    half = length // tl
    full = lambda a: pl.BlockSpec(a.shape, lambda i: (0,) * a.ndim)
    args = [w1, p['f_b1'].reshape(1, fw), p['f_w2'], p['f_b2'].reshape(1, fw), p['f_w3'], p['f_b3'].reshape(1, fw),
            p['f_freq']]
    return pl.pallas_call(
        functools.partial(_filter_body, length=length),
        grid=(2 * half,),
        in_specs=[pl.BlockSpec((tl, FILTER_EMB_PAD), lambda i: (i, 0)), pl.BlockSpec((tl, 1), lambda i: (i, 0))]
                 + [full(a) for a in args]
                 + [pl.BlockSpec((fw, D_MODEL), lambda i: (0, jnp.where(i < half, 0, 1))),
                    full(deltas)],
        out_specs=pl.BlockSpec((tl, D_MODEL), lambda i: (i, 0)),
        out_shape=jax.ShapeDtypeStruct((2 * length, D_MODEL), F32),
        compiler_params=_cparams("parallel"),
        name="hyena_filter",
    )(z, t, *args, p['f_wout'], deltas)


def _colmm_body(f_ref, x_ref, o_ref):
    o_ref[0] = jnp.dot(f_ref[...], x_ref[0].astype(BF16), preferred_element_type=F32).astype(o_ref.dtype)


def _colmm(f, x, out_dtype):
    bsz, q, n = x.shape
    pp = f.shape[0]
    tn = _tile(n, (4096, 2048, 1024, 512, 256, 128))
    return pl.pallas_call(
        _colmm_body,
        grid=(bsz, n // tn),
        in_specs=[pl.BlockSpec((pp, q), lambda i, j: (0, 0)),
                  pl.BlockSpec((1, q, tn), lambda i, j: (i, 0, j))],
        out_specs=pl.BlockSpec((1, pp, tn), lambda i, j: (i, 0, j)),
        out_shape=jax.ShapeDtypeStruct((bsz, pp, n), out_dtype),
        compiler_params=_cparams("parallel", "parallel"),
        name="dft_outer",
    )(f.astype(BF16), x)


def _dft_inner_fwd_body(a_ref, m_ref, o_ref):
    n2 = a_ref.shape[3]
    a = jnp.concatenate([a_ref[0, 0, 0], a_ref[0, 1, 0]], axis=0).astype(BF16)
    x = jnp.dot(m_ref[0], a, preferred_element_type=F32)
    o_ref[0, 0, 0] = x[:n2]
    o_ref[0, 1, 0] = x[n2:]


def _dft_inner_conv_body(a_ref, g_ref, m_ref, mi_ref, o_ref):
    n2 = a_ref.shape[3]
    a = jnp.concatenate([a_ref[0, 0, 0], a_ref[0, 1, 0]], axis=0).astype(BF16)
    x = jnp.dot(m_ref[0], a, preferred_element_type=F32)
    xre, xim = x[:n2], x[n2:]
    gre, gim = g_ref[0, 0, 0], g_ref[0, 1, 0]
    y = jnp.concatenate([xre * gre - xim * gim, xre * gim + xim * gre], axis=0).astype(BF16)
    o = jnp.dot(mi_ref[0], y, preferred_element_type=F32)
    o_ref[0, 0, 0] = o[:n2].astype(o_ref.dtype)
    o_ref[0, 1, 0] = o[n2:].astype(o_ref.dtype)


def _dft_tables(n1, n2):
    n = n1 * n2
    i1 = jnp.arange(n1, dtype=jnp.int32)
    ph1 = (2.0 * math.pi / n1) * ((i1[:, None] * i1[None, :]) % n1).astype(F32)
    c1, s1 = jnp.cos(ph1), jnp.sin(ph1)
    k1 = i1[:, None, None]
    k2 = jnp.arange(n2, dtype=jnp.int32)[None, :, None]
    m2 = jnp.arange(n2, dtype=jnp.int32)[None, None, :]
    th = (2.0 * math.pi / n) * ((m2 * (k1 + n1 * k2)) % n).astype(F32)
    mre, mim = jnp.cos(th), -jnp.sin(th)
    m_fwd = jnp.concatenate([jnp.concatenate([mre, -mim], axis=2), jnp.concatenate([mim, mre], axis=2)], axis=1)
    qre, qim = jnp.swapaxes(mre, 1, 2), -jnp.swapaxes(mim, 1, 2)
    m_inv = jnp.concatenate([jnp.concatenate([qre, -qim], axis=2), jnp.concatenate([qim, qre], axis=2)], axis=1)
    f_fwd = jnp.concatenate([c1, -s1], axis=0)
    f_inv = jnp.concatenate([c1, -s1], axis=1)
    return f_fwd, f_inv, m_fwd.astype(BF16), m_inv.astype(BF16)


def _two_sided_long_conv(z, filt):
    b, l, d = z.shape
    n = 2 * l
    n1 = max(DFT_MIN_OUTER, 2 ** ((int(math.log2(n)) - 1) // 2))
    n2 = n // n1
    f_fwd, f_inv, m_fwd, m_inv = _dft_tables(n1, n2)
    blk = pl.BlockSpec((1, 2, 1, n2, d), lambda i, j: (i, 0, j, 0, 0))
    gblk = pl.BlockSpec((1, 2, 1, n2, d), lambda i, j: (0, 0, j, 0, 0))
    mat = pl.BlockSpec((1, 2 * n2, 2 * n2), lambda i, j: (j, 0, 0))
    ga = _colmm(f_fwd, filt.reshape(1, n1, n2 * d), BF16).reshape(1, 2, n1, n2, d)
    gspec = pl.pallas_call(
        _dft_inner_fwd_body,
        grid=(1, n1),
        in_specs=[blk, mat],
        out_specs=blk,
        out_shape=jax.ShapeDtypeStruct((1, 2, n1, n2, d), F32),
        compiler_params=_cparams("parallel", "parallel"),
        name="dft_inner_fwd",
    )(ga, m_fwd)
    za = _colmm(f_fwd[:, :n1 // 2], z.reshape(b, n1 // 2, n2 * d), BF16).reshape(b, 2, n1, n2, d)
    bb = pl.pallas_call(
        _dft_inner_conv_body,
        grid=(b, n1),
        in_specs=[blk, gblk, mat, mat],
        out_specs=blk,
        out_shape=jax.ShapeDtypeStruct((b, 2, n1, n2, d), BF16),
        compiler_params=_cparams("parallel", "parallel"),
        name="dft_inner_conv",
    )(za, gspec, m_fwd, m_inv)
    return _colmm(f_inv[:n1 // 2], bb.reshape(b, 2 * n1, n2 * d), F32).reshape(b, l, d)


def _hyena_mixer(h, l_ctx, p):
    b, l, d = h.shape
    u = _linear(h.reshape(b * l, d).astype(BF16), p['w_in'], p['b_in'], out_dtype=BF16, tn=1024)
    u = u.reshape(b, l, 3 * d).astype(F32)
    prev, nxt = _stream_neighbours(u, l_ctx)
    sw = p['short_w']
    u = prev * sw[0] + u * sw[1] + nxt * sw[2] + p['short_b']
    x0, x1, v = jnp.split(u, 3, axis=-1)
    z = v * x1
    conv = jnp.concatenate([_two_sided_long_conv(z[:, :l_ctx], _hyena_filters(l_ctx, p)),
                            _two_sided_long_conv(z[:, l_ctx:], _hyena_filters(l - l_ctx, p))], axis=1)
    z = conv + z * p['bias']
    return _linear((x0 * z).reshape(b * l, d).astype(BF16), p['w_out'], p['b_out']).reshape(b, l, d)


def _attn_body(q_ref, k_ref, v_ref, o_ref, m_ref, acc_ref, *, dv, parts):
    j = pl.program_id(3)
    tq, tk = q_ref.shape[2], k_ref.shape[2]
    sub = tq // parts

    @pl.when(j == 0)
    def _():
        m_ref[...] = jnp.full_like(m_ref, -jnp.inf)
        acc_ref[...] = jnp.zeros_like(acc_ref)

    k = k_ref[0, 0]
    v = v_ref[0]
    slabs = [pl.ds(i * sub, sub) for i in range(parts)]
    s = [lax.dot_general(q_ref[0, 0, x, :], k, (((1,), (1,)), ((), ())), preferred_element_type=F32) for x in slabs]
    m_prev = [m_ref[x, :] for x in slabs]
    m_new = [jnp.maximum(mp, jnp.max(si, axis=-1, keepdims=True)) for mp, si in zip(m_prev, s)]
    pr = [jnp.exp2(si - jnp.concatenate([mn] * (tk // LANES), axis=1)).astype(BF16) for si, mn in zip(s, m_new)]
    alpha = [jnp.exp2(mp - mn) for mp, mn in zip(m_prev, m_new)]
    pv = [jnp.dot(p_, v, preferred_element_type=F32) for p_ in pr]
    for x, a, o, mn in zip(slabs, alpha, pv, m_new):
        acc_ref[x, :] = jnp.concatenate([a] * (acc_ref.shape[1] // LANES), axis=1) * acc_ref[x, :] + o
        m_ref[x, :] = mn

    @pl.when(j == pl.num_programs(3) - 1)
    def _():
        acc = acc_ref[...]
        o_ref[0] = (acc[:, :dv] / acc[:, dv:dv + 1]).astype(o_ref.dtype)


def _attend(q, k, v_ext):
    b, nh, sq, dq = q.shape
    sk = k.shape[2]
    dv = v_ext.shape[2] // (2 * nh)
    tq = _tile(sq, (2048, 1024, 512, 256, 128, 64, 32, 16, 8))
    tk = _tile(sk, (1280, 1024, 640, 512, 256, 128))
    parts = max(1, tq // ATTN_SLAB)
    return pl.pallas_call(
        functools.partial(_attn_body, dv=dv, parts=parts),
        grid=(b, nh, sq // tq, sk // tk),
        in_specs=[pl.BlockSpec((1, 1, tq, dq), lambda bi, hi, i, j: (bi, hi, i, 0)),
                  pl.BlockSpec((1, 1, tk, dq), lambda bi, hi, i, j: (bi, hi, j, 0)),
                  pl.BlockSpec((1, tk, 2 * dv), lambda bi, hi, i, j: (bi, j, hi))],
        out_specs=pl.BlockSpec((1, tq, dv), lambda bi, hi, i, j: (bi, i, hi)),
        out_shape=jax.ShapeDtypeStruct((b, sq, nh * dv), BF16),
        scratch_shapes=[pltpu.VMEM((tq, LANES), F32), pltpu.VMEM((tq, 2 * dv), F32)],
        compiler_params=_cparams("parallel", "parallel", "parallel", "arbitrary"),
        name="attention",
    )(q, k, v_ext)


def _mla_prep_body(qkv_ref, wq_ref, wkv_ref, gq_ref, gkv_ref, gqn_ref, gkn_ref, cos_ref, sin_ref, q_out, k_out, v_out):
    x = qkv_ref[0]
    qc, kvc, kpe = x[:, :Q_LORA], x[:, Q_LORA:Q_LORA + KV_LORA], x[:, Q_LORA + KV_LORA:]
    qn = qc * lax.rsqrt(jnp.mean(qc * qc, axis=-1, keepdims=True) + NORM_EPS) * gq_ref[...]
    kvn = kvc * lax.rsqrt(jnp.mean(kvc * kvc, axis=-1, keepdims=True) + NORM_EPS) * gkv_ref[...]
    q = jnp.dot(qn.astype(BF16), wq_ref[...], preferred_element_type=F32)
    kv = jnp.dot(kvn.astype(BF16), wkv_ref[...], preferred_element_type=F32)
    cos, sin = cos_ref[0], sin_ref[0]
    quarter = QK_ROPE // 4

    def rope(t):
        swapped = jnp.concatenate([t[:, quarter:2 * quarter], t[:, :quarter],
                                   t[:, 3 * quarter:], t[:, 2 * quarter:3 * quarter]], axis=-1)
        return t * cos + swapped * sin

    gqn, gkn = gqn_ref[...], gkn_ref[...]
    kpe_rot = rope(kpe * gkn[:, QK_NOPE:])
    kpe_ss = jnp.sum(kpe * kpe, axis=-1, keepdims=True)
    ones_col = jnp.where(lax.broadcasted_iota(jnp.int32, (x.shape[0], V_HEAD), 1) == 0, 1.0, 0.0).astype(BF16)
    for h in range(MLA_HEADS):
        qh = q[:, h * MLA_SLOT:h * MLA_SLOT + QK_HEAD]
        qh = qh * lax.rsqrt(jnp.sum(qh * qh, axis=-1, keepdims=True) * (1.0 / QK_HEAD) + NORM_EPS) * gqn
        qh = jnp.concatenate([qh[:, :QK_NOPE], rope(qh[:, QK_NOPE:])], axis=-1)
        q_out[0, h] = (qh * (QK_HEAD ** -0.5 * LOG2_E)).astype(q_out.dtype)
        kn = kv[:, h * MLA_SLOT:h * MLA_SLOT + QK_NOPE]
        rstd = lax.rsqrt((jnp.sum(kn * kn, axis=-1, keepdims=True) + kpe_ss) * (1.0 / QK_HEAD) + NORM_EPS)
        k_out[0, h] = jnp.concatenate([kn * rstd * gkn[:, :QK_NOPE], kpe_rot * rstd], axis=-1).astype(k_out.dtype)
        v_out[0, :, h * MLA_SLOT:h * MLA_SLOT + V_HEAD] = kv[:, h * MLA_SLOT + QK_NOPE:(h + 1) * MLA_SLOT].astype(v_out.dtype)
        v_out[0, :, h * MLA_SLOT + V_HEAD:(h + 1) * MLA_SLOT] = ones_col


def _mla_prep(qkv, p, cos, sin):
    b, l, w = qkv.shape
    tm = _tile(l, (256, 128, 64, 32, 16))
    nh = MLA_HEADS
    pad_heads = lambda m, width: jnp.pad(m.reshape(m.shape[0], nh, width), ((0, 0), (0, 0), (0, MLA_SLOT - width))
                                         ).reshape(m.shape[0], nh * MLA_SLOT)
    wq = pad_heads(p['w_qb'], QK_HEAD).astype(BF16)
    wkv = p['w_kvb'].astype(BF16)
    full = lambda a: pl.BlockSpec(a.shape, lambda i, j: (0,) * a.ndim)
    args = [wq, wkv, p['q_norm_g'].reshape(1, Q_LORA), p['kv_norm_g'].reshape(1, KV_LORA),
            p['qn_g'].reshape(1, QK_HEAD), p['kn_g'].reshape(1, QK_HEAD)]
    tab = pl.BlockSpec((1, tm, QK_ROPE), lambda i, j: (0, j, 0))
    head = pl.BlockSpec((1, nh, tm, QK_HEAD), lambda i, j: (i, 0, j, 0))
    return pl.pallas_call(
        _mla_prep_body,
        grid=(b, l // tm),
        in_specs=[pl.BlockSpec((1, tm, w), lambda i, j: (i, j, 0))] + [full(a) for a in args] + [tab, tab],
        out_specs=[head, head, pl.BlockSpec((1, tm, nh * MLA_SLOT), lambda i, j: (i, j, 0))],
        out_shape=[jax.ShapeDtypeStruct((b, nh, l, QK_HEAD), BF16), jax.ShapeDtypeStruct((b, nh, l, QK_HEAD), BF16),
                   jax.ShapeDtypeStruct((b, l, nh * MLA_SLOT), BF16)],
        compiler_params=_cparams("parallel", "parallel"),
        name="mla_prep",
    )(qkv, *args, cos, sin)


def _rope_tables(row, col, l_ctx):
    half = QK_ROPE // 2
    inv = ROPE_THETA ** (-jnp.arange(0, half, 2, dtype=F32) / half)
    ang_r = row.astype(F32)[:, None] * inv[None, :]
    ang_c = col.astype(F32)[:, None] * inv[None, :]
    cos = jnp.concatenate([jnp.cos(ang_r)] * 2 + [jnp.cos(ang_c)] * 2, axis=-1)
    sin = jnp.concatenate([-jnp.sin(ang_r), jnp.sin(ang_r), -jnp.sin(ang_c), jnp.sin(ang_c)], axis=-1)
    cos = jnp.concatenate([jnp.ones((l_ctx, QK_ROPE), F32), cos], axis=0)
    sin = jnp.concatenate([jnp.zeros((l_ctx, QK_ROPE), F32), sin], axis=0)
    return cos[None], sin[None]


def _mla_mixer(h, l_ctx, p, rope_pos):
    b, l, d = h.shape
    qkv = _linear(h.reshape(b * l, d).astype(BF16), p['w_a']).reshape(b, l, -1)
    cos, sin = _rope_tables(rope_pos[0], rope_pos[1], l_ctx)
    q, k, v_ext = _mla_prep(qkv, p, cos, sin)
    o = jnp.concatenate([_attend(q[:, :, :l_ctx], k[:, :, :l_ctx], v_ext[:, :l_ctx]),
                         _attend(q[:, :, l_ctx:], k, v_ext)], axis=1)
    return _linear(o.reshape(b * l, MLA_HEADS * V_HEAD), p['w_o']).reshape(b, l, d)


def _router_body(h_ref, w_ref, b_ref, idx_ref, gate_ref, rank_ref, cnt_ref, run_ref):
    @pl.when(pl.program_id(0) == 0)
    def _():
        run_ref[...] = jnp.zeros_like(run_ref)

    lg = jnp.dot(h_ref[...], w_ref[...], precision=lax.Precision.HIGHEST, preferred_element_type=F32) + b_ref[...]
    tm = lg.shape[0]
    lane = lax.broadcasted_iota(jnp.int32, lg.shape, 1).astype(F32)
    idx_acc = jnp.zeros(lg.shape, F32)
    val_acc = jnp.zeros(lg.shape, F32)
    picked = jnp.zeros(lg.shape, F32)
    sels = []
    top = None
    den = None
    for j in range(TOP_K):
        mx = jnp.max(lg, axis=-1, keepdims=True)
        sel = jnp.min(jnp.where(lg == mx, lane, float(ROUTER_LANES)), axis=-1, keepdims=True)
        if j == 0:
            top = mx
        e = jnp.exp(mx - top)
        den = e if den is None else den + e
        hit = lane == sel
        sels.append(hit)
        idx_acc = jnp.where(lane == float(j), sel, idx_acc)
        val_acc = jnp.where(lane == float(j), e, val_acc)
        picked = jnp.where(hit, 1.0, picked)
        lg = jnp.where(hit, -jnp.inf, lg)
    idx_ref[...] = idx_acc.astype(jnp.int32)
    gate_ref[...] = val_acc / den
    rs = lax.broadcasted_iota(jnp.int32, (tm, tm), 0)
    cs = lax.broadcasted_iota(jnp.int32, (tm, tm), 1)
    before = jnp.where(rs > cs, 1.0, 0.0).astype(BF16)
    base = run_ref[...] + jnp.dot(before, picked.astype(BF16), preferred_element_type=F32)
    rank_acc = jnp.zeros(lg.shape, F32)
    for j in range(TOP_K):
        rk = jnp.sum(jnp.where(sels[j], base, 0.0), axis=-1, keepdims=True)
        rank_acc = jnp.where(lane == float(j), rk, rank_acc)
    rank_ref[...] = rank_acc.astype(jnp.int32)
    run_ref[...] = run_ref[...] + jnp.sum(picked, axis=0, keepdims=True)
    cnt_ref[...] = run_ref[...].astype(jnp.int32)


def _router(h, router_w, router_b):
    t, d = h.shape
    tm = _tile(t, (512, 256, 128, 64, 32, 16, 8))
    w = jnp.concatenate([router_w, jnp.zeros((d, ROUTER_LANES - N_EXPERTS), F32)], axis=1)
    bias = jnp.concatenate([router_b, jnp.full((ROUTER_LANES - N_EXPERTS,), -1e30, F32)]).reshape(1, ROUTER_LANES)
    tok = pl.BlockSpec((tm, ROUTER_LANES), lambda i: (i, 0))
    return pl.pallas_call(
        _router_body,
        grid=(t // tm,),
        in_specs=[pl.BlockSpec((tm, d), lambda i: (i, 0)),
                  pl.BlockSpec((d, ROUTER_LANES), lambda i: (0, 0)),
                  pl.BlockSpec((1, ROUTER_LANES), lambda i: (0, 0))],
        out_specs=[tok, tok, tok, pl.BlockSpec((1, ROUTER_LANES), lambda i: (0, 0))],
        out_shape=[jax.ShapeDtypeStruct((t, ROUTER_LANES), jnp.int32), jax.ShapeDtypeStruct((t, ROUTER_LANES), F32),
                   jax.ShapeDtypeStruct((t, ROUTER_LANES), jnp.int32), jax.ShapeDtypeStruct((1, ROUTER_LANES), jnp.int32)],
        scratch_shapes=[pltpu.VMEM((1, ROUTER_LANES), F32)],
        compiler_params=_cparams("arbitrary"),
        name="router",
    )(h, w, bias)


def _expert_body(be_ref, nb_ref, x_ref, win_ref, bin_ref, wout_ref, bout_ref, gate_ref, o_ref, win_bf, wout_bf):
    i = pl.program_id(0)
    fresh = jnp.logical_or(i == 0, be_ref[i] != be_ref[jnp.maximum(i - 1, 0)])

    @pl.when(fresh)
    def _():
        win_bf[...] = win_ref[0].astype(BF16)
        wout_bf[...] = wout_ref[0].astype(BF16)

    @pl.when(i < nb_ref[0])
    def _():
        sub = MOE_ROWS // MOE_SLABS
        slabs = [pl.ds(j * sub, sub) for j in range(MOE_SLABS)]
        gu = [jnp.dot(x_ref[x, :], win_bf[...], preferred_element_type=F32) + bin_ref[0] for x in slabs]
        glu = [jnp.minimum(g[:, :D_EXPERT], SWIGLU_LIMIT) for g in gu]
        lin = [jnp.clip(g[:, D_EXPERT:], -SWIGLU_LIMIT, SWIGLU_LIMIT) for g in gu]
        act = [(a * jax.nn.sigmoid(SWIGLU_ALPHA * a) * (b + 1.0)).astype(BF16) for a, b in zip(glu, lin)]
        y = [jnp.dot(a, wout_bf[...], preferred_element_type=F32) + bout_ref[0] for a in act]
        for x, yy in zip(slabs, y):
            o_ref[x, :] = (yy * gate_ref[x, :]).astype(o_ref.dtype)

    @pl.when(i >= nb_ref[0])
    def _():
        o_ref[...] = jnp.zeros_like(o_ref)


def _moe(h, h_bf, router_w, router_b, w_in, b_in, w_out, b_out):
    t, d = h.shape
    n_assign = t * TOP_K
    idx, gates, ranks, counts = _router(h, router_w, router_b)
    flat_e = idx[:, :TOP_K].reshape(n_assign)
    flat_g = gates[:, :TOP_K].reshape(n_assign)
    rank = ranks[:, :TOP_K].reshape(n_assign)
    counts = counts[0, :N_EXPERTS]
    padded = (counts + MOE_ROWS - 1) // MOE_ROWS * MOE_ROWS
    pad_end = jnp.cumsum(padded)
    pad_start = pad_end - padded
    experts = jnp.arange(N_EXPERTS, dtype=jnp.int32)
    lookup = lambda table, e: jnp.sum(jnp.where(e[:, None] == experts[None, :], table[None, :], 0), axis=1)
    dest = lookup(pad_start, flat_e) + rank
    n_blocks = -(-n_assign // MOE_ROWS) + N_EXPERTS
    n_rows = n_blocks * MOE_ROWS
    block_start = jnp.arange(n_blocks, dtype=jnp.int32) * MOE_ROWS
    block_e = jnp.minimum(jnp.sum((pad_end[None, :] <= block_start[:, None]).astype(jnp.int32), axis=1), N_EXPERTS - 1)
    order = jnp.argsort(flat_e, stable=True).astype(jnp.int32)
    start = jnp.cumsum(counts) - counts
    in_block = jnp.arange(MOE_ROWS, dtype=jnp.int32)[None, :]
    row_rank = (block_start - lookup(pad_start, block_e))[:, None] + in_block
    row_live = (row_rank < lookup(counts, block_e)[:, None]).reshape(n_rows)
    row_src = order[jnp.minimum(lookup(start, block_e)[:, None] + row_rank, n_assign - 1).reshape(n_rows)]
    row_tok = row_src // TOP_K
    row_gate = jnp.where(row_live, flat_g[row_src], 0.0)
    n_used = (pad_end[-1] // MOE_ROWS).astype(jnp.int32).reshape(1)
    xb = h_bf[row_tok]
    f2 = 2 * D_EXPERT
    grid_spec = pltpu.PrefetchScalarGridSpec(
        num_scalar_prefetch=2,
        grid=(n_blocks,),
        in_specs=[pl.BlockSpec((MOE_ROWS, d), lambda i, be, nb: (i, 0)),
                  pl.BlockSpec((1, d, f2), lambda i, be, nb: (be[i], 0, 0)),
                  pl.BlockSpec((1, 1, f2), lambda i, be, nb: (be[i], 0, 0)),
                  pl.BlockSpec((1, D_EXPERT, d), lambda i, be, nb: (be[i], 0, 0)),
                  pl.BlockSpec((1, 1, d), lambda i, be, nb: (be[i], 0, 0)),
                  pl.BlockSpec((MOE_ROWS, 1), lambda i, be, nb: (i, 0))],
        out_specs=pl.BlockSpec((MOE_ROWS, d), lambda i, be, nb: (i, 0)),
        scratch_shapes=[pltpu.VMEM((d, f2), BF16), pltpu.VMEM((D_EXPERT, d), BF16)],
    )
    yb = pl.pallas_call(
        _expert_body,
        grid_spec=grid_spec,
        out_shape=jax.ShapeDtypeStruct((n_rows, d), BF16),
        compiler_params=_cparams("arbitrary"),
        name="moe_experts",
    )(block_e, n_used, xb, w_in, b_in.reshape(N_EXPERTS, 1, f2),
      w_out, b_out.reshape(N_EXPERTS, 1, d), row_gate.reshape(n_rows, 1))
    dest = dest.reshape(t, TOP_K)
    return [yb[dest[:, j]] for j in range(TOP_K)]


def kernel(x, c, ctx, c_ctx, ada_w, ada_b, norm_mix_g, norm_ffn_g, rwkv_mu, rwkv_w_rkv, rwkv_w0, rwkv_w1, rwkv_w2, rwkv_a0, rwkv_a1, rwkv_a2, rwkv_g1, rwkv_g2, rwkv_k_k, rwkv_k_a, rwkv_r_k, rwkv_ln_g, rwkv_ln_b, rwkv_w_o, rwkv_v0, rwkv_v1, rwkv_v2, hy_w_in, hy_b_in, hy_short_w, hy_short_b, hy_f_w1, hy_f_b1, hy_f_w2, hy_f_b2, hy_f_w3, hy_f_b3, hy_f_freq, hy_f_wout, hy_bias, hy_w_out, hy_b_out, mla_w_a, mla_q_norm_g, mla_kv_norm_g, mla_w_qb, mla_w_kvb, mla_qn_g, mla_kn_g, mla_w_o, moe_router_w, moe_router_b, moe_w_in, moe_b_in, moe_w_out, moe_b_out):
    bsz, s_len, d = x.shape
    l_ctx = ctx.shape[1]
    rows = s_len // GRID_W
    row = jnp.repeat(jnp.arange(rows, dtype=jnp.int32), GRID_W)
    col = jnp.tile(jnp.arange(GRID_W, dtype=jnp.int32), rows)
    silu = jnp.concatenate([jax.nn.silu(c), jax.nn.silu(c_ctx)[None, :]], axis=0)
    silu = jnp.concatenate([silu, jnp.zeros((-(bsz + 1) % 8, d), F32)], axis=0)
    is_ctx = (jnp.arange(l_ctx + s_len, dtype=jnp.int32) < l_ctx)[None, :, None]
    per_row = lambda m: jnp.where(is_ctx, m[:, 0:1], m[:, 1:2])
    xs = jnp.concatenate([ctx, x], axis=1)
    mods = []
    for i in range(DEPTH):
        mod = _linear(silu, ada_w[i], ada_b[i], tn=1024)
        mod = jnp.stack([jnp.broadcast_to(mod[bsz:bsz + 1], (bsz, N_MOD * d)), mod[:bsz]], axis=1)
        mods.append(jnp.split(mod, N_MOD, axis=-1))
    v_first = None
    h = _norm_mod(xs, norm_mix_g[0], mods[0][1], mods[0][0], l_ctx, BF16)
    for i in range(DEPTH):
        j = i // N_MIXERS
        kind = i % N_MIXERS
        sh1, sc1, g1, sh2, sc2, g2 = mods[i]
        if kind == 0:
            p = {'mu': rwkv_mu[j], 'w_rkv': rwkv_w_rkv[j], 'w0': rwkv_w0[j], 'w1': rwkv_w1[j],
                 'w2': rwkv_w2[j], 'a0': rwkv_a0[j], 'a1': rwkv_a1[j], 'a2': rwkv_a2[j],
                 'g1': rwkv_g1[j], 'g2': rwkv_g2[j], 'k_k': rwkv_k_k[j], 'k_a': rwkv_k_a[j],
                 'r_k': rwkv_r_k[j], 'ln_g': rwkv_ln_g[j], 'ln_b': rwkv_ln_b[j], 'w_o': rwkv_w_o[j]}
            vres = None if j == 0 else (rwkv_v0[j - 1], rwkv_v1[j - 1], rwkv_v2[j - 1])
            y, v_cur = _rwkv_mixer(h, l_ctx, p, v_first, vres)
            if j == 0:
                v_first = v_cur
        elif kind == 1:
            p = {'w_in': hy_w_in[j], 'b_in': hy_b_in[j], 'short_w': hy_short_w[j], 'short_b': hy_short_b[j],
                 'f_w1': hy_f_w1[j], 'f_b1': hy_f_b1[j], 'f_w2': hy_f_w2[j], 'f_b2': hy_f_b2[j],
                 'f_w3': hy_f_w3[j], 'f_b3': hy_f_b3[j], 'f_freq': hy_f_freq[j], 'f_wout': hy_f_wout[j],
                 'bias': hy_bias[j], 'w_out': hy_w_out[j], 'b_out': hy_b_out[j]}
            y = _hyena_mixer(h, l_ctx, p)
        else:
            p = {'w_a': mla_w_a[j], 'q_norm_g': mla_q_norm_g[j], 'kv_norm_g': mla_kv_norm_g[j],
                 'w_qb': mla_w_qb[j], 'w_kvb': mla_w_kvb[j], 'qn_g': mla_qn_g[j], 'kn_g': mla_kn_g[j],
                 'w_o': mla_w_o[j]}
            y = _mla_mixer(h, l_ctx, p, (row, col))
        xs, h2, h2_bf = _residual_norm(xs, [y], g1, norm_ffn_g[i], sc2, sh2, l_ctx, (F32, BF16))
        moe_p = (moe_router_w[i], moe_router_b[i], moe_w_in[i], moe_b_in[i], moe_w_out[i], moe_b_out[i])
        parts = [f.reshape(xs.shape) for f in _moe(h2.reshape(-1, d), h2_bf.reshape(-1, d), *moe_p)]
        if i + 1 < DEPTH:
            xs, h = _residual_norm(xs, parts, g2, norm_mix_g[i + 1], mods[i + 1][1], mods[i + 1][0], l_ctx, (BF16,))
        else:
            xs = xs + per_row(g2) * sum(f.astype(F32) for f in parts)
    return xs[:, l_ctx:]
```

```python
import functools
import math

import jax
import jax.numpy as jnp
import numpy as np
from jax import lax
from jax.experimental import pallas as pl
from jax.experimental.pallas import tpu as pltpu

F32 = jnp.float32
BF16 = jnp.bfloat16

D_MODEL = 1024
DEPTH = 4
GRID_W = 64
N_MIXERS = 3
N_MOD = 6
NORM_EPS = 1e-6
RWKV_HEAD = 64
RWKV_HEADS = D_MODEL // RWKV_HEAD
GN_EPS = 64e-5
WKV_CHUNK = 64
WKV_GROUP = 4
FILTER_EMB = 33
FILTER_EMB_PAD = 128
DFT_MIN_OUTER = 32
FAST_DECAY = math.log(1e-2) / 0.3
SLOW_DECAY = math.log(1e-2) / 1.5
MLA_HEADS = 8
QK_NOPE = 128
QK_ROPE = 64
QK_HEAD = QK_NOPE + QK_ROPE
V_HEAD = 128
Q_LORA = 384
KV_LORA = 256
ROPE_THETA = 10000.0
N_EXPERTS = 32
TOP_K = 4
D_EXPERT = 1024
SWIGLU_LIMIT = 7.0
SWIGLU_ALPHA = 1.702
MOE_ROWS = 256
MOE_SLABS = 2
LANES = 128
ROUTER_LANES = LANES
ATTN_SLAB = 512
LOG2_E = 1.4426950408889634
MLA_SLOT = 256

V7X_VMEM_BYTES = 64 * 1024 * 1024
VMEM_LIMIT = V7X_VMEM_BYTES * 3 // 4


def _cparams(*sem):
    return pltpu.CompilerParams(dimension_semantics=sem, vmem_limit_bytes=VMEM_LIMIT)


def _tile(n, prefs):
    for t in prefs:
        if n % t == 0:
            return t
    return n


def _linear_body(a_ref, w_ref, b_ref, o_ref, *, in_act):
    a = a_ref[...]
    if in_act == 'tanh':
        a = jnp.tanh(a.astype(F32))
    elif in_act == 'sigmoid':
        a = jax.nn.sigmoid(a.astype(F32))
    acc = jnp.dot(a.astype(BF16), w_ref[...], preferred_element_type=F32)
    o_ref[...] = (acc + b_ref[...]).astype(o_ref.dtype)


def _linear(a, w, b=None, *, in_act=None, out_dtype=F32, tn=None):
    m, k = a.shape
    n = w.shape[1]
    tm = _tile(m, (512, 256, 128, 64, 32, 16, 8))
    tn = n if tn is None else tn
    assert n % tn == 0
    if b is None:
        b = jnp.zeros((n,), F32)
    return pl.pallas_call(
        functools.partial(_linear_body, in_act=in_act),
        grid=(n // tn, m // tm),
        in_specs=[pl.BlockSpec((tm, k), lambda j, i: (i, 0)),
                  pl.BlockSpec((k, tn), lambda j, i: (0, j)),
                  pl.BlockSpec((1, tn), lambda j, i: (0, j))],
        out_specs=pl.BlockSpec((tm, tn), lambda j, i: (i, j)),
        out_shape=jax.ShapeDtypeStruct((m, n), out_dtype),
        compiler_params=_cparams("parallel", "parallel"),
        name="linear",
    )(a, w.astype(BF16), b.reshape(1, n).astype(F32))


def _norm_mod_body(x_ref, g_ref, sc_ref, sh_ref, o_ref):
    x = x_ref[0]
    y = x * lax.rsqrt(jnp.mean(x * x, axis=-1, keepdims=True) + NORM_EPS)
    o_ref[0] = ((y * g_ref[...]) * (1.0 + sc_ref[0, 0]) + sh_ref[0, 0]).astype(o_ref.dtype)


def _norm_mod(x, g, sc, sh, l_ctx, out_dtype):
    b, l, d = x.shape
    tl = _tile(math.gcd(l_ctx, l - l_ctx), (512, 256, 128, 64, 32, 16, 8))
    stream = lambda i, j: (i, jnp.where(j * tl < l_ctx, 0, 1), 0, 0)
    return pl.pallas_call(
        _norm_mod_body,
        grid=(b, l // tl),
        in_specs=[pl.BlockSpec((1, tl, d), lambda i, j: (i, j, 0)),
                  pl.BlockSpec((1, d), lambda i, j: (0, 0)),
                  pl.BlockSpec((1, 1, 1, d), stream),
                  pl.BlockSpec((1, 1, 1, d), stream)],
        out_specs=pl.BlockSpec((1, tl, d), lambda i, j: (i, j, 0)),
        out_shape=jax.ShapeDtypeStruct((b, l, d), out_dtype),
        compiler_params=_cparams("parallel", "parallel"),
        name="norm_mod",
    )(x, g.reshape(1, d), sc.reshape(b, 2, 1, d), sh.reshape(b, 2, 1, d))


def _residual_norm_body(*refs, n_add):
    x_ref, adds, (gate_ref, g_ref, sc_ref, sh_ref, xo_ref), h_refs = refs[0], refs[1:1 + n_add], refs[1 + n_add:6 + n_add], refs[6 + n_add:]
    y = adds[0][0].astype(F32)
    for a in adds[1:]:
        y = y + a[0].astype(F32)
    x = x_ref[0] + gate_ref[0, 0] * y
    xo_ref[0] = x
    nrm = x * lax.rsqrt(jnp.mean(x * x, axis=-1, keepdims=True) + NORM_EPS)
    h = (nrm * g_ref[...]) * (1.0 + sc_ref[0, 0]) + sh_ref[0, 0]
    for h_ref in h_refs:
        h_ref[0] = h.astype(h_ref.dtype)


def _residual_norm(x, adds, gate, g, sc, sh, l_ctx, h_dtypes):
    b, l, d = x.shape
    tl = _tile(math.gcd(l_ctx, l - l_ctx), (512, 256, 128, 64, 32, 16, 8))
    stream = pl.BlockSpec((1, 1, 1, d), lambda i, j: (i, jnp.where(j * tl < l_ctx, 0, 1), 0, 0))
    rows = pl.BlockSpec((1, tl, d), lambda i, j: (i, j, 0))
    return pl.pallas_call(
        functools.partial(_residual_norm_body, n_add=len(adds)),
        grid=(b, l // tl),
        in_specs=[rows] * (1 + len(adds)) + [stream, pl.BlockSpec((1, d), lambda i, j: (0, 0)), stream, stream],
        out_specs=[rows] * (1 + len(h_dtypes)),
        out_shape=[jax.ShapeDtypeStruct((b, l, d), F32)] + [jax.ShapeDtypeStruct((b, l, d), t) for t in h_dtypes],
        compiler_params=_cparams("parallel", "parallel"),
        name="residual_norm",
    )(x, *adds, gate.reshape(b, 2, 1, d), g.reshape(1, d), sc.reshape(b, 2, 1, d), sh.reshape(b, 2, 1, d))


def _stream_neighbours(t, l_ctx):
    zero = jnp.zeros_like(t[:, :1])
    pos = jnp.arange(t.shape[1], dtype=jnp.int32)[None, :, None]
    prev = jnp.where(pos == l_ctx, 0.0, jnp.concatenate([zero, t[:, :-1]], axis=1))
    nxt = jnp.where(pos == l_ctx - 1, 0.0, jnp.concatenate([t[:, 1:], zero], axis=1))
    return prev, nxt


PROJ_R, PROJ_K, PROJ_V = 0, 1024, 2048
PROJ_W1 = (3072, 3200)
PROJ_A1 = (3328, 3456)
PROJ_G1 = 3584
PROJ_V1 = 3840
PROJ_WIDTH = 4096
LORA_PAD = 128
GATE_PAD = 256


def _sigmoid(x):
    return jax.nn.sigmoid(x)


def _rwkv_post_body(*refs, has_vres):
    if has_vres:
        (proj_ref, vf_ref, w2_ref, a2_ref, g2_ref, v2_ref, vec_ref, gs_ref,
         kk_out, v_out, gate_out, lw_out, ag_out, kd_out) = refs
    else:
        (proj_ref, w2_ref, a2_ref, g2_ref, vec_ref, gs_ref,
         kk_out, v_out, gate_out, lw_out, ag_out, kd_out) = refs
    d = D_MODEL
    vec = vec_ref[...]
    k = proj_ref[:, PROJ_K:PROJ_K + d].astype(F32)
    v = proj_ref[:, PROJ_V:PROJ_V + d].astype(F32)
    if has_vres:
        v1 = proj_ref[:, PROJ_V1:PROJ_V1 + LORA_PAD]
        vgate = _sigmoid(vec[6:7] + jnp.dot(v1, v2_ref[...], preferred_element_type=F32))
        v = v + (vf_ref[...].astype(F32) - v) * vgate
    v_out[...] = v.astype(v_out.dtype)
    kraw = k * vec[4:5]
    ss = jnp.dot((kraw * kraw).astype(BF16), gs_ref[...], preferred_element_type=F32)
    kk_out[...] = (kraw / jnp.maximum(jnp.sqrt(ss), 1e-12)).astype(kk_out.dtype)
    g1 = proj_ref[:, PROJ_G1:PROJ_G1 + GATE_PAD].astype(F32)
    gate_out[...] = jnp.dot(_sigmoid(g1).astype(BF16), g2_ref[...], preferred_element_type=F32).astype(gate_out.dtype)
    for dd in range(2):
        w1 = proj_ref[:, PROJ_W1[dd]:PROJ_W1[dd] + LORA_PAD].astype(F32)
        z = vec[dd:dd + 1] + jnp.dot(jnp.tanh(w1).astype(BF16), w2_ref[dd], preferred_element_type=F32)
        softplus = jnp.maximum(-z, 0.0) + jnp.log(1.0 + jnp.exp(-jnp.abs(z)))
        lw_out[dd] = -jnp.exp(-softplus - 0.5)
        a1 = proj_ref[:, PROJ_A1[dd]:PROJ_A1[dd] + LORA_PAD]
        ag = _sigmoid(vec[2 + dd:3 + dd] + jnp.dot(a1, a2_ref[dd], preferred_element_type=F32))
        ag_out[dd] = ag.astype(ag_out.dtype)
        kd_out[dd] = (k * (1.0 + (ag - 1.0) * vec[5:6])).astype(kd_out.dtype)


def _rwkv_post(proj, v_first, w2p, a2p, g2p, v2p, vec, gsum):
    t, d = proj.shape[0], D_MODEL
    tm = _tile(t, (256, 128, 64, 32, 16))
    has_vres = v_first is not None
    full = lambda a: pl.BlockSpec(a.shape, lambda i: (0,) * a.ndim)
    row = pl.BlockSpec((tm, d), lambda i: (i, 0))
    row2 = pl.BlockSpec((2, tm, d), lambda i: (0, i, 0))
    ins = [proj] + ([v_first] if has_vres else []) + [w2p, a2p, g2p] + ([v2p] if has_vres else []) + [vec, gsum]
    in_specs = ([pl.BlockSpec((tm, PROJ_WIDTH), lambda i: (i, 0))] + ([row] if has_vres else [])
                + [full(a) for a in ins[(2 if has_vres else 1):]])
    return pl.pallas_call(
        functools.partial(_rwkv_post_body, has_vres=has_vres),
        grid=(t // tm,),
        in_specs=in_specs,
        out_specs=[row, row, row, row2, row2, row2],
        out_shape=[jax.ShapeDtypeStruct((t, d), BF16),
                   jax.ShapeDtypeStruct((t, d), BF16),
                   jax.ShapeDtypeStruct((t, d), BF16),
                   jax.ShapeDtypeStruct((2, t, d), F32),
                   jax.ShapeDtypeStruct((2, t, d), BF16),
                   jax.ShapeDtypeStruct((2, t, d), BF16)],
        compiler_params=_cparams("parallel"),
        name="rwkv_post",
    )(*ins)


def _head_groups(y):
    gw = WKV_GROUP * RWKV_HEAD
    lane_head = jnp.right_shift(lax.broadcasted_iota(jnp.int32, (y.shape[0], gw), 1), 6)
    out = []
    for g in range(D_MODEL // gw):
        yg = y[:, g * gw:(g + 1) * gw].astype(BF16)
        out.append(jnp.concatenate([jnp.where(lane_head == h, yg, jnp.zeros_like(yg)) for h in range(WKV_GROUP)], axis=0))
    return out


def _gmm(x, ybd):
    gw = WKV_GROUP * RWKV_HEAD
    return jnp.concatenate([jnp.dot(x[:, g * gw:(g + 1) * gw].astype(BF16), yb, preferred_element_type=F32)
                            for g, yb in enumerate(ybd)], axis=-1)


def _gmm_nt(x, ybd):
    gw = WKV_GROUP * RWKV_HEAD
    return jnp.concatenate([lax.dot_general(x[:, g * gw:(g + 1) * gw].astype(BF16), yb, (((1,), (1,)), ((), ())),
                                            preferred_element_type=F32)
                            for g, yb in enumerate(ybd)], axis=-1)


def _wkv_chunks(r, kk, v, lw, ag, kd, s0, rev):
    each = lambda f, *ls: [f(*a) for a in zip(*ls)]
    c, d = r[0].shape
    row0 = lax.broadcasted_iota(jnp.int32, (c, d), 0)
    col0 = jnp.bitwise_and(lax.broadcasted_iota(jnp.int32, (c, d), 1), RWKV_HEAD - 1)
    incl = [(row0 <= col0) if rv else (row0 >= col0) for rv in rev]
    strict = [(row0 < col0) if rv else (row0 > col0) for rv in rev]
    rs = lax.broadcasted_iota(jnp.int32, (c, c), 0)
    cs = lax.broadcasted_iota(jnp.int32, (c, c), 1)
    tri = [jnp.where((rs <= cs) if rv else (rs >= cs), 1.0, 0.0).astype(BF16) for rv in rev]
    hi = each(lambda x: x.astype(BF16), lw)
    rem = each(lambda x, h: x - h.astype(F32), lw, hi)
    mid = each(lambda x: x.astype(BF16), rem)
    lo = each(lambda x, m: (x - m.astype(F32)).astype(BF16), rem, mid)
    cum = each(lambda t, a, b_, c_: (jnp.dot(t, a, preferred_element_type=F32) + jnp.dot(t, b_, preferred_element_type=F32)
                                     + jnp.dot(t, c_, preferred_element_type=F32)), tri, hi, mid, lo)
    cum_end = [x[0:1, :] if rv else x[c - 1:c, :] for x, rv in zip(cum, rev)]
    g_rest = each(lambda e, x: jnp.exp(e - x), cum_end, cum)
    g_inv = each(lambda x: jnp.exp(-x), cum)
    bvec = each(lambda a, b_: a * b_, kk, ag)
    rt = each(lambda a, x: a * jnp.exp(x), r, cum)
    kt = each(lambda a, g: a * g, kd, g_inv)
    bt = each(lambda a, g: a * g, bvec, g_inv)
    at = each(lambda a, x, w: -a * jnp.exp(x - w), kk, cum, lw)
    ar = each(lambda a, b_: jnp.concatenate([a, b_], axis=0), at, rt)
    ab = each(lambda a, b_: _gmm_nt(a, _head_groups(b_)), ar, bt)
    ak = each(lambda a, b_: _gmm_nt(a, _head_groups(b_)), ar, kt)
    a_ab = each(lambda m, x: jnp.where(m, x[:c], 0.0), strict, ab)
    a_rb = each(lambda m, x: jnp.where(m, x[c:], 0.0), incl, ab)
    a_ak = each(lambda m, x: jnp.where(m, x[:c], 0.0), strict, ak)
    a_rk = each(lambda m, x: jnp.where(m, x[c:], 0.0), incl, ak)
    same16 = jnp.right_shift(row0, 4) == jnp.right_shift(col0, 4)
    same32 = jnp.right_shift(row0, 5) == jnp.right_shift(col0, 5)
    eye = jnp.where(row0 == col0, 1.0, 0.0)
    npow = each(lambda x: jnp.where(same16, x, 0.0), a_ab)
    p = each(lambda x: eye + x, npow)
    for _ in range(3):
        npow = each(lambda x: _gmm(x, _head_groups(x)), npow)
        p = each(lambda x, y: x + _gmm(x, _head_groups(y)), p, npow)
    off32 = jnp.logical_and(same32, jnp.logical_not(same16))
    t1 = each(lambda x, y: _gmm(jnp.where(off32, x, 0.0), _head_groups(y)), a_ab, p)
    p = each(lambda x, y: x + _gmm(x, _head_groups(y)), p, t1)
    t2 = each(lambda x, y: _gmm(jnp.where(same32, 0.0, x), _head_groups(y)), a_ab, p)
    p = each(lambda x, y: x + _gmm(x, _head_groups(y)), p, t2)
    pbd = each(_head_groups, p)
    vbd = each(_head_groups, v)
    wmat = each(lambda x, y: _gmm(x, _head_groups(y)), p, at)
    vt = each(lambda y: _gmm_nt(eye, y), vbd)
    xt = each(lambda x, y: _gmm_nt(x, _head_groups(y)), vt, a_ak)
    upt = each(_gmm_nt, xt, pbd)
    yp = each(_gmm, a_rk, vbd)
    sp = each(lambda x, a, g: _gmm(x, _head_groups(a * g)), vt, kd, g_rest)
    ut = each(lambda s_, w, u: _gmm_nt(s_, _head_groups(w)) + u, s0, wmat, upt)
    ys = each(lambda s_, q, x: _gmm_nt(q, _head_groups(s_)) + x, s0, rt, yp)
    y = each(lambda y0, a, u: y0 + _gmm_nt(a, _head_groups(u)), ys, a_rb, ut)
    s_end = each(lambda s_, e, u, b_, g, x: s_ * jnp.exp(e) + _gmm(u, _head_groups(b_ * g)) + x,
                 s0, cum_end, ut, bvec, g_rest, sp)
    return y, s_end


def _wkv_body(*refs, n_batch):
    ins, outs, s_ref = refs[:12], refs[12:14], refs[14]

    @pl.when(pl.program_id(0) == 0)
    def _():
        s_ref[...] = jnp.zeros_like(s_ref)

    scans = [(dd, bi) for dd in range(2) for bi in range(n_batch)]
    get = lambda k, lead: [ins[6 * dd + k][(0, bi) if lead else bi].astype(F32) for dd, bi in scans]
    y, s_end = _wkv_chunks(get(0, False), get(1, False), get(2, False), get(3, True), get(4, True), get(5, True),
                           [s_ref[i] for i in range(len(scans))], [dd == 1 for dd, _ in scans])
    for i, (dd, bi) in enumerate(scans):
        outs[dd][bi] = y[i]
        s_ref[i] = s_end[i]


def _wkv(proj, kk, v, lw, ag, kd, l_ctx):
    b, l, d = kk.shape
    c = WKV_CHUNK
    nc, nctx = l // c, l_ctx // c
    assert c == RWKV_HEAD and l % c == 0 and l_ctx % c == 0

    def fwd(j):
        return j

    def bwd(j):
        return jnp.where(j < nctx, nctx - 1 - j, nc + nctx - 1 - j)

    in_specs, out_specs = [], []
    for dd, ch in enumerate((fwd, bwd)):
        shared = pl.BlockSpec((b, c, d), lambda j, ch=ch: (0, ch(j), 0))
        per_dir = pl.BlockSpec((1, b, c, d), lambda j, dd=dd, ch=ch: (dd, 0, ch(j), 0))
        in_specs += [shared, shared, shared, per_dir, per_dir, per_dir]
        out_specs.append(shared)
    return pl.pallas_call(
        functools.partial(_wkv_body, n_batch=b),
        grid=(nc,),
        in_specs=in_specs,
        out_specs=out_specs,
        out_shape=[jax.ShapeDtypeStruct((b, l, d), F32)] * 2,
        scratch_shapes=[pltpu.VMEM((2 * b, RWKV_HEAD, d), F32)],
        compiler_params=_cparams("arbitrary"),
        name="wkv",
    )(*([proj, kk, v, lw, ag, kd] * 2))


def _rwkv_readout_body(yf_ref, yr_ref, r_ref, kd_ref, v_ref, gate_ref, gm_ref, gs_ref, vec_ref, wo_ref, o_ref):
    vec = vec_ref[...]
    y = yf_ref[...] + yr_ref[...]
    gm = gm_ref[...]
    yh = y.astype(BF16)
    yl = (y - yh.astype(F32)).astype(BF16)
    mean = jnp.dot(yh, gm, preferred_element_type=F32) + jnp.dot(yl, gm, preferred_element_type=F32)
    yc = y - mean
    var = jnp.dot((yc * yc).astype(BF16), gm, preferred_element_type=F32)
    yn = yc * lax.rsqrt(var + GN_EPS) * vec[0:1] + vec[1:2]
    r = r_ref[...].astype(F32)
    kds = kd_ref[0].astype(F32) + kd_ref[1].astype(F32)
    bonus = jnp.dot((r * kds * vec[2:3]).astype(BF16), gs_ref[...], preferred_element_type=F32)
    out = (yn + bonus * v_ref[...].astype(F32)) * gate_ref[...].astype(F32)
    o_ref[...] = jnp.dot(out.astype(BF16), wo_ref[...], preferred_element_type=F32)


def _rwkv_readout(y_fwd, y_rev, proj, kd, v, gate, gmean, gsum, vec, w_o):
    t, d = v.shape
    tm = _tile(t, (256, 128, 64, 32, 16))
    full = lambda a: pl.BlockSpec(a.shape, lambda i: (0,) * a.ndim)
    row = pl.BlockSpec((tm, d), lambda i: (i, 0))
    row2 = pl.BlockSpec((2, tm, d), lambda i: (0, i, 0))
    return pl.pallas_call(
        _rwkv_readout_body,
        grid=(t // tm,),
        in_specs=[row, row, row, row2, row, row, full(gmean), full(gsum), full(vec), full(w_o)],
        out_specs=row,
        out_shape=jax.ShapeDtypeStruct((t, d), F32),
        compiler_params=_cparams("parallel"),
        name="rwkv_readout",
    )(y_fwd, y_rev, proj, kd, v, gate, gmean, gsum, vec, w_o)


def _pad_cols(w, width):
    return jnp.concatenate([w, jnp.zeros((w.shape[0], width - w.shape[1]), w.dtype)], axis=1)


def _pad_rows(w, height):
    return jnp.concatenate([w, jnp.zeros((height - w.shape[0], w.shape[1]), w.dtype)], axis=0)


def _rwkv_mixer(h, l_ctx, p, v_first, vres):
    b, l, d = h.shape
    t = b * l
    h32 = h.astype(F32)
    prev, nxt = _stream_neighbours(h32, l_ctx)
    xx = 0.5 * (prev + nxt) - h32
    mu = p['mu']
    mats = [(p['w_rkv'][0], 0, d), (p['w_rkv'][1], 2, d), (p['w_rkv'][2], 3, d),
            (p['w1'][0], 1, LORA_PAD), (p['w1'][1], 1, LORA_PAD), (p['a1'][0], 4, LORA_PAD), (p['a1'][1], 4, LORA_PAD),
            (p['g1'], 5, GATE_PAD), (vres[1] if vres is not None else jnp.zeros((d, 1), F32), 3, 2 * LORA_PAD)]
    top = jnp.concatenate([_pad_cols(w, width) for w, _, width in mats], axis=1)
    bot = jnp.concatenate([_pad_cols(mu[m][:, None] * w, width) for w, m, width in mats], axis=1)
    wcat = jnp.concatenate([top, bot], axis=0)
    assert wcat.shape[1] == PROJ_WIDTH
    a_in = jnp.concatenate([h.astype(BF16), xx.astype(BF16)], axis=-1).reshape(t, 2 * d)
    proj = _linear(a_in, wcat, out_dtype=BF16, tn=1024)
    w2p = jnp.stack([_pad_rows(p['w2'][i], LORA_PAD) for i in range(2)]).astype(BF16)
    a2p = jnp.stack([_pad_rows(p['a2'][i], LORA_PAD) for i in range(2)]).astype(BF16)
    g2p = _pad_rows(p['g2'], GATE_PAD).astype(BF16)
    head_of = jnp.arange(d, dtype=jnp.int32) // RWKV_HEAD
    same_head = head_of[:, None] == head_of[None, :]
    gsum = jnp.where(same_head, 1.0, 0.0).astype(BF16)
    gmean = jnp.where(same_head, 1.0 / RWKV_HEAD, 0.0).astype(BF16)
    zero = jnp.zeros((d,), F32)
    vec = jnp.stack([p['w0'][0], p['w0'][1], p['a0'][0], p['a0'][1], p['k_k'], p['k_a'],
                     vres[0] if vres is not None else zero, zero])
    if vres is not None:
        kk, v, gate, lw, ag, kd = _rwkv_post(proj, v_first.reshape(t, d), w2p, a2p, g2p,
                                             _pad_rows(vres[2], LORA_PAD).astype(BF16), vec, gsum)
    else:
        kk, v, gate, lw, ag, kd = _rwkv_post(proj, None, w2p, a2p, g2p, None, vec, gsum)
    sh3 = lambda a: a.reshape(a.shape[:-2] + (b, l, a.shape[-1]))
    y_fwd, y_rev = _wkv(sh3(proj), sh3(kk), sh3(v), sh3(lw), sh3(ag), sh3(kd), l_ctx)
    vec_o = jnp.stack([p['ln_g'], p['ln_b'], p['r_k'].reshape(d)] + [zero] * 5)
    out = _rwkv_readout(y_fwd.reshape(t, d), y_rev.reshape(t, d), proj, kd, v, gate, gmean, gsum, vec_o, p['w_o'].astype(BF16))
    return out.reshape(b, l, d), v.reshape(b, l, d)


def _filter_body(z_ref, t_ref, w1_ref, b1_ref, w2_ref, b2_ref, w3_ref, b3_ref, fr_ref, wo_ref, dl_ref, o_ref, *, length):
    hp = lax.Precision.HIGHEST
    fr = fr_ref[...]
    hdn = jnp.sin(fr[0:1] * (jnp.dot(z_ref[...], w1_ref[...], precision=hp, preferred_element_type=F32) + b1_ref[...]))
    hdn = jnp.sin(fr[1:2] * (jnp.dot(hdn, w2_ref[...], precision=hp, preferred_element_type=F32) + b2_ref[...]))
    hdn = jnp.sin(fr[2:3] * (jnp.dot(hdn, w3_ref[...], precision=hp, preferred_element_type=F32) + b3_ref[...]))
    filt = jnp.dot(hdn, wo_ref[...], precision=hp, preferred_element_type=F32) * jnp.exp(-t_ref[...] * dl_ref[...])
    tl = filt.shape[0]
    n = pl.program_id(0) * tl + lax.broadcasted_iota(jnp.int32, filt.shape, 0)
    o_ref[...] = jnp.where(n == length, 0.0, filt * (0.5 / length))


def _hyena_filters(length, p):
    n = jnp.arange(2 * length, dtype=jnp.int32)
    pos = jnp.where(n < length, n, 2 * length - n).astype(F32)[:, None]
    t = pos / (length - 1)
    bands = (FILTER_EMB - 1) // 2
    f = jnp.linspace(1e-4, bands - 1, bands, dtype=F32)[None, :]
    ang = 2.0 * math.pi * f * pos / length
    z = jnp.concatenate([t, jnp.cos(ang), -jnp.sin(ang), jnp.zeros((2 * length, FILTER_EMB_PAD - FILTER_EMB), F32)], axis=-1)
    w1 = jnp.concatenate([p['f_w1'], jnp.zeros((FILTER_EMB_PAD - FILTER_EMB, p['f_w1'].shape[1]), F32)], axis=0)
    deltas = jnp.abs(jnp.linspace(FAST_DECAY, SLOW_DECAY, D_MODEL, dtype=F32))[None, :]
    fw = p['f_w2'].shape[0]
    tl = _tile(length, (512, 256, 128, 64, 32, 16, 8))
    half = length // tl
    full = lambda a: pl.BlockSpec(a.shape, lambda i: (0,) * a.ndim)
    args = [w1, p['f_b1'].reshape(1, fw), p['f_w2'], p['f_b2'].reshape(1, fw), p['f_w3'], p['f_b3'].reshape(1, fw),
            p['f_freq']]
    return pl.pallas_call(
        functools.partial(_filter_body, length=length),
        grid=(2 * half,),
        in_specs=[pl.BlockSpec((tl, FILTER_EMB_PAD), lambda i: (i, 0)), pl.BlockSpec((tl, 1), lambda i: (i, 0))]
                 + [full(a) for a in args]
                 + [pl.BlockSpec((fw, D_MODEL), lambda i: (0, jnp.where(i < half, 0, 1))),
                    full(deltas)],
        out_specs=pl.BlockSpec((tl, D_MODEL), lambda i: (i, 0)),
        out_shape=jax.ShapeDtypeStruct((2 * length, D_MODEL), F32),
        compiler_params=_cparams("parallel"),
        name="hyena_filter",
    )(z, t, *args, p['f_wout'], deltas)


def _colmm_body(f_ref, x_ref, o_ref):
    o_ref[0] = jnp.dot(f_ref[...], x_ref[0].astype(BF16), preferred_element_type=F32).astype(o_ref.dtype)


def _colmm(f, x, out_dtype):
    bsz, q, n = x.shape
    pp = f.shape[0]
    tn = _tile(n, (4096, 2048, 1024, 512, 256, 128))
    return pl.pallas_call(
        _colmm_body,
        grid=(bsz, n // tn),
        in_specs=[pl.BlockSpec((pp, q), lambda i, j: (0, 0)),
                  pl.BlockSpec((1, q, tn), lambda i, j: (i, 0, j))],
        out_specs=pl.BlockSpec((1, pp, tn), lambda i, j: (i, 0, j)),
        out_shape=jax.ShapeDtypeStruct((bsz, pp, n), out_dtype),
        compiler_params=_cparams("parallel", "parallel"),
        name="dft_outer",
    )(f.astype(BF16), x)


def _dft_inner_fwd_body(a_ref, m_ref, o_ref):
    n2 = a_ref.shape[3]
    a = jnp.concatenate([a_ref[0, 0, 0], a_ref[0, 1, 0]], axis=0).astype(BF16)
    x = jnp.dot(m_ref[0], a, preferred_element_type=F32)
    o_ref[0, 0, 0] = x[:n2]
    o_ref[0, 1, 0] = x[n2:]


def _dft_inner_conv_body(a_ref, g_ref, m_ref, mi_ref, o_ref):
    n2 = a_ref.shape[3]
    a = jnp.concatenate([a_ref[0, 0, 0], a_ref[0, 1, 0]], axis=0).astype(BF16)
    x = jnp.dot(m_ref[0], a, preferred_element_type=F32)
    xre, xim = x[:n2], x[n2:]
    gre, gim = g_ref[0, 0, 0], g_ref[0, 1, 0]
    y = jnp.concatenate([xre * gre - xim * gim, xre * gim + xim * gre], axis=0).astype(BF16)
    o = jnp.dot(mi_ref[0], y, preferred_element_type=F32)
    o_ref[0, 0, 0] = o[:n2].astype(o_ref.dtype)
    o_ref[0, 1, 0] = o[n2:].astype(o_ref.dtype)


def _dft_tables(n1, n2):
    n = n1 * n2
    i1 = jnp.arange(n1, dtype=jnp.int32)
    ph1 = (2.0 * math.pi / n1) * ((i1[:, None] * i1[None, :]) % n1).astype(F32)
    c1, s1 = jnp.cos(ph1), jnp.sin(ph1)
    k1 = i1[:, None, None]
    k2 = jnp.arange(n2, dtype=jnp.int32)[None, :, None]
    m2 = jnp.arange(n2, dtype=jnp.int32)[None, None, :]
    th = (2.0 * math.pi / n) * ((m2 * (k1 + n1 * k2)) % n).astype(F32)
    mre, mim = jnp.cos(th), -jnp.sin(th)
    m_fwd = jnp.concatenate([jnp.concatenate([mre, -mim], axis=2), jnp.concatenate([mim, mre], axis=2)], axis=1)
    qre, qim = jnp.swapaxes(mre, 1, 2), -jnp.swapaxes(mim, 1, 2)
    m_inv = jnp.concatenate([jnp.concatenate([qre, -qim], axis=2), jnp.concatenate([qim, qre], axis=2)], axis=1)
    f_fwd = jnp.concatenate([c1, -s1], axis=0)
    f_inv = jnp.concatenate([c1, -s1], axis=1)
    return f_fwd, f_inv, m_fwd.astype(BF16), m_inv.astype(BF16)


def _two_sided_long_conv(z, filt):
    b, l, d = z.shape
    n = 2 * l
    n1 = max(DFT_MIN_OUTER, 2 ** ((int(math.log2(n)) - 1) // 2))
    n2 = n // n1
    f_fwd, f_inv, m_fwd, m_inv = _dft_tables(n1, n2)
    blk = pl.BlockSpec((1, 2, 1, n2, d), lambda i, j: (i, 0, j, 0, 0))
    gblk = pl.BlockSpec((1, 2, 1, n2, d), lambda i, j: (0, 0, j, 0, 0))
    mat = pl.BlockSpec((1, 2 * n2, 2 * n2), lambda i, j: (j, 0, 0))
    ga = _colmm(f_fwd, filt.reshape(1, n1, n2 * d), BF16).reshape(1, 2, n1, n2, d)
    gspec = pl.pallas_call(
        _dft_inner_fwd_body,
        grid=(1, n1),
        in_specs=[blk, mat],
        out_specs=blk,
        out_shape=jax.ShapeDtypeStruct((1, 2, n1, n2, d), F32),
        compiler_params=_cparams("parallel", "parallel"),
        name="dft_inner_fwd",
    )(ga, m_fwd)
    za = _colmm(f_fwd[:, :n1 // 2], z.reshape(b, n1 // 2, n2 * d), BF16).reshape(b, 2, n1, n2, d)
    bb = pl.pallas_call(
        _dft_inner_conv_body,
        grid=(b, n1),
        in_specs=[blk, gblk, mat, mat],
        out_specs=blk,
        out_shape=jax.ShapeDtypeStruct((b, 2, n1, n2, d), BF16),
        compiler_params=_cparams("parallel", "parallel"),
        name="dft_inner_conv",
    )(za, gspec, m_fwd, m_inv)
    return _colmm(f_inv[:n1 // 2], bb.reshape(b, 2 * n1, n2 * d), F32).reshape(b, l, d)


def _hyena_mixer(h, l_ctx, p):
    b, l, d = h.shape
    u = _linear(h.reshape(b * l, d).astype(BF16), p['w_in'], p['b_in'], out_dtype=BF16, tn=1024)
    u = u.reshape(b, l, 3 * d).astype(F32)
    prev, nxt = _stream_neighbours(u, l_ctx)
    sw = p['short_w']
    u = prev * sw[0] + u * sw[1] + nxt * sw[2] + p['short_b']
    x0, x1, v = jnp.split(u, 3, axis=-1)
    z = v * x1
    conv = jnp.concatenate([_two_sided_long_conv(z[:, :l_ctx], _hyena_filters(l_ctx, p)),
                            _two_sided_long_conv(z[:, l_ctx:], _hyena_filters(l - l_ctx, p))], axis=1)
    z = conv + z * p['bias']
    return _linear((x0 * z).reshape(b * l, d).astype(BF16), p['w_out'], p['b_out']).reshape(b, l, d)


def _attn_body(q_ref, k_ref, v_ref, o_ref, m_ref, acc_ref, *, dv, parts):
    j = pl.program_id(3)
    tq, tk = q_ref.shape[2], k_ref.shape[2]
    sub = tq // parts

    @pl.when(j == 0)
    def _():
        m_ref[...] = jnp.full_like(m_ref, -jnp.inf)
        acc_ref[...] = jnp.zeros_like(acc_ref)

    k = k_ref[0, 0]
    v = v_ref[0]
    slabs = [pl.ds(i * sub, sub) for i in range(parts)]
    s = [lax.dot_general(q_ref[0, 0, x, :], k, (((1,), (1,)), ((), ())), preferred_element_type=F32) for x in slabs]
    m_prev = [m_ref[x, :] for x in slabs]
    m_new = [jnp.maximum(mp, jnp.max(si, axis=-1, keepdims=True)) for mp, si in zip(m_prev, s)]
    pr = [jnp.exp2(si - jnp.concatenate([mn] * (tk // LANES), axis=1)).astype(BF16) for si, mn in zip(s, m_new)]
    alpha = [jnp.exp2(mp - mn) for mp, mn in zip(m_prev, m_new)]
    pv = [jnp.dot(p_, v, preferred_element_type=F32) for p_ in pr]
    for x, a, o, mn in zip(slabs, alpha, pv, m_new):
        acc_ref[x, :] = jnp.concatenate([a] * (acc_ref.shape[1] // LANES), axis=1) * acc_ref[x, :] + o
        m_ref[x, :] = mn

    @pl.when(j == pl.num_programs(3) - 1)
    def _():
        acc = acc_ref[...]
        o_ref[0] = (acc[:, :dv] / acc[:, dv:dv + 1]).astype(o_ref.dtype)


def _attend(q, k, v_ext):
    b, nh, sq, dq = q.shape
    sk = k.shape[2]
    dv = v_ext.shape[2] // (2 * nh)
    tq = _tile(sq, (2048, 1024, 512, 256, 128, 64, 32, 16, 8))
    tk = _tile(sk, (3328, 1280, 1024, 640, 512, 256, 128))
    parts = max(1, tq // ATTN_SLAB)
    return pl.pallas_call(
        functools.partial(_attn_body, dv=dv, parts=parts),
        grid=(b, nh, sq // tq, sk // tk),
        in_specs=[pl.BlockSpec((1, 1, tq, dq), lambda bi, hi, i, j: (bi, hi, i, 0)),
                  pl.BlockSpec((1, 1, tk, dq), lambda bi, hi, i, j: (bi, hi, j, 0)),
                  pl.BlockSpec((1, tk, 2 * dv), lambda bi, hi, i, j: (bi, j, hi))],
        out_specs=pl.BlockSpec((1, tq, dv), lambda bi, hi, i, j: (bi, i, hi)),
        out_shape=jax.ShapeDtypeStruct((b, sq, nh * dv), BF16),
        scratch_shapes=[pltpu.VMEM((tq, LANES), F32), pltpu.VMEM((tq, 2 * dv), F32)],
        compiler_params=_cparams("parallel", "parallel", "parallel", "arbitrary"),
        name="attention",
    )(q, k, v_ext)


def _mla_prep_body(qkv_ref, wq_ref, wkv_ref, gq_ref, gkv_ref, gqn_ref, gkn_ref, cos_ref, sin_ref, q_out, k_out, v_out):
    x = qkv_ref[0]
    qc, kvc, kpe = x[:, :Q_LORA], x[:, Q_LORA:Q_LORA + KV_LORA], x[:, Q_LORA + KV_LORA:]
    qn = qc * lax.rsqrt(jnp.mean(qc * qc, axis=-1, keepdims=True) + NORM_EPS) * gq_ref[...]
    kvn = kvc * lax.rsqrt(jnp.mean(kvc * kvc, axis=-1, keepdims=True) + NORM_EPS) * gkv_ref[...]
    q = jnp.dot(qn.astype(BF16), wq_ref[...], preferred_element_type=F32)
    kv = jnp.dot(kvn.astype(BF16), wkv_ref[...], preferred_element_type=F32)
    cos, sin = cos_ref[0], sin_ref[0]
    quarter = QK_ROPE // 4

    def rope(t):
        swapped = jnp.concatenate([t[:, quarter:2 * quarter], t[:, :quarter],
                                   t[:, 3 * quarter:], t[:, 2 * quarter:3 * quarter]], axis=-1)
        return t * cos + swapped * sin

    gqn, gkn = gqn_ref[...], gkn_ref[...]
    kpe_rot = rope(kpe * gkn[:, QK_NOPE:])
    kpe_ss = jnp.sum(kpe * kpe, axis=-1, keepdims=True)
    ones_col = jnp.where(lax.broadcasted_iota(jnp.int32, (x.shape[0], V_HEAD), 1) == 0, 1.0, 0.0).astype(BF16)
    for h in range(MLA_HEADS):
        qh = q[:, h * MLA_SLOT:h * MLA_SLOT + QK_HEAD]
        qh = qh * lax.rsqrt(jnp.sum(qh * qh, axis=-1, keepdims=True) * (1.0 / QK_HEAD) + NORM_EPS) * gqn
        qh = jnp.concatenate([qh[:, :QK_NOPE], rope(qh[:, QK_NOPE:])], axis=-1)
        q_out[0, h] = (qh * (QK_HEAD ** -0.5 * LOG2_E)).astype(q_out.dtype)
        kn = kv[:, h * MLA_SLOT:h * MLA_SLOT + QK_NOPE]
        rstd = lax.rsqrt((jnp.sum(kn * kn, axis=-1, keepdims=True) + kpe_ss) * (1.0 / QK_HEAD) + NORM_EPS)
        k_out[0, h] = jnp.concatenate([kn * rstd * gkn[:, :QK_NOPE], kpe_rot * rstd], axis=-1).astype(k_out.dtype)
        v_out[0, :, h * MLA_SLOT:h * MLA_SLOT + V_HEAD] = kv[:, h * MLA_SLOT + QK_NOPE:(h + 1) * MLA_SLOT].astype(v_out.dtype)
        v_out[0, :, h * MLA_SLOT + V_HEAD:(h + 1) * MLA_SLOT] = ones_col


def _mla_prep(qkv, p, cos, sin):
    b, l, w = qkv.shape
    tm = _tile(l, (256, 128, 64, 32, 16))
    nh = MLA_HEADS
    pad_heads = lambda m, width: jnp.pad(m.reshape(m.shape[0], nh, width), ((0, 0), (0, 0), (0, MLA_SLOT - width))
                                         ).reshape(m.shape[0], nh * MLA_SLOT)
    wq = pad_heads(p['w_qb'], QK_HEAD).astype(BF16)
    wkv = p['w_kvb'].astype(BF16)
    full = lambda a: pl.BlockSpec(a.shape, lambda i, j: (0,) * a.ndim)
    args = [wq, wkv, p['q_norm_g'].reshape(1, Q_LORA), p['kv_norm_g'].reshape(1, KV_LORA),
            p['qn_g'].reshape(1, QK_HEAD), p['kn_g'].reshape(1, QK_HEAD)]
    tab = pl.BlockSpec((1, tm, QK_ROPE), lambda i, j: (0, j, 0))
    head = pl.BlockSpec((1, nh, tm, QK_HEAD), lambda i, j: (i, 0, j, 0))
    return pl.pallas_call(
        _mla_prep_body,
        grid=(b, l // tm),
        in_specs=[pl.BlockSpec((1, tm, w), lambda i, j: (i, j, 0))] + [full(a) for a in args] + [tab, tab],
        out_specs=[head, head, pl.BlockSpec((1, tm, nh * MLA_SLOT), lambda i, j: (i, j, 0))],
        out_shape=[jax.ShapeDtypeStruct((b, nh, l, QK_HEAD), BF16), jax.ShapeDtypeStruct((b, nh, l, QK_HEAD), BF16),
                   jax.ShapeDtypeStruct((b, l, nh * MLA_SLOT), BF16)],
        compiler_params=_cparams("parallel", "parallel"),
        name="mla_prep",
    )(qkv, *args, cos, sin)


def _rope_tables(row, col, l_ctx):
    half = QK_ROPE // 2
    inv = ROPE_THETA ** (-jnp.arange(0, half, 2, dtype=F32) / half)
    ang_r = row.astype(F32)[:, None] * inv[None, :]
    ang_c = col.astype(F32)[:, None] * inv[None, :]
    cos = jnp.concatenate([jnp.cos(ang_r)] * 2 + [jnp.cos(ang_c)] * 2, axis=-1)
    sin = jnp.concatenate([-jnp.sin(ang_r), jnp.sin(ang_r), -jnp.sin(ang_c), jnp.sin(ang_c)], axis=-1)
    cos = jnp.concatenate([jnp.ones((l_ctx, QK_ROPE), F32), cos], axis=0)
    sin = jnp.concatenate([jnp.zeros((l_ctx, QK_ROPE), F32), sin], axis=0)
    return cos[None], sin[None]


def _mla_mixer(h, l_ctx, p, rope_pos):
    b, l, d = h.shape
    qkv = _linear(h.reshape(b * l, d).astype(BF16), p['w_a']).reshape(b, l, -1)
    cos, sin = _rope_tables(rope_pos[0], rope_pos[1], l_ctx)
    q, k, v_ext = _mla_prep(qkv, p, cos, sin)
    o = jnp.concatenate([_attend(q[:, :, :l_ctx], k[:, :, :l_ctx], v_ext[:, :l_ctx]),
                         _attend(q[:, :, l_ctx:], k, v_ext)], axis=1)
    return _linear(o.reshape(b * l, MLA_HEADS * V_HEAD), p['w_o']).reshape(b, l, d)


def _router_body(h_ref, w_ref, b_ref, idx_ref, gate_ref, rank_ref, cnt_ref, run_ref):
    @pl.when(pl.program_id(0) == 0)
    def _():
        run_ref[...] = jnp.zeros_like(run_ref)

    lg = jnp.dot(h_ref[...], w_ref[...], precision=lax.Precision.HIGHEST, preferred_element_type=F32) + b_ref[...]
    tm = lg.shape[0]
    lane = lax.broadcasted_iota(jnp.int32, lg.shape, 1).astype(F32)
    idx_acc = jnp.zeros(lg.shape, F32)
    val_acc = jnp.zeros(lg.shape, F32)
    picked = jnp.zeros(lg.shape, F32)
    sels = []
    top = None
    den = None
    for j in range(TOP_K):
        mx = jnp.max(lg, axis=-1, keepdims=True)
        sel = jnp.min(jnp.where(lg == mx, lane, float(ROUTER_LANES)), axis=-1, keepdims=True)
        if j == 0:
            top = mx
        e = jnp.exp(mx - top)
        den = e if den is None else den + e
        hit = lane == sel
        sels.append(hit)
        idx_acc = jnp.where(lane == float(j), sel, idx_acc)
        val_acc = jnp.where(lane == float(j), e, val_acc)
        picked = jnp.where(hit, 1.0, picked)
        lg = jnp.where(hit, -jnp.inf, lg)
    idx_ref[...] = idx_acc.astype(jnp.int32)
    gate_ref[...] = val_acc / den
    rs = lax.broadcasted_iota(jnp.int32, (tm, tm), 0)
    cs = lax.broadcasted_iota(jnp.int32, (tm, tm), 1)
    before = jnp.where(rs > cs, 1.0, 0.0).astype(BF16)
    base = run_ref[...] + jnp.dot(before, picked.astype(BF16), preferred_element_type=F32)
    rank_acc = jnp.zeros(lg.shape, F32)
    for j in range(TOP_K):
        rk = jnp.sum(jnp.where(sels[j], base, 0.0), axis=-1, keepdims=True)
        rank_acc = jnp.where(lane == float(j), rk, rank_acc)
    rank_ref[...] = rank_acc.astype(jnp.int32)
    run_ref[...] = run_ref[...] + jnp.sum(picked, axis=0, keepdims=True)
    cnt_ref[...] = run_ref[...].astype(jnp.int32)


def _router(h, router_w, router_b):
    t, d = h.shape
    tm = _tile(t, (512, 256, 128, 64, 32, 16, 8))
    w = jnp.concatenate([router_w, jnp.zeros((d, ROUTER_LANES - N_EXPERTS), F32)], axis=1)
    bias = jnp.concatenate([router_b, jnp.full((ROUTER_LANES - N_EXPERTS,), -1e30, F32)]).reshape(1, ROUTER_LANES)
    tok = pl.BlockSpec((tm, ROUTER_LANES), lambda i: (i, 0))
    return pl.pallas_call(
        _router_body,
        grid=(t // tm,),
        in_specs=[pl.BlockSpec((tm, d), lambda i: (i, 0)),
                  pl.BlockSpec((d, ROUTER_LANES), lambda i: (0, 0)),
                  pl.BlockSpec((1, ROUTER_LANES), lambda i: (0, 0))],
        out_specs=[tok, tok, tok, pl.BlockSpec((1, ROUTER_LANES), lambda i: (0, 0))],
        out_shape=[jax.ShapeDtypeStruct((t, ROUTER_LANES), jnp.int32), jax.ShapeDtypeStruct((t, ROUTER_LANES), F32),
                   jax.ShapeDtypeStruct((t, ROUTER_LANES), jnp.int32), jax.ShapeDtypeStruct((1, ROUTER_LANES), jnp.int32)],
        scratch_shapes=[pltpu.VMEM((1, ROUTER_LANES), F32)],
        compiler_params=_cparams("arbitrary"),
        name="router",
    )(h, w, bias)


def _expert_body(be_ref, nb_ref, x_ref, win_ref, bin_ref, wout_ref, bout_ref, gate_ref, o_ref, win_bf, wout_bf):
    i = pl.program_id(0)
    fresh = jnp.logical_or(i == 0, be_ref[i] != be_ref[jnp.maximum(i - 1, 0)])

    @pl.when(fresh)
    def _():
        win_bf[...] = win_ref[0].astype(BF16)
        wout_bf[...] = wout_ref[0].astype(BF16)

    @pl.when(i < nb_ref[0])
    def _():
        sub = MOE_ROWS // MOE_SLABS
        slabs = [pl.ds(j * sub, sub) for j in range(MOE_SLABS)]
        gu = [jnp.dot(x_ref[x, :], win_bf[...], preferred_element_type=F32) + bin_ref[0] for x in slabs]
        glu = [jnp.minimum(g[:, :D_EXPERT], SWIGLU_LIMIT) for g in gu]
        lin = [jnp.clip(g[:, D_EXPERT:], -SWIGLU_LIMIT, SWIGLU_LIMIT) for g in gu]
        act = [(a * jax.nn.sigmoid(SWIGLU_ALPHA * a) * (b + 1.0)).astype(BF16) for a, b in zip(glu, lin)]
        y = [jnp.dot(a, wout_bf[...], preferred_element_type=F32) + bout_ref[0] for a in act]
        for x, yy in zip(slabs, y):
            o_ref[x, :] = (yy * gate_ref[x, :]).astype(o_ref.dtype)

    @pl.when(i >= nb_ref[0])
    def _():
        o_ref[...] = jnp.zeros_like(o_ref)


def _moe(h, h_bf, router_w, router_b, w_in, b_in, w_out, b_out):
    t, d = h.shape
    n_assign = t * TOP_K
    idx, gates, ranks, counts = _router(h, router_w, router_b)
    flat_e = idx[:, :TOP_K].reshape(n_assign)
    flat_g = gates[:, :TOP_K].reshape(n_assign)
    rank = ranks[:, :TOP_K].reshape(n_assign)
    counts = counts[0, :N_EXPERTS]
    padded = (counts + MOE_ROWS - 1) // MOE_ROWS * MOE_ROWS
    pad_end = jnp.cumsum(padded)
    pad_start = pad_end - padded
    experts = jnp.arange(N_EXPERTS, dtype=jnp.int32)
    lookup = lambda table, e: jnp.sum(jnp.where(e[:, None] == experts[None, :], table[None, :], 0), axis=1)
    dest = lookup(pad_start, flat_e) + rank
    n_blocks = -(-n_assign // MOE_ROWS) + N_EXPERTS
    n_rows = n_blocks * MOE_ROWS
    block_start = jnp.arange(n_blocks, dtype=jnp.int32) * MOE_ROWS
    block_e = jnp.minimum(jnp.sum((pad_end[None, :] <= block_start[:, None]).astype(jnp.int32), axis=1), N_EXPERTS - 1)
    order = jnp.argsort(flat_e, stable=True).astype(jnp.int32)
    start = jnp.cumsum(counts) - counts
    in_block = jnp.arange(MOE_ROWS, dtype=jnp.int32)[None, :]
    row_rank = (block_start - lookup(pad_start, block_e))[:, None] + in_block
    row_live = (row_rank < lookup(counts, block_e)[:, None]).reshape(n_rows)
    row_src = order[jnp.minimum(lookup(start, block_e)[:, None] + row_rank, n_assign - 1).reshape(n_rows)]
    row_tok = row_src // TOP_K
    row_gate = jnp.where(row_live, flat_g[row_src], 0.0)
    n_used = (pad_end[-1] // MOE_ROWS).astype(jnp.int32).reshape(1)
    xb = h_bf[row_tok]
    f2 = 2 * D_EXPERT
    grid_spec = pltpu.PrefetchScalarGridSpec(
        num_scalar_prefetch=2,
        grid=(n_blocks,),
        in_specs=[pl.BlockSpec((MOE_ROWS, d), lambda i, be, nb: (i, 0)),
                  pl.BlockSpec((1, d, f2), lambda i, be, nb: (be[i], 0, 0)),
                  pl.BlockSpec((1, 1, f2), lambda i, be, nb: (be[i], 0, 0)),
                  pl.BlockSpec((1, D_EXPERT, d), lambda i, be, nb: (be[i], 0, 0)),
                  pl.BlockSpec((1, 1, d), lambda i, be, nb: (be[i], 0, 0)),
                  pl.BlockSpec((MOE_ROWS, 1), lambda i, be, nb: (i, 0))],
        out_specs=pl.BlockSpec((MOE_ROWS, d), lambda i, be, nb: (i, 0)),
        scratch_shapes=[pltpu.VMEM((d, f2), BF16), pltpu.VMEM((D_EXPERT, d), BF16)],
    )
    yb = pl.pallas_call(
        _expert_body,
        grid_spec=grid_spec,
        out_shape=jax.ShapeDtypeStruct((n_rows, d), BF16),
        compiler_params=_cparams("arbitrary"),
        name="moe_experts",
    )(block_e, n_used, xb, w_in, b_in.reshape(N_EXPERTS, 1, f2),
      w_out, b_out.reshape(N_EXPERTS, 1, d), row_gate.reshape(n_rows, 1))
    dest = dest.reshape(t, TOP_K)
    return [yb[dest[:, j]] for j in range(TOP_K)]


def kernel(x, c, ctx, c_ctx, ada_w, ada_b, norm_mix_g, norm_ffn_g, rwkv_mu, rwkv_w_rkv, rwkv_w0, rwkv_w1, rwkv_w2, rwkv_a0, rwkv_a1, rwkv_a2, rwkv_g1, rwkv_g2, rwkv_k_k, rwkv_k_a, rwkv_r_k, rwkv_ln_g, rwkv_ln_b, rwkv_w_o, rwkv_v0, rwkv_v1, rwkv_v2, hy_w_in, hy_b_in, hy_short_w, hy_short_b, hy_f_w1, hy_f_b1, hy_f_w2, hy_f_b2, hy_f_w3, hy_f_b3, hy_f_freq, hy_f_wout, hy_bias, hy_w_out, hy_b_out, mla_w_a, mla_q_norm_g, mla_kv_norm_g, mla_w_qb, mla_w_kvb, mla_qn_g, mla_kn_g, mla_w_o, moe_router_w, moe_router_b, moe_w_in, moe_b_in, moe_w_out, moe_b_out):
    bsz, s_len, d = x.shape
    l_ctx = ctx.shape[1]
    rows = s_len // GRID_W
    row = jnp.repeat(jnp.arange(rows, dtype=jnp.int32), GRID_W)
    col = jnp.tile(jnp.arange(GRID_W, dtype=jnp.int32), rows)
    silu = jnp.concatenate([jax.nn.silu(c), jax.nn.silu(c_ctx)[None, :]], axis=0)
    silu = jnp.concatenate([silu, jnp.zeros((-(bsz + 1) % 8, d), F32)], axis=0)
    is_ctx = (jnp.arange(l_ctx + s_len, dtype=jnp.int32) < l_ctx)[None, :, None]
    per_row = lambda m: jnp.where(is_ctx, m[:, 0:1], m[:, 1:2])
    xs = jnp.concatenate([ctx, x], axis=1)
    mods = []
    for i in range(DEPTH):
        mod = _linear(silu, ada_w[i], ada_b[i], tn=1024)
        mod = jnp.stack([jnp.broadcast_to(mod[bsz:bsz + 1], (bsz, N_MOD * d)), mod[:bsz]], axis=1)
        mods.append(jnp.split(mod, N_MOD, axis=-1))
    v_first = None
    h = _norm_mod(xs, norm_mix_g[0], mods[0][1], mods[0][0], l_ctx, BF16)
    for i in range(DEPTH):
        j = i // N_MIXERS
        kind = i % N_MIXERS
        sh1, sc1, g1, sh2, sc2, g2 = mods[i]
        if kind == 0:
            p = {'mu': rwkv_mu[j], 'w_rkv': rwkv_w_rkv[j], 'w0': rwkv_w0[j], 'w1': rwkv_w1[j],
                 'w2': rwkv_w2[j], 'a0': rwkv_a0[j], 'a1': rwkv_a1[j], 'a2': rwkv_a2[j],
                 'g1': rwkv_g1[j], 'g2': rwkv_g2[j], 'k_k': rwkv_k_k[j], 'k_a': rwkv_k_a[j],
                 'r_k': rwkv_r_k[j], 'ln_g': rwkv_ln_g[j], 'ln_b': rwkv_ln_b[j], 'w_o': rwkv_w_o[j]}
            vres = None if j == 0 else (rwkv_v0[j - 1], rwkv_v1[j - 1], rwkv_v2[j - 1])
            y, v_cur = _rwkv_mixer(h, l_ctx, p, v_first, vres)
            if j == 0:
                v_first = v_cur
        elif kind == 1:
            p = {'w_in': hy_w_in[j], 'b_in': hy_b_in[j], 'short_w': hy_short_w[j], 'short_b': hy_short_b[j],
                 'f_w1': hy_f_w1[j], 'f_b1': hy_f_b1[j], 'f_w2': hy_f_w2[j], 'f_b2': hy_f_b2[j],
                 'f_w3': hy_f_w3[j], 'f_b3': hy_f_b3[j], 'f_freq': hy_f_freq[j], 'f_wout': hy_f_wout[j],
                 'bias': hy_bias[j], 'w_out': hy_w_out[j], 'b_out': hy_b_out[j]}
            y = _hyena_mixer(h, l_ctx, p)
        else:
            p = {'w_a': mla_w_a[j], 'q_norm_g': mla_q_norm_g[j], 'kv_norm_g': mla_kv_norm_g[j],
                 'w_qb': mla_w_qb[j], 'w_kvb': mla_w_kvb[j], 'qn_g': mla_qn_g[j], 'kn_g': mla_kn_g[j],
                 'w_o': mla_w_o[j]}
            y = _mla_mixer(h, l_ctx, p, (row, col))
        xs, h2, h2_bf = _residual_norm(xs, [y], g1, norm_ffn_g[i], sc2, sh2, l_ctx, (F32, BF16))
        moe_p = (moe_router_w[i], moe_router_b[i], moe_w_in[i], moe_b_in[i], moe_w_out[i], moe_b_out[i])
        parts = [f.reshape(xs.shape) for f in _moe(h2.reshape(-1, d), h2_bf.reshape(-1, d), *moe_p)]
        if i + 1 < DEPTH:
            xs, h = _residual_norm(xs, parts, g2, norm_mix_g[i + 1], mods[i + 1][1], mods[i + 1][0], l_ctx, (BF16,))
        else:
            xs = xs + per_row(g2) * sum(f.astype(F32) for f in parts)
    return xs[:, l_ctx:]
```

```python
import functools
import math

import jax
import jax.numpy as jnp
from jax import lax
from jax.experimental import pallas as pl
from jax.experimental.pallas import tpu as pltpu

F32 = jnp.float32
BF16 = jnp.bfloat16

D_MODEL = 1024
DEPTH = 4
GRID_W = 64
N_MIXERS = 3
N_MOD = 6
NORM_EPS = 1e-6
RWKV_HEAD = 64
GN_EPS = 64e-5
WKV_CHUNK = 64
WKV_GROUP = 4
FILTER_EMB = 33
FILTER_EMB_PAD = 128
DFT_MIN_OUTER = 32
FAST_DECAY = math.log(1e-2) / 0.3
SLOW_DECAY = math.log(1e-2) / 1.5
MLA_HEADS = 8
QK_NOPE = 128
QK_ROPE = 64
QK_HEAD = QK_NOPE + QK_ROPE
V_HEAD = 128
Q_LORA = 384
KV_LORA = 256
ROPE_THETA = 10000.0
N_EXPERTS = 32
TOP_K = 4
D_EXPERT = 1024
SWIGLU_LIMIT = 7.0
SWIGLU_ALPHA = 1.702
MOE_ROWS = 256
MOE_SLABS = 2
LANES = 128
ROUTER_LANES = LANES
ATTN_SLAB = 512
LOG2_E = 1.4426950408889634
MLA_SLOT = 256

V7X_VMEM_BYTES = 64 * 1024 * 1024
VMEM_LIMIT = V7X_VMEM_BYTES * 3 // 4


def _cparams(*sem):
    return pltpu.CompilerParams(dimension_semantics=sem, vmem_limit_bytes=VMEM_LIMIT)


def _tile(n, prefs):
    for t in prefs:
        if n % t == 0:
            return t
    return n


def _linear_body(a_ref, w_ref, b_ref, o_ref):
    acc = jnp.dot(a_ref[...].astype(BF16), w_ref[...], preferred_element_type=F32)
    o_ref[...] = (acc + b_ref[...]).astype(o_ref.dtype)


def _linear(a, w, b=None, *, out_dtype=F32, tn=None):
    m, k = a.shape
    n = w.shape[1]
    tm = _tile(m, (512, 256, 128, 64, 32, 16, 8))
    tn = n if tn is None else tn
    assert n % tn == 0
    if b is None:
        b = jnp.zeros((n,), F32)
    return pl.pallas_call(
        _linear_body,
        grid=(n // tn, m // tm),
        in_specs=[pl.BlockSpec((tm, k), lambda j, i: (i, 0)),
                  pl.BlockSpec((k, tn), lambda j, i: (0, j)),
                  pl.BlockSpec((1, tn), lambda j, i: (0, j))],
        out_specs=pl.BlockSpec((tm, tn), lambda j, i: (i, j)),
        out_shape=jax.ShapeDtypeStruct((m, n), out_dtype),
        compiler_params=_cparams("parallel", "parallel"),
        name="linear",
    )(a, w.astype(BF16), b.reshape(1, n).astype(F32))


def _norm_mod_body(x_ref, g_ref, sc_ref, sh_ref, o_ref):
    x = x_ref[0]
    y = x * lax.rsqrt(jnp.mean(x * x, axis=-1, keepdims=True) + NORM_EPS)
    o_ref[0] = ((y * g_ref[...]) * (1.0 + sc_ref[0, 0]) + sh_ref[0, 0]).astype(o_ref.dtype)


def _norm_mod(x, g, sc, sh, l_ctx, out_dtype):
    b, l, d = x.shape
    tl = _tile(math.gcd(l_ctx, l - l_ctx), (512, 256, 128, 64, 32, 16, 8))
    stream = lambda i, j: (i, jnp.where(j * tl < l_ctx, 0, 1), 0, 0)
    return pl.pallas_call(
        _norm_mod_body,
        grid=(b, l // tl),
        in_specs=[pl.BlockSpec((1, tl, d), lambda i, j: (i, j, 0)),
                  pl.BlockSpec((1, d), lambda i, j: (0, 0)),
                  pl.BlockSpec((1, 1, 1, d), stream),
                  pl.BlockSpec((1, 1, 1, d), stream)],
        out_specs=pl.BlockSpec((1, tl, d), lambda i, j: (i, j, 0)),
        out_shape=jax.ShapeDtypeStruct((b, l, d), out_dtype),
        compiler_params=_cparams("parallel", "parallel"),
        name="norm_mod",
    )(x, g.reshape(1, d), sc.reshape(b, 2, 1, d), sh.reshape(b, 2, 1, d))


def _residual_norm_body(*refs, n_add):
    x_ref, adds, (gate_ref, g_ref, sc_ref, sh_ref, xo_ref), h_refs = refs[0], refs[1:1 + n_add], refs[1 + n_add:6 + n_add], refs[6 + n_add:]
    y = adds[0][0].astype(F32)
    for a in adds[1:]:
        y = y + a[0].astype(F32)
    x = x_ref[0] + gate_ref[0, 0] * y
    xo_ref[0] = x
    nrm = x * lax.rsqrt(jnp.mean(x * x, axis=-1, keepdims=True) + NORM_EPS)
    h = (nrm * g_ref[...]) * (1.0 + sc_ref[0, 0]) + sh_ref[0, 0]
    for h_ref in h_refs:
        h_ref[0] = h.astype(h_ref.dtype)


def _residual_norm(x, adds, gate, g, sc, sh, l_ctx, h_dtypes):
    b, l, d = x.shape
    tl = _tile(math.gcd(l_ctx, l - l_ctx), (512, 256, 128, 64, 32, 16, 8))
    stream = pl.BlockSpec((1, 1, 1, d), lambda i, j: (i, jnp.where(j * tl < l_ctx, 0, 1), 0, 0))
    rows = pl.BlockSpec((1, tl, d), lambda i, j: (i, j, 0))
    return pl.pallas_call(
        functools.partial(_residual_norm_body, n_add=len(adds)),
        grid=(b, l // tl),
        in_specs=[rows] * (1 + len(adds)) + [stream, pl.BlockSpec((1, d), lambda i, j: (0, 0)), stream, stream],
        out_specs=[rows] * (1 + len(h_dtypes)),
        out_shape=[jax.ShapeDtypeStruct((b, l, d), F32)] + [jax.ShapeDtypeStruct((b, l, d), t) for t in h_dtypes],
        compiler_params=_cparams("parallel", "parallel"),
        name="residual_norm",
    )(x, *adds, gate.reshape(b, 2, 1, d), g.reshape(1, d), sc.reshape(b, 2, 1, d), sh.reshape(b, 2, 1, d))


def _stream_neighbours(t, l_ctx):
    zero = jnp.zeros_like(t[:, :1])
    pos = jnp.arange(t.shape[1], dtype=jnp.int32)[None, :, None]
    prev = jnp.where(pos == l_ctx, 0.0, jnp.concatenate([zero, t[:, :-1]], axis=1))
    nxt = jnp.where(pos == l_ctx - 1, 0.0, jnp.concatenate([t[:, 1:], zero], axis=1))
    return prev, nxt


PROJ_K, PROJ_V = 1024, 2048
PROJ_W1 = (3072, 3200)
PROJ_A1 = (3328, 3456)
PROJ_G1 = 3584
PROJ_V1 = 3840
PROJ_WIDTH = 4096
LORA_PAD = 128
GATE_PAD = 256


def _sigmoid(x):
    return jax.nn.sigmoid(x)


def _rwkv_post_body(*refs, has_vres):
    if has_vres:
        (proj_ref, vf_ref, w2_ref, a2_ref, g2_ref, v2_ref, vec_ref, gs_ref,
         kk_out, v_out, gate_out, lw_out, ag_out, kd_out) = refs
    else:
        (proj_ref, w2_ref, a2_ref, g2_ref, vec_ref, gs_ref,
         kk_out, v_out, gate_out, lw_out, ag_out, kd_out) = refs
    d = D_MODEL
    vec = vec_ref[...]
    k = proj_ref[:, PROJ_K:PROJ_K + d].astype(F32)
    v = proj_ref[:, PROJ_V:PROJ_V + d].astype(F32)
    if has_vres:
        v1 = proj_ref[:, PROJ_V1:PROJ_V1 + LORA_PAD]
        vgate = _sigmoid(vec[6:7] + jnp.dot(v1, v2_ref[...], preferred_element_type=F32))
        v = v + (vf_ref[...].astype(F32) - v) * vgate
    v_out[...] = v.astype(v_out.dtype)
    kraw = k * vec[4:5]
    ss = jnp.dot((kraw * kraw).astype(BF16), gs_ref[...], preferred_element_type=F32)
    kk_out[...] = (kraw / jnp.maximum(jnp.sqrt(ss), 1e-12)).astype(kk_out.dtype)
    g1 = proj_ref[:, PROJ_G1:PROJ_G1 + GATE_PAD].astype(F32)
    gate_out[...] = jnp.dot(_sigmoid(g1).astype(BF16), g2_ref[...], preferred_element_type=F32).astype(gate_out.dtype)
    for dd in range(2):
        w1 = proj_ref[:, PROJ_W1[dd]:PROJ_W1[dd] + LORA_PAD].astype(F32)
        z = vec[dd:dd + 1] + jnp.dot(jnp.tanh(w1).astype(BF16), w2_ref[dd], preferred_element_type=F32)
        softplus = jnp.maximum(-z, 0.0) + jnp.log(1.0 + jnp.exp(-jnp.abs(z)))
        lw_out[dd] = -jnp.exp(-softplus - 0.5)
        a1 = proj_ref[:, PROJ_A1[dd]:PROJ_A1[dd] + LORA_PAD]
        ag = _sigmoid(vec[2 + dd:3 + dd] + jnp.dot(a1, a2_ref[dd], preferred_element_type=F32))
        ag_out[dd] = ag.astype(ag_out.dtype)
        kd_out[dd] = (k * (1.0 + (ag - 1.0) * vec[5:6])).astype(kd_out.dtype)


def _rwkv_post(proj, v_first, w2p, a2p, g2p, v2p, vec, gsum):
    t, d = proj.shape[0], D_MODEL
    tm = _tile(t, (256, 128, 64, 32, 16))
    has_vres = v_first is not None
    full = lambda a: pl.BlockSpec(a.shape, lambda i: (0,) * a.ndim)
    row = pl.BlockSpec((tm, d), lambda i: (i, 0))
    row2 = pl.BlockSpec((2, tm, d), lambda i: (0, i, 0))
    ins = [proj] + ([v_first] if has_vres else []) + [w2p, a2p, g2p] + ([v2p] if has_vres else []) + [vec, gsum]
    in_specs = ([pl.BlockSpec((tm, PROJ_WIDTH), lambda i: (i, 0))] + ([row] if has_vres else [])
                + [full(a) for a in ins[(2 if has_vres else 1):]])
    return pl.pallas_call(
        functools.partial(_rwkv_post_body, has_vres=has_vres),
        grid=(t // tm,),
        in_specs=in_specs,
        out_specs=[row, row, row, row2, row2, row2],
        out_shape=[jax.ShapeDtypeStruct((t, d), BF16),
                   jax.ShapeDtypeStruct((t, d), BF16),
                   jax.ShapeDtypeStruct((t, d), BF16),
                   jax.ShapeDtypeStruct((2, t, d), F32),
                   jax.ShapeDtypeStruct((2, t, d), BF16),
                   jax.ShapeDtypeStruct((2, t, d), BF16)],
        compiler_params=_cparams("parallel"),
        name="rwkv_post",
    )(*ins)


def _head_groups(y):
    gw = WKV_GROUP * RWKV_HEAD
    lane_head = jnp.right_shift(lax.broadcasted_iota(jnp.int32, (y.shape[0], gw), 1), 6)
    out = []
    for g in range(D_MODEL // gw):
        yg = y[:, g * gw:(g + 1) * gw].astype(BF16)
        out.append(jnp.concatenate([jnp.where(lane_head == h, yg, jnp.zeros_like(yg)) for h in range(WKV_GROUP)], axis=0))
    return out


def _gmm(x, ybd):
    gw = WKV_GROUP * RWKV_HEAD
    return jnp.concatenate([jnp.dot(x[:, g * gw:(g + 1) * gw].astype(BF16), yb, preferred_element_type=F32)
                            for g, yb in enumerate(ybd)], axis=-1)


def _gmm_nt(x, ybd):
    gw = WKV_GROUP * RWKV_HEAD
    return jnp.concatenate([lax.dot_general(x[:, g * gw:(g + 1) * gw].astype(BF16), yb, (((1,), (1,)), ((), ())),
                                            preferred_element_type=F32)
                            for g, yb in enumerate(ybd)], axis=-1)


def _wkv_chunks(r, kk, v, lw, ag, kd, s0, rev):
    each = lambda f, *ls: [f(*a) for a in zip(*ls)]
    c, d = r[0].shape
    row0 = lax.broadcasted_iota(jnp.int32, (c, d), 0)
    col0 = jnp.bitwise_and(lax.broadcasted_iota(jnp.int32, (c, d), 1), RWKV_HEAD - 1)
    incl = [(row0 <= col0) if rv else (row0 >= col0) for rv in rev]
    strict = [(row0 < col0) if rv else (row0 > col0) for rv in rev]
    rs = lax.broadcasted_iota(jnp.int32, (c, c), 0)
    cs = lax.broadcasted_iota(jnp.int32, (c, c), 1)
    tri = [jnp.where((rs <= cs) if rv else (rs >= cs), 1.0, 0.0).astype(BF16) for rv in rev]
    hi = each(lambda x: x.astype(BF16), lw)
    rem = each(lambda x, h: x - h.astype(F32), lw, hi)
    mid = each(lambda x: x.astype(BF16), rem)
    lo = each(lambda x, m: (x - m.astype(F32)).astype(BF16), rem, mid)
    cum = each(lambda t, a, b_, c_: (jnp.dot(t, a, preferred_element_type=F32) + jnp.dot(t, b_, preferred_element_type=F32)
                                     + jnp.dot(t, c_, preferred_element_type=F32)), tri, hi, mid, lo)
    cum_end = [x[0:1, :] if rv else x[c - 1:c, :] for x, rv in zip(cum, rev)]
    g_rest = each(lambda e, x: jnp.exp(e - x), cum_end, cum)
    g_inv = each(lambda x: jnp.exp(-x), cum)
    bvec = each(lambda a, b_: a * b_, kk, ag)
    rt = each(lambda a, x: a * jnp.exp(x), r, cum)
    kt = each(lambda a, g: a * g, kd, g_inv)
    bt = each(lambda a, g: a * g, bvec, g_inv)
    at = each(lambda a, x, w: -a * jnp.exp(x - w), kk, cum, lw)
    ar = each(lambda a, b_: jnp.concatenate([a, b_], axis=0), at, rt)
    ab = each(lambda a, b_: _gmm_nt(a, _head_groups(b_)), ar, bt)
    ak = each(lambda a, b_: _gmm_nt(a, _head_groups(b_)), ar, kt)
    a_ab = each(lambda m, x: jnp.where(m, x[:c], 0.0), strict, ab)
    a_rb = each(lambda m, x: jnp.where(m, x[c:], 0.0), incl, ab)
    a_ak = each(lambda m, x: jnp.where(m, x[:c], 0.0), strict, ak)
    a_rk = each(lambda m, x: jnp.where(m, x[c:], 0.0), incl, ak)
    same16 = jnp.right_shift(row0, 4) == jnp.right_shift(col0, 4)
    same32 = jnp.right_shift(row0, 5) == jnp.right_shift(col0, 5)
    eye = jnp.where(row0 == col0, 1.0, 0.0)
    npow = each(lambda x: jnp.where(same16, x, 0.0), a_ab)
    p = each(lambda x: eye + x, npow)
    for _ in range(3):
        npow = each(lambda x: _gmm(x, _head_groups(x)), npow)
        p = each(lambda x, y: x + _gmm(x, _head_groups(y)), p, npow)
    off32 = jnp.logical_and(same32, jnp.logical_not(same16))
    t1 = each(lambda x, y: _gmm(jnp.where(off32, x, 0.0), _head_groups(y)), a_ab, p)
    p = each(lambda x, y: x + _gmm(x, _head_groups(y)), p, t1)
    t2 = each(lambda x, y: _gmm(jnp.where(same32, 0.0, x), _head_groups(y)), a_ab, p)
    p = each(lambda x, y: x + _gmm(x, _head_groups(y)), p, t2)
    pbd = each(_head_groups, p)
    vbd = each(_head_groups, v)
    wmat = each(lambda x, y: _gmm(x, _head_groups(y)), p, at)
    vt = each(lambda y: _gmm_nt(eye, y), vbd)
    xt = each(lambda x, y: _gmm_nt(x, _head_groups(y)), vt, a_ak)
    upt = each(_gmm_nt, xt, pbd)
    yp = each(_gmm, a_rk, vbd)
    sp = each(lambda x, a, g: _gmm(x, _head_groups(a * g)), vt, kd, g_rest)
    ut = each(lambda s_, w, u: _gmm_nt(s_, _head_groups(w)) + u, s0, wmat, upt)
    ys = each(lambda s_, q, x: _gmm_nt(q, _head_groups(s_)) + x, s0, rt, yp)
    y = each(lambda y0, a, u: y0 + _gmm_nt(a, _head_groups(u)), ys, a_rb, ut)
    s_end = each(lambda s_, e, u, b_, g, x: s_ * jnp.exp(e) + _gmm(u, _head_groups(b_ * g)) + x,
                 s0, cum_end, ut, bvec, g_rest, sp)
    return y, s_end


def _wkv_body(*refs, n_batch):
    ins, outs, s_ref = refs[:12], refs[12:14], refs[14]

    @pl.when(pl.program_id(0) == 0)
    def _():
        s_ref[...] = jnp.zeros_like(s_ref)

    scans = [(dd, bi) for dd in range(2) for bi in range(n_batch)]
    get = lambda k, lead: [ins[6 * dd + k][(0, bi) if lead else bi].astype(F32) for dd, bi in scans]
    y, s_end = _wkv_chunks(get(0, False), get(1, False), get(2, False), get(3, True), get(4, True), get(5, True),
                           [s_ref[i] for i in range(len(scans))], [dd == 1 for dd, _ in scans])
    for i, (dd, bi) in enumerate(scans):
        outs[dd][bi] = y[i]
        s_ref[i] = s_end[i]


def _wkv(proj, kk, v, lw, ag, kd, l_ctx):
    b, l, d = kk.shape
    c = WKV_CHUNK
    nc, nctx = l // c, l_ctx // c
    assert c == RWKV_HEAD and l % c == 0 and l_ctx % c == 0

    def fwd(j):
        return j

    def bwd(j):
        return jnp.where(j < nctx, nctx - 1 - j, nc + nctx - 1 - j)

    in_specs, out_specs = [], []
    for dd, ch in enumerate((fwd, bwd)):
        shared = pl.BlockSpec((b, c, d), lambda j, ch=ch: (0, ch(j), 0))
        per_dir = pl.BlockSpec((1, b, c, d), lambda j, dd=dd, ch=ch: (dd, 0, ch(j), 0))
        in_specs += [shared, shared, shared, per_dir, per_dir, per_dir]
        out_specs.append(shared)
    return pl.pallas_call(
        functools.partial(_wkv_body, n_batch=b),
        grid=(nc,),
        in_specs=in_specs,
        out_specs=out_specs,
        out_shape=[jax.ShapeDtypeStruct((b, l, d), F32)] * 2,
        scratch_shapes=[pltpu.VMEM((2 * b, RWKV_HEAD, d), F32)],
        compiler_params=_cparams("arbitrary"),
        name="wkv",
    )(*([proj, kk, v, lw, ag, kd] * 2))


def _rwkv_readout_body(yf_ref, yr_ref, r_ref, kd_ref, v_ref, gate_ref, gm_ref, gs_ref, vec_ref, wo_ref, o_ref):
    vec = vec_ref[...]
    y = yf_ref[...] + yr_ref[...]
    gm = gm_ref[...]
    yh = y.astype(BF16)
    yl = (y - yh.astype(F32)).astype(BF16)
    mean = jnp.dot(yh, gm, preferred_element_type=F32) + jnp.dot(yl, gm, preferred_element_type=F32)
    yc = y - mean
    var = jnp.dot((yc * yc).astype(BF16), gm, preferred_element_type=F32)
    yn = yc * lax.rsqrt(var + GN_EPS) * vec[0:1] + vec[1:2]
    r = r_ref[...].astype(F32)
    kds = kd_ref[0].astype(F32) + kd_ref[1].astype(F32)
    bonus = jnp.dot((r * kds * vec[2:3]).astype(BF16), gs_ref[...], preferred_element_type=F32)
    out = (yn + bonus * v_ref[...].astype(F32)) * gate_ref[...].astype(F32)
    o_ref[...] = jnp.dot(out.astype(BF16), wo_ref[...], preferred_element_type=F32)


def _rwkv_readout(y_fwd, y_rev, proj, kd, v, gate, gmean, gsum, vec, w_o):
    t, d = v.shape
    tm = _tile(t, (256, 128, 64, 32, 16))
    full = lambda a: pl.BlockSpec(a.shape, lambda i: (0,) * a.ndim)
    row = pl.BlockSpec((tm, d), lambda i: (i, 0))
    row2 = pl.BlockSpec((2, tm, d), lambda i: (0, i, 0))
    return pl.pallas_call(
        _rwkv_readout_body,
        grid=(t // tm,),
        in_specs=[row, row, row, row2, row, row, full(gmean), full(gsum), full(vec), full(w_o)],
        out_specs=row,
        out_shape=jax.ShapeDtypeStruct((t, d), F32),
        compiler_params=_cparams("parallel"),
        name="rwkv_readout",
    )(y_fwd, y_rev, proj, kd, v, gate, gmean, gsum, vec, w_o)


def _pad_cols(w, width):
    return jnp.concatenate([w, jnp.zeros((w.shape[0], width - w.shape[1]), w.dtype)], axis=1)


def _pad_rows(w, height):
    return jnp.concatenate([w, jnp.zeros((height - w.shape[0], w.shape[1]), w.dtype)], axis=0)


def _rwkv_mixer(h, l_ctx, p, v_first, vres):
    b, l, d = h.shape
    t = b * l
    h32 = h.astype(F32)
    prev, nxt = _stream_neighbours(h32, l_ctx)
    xx = 0.5 * (prev + nxt) - h32
    mu = p['mu']
    mats = [(p['w_rkv'][0], 0, d), (p['w_rkv'][1], 2, d), (p['w_rkv'][2], 3, d),
            (p['w1'][0], 1, LORA_PAD), (p['w1'][1], 1, LORA_PAD), (p['a1'][0], 4, LORA_PAD), (p['a1'][1], 4, LORA_PAD),
            (p['g1'], 5, GATE_PAD), (vres[1] if vres is not None else jnp.zeros((d, 1), F32), 3, 2 * LORA_PAD)]
    top = jnp.concatenate([_pad_cols(w, width) for w, _, width in mats], axis=1)
    bot = jnp.concatenate([_pad_cols(mu[m][:, None] * w, width) for w, m, width in mats], axis=1)
    wcat = jnp.concatenate([top, bot], axis=0)
    assert wcat.shape[1] == PROJ_WIDTH
    a_in = jnp.concatenate([h.astype(BF16), xx.astype(BF16)], axis=-1).reshape(t, 2 * d)
    proj = _linear(a_in, wcat, out_dtype=BF16, tn=PROJ_WIDTH // 2)
    w2p = jnp.stack([_pad_rows(p['w2'][i], LORA_PAD) for i in range(2)]).astype(BF16)
    a2p = jnp.stack([_pad_rows(p['a2'][i], LORA_PAD) for i in range(2)]).astype(BF16)
    g2p = _pad_rows(p['g2'], GATE_PAD).astype(BF16)
    head_of = jnp.arange(d, dtype=jnp.int32) // RWKV_HEAD
    same_head = head_of[:, None] == head_of[None, :]
    gsum = jnp.where(same_head, 1.0, 0.0).astype(BF16)
    gmean = jnp.where(same_head, 1.0 / RWKV_HEAD, 0.0).astype(BF16)
    zero = jnp.zeros((d,), F32)
    vec = jnp.stack([p['w0'][0], p['w0'][1], p['a0'][0], p['a0'][1], p['k_k'], p['k_a'],
                     vres[0] if vres is not None else zero, zero])
    if vres is not None:
        kk, v, gate, lw, ag, kd = _rwkv_post(proj, v_first.reshape(t, d), w2p, a2p, g2p,
                                             _pad_rows(vres[2], LORA_PAD).astype(BF16), vec, gsum)
    else:
        kk, v, gate, lw, ag, kd = _rwkv_post(proj, None, w2p, a2p, g2p, None, vec, gsum)
    sh3 = lambda a: a.reshape(a.shape[:-2] + (b, l, a.shape[-1]))
    y_fwd, y_rev = _wkv(sh3(proj), sh3(kk), sh3(v), sh3(lw), sh3(ag), sh3(kd), l_ctx)
    vec_o = jnp.stack([p['ln_g'], p['ln_b'], p['r_k'].reshape(d)] + [zero] * 5)
    out = _rwkv_readout(y_fwd.reshape(t, d), y_rev.reshape(t, d), proj, kd, v, gate, gmean, gsum, vec_o, p['w_o'].astype(BF16))
    return out.reshape(b, l, d), v.reshape(b, l, d)


def _filter_body(z_ref, t_ref, w1_ref, b1_ref, w2_ref, b2_ref, w3_ref, b3_ref, fr_ref, wo_ref, dl_ref, o_ref, *, length):
    hp = lax.Precision.HIGHEST
    fr = fr_ref[...]
    hdn = jnp.sin(fr[0:1] * (jnp.dot(z_ref[...], w1_ref[...], precision=hp, preferred_element_type=F32) + b1_ref[...]))
    hdn = jnp.sin(fr[1:2] * (jnp.dot(hdn, w2_ref[...], precision=hp, preferred_element_type=F32) + b2_ref[...]))
    hdn = jnp.sin(fr[2:3] * (jnp.dot(hdn, w3_ref[...], precision=hp, preferred_element_type=F32) + b3_ref[...]))
    filt = jnp.dot(hdn, wo_ref[...], precision=hp, preferred_element_type=F32) * jnp.exp(-t_ref[...] * dl_ref[...])
    tl = filt.shape[0]
    n = pl.program_id(0) * tl + lax.broadcasted_iota(jnp.int32, filt.shape, 0)
    o_ref[...] = jnp.where(n == length, 0.0, filt * (0.5 / length))


def _hyena_filters(length, p):
    n = jnp.arange(2 * length, dtype=jnp.int32)
    pos = jnp.where(n < length, n, 2 * length - n).astype(F32)[:, None]
    t = pos / (length - 1)
    bands = (FILTER_EMB - 1) // 2
    f = jnp.linspace(1e-4, bands - 1, bands, dtype=F32)[None, :]
    ang = 2.0 * math.pi * f * pos / length
    z = jnp.concatenate([t, jnp.cos(ang), -jnp.sin(ang), jnp.zeros((2 * length, FILTER_EMB_PAD - FILTER_EMB), F32)], axis=-1)
    w1 = jnp.concatenate([p['f_w1'], jnp.zeros((FILTER_EMB_PAD - FILTER_EMB, p['f_w1'].shape[1]), F32)], axis=0)
    deltas = jnp.abs(jnp.linspace(FAST_DECAY, SLOW_DECAY, D_MODEL, dtype=F32))[None, :]
    fw = p['f_w2'].shape[0]
    tl = _tile(length, (512, 256, 128, 64, 32, 16, 8))
    half = length // tl
    full = lambda a: pl.BlockSpec(a.shape, lambda i: (0,) * a.ndim)
    args = [w1, p['f_b1'].reshape(1, fw), p['f_w2'], p['f_b2'].reshape(1, fw), p['f_w3'], p['f_b3'].reshape(1, fw),
            p['f_freq']]
    return pl.pallas_call(
        functools.partial(_filter_body, length=length),
        grid=(2 * half,),
        in_specs=[pl.BlockSpec((tl, FILTER_EMB_PAD), lambda i: (i, 0)), pl.BlockSpec((tl, 1), lambda i: (i, 0))]
                 + [full(a) for a in args]
                 + [pl.BlockSpec((fw, D_MODEL), lambda i: (0, jnp.where(i < half, 0, 1))),
                    full(deltas)],
        out_specs=pl.BlockSpec((tl, D_MODEL), lambda i: (i, 0)),
        out_shape=jax.ShapeDtypeStruct((2 * length, D_MODEL), F32),
        compiler_params=_cparams("parallel"),
        name="hyena_filter",
    )(z, t, *args, p['f_wout'], deltas)


def _colmm_body(f_ref, x_ref, o_ref):
    o_ref[0] = jnp.dot(f_ref[...], x_ref[0].astype(BF16), preferred_element_type=F32).astype(o_ref.dtype)


def _colmm(f, x, out_dtype):
    bsz, q, n = x.shape
    pp = f.shape[0]
    tn = _tile(n, (4096, 2048, 1024, 512, 256, 128))
    return pl.pallas_call(
        _colmm_body,
        grid=(bsz, n // tn),
        in_specs=[pl.BlockSpec((pp, q), lambda i, j: (0, 0)),
                  pl.BlockSpec((1, q, tn), lambda i, j: (i, 0, j))],
        out_specs=pl.BlockSpec((1, pp, tn), lambda i, j: (i, 0, j)),
        out_shape=jax.ShapeDtypeStruct((bsz, pp, n), out_dtype),
        compiler_params=_cparams("parallel", "parallel"),
        name="dft_outer",
    )(f.astype(BF16), x)


def _dft_inner_fwd_body(a_ref, m_ref, o_ref):
    n2 = a_ref.shape[3]
    a = jnp.concatenate([a_ref[0, 0, 0], a_ref[0, 1, 0]], axis=0).astype(BF16)
    x = jnp.dot(m_ref[0], a, preferred_element_type=F32)
    o_ref[0, 0, 0] = x[:n2]
    o_ref[0, 1, 0] = x[n2:]


def _dft_inner_conv_body(a_ref, g_ref, m_ref, mi_ref, o_ref):
    n2 = a_ref.shape[3]
    a = jnp.concatenate([a_ref[0, 0, 0], a_ref[0, 1, 0]], axis=0).astype(BF16)
    x = jnp.dot(m_ref[0], a, preferred_element_type=F32)
    xre, xim = x[:n2], x[n2:]
    gre, gim = g_ref[0, 0, 0], g_ref[0, 1, 0]
    y = jnp.concatenate([xre * gre - xim * gim, xre * gim + xim * gre], axis=0).astype(BF16)
    o = jnp.dot(mi_ref[0], y, preferred_element_type=F32)
    o_ref[0, 0, 0] = o[:n2].astype(o_ref.dtype)
    o_ref[0, 1, 0] = o[n2:].astype(o_ref.dtype)


def _dft_tables(n1, n2):
    n = n1 * n2
    i1 = jnp.arange(n1, dtype=jnp.int32)
    ph1 = (2.0 * math.pi / n1) * ((i1[:, None] * i1[None, :]) % n1).astype(F32)
    c1, s1 = jnp.cos(ph1), jnp.sin(ph1)
    k1 = i1[:, None, None]
    k2 = jnp.arange(n2, dtype=jnp.int32)[None, :, None]
    m2 = jnp.arange(n2, dtype=jnp.int32)[None, None, :]
    th = (2.0 * math.pi / n) * ((m2 * (k1 + n1 * k2)) % n).astype(F32)
    mre, mim = jnp.cos(th), -jnp.sin(th)
    m_fwd = jnp.concatenate([jnp.concatenate([mre, -mim], axis=2), jnp.concatenate([mim, mre], axis=2)], axis=1)
    qre, qim = jnp.swapaxes(mre, 1, 2), -jnp.swapaxes(mim, 1, 2)
    m_inv = jnp.concatenate([jnp.concatenate([qre, -qim], axis=2), jnp.concatenate([qim, qre], axis=2)], axis=1)
    f_fwd = jnp.concatenate([c1, -s1], axis=0)
    f_inv = jnp.concatenate([c1, -s1], axis=1)
    return f_fwd, f_inv, m_fwd.astype(BF16), m_inv.astype(BF16)


def _two_sided_long_conv(z, filt):
    b, l, d = z.shape
    n = 2 * l
    n1 = max(DFT_MIN_OUTER, 2 ** ((int(math.log2(n)) - 1) // 2))
    n2 = n // n1
    f_fwd, f_inv, m_fwd, m_inv = _dft_tables(n1, n2)
    blk = pl.BlockSpec((1, 2, 1, n2, d), lambda i, j: (i, 0, j, 0, 0))
    gblk = pl.BlockSpec((1, 2, 1, n2, d), lambda i, j: (0, 0, j, 0, 0))
    mat = pl.BlockSpec((1, 2 * n2, 2 * n2), lambda i, j: (j, 0, 0))
    ga = _colmm(f_fwd, filt.reshape(1, n1, n2 * d), BF16).reshape(1, 2, n1, n2, d)
    gspec = pl.pallas_call(
        _dft_inner_fwd_body,
        grid=(1, n1),
        in_specs=[blk, mat],
        out_specs=blk,
        out_shape=jax.ShapeDtypeStruct((1, 2, n1, n2, d), F32),
        compiler_params=_cparams("parallel", "parallel"),
        name="dft_inner_fwd",
    )(ga, m_fwd)
    za = _colmm(f_fwd[:, :n1 // 2], z.reshape(b, n1 // 2, n2 * d), BF16).reshape(b, 2, n1, n2, d)
    bb = pl.pallas_call(
        _dft_inner_conv_body,
        grid=(b, n1),
        in_specs=[blk, gblk, mat, mat],
        out_specs=blk,
        out_shape=jax.ShapeDtypeStruct((b, 2, n1, n2, d), BF16),
        compiler_params=_cparams("parallel", "parallel"),
        name="dft_inner_conv",
    )(za, gspec, m_fwd, m_inv)
    return _colmm(f_inv[:n1 // 2], bb.reshape(b, 2 * n1, n2 * d), F32).reshape(b, l, d)


def _hyena_mixer(h, l_ctx, p):
    b, l, d = h.shape
    u = _linear(h.reshape(b * l, d).astype(BF16), p['w_in'], p['b_in'], out_dtype=BF16)
    u = u.reshape(b, l, 3 * d).astype(F32)
    prev, nxt = _stream_neighbours(u, l_ctx)
    sw = p['short_w']
    u = prev * sw[0] + u * sw[1] + nxt * sw[2] + p['short_b']
    x0, x1, v = jnp.split(u, 3, axis=-1)
    z = v * x1
    conv = jnp.concatenate([_two_sided_long_conv(z[:, :l_ctx], _hyena_filters(l_ctx, p)),
                            _two_sided_long_conv(z[:, l_ctx:], _hyena_filters(l - l_ctx, p))], axis=1)
    z = conv + z * p['bias']
    return _linear((x0 * z).reshape(b * l, d).astype(BF16), p['w_out'], p['b_out']).reshape(b, l, d)


def _attn_body(q_ref, k_ref, v_ref, o_ref, m_ref, acc_ref, *, dv, parts):
    j = pl.program_id(3)
    tq, tk = q_ref.shape[2], k_ref.shape[2]
    sub = tq // parts

    @pl.when(j == 0)
    def _():
        m_ref[...] = jnp.full_like(m_ref, -jnp.inf)
        acc_ref[...] = jnp.zeros_like(acc_ref)

    k = k_ref[0, 0]
    v = v_ref[0]
    slabs = [pl.ds(i * sub, sub) for i in range(parts)]
    s = [lax.dot_general(q_ref[0, 0, x, :], k, (((1,), (1,)), ((), ())), preferred_element_type=F32) for x in slabs]
    m_prev = [m_ref[x, :] for x in slabs]
    m_new = [jnp.maximum(mp, jnp.max(si, axis=-1, keepdims=True)) for mp, si in zip(m_prev, s)]
    pr = [jnp.exp2(si - jnp.concatenate([mn] * (tk // LANES), axis=1)).astype(BF16) for si, mn in zip(s, m_new)]
    alpha = [jnp.exp2(mp - mn) for mp, mn in zip(m_prev, m_new)]
    pv = [jnp.dot(p_, v, preferred_element_type=F32) for p_ in pr]
    for x, a, o, mn in zip(slabs, alpha, pv, m_new):
        acc_ref[x, :] = jnp.concatenate([a] * (acc_ref.shape[1] // LANES), axis=1) * acc_ref[x, :] + o
        m_ref[x, :] = mn

    @pl.when(j == pl.num_programs(3) - 1)
    def _():
        acc = acc_ref[...]
        o_ref[0] = (acc[:, :dv] / acc[:, dv:dv + 1]).astype(o_ref.dtype)


def _attend(q, k, v_ext):
    b, nh, sq, dq = q.shape
    sk = k.shape[2]
    dv = v_ext.shape[2] // (2 * nh)
    tq = _tile(sq, (2048, 1024, 512, 256, 128, 64, 32, 16, 8))
    tk = _tile(sk, (3328, 1280, 1024, 640, 512, 256, 128))
    parts = max(1, tq // ATTN_SLAB)
    return pl.pallas_call(
        functools.partial(_attn_body, dv=dv, parts=parts),
        grid=(b, nh, sq // tq, sk // tk),
        in_specs=[pl.BlockSpec((1, 1, tq, dq), lambda bi, hi, i, j: (bi, hi, i, 0)),
                  pl.BlockSpec((1, 1, tk, dq), lambda bi, hi, i, j: (bi, hi, j, 0)),
                  pl.BlockSpec((1, tk, 2 * dv), lambda bi, hi, i, j: (bi, j, hi))],
        out_specs=pl.BlockSpec((1, tq, dv), lambda bi, hi, i, j: (bi, i, hi)),
        out_shape=jax.ShapeDtypeStruct((b, sq, nh * dv), BF16),
        scratch_shapes=[pltpu.VMEM((tq, LANES), F32), pltpu.VMEM((tq, 2 * dv), F32)],
        compiler_params=_cparams("parallel", "parallel", "parallel", "arbitrary"),
        name="attention",
    )(q, k, v_ext)


def _mla_prep_body(qkv_ref, wq_ref, wkv_ref, gq_ref, gkv_ref, gqn_ref, gkn_ref, cos_ref, sin_ref, q_out, k_out, v_out):
    x = qkv_ref[0]
    qc, kvc, kpe = x[:, :Q_LORA], x[:, Q_LORA:Q_LORA + KV_LORA], x[:, Q_LORA + KV_LORA:]
    qn = qc * lax.rsqrt(jnp.mean(qc * qc, axis=-1, keepdims=True) + NORM_EPS) * gq_ref[...]
    kvn = kvc * lax.rsqrt(jnp.mean(kvc * kvc, axis=-1, keepdims=True) + NORM_EPS) * gkv_ref[...]
    q = jnp.dot(qn.astype(BF16), wq_ref[...], preferred_element_type=F32)
    kv = jnp.dot(kvn.astype(BF16), wkv_ref[...], preferred_element_type=F32)
    cos, sin = cos_ref[0], sin_ref[0]
    quarter = QK_ROPE // 4

    def rope(t):
        swapped = jnp.concatenate([t[:, quarter:2 * quarter], t[:, :quarter],
                                   t[:, 3 * quarter:], t[:, 2 * quarter:3 * quarter]], axis=-1)
        return t * cos + swapped * sin

    gqn, gkn = gqn_ref[...], gkn_ref[...]
    kpe_rot = rope(kpe * gkn[:, QK_NOPE:])
    kpe_ss = jnp.sum(kpe * kpe, axis=-1, keepdims=True)
    ones_col = jnp.where(lax.broadcasted_iota(jnp.int32, (x.shape[0], V_HEAD), 1) == 0, 1.0, 0.0).astype(BF16)
    for h in range(MLA_HEADS):
        qh = q[:, h * MLA_SLOT:h * MLA_SLOT + QK_HEAD]
        qh = qh * lax.rsqrt(jnp.sum(qh * qh, axis=-1, keepdims=True) * (1.0 / QK_HEAD) + NORM_EPS) * gqn
        qh = jnp.concatenate([qh[:, :QK_NOPE], rope(qh[:, QK_NOPE:])], axis=-1)
        q_out[0, h] = (qh * (QK_HEAD ** -0.5 * LOG2_E)).astype(q_out.dtype)
        kn = kv[:, h * MLA_SLOT:h * MLA_SLOT + QK_NOPE]
        rstd = lax.rsqrt((jnp.sum(kn * kn, axis=-1, keepdims=True) + kpe_ss) * (1.0 / QK_HEAD) + NORM_EPS)
        k_out[0, h] = jnp.concatenate([kn * rstd * gkn[:, :QK_NOPE], kpe_rot * rstd], axis=-1).astype(k_out.dtype)
        v_out[0, :, h * MLA_SLOT:h * MLA_SLOT + V_HEAD] = kv[:, h * MLA_SLOT + QK_NOPE:(h + 1) * MLA_SLOT].astype(v_out.dtype)
        v_out[0, :, h * MLA_SLOT + V_HEAD:(h + 1) * MLA_SLOT] = ones_col


def _mla_prep(qkv, p, cos, sin):
    b, l, w = qkv.shape
    tm = _tile(l, (256, 128, 64, 32, 16))
    nh = MLA_HEADS
    pad_heads = lambda m, width: jnp.pad(m.reshape(m.shape[0], nh, width), ((0, 0), (0, 0), (0, MLA_SLOT - width))
                                         ).reshape(m.shape[0], nh * MLA_SLOT)
    wq = pad_heads(p['w_qb'], QK_HEAD).astype(BF16)
    wkv = p['w_kvb'].astype(BF16)
    full = lambda a: pl.BlockSpec(a.shape, lambda i, j: (0,) * a.ndim)
    args = [wq, wkv, p['q_norm_g'].reshape(1, Q_LORA), p['kv_norm_g'].reshape(1, KV_LORA),
            p['qn_g'].reshape(1, QK_HEAD), p['kn_g'].reshape(1, QK_HEAD)]
    tab = pl.BlockSpec((1, tm, QK_ROPE), lambda i, j: (0, j, 0))
    head = pl.BlockSpec((1, nh, tm, QK_HEAD), lambda i, j: (i, 0, j, 0))
    return pl.pallas_call(
        _mla_prep_body,
        grid=(b, l // tm),
        in_specs=[pl.BlockSpec((1, tm, w), lambda i, j: (i, j, 0))] + [full(a) for a in args] + [tab, tab],
        out_specs=[head, head, pl.BlockSpec((1, tm, nh * MLA_SLOT), lambda i, j: (i, j, 0))],
        out_shape=[jax.ShapeDtypeStruct((b, nh, l, QK_HEAD), BF16), jax.ShapeDtypeStruct((b, nh, l, QK_HEAD), BF16),
                   jax.ShapeDtypeStruct((b, l, nh * MLA_SLOT), BF16)],
        compiler_params=_cparams("parallel", "parallel"),
        name="mla_prep",
    )(qkv, *args, cos, sin)


def _rope_tables(row, col, l_ctx):
    half = QK_ROPE // 2
    inv = ROPE_THETA ** (-jnp.arange(0, half, 2, dtype=F32) / half)
    ang_r = row.astype(F32)[:, None] * inv[None, :]
    ang_c = col.astype(F32)[:, None] * inv[None, :]
    cos = jnp.concatenate([jnp.cos(ang_r)] * 2 + [jnp.cos(ang_c)] * 2, axis=-1)
    sin = jnp.concatenate([-jnp.sin(ang_r), jnp.sin(ang_r), -jnp.sin(ang_c), jnp.sin(ang_c)], axis=-1)
    cos = jnp.concatenate([jnp.ones((l_ctx, QK_ROPE), F32), cos], axis=0)
    sin = jnp.concatenate([jnp.zeros((l_ctx, QK_ROPE), F32), sin], axis=0)
    return cos[None], sin[None]


def _mla_mixer(h, l_ctx, p, rope_pos):
    b, l, d = h.shape
    qkv = _linear(h.reshape(b * l, d).astype(BF16), p['w_a']).reshape(b, l, -1)
    cos, sin = _rope_tables(rope_pos[0], rope_pos[1], l_ctx)
    q, k, v_ext = _mla_prep(qkv, p, cos, sin)
    o = jnp.concatenate([_attend(q[:, :, :l_ctx], k[:, :, :l_ctx], v_ext[:, :l_ctx]),
                         _attend(q[:, :, l_ctx:], k, v_ext)], axis=1)
    return _linear(o.reshape(b * l, MLA_HEADS * V_HEAD), p['w_o']).reshape(b, l, d)


def _router_body(h_ref, w_ref, b_ref, idx_ref, gate_ref, rank_ref, cnt_ref, run_ref):
    @pl.when(pl.program_id(0) == 0)
    def _():
        run_ref[...] = jnp.zeros_like(run_ref)

    lg = jnp.dot(h_ref[...], w_ref[...], precision=lax.Precision.HIGHEST, preferred_element_type=F32) + b_ref[...]
    tm = lg.shape[0]
    lane = lax.broadcasted_iota(jnp.int32, lg.shape, 1).astype(F32)
    idx_acc = jnp.zeros(lg.shape, F32)
    val_acc = jnp.zeros(lg.shape, F32)
    picked = jnp.zeros(lg.shape, F32)
    sels = []
    top = None
    den = None
    for j in range(TOP_K):
        mx = jnp.max(lg, axis=-1, keepdims=True)
        sel = jnp.min(jnp.where(lg == mx, lane, float(ROUTER_LANES)), axis=-1, keepdims=True)
        if j == 0:
            top = mx
        e = jnp.exp(mx - top)
        den = e if den is None else den + e
        hit = lane == sel
        sels.append(hit)
        idx_acc = jnp.where(lane == float(j), sel, idx_acc)
        val_acc = jnp.where(lane == float(j), e, val_acc)
        picked = jnp.where(hit, 1.0, picked)
        lg = jnp.where(hit, -jnp.inf, lg)
    idx_ref[...] = idx_acc.astype(jnp.int32)
    gate_ref[...] = val_acc / den
    rs = lax.broadcasted_iota(jnp.int32, (tm, tm), 0)
    cs = lax.broadcasted_iota(jnp.int32, (tm, tm), 1)
    before = jnp.where(rs > cs, 1.0, 0.0).astype(BF16)
    base = run_ref[...] + jnp.dot(before, picked.astype(BF16), preferred_element_type=F32)
    rank_acc = jnp.zeros(lg.shape, F32)
    for j in range(TOP_K):
        rk = jnp.sum(jnp.where(sels[j], base, 0.0), axis=-1, keepdims=True)
        rank_acc = jnp.where(lane == float(j), rk, rank_acc)
    rank_ref[...] = rank_acc.astype(jnp.int32)
    run_ref[...] = run_ref[...] + jnp.sum(picked, axis=0, keepdims=True)
    cnt_ref[...] = run_ref[...].astype(jnp.int32)


def _router(h, router_w, router_b):
    t, d = h.shape
    tm = _tile(t, (512, 256, 128, 64, 32, 16, 8))
    w = jnp.concatenate([router_w, jnp.zeros((d, ROUTER_LANES - N_EXPERTS), F32)], axis=1)
    bias = jnp.concatenate([router_b, jnp.full((ROUTER_LANES - N_EXPERTS,), -1e30, F32)]).reshape(1, ROUTER_LANES)
    tok = pl.BlockSpec((tm, ROUTER_LANES), lambda i: (i, 0))
    return pl.pallas_call(
        _router_body,
        grid=(t // tm,),
        in_specs=[pl.BlockSpec((tm, d), lambda i: (i, 0)),
                  pl.BlockSpec((d, ROUTER_LANES), lambda i: (0, 0)),
                  pl.BlockSpec((1, ROUTER_LANES), lambda i: (0, 0))],
        out_specs=[tok, tok, tok, pl.BlockSpec((1, ROUTER_LANES), lambda i: (0, 0))],
        out_shape=[jax.ShapeDtypeStruct((t, ROUTER_LANES), jnp.int32), jax.ShapeDtypeStruct((t, ROUTER_LANES), F32),
                   jax.ShapeDtypeStruct((t, ROUTER_LANES), jnp.int32), jax.ShapeDtypeStruct((1, ROUTER_LANES), jnp.int32)],
        scratch_shapes=[pltpu.VMEM((1, ROUTER_LANES), F32)],
        compiler_params=_cparams("arbitrary"),
        name="router",
    )(h, w, bias)


def _expert_body(be_ref, nb_ref, x_ref, win_ref, bin_ref, wout_ref, bout_ref, gate_ref, o_ref, win_bf, wout_bf):
    i = pl.program_id(0)
    fresh = jnp.logical_or(i == 0, be_ref[i] != be_ref[jnp.maximum(i - 1, 0)])

    @pl.when(fresh)
    def _():
        win_bf[...] = win_ref[0].astype(BF16)
        wout_bf[...] = wout_ref[0].astype(BF16)

    @pl.when(i < nb_ref[0])
    def _():
        sub = MOE_ROWS // MOE_SLABS
        slabs = [pl.ds(j * sub, sub) for j in range(MOE_SLABS)]
        gu = [jnp.dot(x_ref[x, :], win_bf[...], preferred_element_type=F32) + bin_ref[0] for x in slabs]
        glu = [jnp.minimum(g[:, :D_EXPERT], SWIGLU_LIMIT) for g in gu]
        lin = [jnp.clip(g[:, D_EXPERT:], -SWIGLU_LIMIT, SWIGLU_LIMIT) for g in gu]
        act = [(a * jax.nn.sigmoid(SWIGLU_ALPHA * a) * (b + 1.0)).astype(BF16) for a, b in zip(glu, lin)]
        y = [jnp.dot(a, wout_bf[...], preferred_element_type=F32) + bout_ref[0] for a in act]
        for x, yy in zip(slabs, y):
            o_ref[x, :] = (yy * gate_ref[x, :]).astype(o_ref.dtype)

    @pl.when(i >= nb_ref[0])
    def _():
        o_ref[...] = jnp.zeros_like(o_ref)


def _moe(h, h_bf, router_w, router_b, w_in, b_in, w_out, b_out):
    t, d = h.shape
    n_assign = t * TOP_K
    idx, gates, ranks, counts = _router(h, router_w, router_b)
    flat_e = idx[:, :TOP_K].reshape(n_assign)
    flat_g = gates[:, :TOP_K].reshape(n_assign)
    rank = ranks[:, :TOP_K].reshape(n_assign)
    counts = counts[0, :N_EXPERTS]
    padded = (counts + MOE_ROWS - 1) // MOE_ROWS * MOE_ROWS
    pad_end = jnp.cumsum(padded)
    pad_start = pad_end - padded
    experts = jnp.arange(N_EXPERTS, dtype=jnp.int32)
    lookup = lambda table, e: jnp.sum(jnp.where(e[:, None] == experts[None, :], table[None, :], 0), axis=1)
    dest = lookup(pad_start, flat_e) + rank
    n_blocks = -(-n_assign // MOE_ROWS) + N_EXPERTS
    n_rows = n_blocks * MOE_ROWS
    block_start = jnp.arange(n_blocks, dtype=jnp.int32) * MOE_ROWS
    block_e = jnp.minimum(jnp.sum((pad_end[None, :] <= block_start[:, None]).astype(jnp.int32), axis=1), N_EXPERTS - 1)
    order = jnp.argsort(flat_e, stable=True).astype(jnp.int32)
    start = jnp.cumsum(counts) - counts
    in_block = jnp.arange(MOE_ROWS, dtype=jnp.int32)[None, :]
    row_rank = (block_start - lookup(pad_start, block_e))[:, None] + in_block
    row_live = (row_rank < lookup(counts, block_e)[:, None]).reshape(n_rows)
    row_src = order[jnp.minimum(lookup(start, block_e)[:, None] + row_rank, n_assign - 1).reshape(n_rows)]
    row_tok = row_src // TOP_K
    row_gate = jnp.where(row_live, flat_g[row_src], 0.0)
    n_used = (pad_end[-1] // MOE_ROWS).astype(jnp.int32).reshape(1)
    xb = h_bf[row_tok]
    f2 = 2 * D_EXPERT
    grid_spec = pltpu.PrefetchScalarGridSpec(
        num_scalar_prefetch=2,
        grid=(n_blocks,),
        in_specs=[pl.BlockSpec((MOE_ROWS, d), lambda i, be, nb: (i, 0)),
                  pl.BlockSpec((1, d, f2), lambda i, be, nb: (be[i], 0, 0)),
                  pl.BlockSpec((1, 1, f2), lambda i, be, nb: (be[i], 0, 0)),
                  pl.BlockSpec((1, D_EXPERT, d), lambda i, be, nb: (be[i], 0, 0)),
                  pl.BlockSpec((1, 1, d), lambda i, be, nb: (be[i], 0, 0)),
                  pl.BlockSpec((MOE_ROWS, 1), lambda i, be, nb: (i, 0))],
        out_specs=pl.BlockSpec((MOE_ROWS, d), lambda i, be, nb: (i, 0)),
        scratch_shapes=[pltpu.VMEM((d, f2), BF16), pltpu.VMEM((D_EXPERT, d), BF16)],
    )
    yb = pl.pallas_call(
        _expert_body,
        grid_spec=grid_spec,
        out_shape=jax.ShapeDtypeStruct((n_rows, d), BF16),
        compiler_params=_cparams("arbitrary"),
        name="moe_experts",
    )(block_e, n_used, xb, w_in, b_in.reshape(N_EXPERTS, 1, f2),
      w_out, b_out.reshape(N_EXPERTS, 1, d), row_gate.reshape(n_rows, 1))
    dest = dest.reshape(t, TOP_K)
    return [yb[dest[:, j]] for j in range(TOP_K)]


def kernel(x, c, ctx, c_ctx, ada_w, ada_b, norm_mix_g, norm_ffn_g, rwkv_mu, rwkv_w_rkv, rwkv_w0, rwkv_w1, rwkv_w2, rwkv_a0, rwkv_a1, rwkv_a2, rwkv_g1, rwkv_g2, rwkv_k_k, rwkv_k_a, rwkv_r_k, rwkv_ln_g, rwkv_ln_b, rwkv_w_o, rwkv_v0, rwkv_v1, rwkv_v2, hy_w_in, hy_b_in, hy_short_w, hy_short_b, hy_f_w1, hy_f_b1, hy_f_w2, hy_f_b2, hy_f_w3, hy_f_b3, hy_f_freq, hy_f_wout, hy_bias, hy_w_out, hy_b_out, mla_w_a, mla_q_norm_g, mla_kv_norm_g, mla_w_qb, mla_w_kvb, mla_qn_g, mla_kn_g, mla_w_o, moe_router_w, moe_router_b, moe_w_in, moe_b_in, moe_w_out, moe_b_out):
    bsz, s_len, d = x.shape
    l_ctx = ctx.shape[1]
    rows = s_len // GRID_W
    row = jnp.repeat(jnp.arange(rows, dtype=jnp.int32), GRID_W)
    col = jnp.tile(jnp.arange(GRID_W, dtype=jnp.int32), rows)
    silu = jnp.concatenate([jax.nn.silu(c), jax.nn.silu(c_ctx)[None, :]], axis=0)
    silu = jnp.concatenate([silu, jnp.zeros((-(bsz + 1) % 8, d), F32)], axis=0)
    is_ctx = (jnp.arange(l_ctx + s_len, dtype=jnp.int32) < l_ctx)[None, :, None]
    per_row = lambda m: jnp.where(is_ctx, m[:, 0:1], m[:, 1:2])
    xs = jnp.concatenate([ctx, x], axis=1)
    mods = []
    for i in range(DEPTH):
        mod = _linear(silu, ada_w[i], ada_b[i], tn=1024)
        mod = jnp.stack([jnp.broadcast_to(mod[bsz:bsz + 1], (bsz, N_MOD * d)), mod[:bsz]], axis=1)
        mods.append(jnp.split(mod, N_MOD, axis=-1))
    v_first = None
    h = _norm_mod(xs, norm_mix_g[0], mods[0][1], mods[0][0], l_ctx, BF16)
    for i in range(DEPTH):
        j = i // N_MIXERS
        kind = i % N_MIXERS
        sh1, sc1, g1, sh2, sc2, g2 = mods[i]
        if kind == 0:
            p = {'mu': rwkv_mu[j], 'w_rkv': rwkv_w_rkv[j], 'w0': rwkv_w0[j], 'w1': rwkv_w1[j],
                 'w2': rwkv_w2[j], 'a0': rwkv_a0[j], 'a1': rwkv_a1[j], 'a2': rwkv_a2[j],
                 'g1': rwkv_g1[j], 'g2': rwkv_g2[j], 'k_k': rwkv_k_k[j], 'k_a': rwkv_k_a[j],
                 'r_k': rwkv_r_k[j], 'ln_g': rwkv_ln_g[j], 'ln_b': rwkv_ln_b[j], 'w_o': rwkv_w_o[j]}
            vres = None if j == 0 else (rwkv_v0[j - 1], rwkv_v1[j - 1], rwkv_v2[j - 1])
            y, v_cur = _rwkv_mixer(h, l_ctx, p, v_first, vres)
            if j == 0:
                v_first = v_cur
        elif kind == 1:
            p = {'w_in': hy_w_in[j], 'b_in': hy_b_in[j], 'short_w': hy_short_w[j], 'short_b': hy_short_b[j],
                 'f_w1': hy_f_w1[j], 'f_b1': hy_f_b1[j], 'f_w2': hy_f_w2[j], 'f_b2': hy_f_b2[j],
                 'f_w3': hy_f_w3[j], 'f_b3': hy_f_b3[j], 'f_freq': hy_f_freq[j], 'f_wout': hy_f_wout[j],
                 'bias': hy_bias[j], 'w_out': hy_w_out[j], 'b_out': hy_b_out[j]}
            y = _hyena_mixer(h, l_ctx, p)
        else:
            p = {'w_a': mla_w_a[j], 'q_norm_g': mla_q_norm_g[j], 'kv_norm_g': mla_kv_norm_g[j],
                 'w_qb': mla_w_qb[j], 'w_kvb': mla_w_kvb[j], 'qn_g': mla_qn_g[j], 'kn_g': mla_kn_g[j],
                 'w_o': mla_w_o[j]}
            y = _mla_mixer(h, l_ctx, p, (row, col))
        xs, h2, h2_bf = _residual_norm(xs, [y], g1, norm_ffn_g[i], sc2, sh2, l_ctx, (F32, BF16))
        moe_p = (moe_router_w[i], moe_router_b[i], moe_w_in[i], moe_b_in[i], moe_w_out[i], moe_b_out[i])
        parts = [f.reshape(xs.shape) for f in _moe(h2.reshape(-1, d), h2_bf.reshape(-1, d), *moe_p)]
        if i + 1 < DEPTH:
            xs, h = _residual_norm(xs, parts, g2, norm_mix_g[i + 1], mods[i + 1][1], mods[i + 1][0], l_ctx, (BF16,))
        else:
            xs = xs + per_row(g2) * sum(f.astype(F32) for f in parts)
    return xs[:, l_ctx:]
```

```python
import functools
import math

import jax
import jax.numpy as jnp
from jax import lax
from jax.experimental import pallas as pl
from jax.experimental.pallas import tpu as pltpu

F32 = jnp.float32
BF16 = jnp.bfloat16

D_MODEL = 1024
DEPTH = 4
GRID_W = 64
N_MIXERS = 3
N_MOD = 6
NORM_EPS = 1e-6
RWKV_HEAD = 64
GN_EPS = 64e-5
WKV_CHUNK = 64
WKV_GROUP = 4
FILTER_EMB = 33
FILTER_EMB_PAD = 128
DFT_MIN_OUTER = 32
FAST_DECAY = math.log(1e-2) / 0.3
SLOW_DECAY = math.log(1e-2) / 1.5
MLA_HEADS = 8
QK_NOPE = 128
QK_ROPE = 64
QK_HEAD = QK_NOPE + QK_ROPE
V_HEAD = 128
Q_LORA = 384
KV_LORA = 256
ROPE_THETA = 10000.0
N_EXPERTS = 32
TOP_K = 4
D_EXPERT = 1024
SWIGLU_LIMIT = 7.0
SWIGLU_ALPHA = 1.702
MOE_ROWS = 512
MOE_SLABS = 2
LANES = 128
ROUTER_LANES = LANES
ATTN_SLAB = 512
LOG2_E = 1.4426950408889634
MLA_SLOT = 256

V7X_VMEM_BYTES = 64 * 1024 * 1024
VMEM_LIMIT = V7X_VMEM_BYTES * 3 // 4


def _cparams(*sem):
    return pltpu.CompilerParams(dimension_semantics=sem, vmem_limit_bytes=VMEM_LIMIT)


def _tile(n, prefs):
    for t in prefs:
        if n % t == 0:
            return t
    return n


def _linear_body(a_ref, w_ref, b_ref, o_ref):
    acc = jnp.dot(a_ref[...].astype(BF16), w_ref[...], preferred_element_type=F32)
    o_ref[...] = (acc + b_ref[...]).astype(o_ref.dtype)


def _linear(a, w, b=None, *, out_dtype=F32, tn=None):
    m, k = a.shape
    n = w.shape[1]
    tm = _tile(m, (512, 256, 128, 64, 32, 16, 8))
    tn = n if tn is None else tn
    assert n % tn == 0
    if b is None:
        b = jnp.zeros((n,), F32)
    return pl.pallas_call(
        _linear_body,
        grid=(n // tn, m // tm),
        in_specs=[pl.BlockSpec((tm, k), lambda j, i: (i, 0)),
                  pl.BlockSpec((k, tn), lambda j, i: (0, j)),
                  pl.BlockSpec((1, tn), lambda j, i: (0, j))],
        out_specs=pl.BlockSpec((tm, tn), lambda j, i: (i, j)),
        out_shape=jax.ShapeDtypeStruct((m, n), out_dtype),
        compiler_params=_cparams("parallel", "parallel"),
        name="linear",
    )(a, w.astype(BF16), b.reshape(1, n).astype(F32))


def _norm_mod_body(x_ref, g_ref, sc_ref, sh_ref, o_ref):
    x = x_ref[0]
    y = x * lax.rsqrt(jnp.mean(x * x, axis=-1, keepdims=True) + NORM_EPS)
    o_ref[0] = ((y * g_ref[...]) * (1.0 + sc_ref[0, 0]) + sh_ref[0, 0]).astype(o_ref.dtype)


def _norm_mod(x, g, sc, sh, l_ctx, out_dtype):
    b, l, d = x.shape
    tl = _tile(math.gcd(l_ctx, l - l_ctx), (512, 256, 128, 64, 32, 16, 8))
    stream = lambda i, j: (i, jnp.where(j * tl < l_ctx, 0, 1), 0, 0)
    return pl.pallas_call(
        _norm_mod_body,
        grid=(b, l // tl),
        in_specs=[pl.BlockSpec((1, tl, d), lambda i, j: (i, j, 0)),
                  pl.BlockSpec((1, d), lambda i, j: (0, 0)),
                  pl.BlockSpec((1, 1, 1, d), stream),
                  pl.BlockSpec((1, 1, 1, d), stream)],
        out_specs=pl.BlockSpec((1, tl, d), lambda i, j: (i, j, 0)),
        out_shape=jax.ShapeDtypeStruct((b, l, d), out_dtype),
        compiler_params=_cparams("parallel", "parallel"),
        name="norm_mod",
    )(x, g.reshape(1, d), sc.reshape(b, 2, 1, d), sh.reshape(b, 2, 1, d))


def _residual_norm_body(*refs, n_add):
    x_ref, adds, (gate_ref, g_ref, sc_ref, sh_ref, xo_ref), h_refs = refs[0], refs[1:1 + n_add], refs[1 + n_add:6 + n_add], refs[6 + n_add:]
    y = adds[0][0].astype(F32)
    for a in adds[1:]:
        y = y + a[0].astype(F32)
    x = x_ref[0] + gate_ref[0, 0] * y
    xo_ref[0] = x
    nrm = x * lax.rsqrt(jnp.mean(x * x, axis=-1, keepdims=True) + NORM_EPS)
    h = (nrm * g_ref[...]) * (1.0 + sc_ref[0, 0]) + sh_ref[0, 0]
    for h_ref in h_refs:
        h_ref[0] = h.astype(h_ref.dtype)


def _residual_norm(x, adds, gate, g, sc, sh, l_ctx, h_dtypes):
    b, l, d = x.shape
    tl = _tile(math.gcd(l_ctx, l - l_ctx), (512, 256, 128, 64, 32, 16, 8))
    stream = pl.BlockSpec((1, 1, 1, d), lambda i, j: (i, jnp.where(j * tl < l_ctx, 0, 1), 0, 0))
    rows = pl.BlockSpec((1, tl, d), lambda i, j: (i, j, 0))
    return pl.pallas_call(
        functools.partial(_residual_norm_body, n_add=len(adds)),
        grid=(b, l // tl),
        in_specs=[rows] * (1 + len(adds)) + [stream, pl.BlockSpec((1, d), lambda i, j: (0, 0)), stream, stream],
        out_specs=[rows] * (1 + len(h_dtypes)),
        out_shape=[jax.ShapeDtypeStruct((b, l, d), F32)] + [jax.ShapeDtypeStruct((b, l, d), t) for t in h_dtypes],
        compiler_params=_cparams("parallel", "parallel"),
        name="residual_norm",
    )(x, *adds, gate.reshape(b, 2, 1, d), g.reshape(1, d), sc.reshape(b, 2, 1, d), sh.reshape(b, 2, 1, d))


def _stream_neighbours(t, l_ctx):
    zero = jnp.zeros_like(t[:, :1])
    pos = jnp.arange(t.shape[1], dtype=jnp.int32)[None, :, None]
    prev = jnp.where(pos == l_ctx, 0.0, jnp.concatenate([zero, t[:, :-1]], axis=1))
    nxt = jnp.where(pos == l_ctx - 1, 0.0, jnp.concatenate([t[:, 1:], zero], axis=1))
    return prev, nxt


PROJ_K, PROJ_V = 1024, 2048
PROJ_W1 = (3072, 3200)
PROJ_A1 = (3328, 3456)
PROJ_G1 = 3584
PROJ_V1 = 3840
PROJ_WIDTH = 4096
LORA_PAD = 128
GATE_PAD = 256


def _sigmoid(x):
    return jax.nn.sigmoid(x)


def _rwkv_post_body(*refs, has_vres):
    if has_vres:
        (proj_ref, vf_ref, w2_ref, a2_ref, g2_ref, v2_ref, vec_ref, gs_ref,
         kk_out, v_out, gate_out, lw_out, ag_out, kd_out) = refs
    else:
        (proj_ref, w2_ref, a2_ref, g2_ref, vec_ref, gs_ref,
         kk_out, v_out, gate_out, lw_out, ag_out, kd_out) = refs
    d = D_MODEL
    vec = vec_ref[...]
    k = proj_ref[:, PROJ_K:PROJ_K + d].astype(F32)
    v = proj_ref[:, PROJ_V:PROJ_V + d].astype(F32)
    if has_vres:
        v1 = proj_ref[:, PROJ_V1:PROJ_V1 + LORA_PAD]
        vgate = _sigmoid(vec[6:7] + jnp.dot(v1, v2_ref[...], preferred_element_type=F32))
        v = v + (vf_ref[...].astype(F32) - v) * vgate
    v_out[...] = v.astype(v_out.dtype)
    kraw = k * vec[4:5]
    ss = jnp.dot((kraw * kraw).astype(BF16), gs_ref[...], preferred_element_type=F32)
    kk_out[...] = (kraw / jnp.maximum(jnp.sqrt(ss), 1e-12)).astype(kk_out.dtype)
    g1 = proj_ref[:, PROJ_G1:PROJ_G1 + GATE_PAD].astype(F32)
    gate_out[...] = jnp.dot(_sigmoid(g1).astype(BF16), g2_ref[...], preferred_element_type=F32).astype(gate_out.dtype)
    for dd in range(2):
        w1 = proj_ref[:, PROJ_W1[dd]:PROJ_W1[dd] + LORA_PAD].astype(F32)
        z = vec[dd:dd + 1] + jnp.dot(jnp.tanh(w1).astype(BF16), w2_ref[dd], preferred_element_type=F32)
        softplus = jnp.maximum(-z, 0.0) + jnp.log(1.0 + jnp.exp(-jnp.abs(z)))
        lw_out[dd] = -jnp.exp(-softplus - 0.5)
        a1 = proj_ref[:, PROJ_A1[dd]:PROJ_A1[dd] + LORA_PAD]
        ag = _sigmoid(vec[2 + dd:3 + dd] + jnp.dot(a1, a2_ref[dd], preferred_element_type=F32))
        ag_out[dd] = ag.astype(ag_out.dtype)
        kd_out[dd] = (k * (1.0 + (ag - 1.0) * vec[5:6])).astype(kd_out.dtype)


def _rwkv_post(proj, v_first, w2p, a2p, g2p, v2p, vec, gsum):
    t, d = proj.shape[0], D_MODEL
    tm = _tile(t, (256, 128, 64, 32, 16))
    has_vres = v_first is not None
    full = lambda a: pl.BlockSpec(a.shape, lambda i: (0,) * a.ndim)
    row = pl.BlockSpec((tm, d), lambda i: (i, 0))
    row2 = pl.BlockSpec((2, tm, d), lambda i: (0, i, 0))
    ins = [proj] + ([v_first] if has_vres else []) + [w2p, a2p, g2p] + ([v2p] if has_vres else []) + [vec, gsum]
    in_specs = ([pl.BlockSpec((tm, PROJ_WIDTH), lambda i: (i, 0))] + ([row] if has_vres else [])
                + [full(a) for a in ins[(2 if has_vres else 1):]])
    return pl.pallas_call(
        functools.partial(_rwkv_post_body, has_vres=has_vres),
        grid=(t // tm,),
        in_specs=in_specs,
        out_specs=[row, row, row, row2, row2, row2],
        out_shape=[jax.ShapeDtypeStruct((t, d), BF16),
                   jax.ShapeDtypeStruct((t, d), BF16),
                   jax.ShapeDtypeStruct((t, d), BF16),
                   jax.ShapeDtypeStruct((2, t, d), F32),
                   jax.ShapeDtypeStruct((2, t, d), BF16),
                   jax.ShapeDtypeStruct((2, t, d), BF16)],
        compiler_params=_cparams("parallel"),
        name="rwkv_post",
    )(*ins)


def _head_groups(y):
    gw = WKV_GROUP * RWKV_HEAD
    lane_head = jnp.right_shift(lax.broadcasted_iota(jnp.int32, (y.shape[0], gw), 1), 6)
    out = []
    for g in range(D_MODEL // gw):
        yg = y[:, g * gw:(g + 1) * gw].astype(BF16)
        out.append(jnp.concatenate([jnp.where(lane_head == h, yg, jnp.zeros_like(yg)) for h in range(WKV_GROUP)], axis=0))
    return out


def _gmm(x, ybd):
    gw = WKV_GROUP * RWKV_HEAD
    return jnp.concatenate([jnp.dot(x[:, g * gw:(g + 1) * gw].astype(BF16), yb, preferred_element_type=F32)
                            for g, yb in enumerate(ybd)], axis=-1)


def _gmm_nt(x, ybd):
    gw = WKV_GROUP * RWKV_HEAD
    return jnp.concatenate([lax.dot_general(x[:, g * gw:(g + 1) * gw].astype(BF16), yb, (((1,), (1,)), ((), ())),
                                            preferred_element_type=F32)
                            for g, yb in enumerate(ybd)], axis=-1)


def _wkv_chunks(r, kk, v, lw, ag, kd, s0, rev):
    each = lambda f, *ls: [f(*a) for a in zip(*ls)]
    c, d = r[0].shape
    row0 = lax.broadcasted_iota(jnp.int32, (c, d), 0)
    col0 = jnp.bitwise_and(lax.broadcasted_iota(jnp.int32, (c, d), 1), RWKV_HEAD - 1)
    incl = [(row0 <= col0) if rv else (row0 >= col0) for rv in rev]
    strict = [(row0 < col0) if rv else (row0 > col0) for rv in rev]
    rs = lax.broadcasted_iota(jnp.int32, (c, c), 0)
    cs = lax.broadcasted_iota(jnp.int32, (c, c), 1)
    tri = [jnp.where((rs <= cs) if rv else (rs >= cs), 1.0, 0.0).astype(BF16) for rv in rev]
    hi = each(lambda x: x.astype(BF16), lw)
    rem = each(lambda x, h: x - h.astype(F32), lw, hi)
    mid = each(lambda x: x.astype(BF16), rem)
    lo = each(lambda x, m: (x - m.astype(F32)).astype(BF16), rem, mid)
    cum = each(lambda t, a, b_, c_: (jnp.dot(t, a, preferred_element_type=F32) + jnp.dot(t, b_, preferred_element_type=F32)
                                     + jnp.dot(t, c_, preferred_element_type=F32)), tri, hi, mid, lo)
    cum_end = [x[0:1, :] if rv else x[c - 1:c, :] for x, rv in zip(cum, rev)]
    g_rest = each(lambda e, x: jnp.exp(e - x), cum_end, cum)
    g_inv = each(lambda x: jnp.exp(-x), cum)
    bvec = each(lambda a, b_: a * b_, kk, ag)
    rt = each(lambda a, x: a * jnp.exp(x), r, cum)
    kt = each(lambda a, g: a * g, kd, g_inv)
    bt = each(lambda a, g: a * g, bvec, g_inv)
    at = each(lambda a, x, w: -a * jnp.exp(x - w), kk, cum, lw)
    ar = each(lambda a, b_: jnp.concatenate([a, b_], axis=0), at, rt)
    ab = each(lambda a, b_: _gmm_nt(a, _head_groups(b_)), ar, bt)
    ak = each(lambda a, b_: _gmm_nt(a, _head_groups(b_)), ar, kt)
    a_ab = each(lambda m, x: jnp.where(m, x[:c], 0.0), strict, ab)
    a_rb = each(lambda m, x: jnp.where(m, x[c:], 0.0), incl, ab)
    a_ak = each(lambda m, x: jnp.where(m, x[:c], 0.0), strict, ak)
    a_rk = each(lambda m, x: jnp.where(m, x[c:], 0.0), incl, ak)
    same16 = jnp.right_shift(row0, 4) == jnp.right_shift(col0, 4)
    same32 = jnp.right_shift(row0, 5) == jnp.right_shift(col0, 5)
    eye = jnp.where(row0 == col0, 1.0, 0.0)
    npow = each(lambda x: jnp.where(same16, x, 0.0), a_ab)
    p = each(lambda x: eye + x, npow)
    for _ in range(3):
        npow = each(lambda x: _gmm(x, _head_groups(x)), npow)
        p = each(lambda x, y: x + _gmm(x, _head_groups(y)), p, npow)
    off32 = jnp.logical_and(same32, jnp.logical_not(same16))
    t1 = each(lambda x, y: _gmm(jnp.where(off32, x, 0.0), _head_groups(y)), a_ab, p)
    p = each(lambda x, y: x + _gmm(x, _head_groups(y)), p, t1)
    t2 = each(lambda x, y: _gmm(jnp.where(same32, 0.0, x), _head_groups(y)), a_ab, p)
    p = each(lambda x, y: x + _gmm(x, _head_groups(y)), p, t2)
    pbd = each(_head_groups, p)
    vbd = each(_head_groups, v)
    wmat = each(lambda x, y: _gmm(x, _head_groups(y)), p, at)
    vt = each(lambda y: _gmm_nt(eye, y), vbd)
    xt = each(lambda x, y: _gmm_nt(x, _head_groups(y)), vt, a_ak)
    upt = each(_gmm_nt, xt, pbd)
    yp = each(_gmm, a_rk, vbd)
    sp = each(lambda x, a, g: _gmm(x, _head_groups(a * g)), vt, kd, g_rest)
    ut = each(lambda s_, w, u: _gmm_nt(s_, _head_groups(w)) + u, s0, wmat, upt)
    ys = each(lambda s_, q, x: _gmm_nt(q, _head_groups(s_)) + x, s0, rt, yp)
    y = each(lambda y0, a, u: y0 + _gmm_nt(a, _head_groups(u)), ys, a_rb, ut)
    s_end = each(lambda s_, e, u, b_, g, x: s_ * jnp.exp(e) + _gmm(u, _head_groups(b_ * g)) + x,
                 s0, cum_end, ut, bvec, g_rest, sp)
    return y, s_end


def _wkv_body(*refs, n_batch):
    ins, outs, s_ref = refs[:12], refs[12:14], refs[14]

    @pl.when(pl.program_id(0) == 0)
    def _():
        s_ref[...] = jnp.zeros_like(s_ref)

    scans = [(dd, bi) for dd in range(2) for bi in range(n_batch)]
    get = lambda k, lead: [ins[6 * dd + k][(0, bi) if lead else bi].astype(F32) for dd, bi in scans]
    y, s_end = _wkv_chunks(get(0, False), get(1, False), get(2, False), get(3, True), get(4, True), get(5, True),
                           [s_ref[i] for i in range(len(scans))], [dd == 1 for dd, _ in scans])
    for i, (dd, bi) in enumerate(scans):
        outs[dd][bi] = y[i]
        s_ref[i] = s_end[i]


def _wkv(proj, kk, v, lw, ag, kd, l_ctx):
    b, l, d = kk.shape
    c = WKV_CHUNK
    nc, nctx = l // c, l_ctx // c
    assert c == RWKV_HEAD and l % c == 0 and l_ctx % c == 0

    def fwd(j):
        return j

    def bwd(j):
        return jnp.where(j < nctx, nctx - 1 - j, nc + nctx - 1 - j)

    in_specs, out_specs = [], []
    for dd, ch in enumerate((fwd, bwd)):
        shared = pl.BlockSpec((b, c, d), lambda j, ch=ch: (0, ch(j), 0))
        per_dir = pl.BlockSpec((1, b, c, d), lambda j, dd=dd, ch=ch: (dd, 0, ch(j), 0))
        in_specs += [shared, shared, shared, per_dir, per_dir, per_dir]
        out_specs.append(shared)
    return pl.pallas_call(
        functools.partial(_wkv_body, n_batch=b),
        grid=(nc,),
        in_specs=in_specs,
        out_specs=out_specs,
        out_shape=[jax.ShapeDtypeStruct((b, l, d), F32)] * 2,
        scratch_shapes=[pltpu.VMEM((2 * b, RWKV_HEAD, d), F32)],
        compiler_params=_cparams("arbitrary"),
        name="wkv",
    )(*([proj, kk, v, lw, ag, kd] * 2))


def _rwkv_readout_body(yf_ref, yr_ref, r_ref, kd_ref, v_ref, gate_ref, gm_ref, gs_ref, vec_ref, wo_ref, o_ref):
    vec = vec_ref[...]
    y = yf_ref[...] + yr_ref[...]
    gm = gm_ref[...]
    yh = y.astype(BF16)
    yl = (y - yh.astype(F32)).astype(BF16)
    mean = jnp.dot(yh, gm, preferred_element_type=F32) + jnp.dot(yl, gm, preferred_element_type=F32)
    yc = y - mean
    var = jnp.dot((yc * yc).astype(BF16), gm, preferred_element_type=F32)
    yn = yc * lax.rsqrt(var + GN_EPS) * vec[0:1] + vec[1:2]
    r = r_ref[...].astype(F32)
    kds = kd_ref[0].astype(F32) + kd_ref[1].astype(F32)
    bonus = jnp.dot((r * kds * vec[2:3]).astype(BF16), gs_ref[...], preferred_element_type=F32)
    out = (yn + bonus * v_ref[...].astype(F32)) * gate_ref[...].astype(F32)
    o_ref[...] = jnp.dot(out.astype(BF16), wo_ref[...], preferred_element_type=F32)


def _rwkv_readout(y_fwd, y_rev, proj, kd, v, gate, gmean, gsum, vec, w_o):
    t, d = v.shape
    tm = _tile(t, (256, 128, 64, 32, 16))
    full = lambda a: pl.BlockSpec(a.shape, lambda i: (0,) * a.ndim)
    row = pl.BlockSpec((tm, d), lambda i: (i, 0))
    row2 = pl.BlockSpec((2, tm, d), lambda i: (0, i, 0))
    return pl.pallas_call(
        _rwkv_readout_body,
        grid=(t // tm,),
        in_specs=[row, row, row, row2, row, row, full(gmean), full(gsum), full(vec), full(w_o)],
        out_specs=row,
        out_shape=jax.ShapeDtypeStruct((t, d), F32),
        compiler_params=_cparams("parallel"),
        name="rwkv_readout",
    )(y_fwd, y_rev, proj, kd, v, gate, gmean, gsum, vec, w_o)


def _pad_cols(w, width):
    return jnp.concatenate([w, jnp.zeros((w.shape[0], width - w.shape[1]), w.dtype)], axis=1)


def _pad_rows(w, height):
    return jnp.concatenate([w, jnp.zeros((height - w.shape[0], w.shape[1]), w.dtype)], axis=0)


def _rwkv_mixer(h, l_ctx, p, v_first, vres):
    b, l, d = h.shape
    t = b * l
    h32 = h.astype(F32)
    prev, nxt = _stream_neighbours(h32, l_ctx)
    xx = 0.5 * (prev + nxt) - h32
    mu = p['mu']
    mats = [(p['w_rkv'][0], 0, d), (p['w_rkv'][1], 2, d), (p['w_rkv'][2], 3, d),
            (p['w1'][0], 1, LORA_PAD), (p['w1'][1], 1, LORA_PAD), (p['a1'][0], 4, LORA_PAD), (p['a1'][1], 4, LORA_PAD),
            (p['g1'], 5, GATE_PAD), (vres[1] if vres is not None else jnp.zeros((d, 1), F32), 3, 2 * LORA_PAD)]
    top = jnp.concatenate([_pad_cols(w, width) for w, _, width in mats], axis=1)
    bot = jnp.concatenate([_pad_cols(mu[m][:, None] * w, width) for w, m, width in mats], axis=1)
    wcat = jnp.concatenate([top, bot], axis=0)
    assert wcat.shape[1] == PROJ_WIDTH
    a_in = jnp.concatenate([h.astype(BF16), xx.astype(BF16)], axis=-1).reshape(t, 2 * d)
    proj = _linear(a_in, wcat, out_dtype=BF16, tn=PROJ_WIDTH // 2)
    w2p = jnp.stack([_pad_rows(p['w2'][i], LORA_PAD) for i in range(2)]).astype(BF16)
    a2p = jnp.stack([_pad_rows(p['a2'][i], LORA_PAD) for i in range(2)]).astype(BF16)
    g2p = _pad_rows(p['g2'], GATE_PAD).astype(BF16)
    head_of = jnp.arange(d, dtype=jnp.int32) // RWKV_HEAD
    same_head = head_of[:, None] == head_of[None, :]
    gsum = jnp.where(same_head, 1.0, 0.0).astype(BF16)
    gmean = jnp.where(same_head, 1.0 / RWKV_HEAD, 0.0).astype(BF16)
    zero = jnp.zeros((d,), F32)
    vec = jnp.stack([p['w0'][0], p['w0'][1], p['a0'][0], p['a0'][1], p['k_k'], p['k_a'],
                     vres[0] if vres is not None else zero, zero])
    if vres is not None:
        kk, v, gate, lw, ag, kd = _rwkv_post(proj, v_first.reshape(t, d), w2p, a2p, g2p,
                                             _pad_rows(vres[2], LORA_PAD).astype(BF16), vec, gsum)
    else:
        kk, v, gate, lw, ag, kd = _rwkv_post(proj, None, w2p, a2p, g2p, None, vec, gsum)
    sh3 = lambda a: a.reshape(a.shape[:-2] + (b, l, a.shape[-1]))
    y_fwd, y_rev = _wkv(sh3(proj), sh3(kk), sh3(v), sh3(lw), sh3(ag), sh3(kd), l_ctx)
    vec_o = jnp.stack([p['ln_g'], p['ln_b'], p['r_k'].reshape(d)] + [zero] * 5)
    out = _rwkv_readout(y_fwd.reshape(t, d), y_rev.reshape(t, d), proj, kd, v, gate, gmean, gsum, vec_o, p['w_o'].astype(BF16))
    return out.reshape(b, l, d), v.reshape(b, l, d)


def _filter_body(z_ref, t_ref, w1_ref, b1_ref, w2_ref, b2_ref, w3_ref, b3_ref, fr_ref, wo_ref, dl_ref, o_ref, *, length):
    hp = lax.Precision.HIGHEST
    fr = fr_ref[...]
    hdn = jnp.sin(fr[0:1] * (jnp.dot(z_ref[...], w1_ref[...], precision=hp, preferred_element_type=F32) + b1_ref[...]))
    hdn = jnp.sin(fr[1:2] * (jnp.dot(hdn, w2_ref[...], precision=hp, preferred_element_type=F32) + b2_ref[...]))
    hdn = jnp.sin(fr[2:3] * (jnp.dot(hdn, w3_ref[...], precision=hp, preferred_element_type=F32) + b3_ref[...]))
    filt = jnp.dot(hdn, wo_ref[...], precision=hp, preferred_element_type=F32) * jnp.exp(-t_ref[...] * dl_ref[...])
    tl = filt.shape[0]
    n = pl.program_id(0) * tl + lax.broadcasted_iota(jnp.int32, filt.shape, 0)
    o_ref[...] = jnp.where(n == length, 0.0, filt * (0.5 / length))


def _hyena_filters(length, p):
    n = jnp.arange(2 * length, dtype=jnp.int32)
    pos = jnp.where(n < length, n, 2 * length - n).astype(F32)[:, None]
    t = pos / (length - 1)
    bands = (FILTER_EMB - 1) // 2
    f = jnp.linspace(1e-4, bands - 1, bands, dtype=F32)[None, :]
    ang = 2.0 * math.pi * f * pos / length
    z = jnp.concatenate([t, jnp.cos(ang), -jnp.sin(ang), jnp.zeros((2 * length, FILTER_EMB_PAD - FILTER_EMB), F32)], axis=-1)
    w1 = jnp.concatenate([p['f_w1'], jnp.zeros((FILTER_EMB_PAD - FILTER_EMB, p['f_w1'].shape[1]), F32)], axis=0)
    deltas = jnp.abs(jnp.linspace(FAST_DECAY, SLOW_DECAY, D_MODEL, dtype=F32))[None, :]
    fw = p['f_w2'].shape[0]
    tl = _tile(length, (512, 256, 128, 64, 32, 16, 8))
    half = length // tl
    full = lambda a: pl.BlockSpec(a.shape, lambda i: (0,) * a.ndim)
    args = [w1, p['f_b1'].reshape(1, fw), p['f_w2'], p['f_b2'].reshape(1, fw), p['f_w3'], p['f_b3'].reshape(1, fw),
            p['f_freq']]
    return pl.pallas_call(
        functools.partial(_filter_body, length=length),
        grid=(2 * half,),
        in_specs=[pl.BlockSpec((tl, FILTER_EMB_PAD), lambda i: (i, 0)), pl.BlockSpec((tl, 1), lambda i: (i, 0))]
                 + [full(a) for a in args]
                 + [pl.BlockSpec((fw, D_MODEL), lambda i: (0, jnp.where(i < half, 0, 1))),
                    full(deltas)],
        out_specs=pl.BlockSpec((tl, D_MODEL), lambda i: (i, 0)),
        out_shape=jax.ShapeDtypeStruct((2 * length, D_MODEL), F32),
        compiler_params=_cparams("parallel"),
        name="hyena_filter",
    )(z, t, *args, p['f_wout'], deltas)


def _colmm_body(f_ref, x_ref, o_ref):
    o_ref[0] = jnp.dot(f_ref[...], x_ref[0].astype(BF16), preferred_element_type=F32).astype(o_ref.dtype)


def _colmm(f, x, out_dtype):
    bsz, q, n = x.shape
    pp = f.shape[0]
    tn = _tile(n, (4096, 2048, 1024, 512, 256, 128))
    return pl.pallas_call(
        _colmm_body,
        grid=(bsz, n // tn),
        in_specs=[pl.BlockSpec((pp, q), lambda i, j: (0, 0)),
                  pl.BlockSpec((1, q, tn), lambda i, j: (i, 0, j))],
        out_specs=pl.BlockSpec((1, pp, tn), lambda i, j: (i, 0, j)),
        out_shape=jax.ShapeDtypeStruct((bsz, pp, n), out_dtype),
        compiler_params=_cparams("parallel", "parallel"),
        name="dft_outer",
    )(f.astype(BF16), x)


def _dft_inner_fwd_body(a_ref, m_ref, o_ref):
    n2 = a_ref.shape[3]
    a = jnp.concatenate([a_ref[0, 0, 0], a_ref[0, 1, 0]], axis=0).astype(BF16)
    x = jnp.dot(m_ref[0], a, preferred_element_type=F32)
    o_ref[0, 0, 0] = x[:n2]
    o_ref[0, 1, 0] = x[n2:]


def _dft_inner_conv_body(a_ref, g_ref, m_ref, mi_ref, o_ref):
    n2 = a_ref.shape[3]
    a = jnp.concatenate([a_ref[0, 0, 0], a_ref[0, 1, 0]], axis=0).astype(BF16)
    x = jnp.dot(m_ref[0], a, preferred_element_type=F32)
    xre, xim = x[:n2], x[n2:]
    gre, gim = g_ref[0, 0, 0], g_ref[0, 1, 0]
    y = jnp.concatenate([xre * gre - xim * gim, xre * gim + xim * gre], axis=0).astype(BF16)
    o = jnp.dot(mi_ref[0], y, preferred_element_type=F32)
    o_ref[0, 0, 0] = o[:n2].astype(o_ref.dtype)
    o_ref[0, 1, 0] = o[n2:].astype(o_ref.dtype)


def _dft_tables(n1, n2):
    n = n1 * n2
    i1 = jnp.arange(n1, dtype=jnp.int32)
    ph1 = (2.0 * math.pi / n1) * ((i1[:, None] * i1[None, :]) % n1).astype(F32)
    c1, s1 = jnp.cos(ph1), jnp.sin(ph1)
    k1 = i1[:, None, None]
    k2 = jnp.arange(n2, dtype=jnp.int32)[None, :, None]
    m2 = jnp.arange(n2, dtype=jnp.int32)[None, None, :]
    th = (2.0 * math.pi / n) * ((m2 * (k1 + n1 * k2)) % n).astype(F32)
    mre, mim = jnp.cos(th), -jnp.sin(th)
    m_fwd = jnp.concatenate([jnp.concatenate([mre, -mim], axis=2), jnp.concatenate([mim, mre], axis=2)], axis=1)
    qre, qim = jnp.swapaxes(mre, 1, 2), -jnp.swapaxes(mim, 1, 2)
    m_inv = jnp.concatenate([jnp.concatenate([qre, -qim], axis=2), jnp.concatenate([qim, qre], axis=2)], axis=1)
    f_fwd = jnp.concatenate([c1, -s1], axis=0)
    f_inv = jnp.concatenate([c1, -s1], axis=1)
    return f_fwd, f_inv, m_fwd.astype(BF16), m_inv.astype(BF16)


def _two_sided_long_conv(z, filt):
    b, l, d = z.shape
    n = 2 * l
    n1 = max(DFT_MIN_OUTER, 2 ** ((int(math.log2(n)) - 1) // 2))
    n2 = n // n1
    f_fwd, f_inv, m_fwd, m_inv = _dft_tables(n1, n2)
    blk = pl.BlockSpec((1, 2, 1, n2, d), lambda i, j: (i, 0, j, 0, 0))
    gblk = pl.BlockSpec((1, 2, 1, n2, d), lambda i, j: (0, 0, j, 0, 0))
    mat = pl.BlockSpec((1, 2 * n2, 2 * n2), lambda i, j: (j, 0, 0))
    ga = _colmm(f_fwd, filt.reshape(1, n1, n2 * d), BF16).reshape(1, 2, n1, n2, d)
    gspec = pl.pallas_call(
        _dft_inner_fwd_body,
        grid=(1, n1),
        in_specs=[blk, mat],
        out_specs=blk,
        out_shape=jax.ShapeDtypeStruct((1, 2, n1, n2, d), F32),
        compiler_params=_cparams("parallel", "parallel"),
        name="dft_inner_fwd",
    )(ga, m_fwd)
    za = _colmm(f_fwd[:, :n1 // 2], z.reshape(b, n1 // 2, n2 * d), BF16).reshape(b, 2, n1, n2, d)
    bb = pl.pallas_call(
        _dft_inner_conv_body,
        grid=(b, n1),
        in_specs=[blk, gblk, mat, mat],
        out_specs=blk,
        out_shape=jax.ShapeDtypeStruct((b, 2, n1, n2, d), BF16),
        compiler_params=_cparams("parallel", "parallel"),
        name="dft_inner_conv",
    )(za, gspec, m_fwd, m_inv)
    return _colmm(f_inv[:n1 // 2], bb.reshape(b, 2 * n1, n2 * d), F32).reshape(b, l, d)


def _hyena_mixer(h, l_ctx, p):
    b, l, d = h.shape
    u = _linear(h.reshape(b * l, d).astype(BF16), p['w_in'], p['b_in'], out_dtype=BF16)
    u = u.reshape(b, l, 3 * d).astype(F32)
    prev, nxt = _stream_neighbours(u, l_ctx)
    sw = p['short_w']
    u = prev * sw[0] + u * sw[1] + nxt * sw[2] + p['short_b']
    x0, x1, v = jnp.split(u, 3, axis=-1)
    z = v * x1
    conv = jnp.concatenate([_two_sided_long_conv(z[:, :l_ctx], _hyena_filters(l_ctx, p)),
                            _two_sided_long_conv(z[:, l_ctx:], _hyena_filters(l - l_ctx, p))], axis=1)
    z = conv + z * p['bias']
    return _linear((x0 * z).reshape(b * l, d).astype(BF16), p['w_out'], p['b_out']).reshape(b, l, d)


def _attn_body(q_ref, k_ref, v_ref, o_ref, m_ref, acc_ref, *, dv, parts):
    j = pl.program_id(3)
    tq, tk = q_ref.shape[2], k_ref.shape[2]
    sub = tq // parts

    @pl.when(j == 0)
    def _():
        m_ref[...] = jnp.full_like(m_ref, -jnp.inf)
        acc_ref[...] = jnp.zeros_like(acc_ref)

    k = k_ref[0, 0]
    v = v_ref[0]
    slabs = [pl.ds(i * sub, sub) for i in range(parts)]
    s = [lax.dot_general(q_ref[0, 0, x, :], k, (((1,), (1,)), ((), ())), preferred_element_type=F32) for x in slabs]
    m_prev = [m_ref[x, :] for x in slabs]
    m_new = [jnp.maximum(mp, jnp.max(si, axis=-1, keepdims=True)) for mp, si in zip(m_prev, s)]
    pr = [jnp.exp2(si - jnp.concatenate([mn] * (tk // LANES), axis=1)).astype(BF16) for si, mn in zip(s, m_new)]
    alpha = [jnp.exp2(mp - mn) for mp, mn in zip(m_prev, m_new)]
    pv = [jnp.dot(p_, v, preferred_element_type=F32) for p_ in pr]
    for x, a, o, mn in zip(slabs, alpha, pv, m_new):
        acc_ref[x, :] = jnp.concatenate([a] * (acc_ref.shape[1] // LANES), axis=1) * acc_ref[x, :] + o
        m_ref[x, :] = mn

    @pl.when(j == pl.num_programs(3) - 1)
    def _():
        acc = acc_ref[...]
        o_ref[0] = (acc[:, :dv] / acc[:, dv:dv + 1]).astype(o_ref.dtype)


def _attend(q, k, v_ext):
    b, nh, sq, dq = q.shape
    sk = k.shape[2]
    dv = v_ext.shape[2] // (2 * nh)
    tq = _tile(sq, (2048, 1024, 512, 256, 128, 64, 32, 16, 8))
    tk = _tile(sk, (3328, 1280, 1024, 640, 512, 256, 128))
    parts = max(1, tq // ATTN_SLAB)
    return pl.pallas_call(
        functools.partial(_attn_body, dv=dv, parts=parts),
        grid=(b, nh, sq // tq, sk // tk),
        in_specs=[pl.BlockSpec((1, 1, tq, dq), lambda bi, hi, i, j: (bi, hi, i, 0)),
                  pl.BlockSpec((1, 1, tk, dq), lambda bi, hi, i, j: (bi, hi, j, 0)),
                  pl.BlockSpec((1, tk, 2 * dv), lambda bi, hi, i, j: (bi, j, hi))],
        out_specs=pl.BlockSpec((1, tq, dv), lambda bi, hi, i, j: (bi, i, hi)),
        out_shape=jax.ShapeDtypeStruct((b, sq, nh * dv), BF16),
        scratch_shapes=[pltpu.VMEM((tq, LANES), F32), pltpu.VMEM((tq, 2 * dv), F32)],
        compiler_params=_cparams("parallel", "parallel", "parallel", "arbitrary"),
        name="attention",
    )(q, k, v_ext)


def _mla_prep_body(qkv_ref, wq_ref, wkv_ref, gq_ref, gkv_ref, gqn_ref, gkn_ref, cos_ref, sin_ref, q_out, k_out, v_out):
    x = qkv_ref[0]
    qc, kvc, kpe = x[:, :Q_LORA], x[:, Q_LORA:Q_LORA + KV_LORA], x[:, Q_LORA + KV_LORA:]
    qn = qc * lax.rsqrt(jnp.mean(qc * qc, axis=-1, keepdims=True) + NORM_EPS) * gq_ref[...]
    kvn = kvc * lax.rsqrt(jnp.mean(kvc * kvc, axis=-1, keepdims=True) + NORM_EPS) * gkv_ref[...]
    q = jnp.dot(qn.astype(BF16), wq_ref[...], preferred_element_type=F32)
    kv = jnp.dot(kvn.astype(BF16), wkv_ref[...], preferred_element_type=F32)
    cos, sin = cos_ref[0], sin_ref[0]
    quarter = QK_ROPE // 4

    def rope(t):
        swapped = jnp.concatenate([t[:, quarter:2 * quarter], t[:, :quarter],
                                   t[:, 3 * quarter:], t[:, 2 * quarter:3 * quarter]], axis=-1)
        return t * cos + swapped * sin

    gqn, gkn = gqn_ref[...], gkn_ref[...]
    kpe_rot = rope(kpe * gkn[:, QK_NOPE:])
    kpe_ss = jnp.sum(kpe * kpe, axis=-1, keepdims=True)
    ones_col = jnp.where(lax.broadcasted_iota(jnp.int32, (x.shape[0], V_HEAD), 1) == 0, 1.0, 0.0).astype(BF16)
    for h in range(MLA_HEADS):
        qh = q[:, h * MLA_SLOT:h * MLA_SLOT + QK_HEAD]
        qh = qh * lax.rsqrt(jnp.sum(qh * qh, axis=-1, keepdims=True) * (1.0 / QK_HEAD) + NORM_EPS) * gqn
        qh = jnp.concatenate([qh[:, :QK_NOPE], rope(qh[:, QK_NOPE:])], axis=-1)
        q_out[0, h] = (qh * (QK_HEAD ** -0.5 * LOG2_E)).astype(q_out.dtype)
        kn = kv[:, h * MLA_SLOT:h * MLA_SLOT + QK_NOPE]
        rstd = lax.rsqrt((jnp.sum(kn * kn, axis=-1, keepdims=True) + kpe_ss) * (1.0 / QK_HEAD) + NORM_EPS)
        k_out[0, h] = jnp.concatenate([kn * rstd * gkn[:, :QK_NOPE], kpe_rot * rstd], axis=-1).astype(k_out.dtype)
        v_out[0, :, h * MLA_SLOT:h * MLA_SLOT + V_HEAD] = kv[:, h * MLA_SLOT + QK_NOPE:(h + 1) * MLA_SLOT].astype(v_out.dtype)
        v_out[0, :, h * MLA_SLOT + V_HEAD:(h + 1) * MLA_SLOT] = ones_col


def _mla_prep(qkv, p, cos, sin):
    b, l, w = qkv.shape
    tm = _tile(l, (256, 128, 64, 32, 16))
    nh = MLA_HEADS
    pad_heads = lambda m, width: jnp.pad(m.reshape(m.shape[0], nh, width), ((0, 0), (0, 0), (0, MLA_SLOT - width))
                                         ).reshape(m.shape[0], nh * MLA_SLOT)
    wq = pad_heads(p['w_qb'], QK_HEAD).astype(BF16)
    wkv = p['w_kvb'].astype(BF16)
    full = lambda a: pl.BlockSpec(a.shape, lambda i, j: (0,) * a.ndim)
    args = [wq, wkv, p['q_norm_g'].reshape(1, Q_LORA), p['kv_norm_g'].reshape(1, KV_LORA),
            p['qn_g'].reshape(1, QK_HEAD), p['kn_g'].reshape(1, QK_HEAD)]
    tab = pl.BlockSpec((1, tm, QK_ROPE), lambda i, j: (0, j, 0))
    head = pl.BlockSpec((1, nh, tm, QK_HEAD), lambda i, j: (i, 0, j, 0))
    return pl.pallas_call(
        _mla_prep_body,
        grid=(b, l // tm),
        in_specs=[pl.BlockSpec((1, tm, w), lambda i, j: (i, j, 0))] + [full(a) for a in args] + [tab, tab],
        out_specs=[head, head, pl.BlockSpec((1, tm, nh * MLA_SLOT), lambda i, j: (i, j, 0))],
        out_shape=[jax.ShapeDtypeStruct((b, nh, l, QK_HEAD), BF16), jax.ShapeDtypeStruct((b, nh, l, QK_HEAD), BF16),
                   jax.ShapeDtypeStruct((b, l, nh * MLA_SLOT), BF16)],
        compiler_params=_cparams("parallel", "parallel"),
        name="mla_prep",
    )(qkv, *args, cos, sin)


def _rope_tables(row, col, l_ctx):
    half = QK_ROPE // 2
    inv = ROPE_THETA ** (-jnp.arange(0, half, 2, dtype=F32) / half)
    ang_r = row.astype(F32)[:, None] * inv[None, :]
    ang_c = col.astype(F32)[:, None] * inv[None, :]
    cos = jnp.concatenate([jnp.cos(ang_r)] * 2 + [jnp.cos(ang_c)] * 2, axis=-1)
    sin = jnp.concatenate([-jnp.sin(ang_r), jnp.sin(ang_r), -jnp.sin(ang_c), jnp.sin(ang_c)], axis=-1)
    cos = jnp.concatenate([jnp.ones((l_ctx, QK_ROPE), F32), cos], axis=0)
    sin = jnp.concatenate([jnp.zeros((l_ctx, QK_ROPE), F32), sin], axis=0)
    return cos[None], sin[None]


def _mla_mixer(h, l_ctx, p, rope_pos):
    b, l, d = h.shape
    qkv = _linear(h.reshape(b * l, d).astype(BF16), p['w_a']).reshape(b, l, -1)
    cos, sin = _rope_tables(rope_pos[0], rope_pos[1], l_ctx)
    q, k, v_ext = _mla_prep(qkv, p, cos, sin)
    o = jnp.concatenate([_attend(q[:, :, :l_ctx], k[:, :, :l_ctx], v_ext[:, :l_ctx]),
                         _attend(q[:, :, l_ctx:], k, v_ext)], axis=1)
    return _linear(o.reshape(b * l, MLA_HEADS * V_HEAD), p['w_o']).reshape(b, l, d)


def _router_body(h_ref, w_ref, b_ref, idx_ref, gate_ref, rank_ref, cnt_ref, run_ref):
    @pl.when(pl.program_id(0) == 0)
    def _():
        run_ref[...] = jnp.zeros_like(run_ref)

    lg = jnp.dot(h_ref[...], w_ref[...], precision=lax.Precision.HIGHEST, preferred_element_type=F32) + b_ref[...]
    tm = lg.shape[0]
    lane = lax.broadcasted_iota(jnp.int32, lg.shape, 1).astype(F32)
    idx_acc = jnp.zeros(lg.shape, F32)
    val_acc = jnp.zeros(lg.shape, F32)
    picked = jnp.zeros(lg.shape, F32)
    sels = []
    top = None
    den = None
    for j in range(TOP_K):
        mx = jnp.max(lg, axis=-1, keepdims=True)
        sel = jnp.min(jnp.where(lg == mx, lane, float(ROUTER_LANES)), axis=-1, keepdims=True)
        if j == 0:
            top = mx
        e = jnp.exp(mx - top)
        den = e if den is None else den + e
        hit = lane == sel
        sels.append(hit)
        idx_acc = jnp.where(lane == float(j), sel, idx_acc)
        val_acc = jnp.where(lane == float(j), e, val_acc)
        picked = jnp.where(hit, 1.0, picked)
        lg = jnp.where(hit, -jnp.inf, lg)
    idx_ref[...] = idx_acc.astype(jnp.int32)
    gate_ref[...] = val_acc / den
    rs = lax.broadcasted_iota(jnp.int32, (tm, tm), 0)
    cs = lax.broadcasted_iota(jnp.int32, (tm, tm), 1)
    before = jnp.where(rs > cs, 1.0, 0.0).astype(BF16)
    base = run_ref[...] + jnp.dot(before, picked.astype(BF16), preferred_element_type=F32)
    rank_acc = jnp.zeros(lg.shape, F32)
    for j in range(TOP_K):
        rk = jnp.sum(jnp.where(sels[j], base, 0.0), axis=-1, keepdims=True)
        rank_acc = jnp.where(lane == float(j), rk, rank_acc)
    rank_ref[...] = rank_acc.astype(jnp.int32)
    run_ref[...] = run_ref[...] + jnp.sum(picked, axis=0, keepdims=True)
    cnt_ref[...] = run_ref[...].astype(jnp.int32)


def _router(h, router_w, router_b):
    t, d = h.shape
    tm = _tile(t, (512, 256, 128, 64, 32, 16, 8))
    w = jnp.concatenate([router_w, jnp.zeros((d, ROUTER_LANES - N_EXPERTS), F32)], axis=1)
    bias = jnp.concatenate([router_b, jnp.full((ROUTER_LANES - N_EXPERTS,), -1e30, F32)]).reshape(1, ROUTER_LANES)
    tok = pl.BlockSpec((tm, ROUTER_LANES), lambda i: (i, 0))
    return pl.pallas_call(
        _router_body,
        grid=(t // tm,),
        in_specs=[pl.BlockSpec((tm, d), lambda i: (i, 0)),
                  pl.BlockSpec((d, ROUTER_LANES), lambda i: (0, 0)),
                  pl.BlockSpec((1, ROUTER_LANES), lambda i: (0, 0))],
        out_specs=[tok, tok, tok, pl.BlockSpec((1, ROUTER_LANES), lambda i: (0, 0))],
        out_shape=[jax.ShapeDtypeStruct((t, ROUTER_LANES), jnp.int32), jax.ShapeDtypeStruct((t, ROUTER_LANES), F32),
                   jax.ShapeDtypeStruct((t, ROUTER_LANES), jnp.int32), jax.ShapeDtypeStruct((1, ROUTER_LANES), jnp.int32)],
        scratch_shapes=[pltpu.VMEM((1, ROUTER_LANES), F32)],
        compiler_params=_cparams("arbitrary"),
        name="router",
    )(h, w, bias)


def _expert_body(be_ref, nb_ref, x_ref, win_ref, bin_ref, wout_ref, bout_ref, gate_ref, o_ref, win_bf, wout_bf):
    i = pl.program_id(0)
    fresh = jnp.logical_or(i == 0, be_ref[i] != be_ref[jnp.maximum(i - 1, 0)])

    @pl.when(fresh)
    def _():
        win_bf[...] = win_ref[0].astype(BF16)
        wout_bf[...] = wout_ref[0].astype(BF16)

    @pl.when(i < nb_ref[0])
    def _():
        sub = MOE_ROWS // MOE_SLABS
        slabs = [pl.ds(j * sub, sub) for j in range(MOE_SLABS)]
        gu = [jnp.dot(x_ref[x, :], win_bf[...], preferred_element_type=F32) + bin_ref[0] for x in slabs]
        glu = [jnp.minimum(g[:, :D_EXPERT], SWIGLU_LIMIT) for g in gu]
        lin = [jnp.clip(g[:, D_EXPERT:], -SWIGLU_LIMIT, SWIGLU_LIMIT) for g in gu]
        act = [(a * jax.nn.sigmoid(SWIGLU_ALPHA * a) * (b + 1.0)).astype(BF16) for a, b in zip(glu, lin)]
        y = [jnp.dot(a, wout_bf[...], preferred_element_type=F32) + bout_ref[0] for a in act]
        for x, yy in zip(slabs, y):
            o_ref[x, :] = (yy * gate_ref[x, :]).astype(o_ref.dtype)

    @pl.when(i >= nb_ref[0])
    def _():
        o_ref[...] = jnp.zeros_like(o_ref)


def _moe(h, h_bf, router_w, router_b, w_in, b_in, w_out, b_out):
    t, d = h.shape
    n_assign = t * TOP_K
    idx, gates, ranks, counts = _router(h, router_w, router_b)
    flat_e = idx[:, :TOP_K].reshape(n_assign)
    flat_g = gates[:, :TOP_K].reshape(n_assign)
    rank = ranks[:, :TOP_K].reshape(n_assign)
    counts = counts[0, :N_EXPERTS]
    padded = (counts + MOE_ROWS - 1) // MOE_ROWS * MOE_ROWS
    pad_end = jnp.cumsum(padded)
    pad_start = pad_end - padded
    experts = jnp.arange(N_EXPERTS, dtype=jnp.int32)
    lookup = lambda table, e: jnp.sum(jnp.where(e[:, None] == experts[None, :], table[None, :], 0), axis=1)
    dest = lookup(pad_start, flat_e) + rank
    n_blocks = -(-n_assign // MOE_ROWS) + N_EXPERTS
    n_rows = n_blocks * MOE_ROWS
    block_start = jnp.arange(n_blocks, dtype=jnp.int32) * MOE_ROWS
    block_e = jnp.minimum(jnp.sum((pad_end[None, :] <= block_start[:, None]).astype(jnp.int32), axis=1), N_EXPERTS - 1)
    order = jnp.argsort(flat_e, stable=True).astype(jnp.int32)
    start = jnp.cumsum(counts) - counts
    in_block = jnp.arange(MOE_ROWS, dtype=jnp.int32)[None, :]
    row_rank = (block_start - lookup(pad_start, block_e))[:, None] + in_block
    row_live = (row_rank < lookup(counts, block_e)[:, None]).reshape(n_rows)
    row_src = order[jnp.minimum(lookup(start, block_e)[:, None] + row_rank, n_assign - 1).reshape(n_rows)]
    row_tok = row_src // TOP_K
    row_gate = jnp.where(row_live, flat_g[row_src], 0.0)
    n_used = (pad_end[-1] // MOE_ROWS).astype(jnp.int32).reshape(1)
    xb = h_bf[row_tok]
    f2 = 2 * D_EXPERT
    grid_spec = pltpu.PrefetchScalarGridSpec(
        num_scalar_prefetch=2,
        grid=(n_blocks,),
        in_specs=[pl.BlockSpec((MOE_ROWS, d), lambda i, be, nb: (i, 0)),
                  pl.BlockSpec((1, d, f2), lambda i, be, nb: (be[i], 0, 0)),
                  pl.BlockSpec((1, 1, f2), lambda i, be, nb: (be[i], 0, 0)),
                  pl.BlockSpec((1, D_EXPERT, d), lambda i, be, nb: (be[i], 0, 0)),
                  pl.BlockSpec((1, 1, d), lambda i, be, nb: (be[i], 0, 0)),
                  pl.BlockSpec((MOE_ROWS, 1), lambda i, be, nb: (i, 0))],
        out_specs=pl.BlockSpec((MOE_ROWS, d), lambda i, be, nb: (i, 0)),
        scratch_shapes=[pltpu.VMEM((d, f2), BF16), pltpu.VMEM((D_EXPERT, d), BF16)],
    )
    yb = pl.pallas_call(
        _expert_body,
        grid_spec=grid_spec,
        out_shape=jax.ShapeDtypeStruct((n_rows, d), BF16),
        compiler_params=_cparams("arbitrary"),
        name="moe_experts",
    )(block_e, n_used, xb, w_in, b_in.reshape(N_EXPERTS, 1, f2),
      w_out, b_out.reshape(N_EXPERTS, 1, d), row_gate.reshape(n_rows, 1))
    dest = dest.reshape(t, TOP_K)
    return [yb[dest[:, j]] for j in range(TOP_K)]


def kernel(x, c, ctx, c_ctx, ada_w, ada_b, norm_mix_g, norm_ffn_g, rwkv_mu, rwkv_w_rkv, rwkv_w0, rwkv_w1, rwkv_w2, rwkv_a0, rwkv_a1, rwkv_a2, rwkv_g1, rwkv_g2, rwkv_k_k, rwkv_k_a, rwkv_r_k, rwkv_ln_g, rwkv_ln_b, rwkv_w_o, rwkv_v0, rwkv_v1, rwkv_v2, hy_w_in, hy_b_in, hy_short_w, hy_short_b, hy_f_w1, hy_f_b1, hy_f_w2, hy_f_b2, hy_f_w3, hy_f_b3, hy_f_freq, hy_f_wout, hy_bias, hy_w_out, hy_b_out, mla_w_a, mla_q_norm_g, mla_kv_norm_g, mla_w_qb, mla_w_kvb, mla_qn_g, mla_kn_g, mla_w_o, moe_router_w, moe_router_b, moe_w_in, moe_b_in, moe_w_out, moe_b_out):
    bsz, s_len, d = x.shape
    l_ctx = ctx.shape[1]
    rows = s_len // GRID_W
    row = jnp.repeat(jnp.arange(rows, dtype=jnp.int32), GRID_W)
    col = jnp.tile(jnp.arange(GRID_W, dtype=jnp.int32), rows)
    silu = jnp.concatenate([jax.nn.silu(c), jax.nn.silu(c_ctx)[None, :]], axis=0)
    silu = jnp.concatenate([silu, jnp.zeros((-(bsz + 1) % 8, d), F32)], axis=0)
    is_ctx = (jnp.arange(l_ctx + s_len, dtype=jnp.int32) < l_ctx)[None, :, None]
    per_row = lambda m: jnp.where(is_ctx, m[:, 0:1], m[:, 1:2])
    xs = jnp.concatenate([ctx, x], axis=1)
    mods = []
    for i in range(DEPTH):
        mod = _linear(silu, ada_w[i], ada_b[i], tn=1024)
        mod = jnp.stack([jnp.broadcast_to(mod[bsz:bsz + 1], (bsz, N_MOD * d)), mod[:bsz]], axis=1)
        mods.append(jnp.split(mod, N_MOD, axis=-1))
    v_first = None
    h = _norm_mod(xs, norm_mix_g[0], mods[0][1], mods[0][0], l_ctx, BF16)
    for i in range(DEPTH):
        j = i // N_MIXERS
        kind = i % N_MIXERS
        sh1, sc1, g1, sh2, sc2, g2 = mods[i]
        if kind == 0:
            p = {'mu': rwkv_mu[j], 'w_rkv': rwkv_w_rkv[j], 'w0': rwkv_w0[j], 'w1': rwkv_w1[j],
                 'w2': rwkv_w2[j], 'a0': rwkv_a0[j], 'a1': rwkv_a1[j], 'a2': rwkv_a2[j],
                 'g1': rwkv_g1[j], 'g2': rwkv_g2[j], 'k_k': rwkv_k_k[j], 'k_a': rwkv_k_a[j],
                 'r_k': rwkv_r_k[j], 'ln_g': rwkv_ln_g[j], 'ln_b': rwkv_ln_b[j], 'w_o': rwkv_w_o[j]}
            vres = None if j == 0 else (rwkv_v0[j - 1], rwkv_v1[j - 1], rwkv_v2[j - 1])
            y, v_cur = _rwkv_mixer(h, l_ctx, p, v_first, vres)
            if j == 0:
                v_first = v_cur
        elif kind == 1:
            p = {'w_in': hy_w_in[j], 'b_in': hy_b_in[j], 'short_w': hy_short_w[j], 'short_b': hy_short_b[j],
                 'f_w1': hy_f_w1[j], 'f_b1': hy_f_b1[j], 'f_w2': hy_f_w2[j], 'f_b2': hy_f_b2[j],
                 'f_w3': hy_f_w3[j], 'f_b3': hy_f_b3[j], 'f_freq': hy_f_freq[j], 'f_wout': hy_f_wout[j],
                 'bias': hy_bias[j], 'w_out': hy_w_out[j], 'b_out': hy_b_out[j]}
            y = _hyena_mixer(h, l_ctx, p)
        else:
            p = {'w_a': mla_w_a[j], 'q_norm_g': mla_q_norm_g[j], 'kv_norm_g': mla_kv_norm_g[j],
                 'w_qb': mla_w_qb[j], 'w_kvb': mla_w_kvb[j], 'qn_g': mla_qn_g[j], 'kn_g': mla_kn_g[j],
                 'w_o': mla_w_o[j]}
            y = _mla_mixer(h, l_ctx, p, (row, col))
        xs, h2, h2_bf = _residual_norm(xs, [y], g1, norm_ffn_g[i], sc2, sh2, l_ctx, (F32, BF16))
        moe_p = (moe_router_w[i], moe_router_b[i], moe_w_in[i], moe_b_in[i], moe_w_out[i], moe_b_out[i])
        parts = [f.reshape(xs.shape) for f in _moe(h2.reshape(-1, d), h2_bf.reshape(-1, d), *moe_p)]
        if i + 1 < DEPTH:
            xs, h = _residual_norm(xs, parts, g2, norm_mix_g[i + 1], mods[i + 1][1], mods[i + 1][0], l_ctx, (BF16,))
        else:
            xs = xs + per_row(g2) * sum(f.astype(F32) for f in parts)
    return xs[:, l_ctx:]
```
